```python
import math
import jax
import jax.numpy as jnp
from jax import lax
import numpy as np

D_MODEL = 4096
BATCH = 2
SEQ = 4096
DEPTH = 1
DEC_BATCH = 32
DEC_SEQ = 4
PAST_LEN = 8192
PAGE_SIZE = 128

SSD_WIDTH = D_MODEL
ATTN_WIDTH = D_MODEL
D_MIX = SSD_WIDTH + ATTN_WIDTH
SSD_HEADDIM = 64
SSD_HEADS = SSD_WIDTH // SSD_HEADDIM
SSD_GROUPS = 8
SSD_RPG = SSD_HEADS // SSD_GROUPS
D_STATE = 128
CONV_W = 4
CONV_CH = SSD_WIDTH + 2 * SSD_GROUPS * D_STATE
SSD_CHUNK = 256
HEAD_DIM = 128
ATTN_HEADS = ATTN_WIDTH // HEAD_DIM
KV_HEADS = 8
KV_WIDTH = KV_HEADS * HEAD_DIM
ROPE_THETA = 500000.0
IDX_HEADS = 32
IDX_DIM = 128
IDX_TOPK = 256
Q_BLOCK = 128
PEER_HEADS = 8
PEER_KEYS = 128
PEER_EXPERTS = PEER_KEYS * PEER_KEYS
PEER_TOPK = 16
PEER_D_KEY = 256
PEER_HALF = PEER_D_KEY // 2
PEER_BLOCK = 128
IN_SIZES = (SSD_WIDTH, CONV_CH, SSD_HEADS, ATTN_WIDTH, KV_WIDTH, KV_WIDTH, IDX_HEADS * IDX_DIM, IDX_HEADS, IDX_DIM)
IN_COLS = sum(IN_SIZES)
IN_SPLITS = tuple(int(s) for s in np.cumsum(IN_SIZES)[:-1])

kernel_name = 'hymba_ssd_dsa_peer_adaln_step'

F32 = jnp.float32


def rms_norm(x, g, eps=1e-6):
    xf = x.astype(F32)
    y = xf * lax.rsqrt(jnp.mean(xf * xf, axis=-1, keepdims=True) + eps)
    return (y * g.astype(F32)).astype(x.dtype)


def rope(x, pos):
    rot = x.shape[-1] // 4
    half = rot // 2
    inv = ROPE_THETA ** (-jnp.arange(half, dtype=F32) / half)
    ang = pos.astype(F32)[:, None] * inv[None, :]
    cos = jnp.cos(ang)[None, :, None, :]
    sin = jnp.sin(ang)[None, :, None, :]
    xf = x.astype(F32)
    x1, x2 = xf[..., :half], xf[..., half:rot]
    out = jnp.concatenate([x1 * cos - x2 * sin, x2 * cos + x1 * sin, xf[..., rot:]], axis=-1)
    return out.astype(x.dtype)


def adaln(c, w_mod, b_mod):
    mod = (jax.nn.silu(c) @ w_mod + b_mod)[:, None, :]
    return jnp.split(mod, 6, axis=-1)


def modulate(x, g, shift, scale):
    return rms_norm(x, g) * (1 + scale) + shift


def mixer_inputs(h, pos, p):
    b, L, _ = h.shape
    z, xbc, dt_raw, q, k, v, qi, wi, ki = jnp.split(h @ p['w_in'], IN_SPLITS, axis=-1)
    q = rope(q.reshape(b, L, ATTN_HEADS, HEAD_DIM), pos)
    k = rope(k.reshape(b, L, KV_HEADS, HEAD_DIM), pos)
    v = v.reshape(b, L, KV_HEADS, HEAD_DIM)
    qi = rope(qi.reshape(b, L, IDX_HEADS, IDX_DIM), pos)
    ki = rope(rms_norm(ki, p['kidx_norm_g'])[:, :, None, :], pos)[:, :, 0, :]
    wi = wi * IDX_HEADS ** -0.5
    return z, xbc, dt_raw, q, k, v, qi, wi, ki


def ssd_scan(x, dt, A, Bm, Cm, h0):
    b, L = x.shape[:2]
    cs = min(SSD_CHUNK, L)
    nc = -(-L // cs)
    pad = nc * cs - L

    def prep(a):
        if pad:
            a = jnp.pad(a, [(0, 0), (0, pad)] + [(0, 0)] * (a.ndim - 2))
        return jnp.swapaxes(a.reshape((b, nc, cs) + a.shape[2:]), 0, 1)

    mask = jnp.tril(jnp.ones((cs, cs), dtype=bool))

    def step(h, inp):
        xc, dtc, Bc, Cc = inp
        acs = jnp.cumsum(dtc * A, axis=1)
        xd = xc * dtc[..., None]
        seg = jnp.where(mask[None, :, :, None, None], acs[:, :, None] - acs[:, None, :], -jnp.inf)
        lm = jnp.exp(seg)
        cb = jnp.einsum('blgn,bsgn->blsg', Cc, Bc)
        y = (jnp.einsum('blsg,blsgr,bsgrp->blgrp', cb, lm, xd)
             + jnp.einsum('blgn,bgrpn,blgr->blgrp', Cc, h, jnp.exp(acs)))
        decay = jnp.exp(acs[:, -1:] - acs)
        h = h * jnp.exp(acs[:, -1])[..., None, None] + jnp.einsum('blgn,blgr,blgrp->bgrpn', Bc, decay, xd)
        return h, y

    h_last, ys = lax.scan(step, h0, (prep(x), prep(dt), prep(Bm), prep(Cm)))
    y = jnp.swapaxes(ys, 0, 1).reshape((b, nc * cs) + x.shape[2:])[:, :L]
    return y, h_last


def ssd_mixer(z, xbc, dt_raw, conv_hist, ssm0, p):
    b, L, _ = xbc.shape
    ext = jnp.concatenate([conv_hist.astype(xbc.dtype), xbc], axis=1)
    conv_new = ext[:, L:]
    acc = p['conv_b']
    for j in range(CONV_W):
        acc = acc + ext[:, j:j + L] * p['conv_w'][j]
    xbc = jax.nn.silu(acc)
    xs, Bm, Cm = jnp.split(xbc, [SSD_WIDTH, SSD_WIDTH + SSD_GROUPS * D_STATE], axis=-1)
    dt = jax.nn.softplus((dt_raw + p['dt_bias']).astype(F32))
    A = -jnp.exp(p['a_log'].astype(F32))
    xh = xs.reshape(b, L, SSD_GROUPS, SSD_RPG, SSD_HEADDIM).astype(F32)
    y, ssm_new = ssd_scan(xh, dt.reshape(b, L, SSD_GROUPS, SSD_RPG), A.reshape(SSD_GROUPS, SSD_RPG),
                          Bm.reshape(b, L, SSD_GROUPS, D_STATE).astype(F32),
                          Cm.reshape(b, L, SSD_GROUPS, D_STATE).astype(F32), ssm0.astype(F32))
    y = y + xh * p['d_skip'].astype(F32).reshape(SSD_GROUPS, SSD_RPG)[:, :, None]
    y = y.reshape(b, L, SSD_WIDTH) * jax.nn.silu(z.astype(F32))
    y = rms_norm(y, p['ssd_norm_g']).astype(z.dtype)
    return y, ssm_new, conv_new


def indexer_scores(qi, wi, ki):
    s = jnp.einsum('bqhd,bsd->bqhs', qi, ki).astype(F32) * IDX_DIM ** -0.5
    return jnp.einsum('bqh,bqhs->bqs', wi.astype(F32), jax.nn.relu(s))


def select_keys(qi, wi, ki, t, topk):
    s = indexer_scores(qi, wi, ki)
    visible = jnp.arange(ki.shape[1])[None, :] <= t[:, None]
    s = jnp.where(visible[None], s, -jnp.inf)
    _, sel = lax.top_k(s, topk)
    valid = sel <= t[None, :, None]
    return sel, valid


def gather_rows(a, idx):
    return jax.vmap(lambda ab, ib: ab[ib])(a, idx)


def sparse_attend(q, kg, vg, valid):
    b, nq = q.shape[:2]
    qg = q.reshape(b, nq, KV_HEADS, ATTN_HEADS // KV_HEADS, HEAD_DIM)
    s = jnp.einsum('bqgrd,bqkgd->bqgrk', qg, kg).astype(F32) * HEAD_DIM ** -0.5
    s = jnp.where(valid[:, :, None, None, :], s, -jnp.inf)
    w = jax.nn.softmax(s, axis=-1).astype(vg.dtype)
    o = jnp.einsum('bqgrk,bqkgd->bqgrd', w, vg)
    return o.reshape(b, nq, ATTN_WIDTH)


def prompt_sparse_attention(q, k, v, qi, wi, ki):
    b, L = q.shape[:2]
    topk = min(IDX_TOPK, L // 4)
    nb = L // Q_BLOCK

    def blocks(a):
        return jnp.swapaxes(a.reshape((b, nb, Q_BLOCK) + a.shape[2:]), 0, 1)

    def one_block(inp):
        qb, qib, wib, start = inp
        t = start + jnp.arange(Q_BLOCK)
        sel, valid = select_keys(qib, wib, ki, t, topk)
        return sparse_attend(qb, gather_rows(k, sel), gather_rows(v, sel), valid)

    out = lax.map(one_block, (blocks(q), blocks(qi), blocks(wi), jnp.arange(nb) * Q_BLOCK))
    return jnp.swapaxes(out, 0, 1).reshape(b, L, ATTN_WIDTH)


def sample_sparse_attention(q, k_new, v_new, qi, wi, ki_new, cache_k, cache_v, cache_kidx, page_table):
    b, T = q.shape[:2]
    past = page_table.shape[1] * PAGE_SIZE
    topk = min(IDX_TOPK, (past + T) // 4)
    ki_past = cache_kidx[page_table].reshape(b, past, IDX_DIM)
    ki_all = jnp.concatenate([ki_past.astype(ki_new.dtype), ki_new], axis=1)
    t = past + jnp.arange(T)
    sel, valid = select_keys(qi, wi, ki_all, t, topk)
    in_past = sel < past
    ps = jnp.minimum(sel, past - 1)
    phys = jax.vmap(lambda pt, s: pt[s // PAGE_SIZE])(page_table, ps)
    off = ps % PAGE_SIZE
    ns = jnp.clip(sel - past, 0, T - 1)

    def gather(pool, new):
        old = pool[phys, off].astype(new.dtype)
        cur = gather_rows(new, ns)
        return jnp.where(in_past[..., None, None], old, cur)

    return sparse_attend(q, gather(cache_k, k_new), gather(cache_v, v_new), valid)


def merge_mixers(y_ssd, o_attn, p):
    y = jnp.concatenate([y_ssd, rms_norm(o_attn, p['attn_norm_g'])], axis=-1)
    return y @ p['w_out']


def peer_ffn(h, p):
    b, L, d = h.shape
    n_tok = b * L
    nb = -(-n_tok // PEER_BLOCK)
    xt = jnp.pad(h.reshape(n_tok, d), ((0, nb * PEER_BLOCK - n_tok), (0, 0)))

    def one_block(xb):
        q = (xb @ p['peer_wq']).reshape(PEER_BLOCK, PEER_HEADS, PEER_D_KEY)
        s1 = jnp.einsum('thd,kd->thk', q[..., :PEER_HALF], p['peer_k1']).astype(F32)
        s2 = jnp.einsum('thd,kd->thk', q[..., PEER_HALF:], p['peer_k2']).astype(F32)
        v1, i1 = lax.top_k(s1, PEER_TOPK)
        v2, i2 = lax.top_k(s2, PEER_TOPK)
        cand = (v1[..., :, None] + v2[..., None, :]).reshape(PEER_BLOCK, PEER_HEADS, PEER_TOPK * PEER_TOPK)
        sv, ci = lax.top_k(cand, PEER_TOPK)
        e = (jnp.take_along_axis(i1, ci // PEER_TOPK, axis=-1) * PEER_KEYS
             + jnp.take_along_axis(i2, ci % PEER_TOPK, axis=-1))
        g = jax.nn.softmax(sv, axis=-1).reshape(PEER_BLOCK, PEER_HEADS * PEER_TOPK)
        e = e.reshape(PEER_BLOCK, PEER_HEADS * PEER_TOPK)
        u = p['peer_u'][e]
        act = jax.nn.gelu(jnp.einsum('td,ted->te', xb, u).astype(F32))
        coef = (g * act).astype(xb.dtype)
        return jnp.einsum('te,ted->td', coef, p['peer_v'][e])

    out = lax.map(one_block, xt.reshape(nb, PEER_BLOCK, d))
    return out.reshape(nb * PEER_BLOCK, d)[:n_tok].reshape(b, L, d)


def prompt_layer(x, c, p):
    b, L, _ = x.shape
    sh1, sc1, g1, sh2, sc2, g2 = adaln(c, p['w_mod'], p['b_mod'])
    h = modulate(x, p['norm1_g'], sh1, sc1)
    z, xbc, dt_raw, q, k, v, qi, wi, ki = mixer_inputs(h, jnp.arange(L), p)
    conv_hist = jnp.zeros((b, CONV_W - 1, CONV_CH), x.dtype)
    ssm0 = jnp.zeros((b, SSD_GROUPS, SSD_RPG, SSD_HEADDIM, D_STATE), F32)
    y_ssd, ssm_new, conv_new = ssd_mixer(z, xbc, dt_raw, conv_hist, ssm0, p)
    o_attn = prompt_sparse_attention(q, k, v, qi, wi, ki)
    x = x + g1 * merge_mixers(y_ssd, o_attn, p)
    x = x + g2 * peer_ffn(modulate(x, p['norm2_g'], sh2, sc2), p)
    return x, (k, v, ki, ssm_new.reshape(b, SSD_HEADS, SSD_HEADDIM, D_STATE), conv_new)


def sample_layer(x, c, cache_k, cache_v, cache_kidx, ssm_state, conv_state, page_table, p):
    b, T, _ = x.shape
    sh1, sc1, g1, sh2, sc2, g2 = adaln(c, p['w_mod'], p['b_mod'])
    h = modulate(x, p['norm1_g'], sh1, sc1)
    pos = page_table.shape[1] * PAGE_SIZE + jnp.arange(T)
    z, xbc, dt_raw, q, k, v, qi, wi, ki = mixer_inputs(h, pos, p)
    ssm0 = ssm_state.reshape(b, SSD_GROUPS, SSD_RPG, SSD_HEADDIM, D_STATE)
    y_ssd, ssm_new, conv_new = ssd_mixer(z, xbc, dt_raw, conv_state, ssm0, p)
    o_attn = sample_sparse_attention(q, k, v, qi, wi, ki, cache_k, cache_v, cache_kidx, page_table)
    x = x + g1 * merge_mixers(y_ssd, o_attn, p)
    x = x + g2 * peer_ffn(modulate(x, p['norm2_g'], sh2, sc2), p)
    return x, (k, v, ki, ssm_new.reshape(b, SSD_HEADS, SSD_HEADDIM, D_STATE), conv_new)


def stack_layers(rows, i):
    return jnp.stack([r[i] for r in rows], axis=0)


def setup_inputs(seed: int = 0) -> dict:
    key = jax.random.key(seed)
    ks = jax.random.split(key, 32)
    n_pages = PAST_LEN // PAGE_SIZE
    n_used = DEC_BATCH * n_pages
    n_pool = n_used + max(1, n_used // 4)

    def nrm(k, shape, scale=1.0):
        return jax.random.normal(k, shape, F32) * scale

    def gain(k, shape):
        return 1.0 + nrm(k, shape, 0.05)

    page_table = jax.random.permutation(ks[9], n_pool)[:n_used].reshape(DEC_BATCH, n_pages).astype(jnp.int32)
    dt0 = jnp.exp(jax.random.uniform(ks[14], (DEPTH, SSD_HEADS), F32, math.log(1e-3), math.log(1e-1)))
    dt_bias = dt0 + jnp.log(-jnp.expm1(-dt0))
    a_log = jnp.log(jax.random.uniform(ks[15], (DEPTH, SSD_HEADS), F32, 1.0, 16.0))
    return {
        'x_prompt': nrm(ks[0], (BATCH, SEQ, D_MODEL)),
        'x_sample': nrm(ks[1], (DEC_BATCH, DEC_SEQ, D_MODEL)),
        'c_prompt': nrm(ks[2], (BATCH, D_MODEL)),
        'c_sample': nrm(ks[3], (DEC_BATCH, D_MODEL)),
        'cache_k': nrm(ks[4], (DEPTH, n_pool, PAGE_SIZE, KV_HEADS, HEAD_DIM)),
        'cache_v': nrm(ks[5], (DEPTH, n_pool, PAGE_SIZE, KV_HEADS, HEAD_DIM)),
        'cache_kidx': nrm(ks[6], (DEPTH, n_pool, PAGE_SIZE, IDX_DIM)),
        'state_ssm': nrm(ks[7], (DEPTH, DEC_BATCH, SSD_HEADS, SSD_HEADDIM, D_STATE), 0.1),
        'state_conv': nrm(ks[8], (DEPTH, DEC_BATCH, CONV_W - 1, CONV_CH)),
        'page_table': page_table,
        'w_mod': nrm(ks[10], (DEPTH, D_MODEL, 6 * D_MODEL), 0.5 * D_MODEL ** -0.5),
        'b_mod': nrm(ks[11], (DEPTH, 6 * D_MODEL), 0.02),
        'norm1_g': gain(ks[12], (DEPTH, D_MODEL)),
        'w_in': nrm(ks[13], (DEPTH, D_MODEL, IN_COLS), D_MODEL ** -0.5),
        'conv_w': nrm(ks[16], (DEPTH, CONV_W, CONV_CH), CONV_W ** -0.5),
        'conv_b': nrm(ks[17], (DEPTH, CONV_CH), 0.02),
        'dt_bias': dt_bias,
        'a_log': a_log,
        'd_skip': 1.0 + nrm(ks[18], (DEPTH, SSD_HEADS), 0.1),
        'ssd_norm_g': gain(ks[19], (DEPTH, SSD_WIDTH)),
        'kidx_norm_g': gain(ks[20], (DEPTH, IDX_DIM)),
        'attn_norm_g': gain(ks[21], (DEPTH, ATTN_WIDTH)),
        'w_out': nrm(ks[22], (DEPTH, D_MIX, D_MODEL), D_MIX ** -0.5),
        'norm2_g': gain(ks[23], (DEPTH, D_MODEL)),
        'peer_wq': nrm(ks[24], (DEPTH, D_MODEL, PEER_HEADS * PEER_D_KEY), D_MODEL ** -0.5),
        'peer_k1': nrm(ks[25], (DEPTH, PEER_KEYS, PEER_HALF), PEER_HALF ** -0.5),
        'peer_k2': nrm(ks[26], (DEPTH, PEER_KEYS, PEER_HALF), PEER_HALF ** -0.5),
        'peer_u': nrm(ks[27], (DEPTH, PEER_EXPERTS, D_MODEL), D_MODEL ** -0.5),
        'peer_v': nrm(ks[28], (DEPTH, PEER_EXPERTS, D_MODEL), 0.5),
        'final_norm_g': gain(ks[29], (D_MODEL,)),
    }


def reference(x_prompt, x_sample, c_prompt, c_sample, cache_k, cache_v, cache_kidx, state_ssm, state_conv,
              page_table, w_mod, b_mod, norm1_g, w_in, conv_w, conv_b, dt_bias, a_log, d_skip, ssd_norm_g,
              kidx_norm_g, attn_norm_g, w_out, norm2_g, peer_wq, peer_k1, peer_k2, peer_u, peer_v, final_norm_g):
    xp, xs = x_prompt, x_sample
    rows_p, rows_s = [], []
    for l in range(DEPTH):
        p = {'w_mod': w_mod[l], 'b_mod': b_mod[l], 'norm1_g': norm1_g[l], 'w_in': w_in[l],
             'conv_w': conv_w[l], 'conv_b': conv_b[l], 'dt_bias': dt_bias[l], 'a_log': a_log[l],
             'd_skip': d_skip[l], 'ssd_norm_g': ssd_norm_g[l], 'kidx_norm_g': kidx_norm_g[l],
             'attn_norm_g': attn_norm_g[l], 'w_out': w_out[l], 'norm2_g': norm2_g[l],
             'peer_wq': peer_wq[l], 'peer_k1': peer_k1[l], 'peer_k2': peer_k2[l],
             'peer_u': peer_u[l], 'peer_v': peer_v[l]}
        xp, st_p = prompt_layer(xp, c_prompt, p)
        xs, st_s = sample_layer(xs, c_sample, cache_k[l], cache_v[l], cache_kidx[l], state_ssm[l],
                                state_conv[l], page_table, p)
        rows_p.append(st_p)
        rows_s.append(st_s)
    y_prompt = rms_norm(xp, final_norm_g)
    y_sample = rms_norm(xs, final_norm_g)
    k_prompt = stack_layers(rows_p, 0)
    v_prompt = stack_layers(rows_p, 1)
    kidx_prompt = stack_layers(rows_p, 2)
    ssm_prompt = stack_layers(rows_p, 3)
    conv_prompt = stack_layers(rows_p, 4)
    k_sample = stack_layers(rows_s, 0)
    v_sample = stack_layers(rows_s, 1)
    kidx_sample = stack_layers(rows_s, 2)
    ssm_sample = stack_layers(rows_s, 3)
    conv_sample = stack_layers(rows_s, 4)
    return (y_prompt, y_sample, k_prompt, v_prompt, kidx_prompt, ssm_prompt, conv_prompt,
            k_sample, v_sample, kidx_sample, ssm_sample, conv_sample)
```

```python
import functools
import math

import jax
import jax.numpy as jnp
import numpy as np
from jax import lax
from jax.experimental import pallas as pl
from jax.experimental.pallas import tpu as pltpu

F32 = jnp.float32
BF16 = jnp.bfloat16
I32 = jnp.int32

SSD_HEADDIM = 64
SSD_GROUPS = 8
D_STATE = 128
CONV_W = 4
SSD_CHUNK = 256
HEAD_DIM = 128
KV_HEADS = 8
ROPE_THETA = 500000.0
IDX_HEADS = 32
IDX_DIM = 128
IDX_TOPK = 256
Q_BLOCK = 128
PEER_HEADS = 8
PEER_KEYS = 128
PEER_TOPK = 16
PEER_D_KEY = 256
PAGE_SIZE = 128
EPS = 1e-6

LANES = 128
SUBLANES = 8
VMEM_LIMIT_BYTES = 56 * 1024 * 1024

PROJ_TN = 512
ELEMWISE_TM = 256
SAMPLE_PAD = 8
SAMPLE_CHUNK = 128
ATTN_TK = 256
PAGES_PER_STEP = 8
PEER_TE = 512
NEG_BIG = -1e30
INT_MIN = -2 ** 31


def _cparams(sem):
    return pltpu.CompilerParams(dimension_semantics=sem, vmem_limit_bytes=VMEM_LIMIT_BYTES)


def _nt(a, b):
    return lax.dot_general(a, b, (((1,), (1,)), ((), ())), preferred_element_type=F32)


def _tn(a, b):
    return lax.dot_general(a, b, (((0,), (0,)), ((), ())), preferred_element_type=F32)


def _rms(x, g):
    return x * lax.rsqrt(jnp.mean(x * x, axis=-1, keepdims=True) + EPS) * g


def _rope_tile(a, cos, sin):
    half = HEAD_DIM // 8
    lane = lax.broadcasted_iota(I32, a.shape, 1)
    sw = jnp.where(lane < half, pltpu.roll(a, HEAD_DIM - half, 1), pltpu.roll(a, half, 1))
    return a * cos + sw * sin


def _float_key(x):
    bits = pltpu.bitcast(x, I32)
    return jnp.where(bits < 0, bits ^ jnp.int32(0x7FFFFFFF), bits)


def _key_float(key):
    bits = jnp.where(key < 0, key ^ jnp.int32(0x7FFFFFFF), key)
    return pltpu.bitcast(bits, F32)


def _bisect_kth(count_ge, k, shape):
    imin = jnp.int32(INT_MIN)

    def body(i, ans):
        cand_u = ans | lax.shift_left(jnp.int32(1), jnp.int32(31) - i)
        return jnp.where(count_ge(cand_u ^ imin) >= k, cand_u, ans)

    ans = lax.fori_loop(0, 32, body, jnp.zeros(shape, I32))
    return ans ^ imin


def _adaln_kernel(c_ref, w_ref, b_ref, o_ref):
    c = c_ref[...]
    a = (c * jax.nn.sigmoid(c)).astype(BF16)
    o_ref[...] = jnp.dot(a, w_ref[...].astype(BF16), preferred_element_type=F32) + b_ref[...]


def _adaln(c, w_mod, b_mod):
    n, d = c.shape
    npad = -(-n // SUBLANES) * SUBLANES
    cp = jnp.pad(c, ((0, npad - n), (0, 0)))
    nout = w_mod.shape[1]
    tn = PROJ_TN
    out = pl.pallas_call(
        _adaln_kernel,
        grid=(nout // tn,),
        in_specs=[pl.BlockSpec((npad, d), lambda j: (0, 0)),
                  pl.BlockSpec((d, tn), lambda j: (0, j)),
                  pl.BlockSpec((1, tn), lambda j: (0, j))],
        out_specs=pl.BlockSpec((npad, tn), lambda j: (0, j)),
        out_shape=jax.ShapeDtypeStruct((npad, nout), F32),
        compiler_params=_cparams(("arbitrary",)),
        name="adaln",
    )(cp, w_mod, b_mod.reshape(1, nout))
    return out[:n]


def _nmm_kernel(*refs, rope_ranges, tail_tile, emit_h, n_ssd_heads):
    if tail_tile is not None:
        x_ref, g_ref, sh_ref, sc_ref, w_ref, cos_ref, sin_ref, kg_ref = refs[:8]
        rest = refs[8:]
    else:
        x_ref, g_ref, sh_ref, sc_ref, w_ref = refs[:5]
        rest = refs[5:]
    if emit_h:
        o_ref, h_ref, h_scr = rest
    else:
        o_ref, h_scr = rest
    j = pl.program_id(1)

    @pl.when(j == 0)
    def _():
        h = _rms(x_ref[...], g_ref[...]) * (1.0 + sc_ref[...]) + sh_ref[...]
        h_scr[...] = h.astype(BF16)
        if emit_h:
            h_ref[...] = h.astype(BF16)

    acc = jnp.dot(h_scr[...], w_ref[...], preferred_element_type=F32)
    if tail_tile is None:
        o_ref[...] = acc
        return

    is_rope = functools.reduce(jnp.logical_or, [(j >= lo) & (j < hi) for lo, hi in rope_ranges])
    is_tail = j == tail_tile
    tn = acc.shape[1]

    @pl.when(jnp.logical_not(is_rope | is_tail))
    def _():
        o_ref[...] = acc

    @pl.when(is_rope)
    def _():
        cos = cos_ref[...]
        sin = sin_ref[...]
        for t in range(tn // HEAD_DIM):
            sl = slice(t * HEAD_DIM, (t + 1) * HEAD_DIM)
            o_ref[:, sl] = _rope_tile(acc[:, sl], cos, sin)

    @pl.when(is_tail)
    def _():
        ki = _rms(acc[:, :IDX_DIM], kg_ref[...])
        o_ref[:, :IDX_DIM] = _rope_tile(ki, cos_ref[...], sin_ref[...])
        o_ref[:, IDX_DIM:] = acc[:, IDX_DIM:]


def _norm_mod_matmul(x, g, shift, scale, w, *, tm, per_token, rows_per_batch, rope=None, emit_h=False):
    m, d = x.shape
    n = w.shape[1]
    tn = PROJ_TN
    assert m % tm == 0 and n % tn == 0
    if per_token:
        mod_spec = pl.BlockSpec((tm, d), lambda i, j: (i, 0))
    else:
        tiles_per_batch = rows_per_batch // tm
        mod_spec = pl.BlockSpec((None, 1, d), lambda i, j: (i // tiles_per_batch, 0, 0))
    in_specs = [pl.BlockSpec((tm, d), lambda i, j: (i, 0)),
                pl.BlockSpec((1, d), lambda i, j: (0, 0)),
                mod_spec, mod_spec,
                pl.BlockSpec((d, tn), lambda i, j: (0, j))]
    args = [x, g.reshape(1, d), shift, scale, w]
    rope_ranges, tail_tile, n_ssd_heads = (), None, 0
    if rope is not None:
        cos, sin, kidx_g, rope_ranges, tail_tile, n_ssd_heads = rope
        in_specs += [pl.BlockSpec((tm, HEAD_DIM), lambda i, j: (i, 0)),
                     pl.BlockSpec((tm, HEAD_DIM), lambda i, j: (i, 0)),
                     pl.BlockSpec((1, IDX_DIM), lambda i, j: (0, 0))]
        args += [cos, sin, kidx_g.reshape(1, IDX_DIM)]
    out_specs = [pl.BlockSpec((tm, tn), lambda i, j: (i, j))]
    out_shape = [jax.ShapeDtypeStruct((m, n), F32)]
    if emit_h:
        out_specs.append(pl.BlockSpec((tm, d), lambda i, j: (i, 0)))
        out_shape.append(jax.ShapeDtypeStruct((m, d), BF16))
    res = pl.pallas_call(
        functools.partial(_nmm_kernel, rope_ranges=rope_ranges, tail_tile=tail_tile, emit_h=emit_h,
                          n_ssd_heads=n_ssd_heads),
        grid=(m // tm, n // tn),
        in_specs=in_specs,
        out_specs=out_specs,
        out_shape=out_shape,
        scratch_shapes=[pltpu.VMEM((tm, d), BF16)],
        compiler_params=_cparams(("arbitrary", "arbitrary")),
        name="norm_mod_matmul",
    )(*args)
    return res if emit_h else res[0]


def _ssd_kernel(x_ref, prev_ref, hist_ref, cw_ref, cb_ref, dt_ref, dtb_ref, alog_ref, dsk_ref, h0_ref,
                y_ref, hout_ref, h_scr, *, cs, rpg, valid_len, nc):
    c = pl.program_id(2)
    xs_w = rpg * SSD_HEADDIM

    @pl.when(c == 0)
    def _():
        h_scr[...] = h0_ref[...]

    prev = jnp.where(c == 0, hist_ref[...], prev_ref[...])
    cat = jnp.concatenate([prev, x_ref[...]], axis=0)
    w = cw_ref[...]
    acc = cb_ref[...]
    for j in range(CONV_W):
        lo = SUBLANES - (CONV_W - 1) + j
        acc = acc + cat[lo:lo + cs] * w[j:j + 1]
    xc = acc * jax.nn.sigmoid(acc)
    xs = xc[:, :xs_w]
    bm = xc[:, xs_w:xs_w + D_STATE]
    cm = xc[:, xs_w + D_STATE:]
    bm_b = bm.astype(BF16)
    cm_b = cm.astype(BF16)

    z = dt_ref[...] + dtb_ref[...]
    dt = jnp.maximum(z, 0.0) + jnp.log1p(jnp.exp(-jnp.abs(z)))
    if valid_len < cs:
        row = lax.broadcasted_iota(I32, dt.shape, 0)
        dt = jnp.where(row < valid_len, dt, 0.0)
    a_neg = -jnp.exp(alog_ref[...])
    ii = lax.broadcasted_iota(I32, (cs, cs), 0)
    jj = lax.broadcasted_iota(I32, (cs, cs), 1)
    causal = ii >= jj
    acs = jnp.dot(causal.astype(F32), dt * a_neg, preferred_element_type=F32,
                  precision=lax.Precision.HIGHEST)
    acs_t = acs.T
    a_last = acs[cs - 1:cs, :]
    cb = _nt(cm_b, bm_b)
    dsk = dsk_ref[...]

    ys = []
    for r in range(rpg):
        acol = acs[:, r:r + 1]
        arow = acs_t[r:r + 1, :]
        lm = jnp.exp(jnp.where(causal, acol - arow, -jnp.inf))
        xr = xs[:, r * SSD_HEADDIM:(r + 1) * SSD_HEADDIM]
        xd = xr * dt[:, r:r + 1]
        y = jnp.dot((cb * lm).astype(BF16), xd.astype(BF16), preferred_element_type=F32)
        hr = h_scr[r]
        y = y + jnp.exp(acol) * _nt(cm_b, hr.astype(BF16))
        al = a_last[:, r:r + 1]
        decay = jnp.exp(al - acol)
        h_scr[r] = hr * jnp.exp(al) + _tn((xd * decay).astype(BF16), bm_b)
        ys.append(y + xr * dsk[:, r:r + 1])
    y_ref[...] = jnp.concatenate(ys, axis=1)

    @pl.when(c == nc - 1)
    def _():
        hout_ref[...] = h_scr[...]


def _ssd(xg, xoff_blocks, hist8, conv_w_g, conv_b_g, dt_g, dtb_g, alog_g, dsk_g, h0, *, nb, nc, cs, d, valid_len):
    g_cnt = SSD_GROUPS
    rpg = d // SSD_HEADDIM // g_cnt
    gw = rpg * SSD_HEADDIM + 2 * D_STATE
    m = nb * nc * cs
    cpb = cs // SUBLANES
    kern = functools.partial(_ssd_kernel, cs=cs, rpg=rpg, valid_len=valid_len, nc=nc)
    y, hout = pl.pallas_call(
        kern,
        grid=(nb, g_cnt, nc),
        in_specs=[
            pl.BlockSpec((cs, gw), lambda b, g, c: (b * nc + c, xoff_blocks + g)),
            pl.BlockSpec((SUBLANES, gw), lambda b, g, c: (jnp.maximum((b * nc + c) * cpb - 1, 0), xoff_blocks + g)),
            pl.BlockSpec((None, SUBLANES, gw), lambda b, g, c: (b, 0, g)),
            pl.BlockSpec((CONV_W, gw), lambda b, g, c: (0, g)),
            pl.BlockSpec((1, gw), lambda b, g, c: (0, g)),
            pl.BlockSpec((None, cs, LANES), lambda b, g, c: (g, b * nc + c, 0)),
            pl.BlockSpec((None, 1, LANES), lambda b, g, c: (g, 0, 0)),
            pl.BlockSpec((None, 1, LANES), lambda b, g, c: (g, 0, 0)),
            pl.BlockSpec((None, 1, LANES), lambda b, g, c: (g, 0, 0)),
            pl.BlockSpec((None, rpg, SSD_HEADDIM, D_STATE), lambda b, g, c: (b, g, 0, 0)),
        ],
        out_specs=[
            pl.BlockSpec((cs, rpg * SSD_HEADDIM), lambda b, g, c: (b * nc + c, g)),
            pl.BlockSpec((None, rpg, SSD_HEADDIM, D_STATE), lambda b, g, c: (b, g, 0, 0)),
        ],
        out_shape=[jax.ShapeDtypeStruct((m, d), F32),
                   jax.ShapeDtypeStruct((nb, d // SSD_HEADDIM, SSD_HEADDIM, D_STATE), F32)],
        scratch_shapes=[pltpu.VMEM((rpg, SSD_HEADDIM, D_STATE), F32)],
        compiler_params=_cparams(("arbitrary", "arbitrary", "arbitrary")),
        name="ssd_scan",
    )(xg, xg, hist8, conv_w_g, conv_b_g, dt_g, dtb_g, alog_g, dsk_g, h0)
    return y, hout


def _gate_norm_kernel(y_ref, z_ref, g_ref, o_ref):
    z = z_ref[...]
    o_ref[...] = _rms(y_ref[...] * (z * jax.nn.sigmoid(z)), g_ref[...]).astype(BF16)


def _gate_norm(y, proj, g, *, tm):
    m, d = y.shape
    return pl.pallas_call(
        _gate_norm_kernel,
        grid=(m // tm,),
        in_specs=[pl.BlockSpec((tm, d), lambda i: (i, 0)),
                  pl.BlockSpec((tm, d), lambda i: (i, 0)),
                  pl.BlockSpec((1, d), lambda i: (0, 0))],
        out_specs=pl.BlockSpec((tm, d), lambda i: (i, 0)),
        out_shape=jax.ShapeDtypeStruct((m, d), BF16),
        compiler_params=_cparams(("arbitrary",)),
        name="gate_norm",
    )(y, proj, g.reshape(1, d))


def _attn_prompt_kernel(qi_ref, tail_ref, ki_ref, q_ref, k_ref, v_ref, o_ref, key_scr, thr_scr,
                        *, topk, tk, n_ssd_heads, rq):
    qb = pl.program_id(1)
    g = pl.program_id(2)
    nkt = (qb * Q_BLOCK + Q_BLOCK + tk - 1) // tk
    imin = jnp.int32(INT_MIN)

    @pl.when(g == 0)
    def _():
        qi = qi_ref[...]
        qis = jnp.concatenate([qi[:, h * IDX_DIM:(h + 1) * IDX_DIM] for h in range(IDX_HEADS)],
                              axis=0).astype(BF16)
        w_t = tail_ref[...].T[n_ssd_heads:n_ssd_heads + IDX_HEADS, :] * (IDX_HEADS ** -0.5 * IDX_DIM ** -0.5)
        tpos = qb * Q_BLOCK + lax.broadcasted_iota(I32, (tk, Q_BLOCK), 1)
        kiota = lax.broadcasted_iota(I32, (tk, Q_BLOCK), 0)

        def score_tile(kt, carry):
            rows = pl.ds(pl.multiple_of(kt * tk, tk), tk)
            x = _nt(ki_ref[rows, :].astype(BF16), qis)
            sc = jnp.zeros((tk, Q_BLOCK), F32)
            for h in range(IDX_HEADS):
                sc = sc + jnp.maximum(x[:, h * Q_BLOCK:(h + 1) * Q_BLOCK], 0.0) * w_t[h:h + 1, :]
            key = _float_key(sc + 0.0)
            key_scr[rows, :] = jnp.where(kt * tk + kiota <= tpos, key, imin)
            return carry

        lax.fori_loop(0, nkt, score_tile, 0)

        def count_ge(thr):
            def body(kt, cnt):
                rows = pl.ds(pl.multiple_of(kt * tk, tk), tk)
                ge = (key_scr[rows, :] >= thr).astype(I32)
                return cnt + jnp.sum(ge.reshape(tk // SUBLANES, SUBLANES, Q_BLOCK), axis=0)
            cnt = lax.fori_loop(0, nkt, body, jnp.zeros((SUBLANES, Q_BLOCK), I32))
            return jnp.sum(cnt, axis=0, keepdims=True)

        thr = _bisect_kth(count_ge, topk, (1, Q_BLOCK))
        thr_scr[...] = jnp.broadcast_to(thr, (SUBLANES, Q_BLOCK))

    q = q_ref[...]
    qs = (jnp.concatenate([q[:, r * HEAD_DIM:(r + 1) * HEAD_DIM] for r in range(rq)], axis=0)
          * (HEAD_DIM ** -0.5)).astype(BF16)
    thr_q = jnp.concatenate([thr_scr[0:1, :]] * rq, axis=1)
    nq = rq * Q_BLOCK

    def attend(kt, carry):
        m, l, acc = carry
        rows = pl.ds(pl.multiple_of(kt * tk, tk), tk)
        s_t = _nt(k_ref[rows, :].astype(BF16), qs)
        key = jnp.concatenate([key_scr[rows, :]] * rq, axis=1)
        sel = (key >= thr_q) & (key != imin)
        m_new = jnp.maximum(m, jnp.max(jnp.where(sel, s_t, NEG_BIG), axis=0, keepdims=True))
        alpha = jnp.exp(m - m_new)
        p = jnp.where(sel, jnp.exp(s_t - m_new), 0.0)
        l = l * alpha + jnp.sum(p, axis=0, keepdims=True)
        v_t = v_ref[rows, :].T.astype(BF16)
        acc = acc * alpha + jnp.dot(v_t, p.astype(BF16), preferred_element_type=F32)
        return m_new, l, acc

    m0 = jnp.full((1, nq), NEG_BIG, F32)
    l0 = jnp.zeros((1, nq), F32)
    a0 = jnp.zeros((HEAD_DIM, nq), F32)
    _, l, acc = lax.fori_loop(0, nkt, attend, (m0, l0, a0))
    o = (acc / l).T
    for r in range(rq):
        o_ref[:, r * HEAD_DIM:(r + 1) * HEAD_DIM] = o[r * Q_BLOCK:(r + 1) * Q_BLOCK, :]


def _attn_prompt(proj, lay, *, nb, seq, d):
    rq = d // HEAD_DIM // KV_HEADS
    nqb = seq // Q_BLOCK
    topk = min(IDX_TOPK, seq // 4)
    tk = min(ATTN_TK, seq)
    qiw = IDX_HEADS * IDX_DIM
    qw = rq * HEAD_DIM
    kern = functools.partial(_attn_prompt_kernel, topk=topk, tk=tk, n_ssd_heads=d // SSD_HEADDIM, rq=rq)
    return pl.pallas_call(
        kern,
        grid=(nb, nqb, KV_HEADS),
        in_specs=[
            pl.BlockSpec((Q_BLOCK, qiw), lambda b, i, g: (b * nqb + i, lay["qi"] // qiw)),
            pl.BlockSpec((Q_BLOCK, LANES), lambda b, i, g: (b * nqb + i, lay["tail"] // LANES + 1)),
            pl.BlockSpec((seq, IDX_DIM), lambda b, i, g: (b, lay["tail"] // IDX_DIM)),
            pl.BlockSpec((Q_BLOCK, qw), lambda b, i, g: (b * nqb + i, lay["q"] // qw + g)),
            pl.BlockSpec((seq, HEAD_DIM), lambda b, i, g: (b, lay["k"] // HEAD_DIM + g)),
            pl.BlockSpec((seq, HEAD_DIM), lambda b, i, g: (b, lay["v"] // HEAD_DIM + g)),
        ],
        out_specs=pl.BlockSpec((Q_BLOCK, qw), lambda b, i, g: (b * nqb + i, g)),
        out_shape=jax.ShapeDtypeStruct((nb * seq, d), F32),
        scratch_shapes=[pltpu.VMEM((seq, Q_BLOCK), I32), pltpu.VMEM((SUBLANES, Q_BLOCK), I32)],
        compiler_params=_cparams(("arbitrary", "arbitrary", "arbitrary")),
        name="attn_prompt",
    )(proj, proj, proj, proj, proj, proj)


def _sample_index_kernel(pt_ref, *refs, npages, topk, pps):
    kp_refs = refs[:pps]
    qi_ref, wb_ref, kinew_ref, mask_ref, key_scr = refs[pps:]
    j = pl.program_id(1)
    imin = jnp.int32(INT_MIN)
    qi = qi_ref[...].astype(BF16)
    wb = wb_ref[...]

    def score_keys(kpage):
        x = _nt(qi, kpage.astype(BF16))
        r = jnp.maximum(x, 0.0) * wb
        sc = jnp.sum(r.reshape(IDX_HEADS, SAMPLE_PAD, PAGE_SIZE), axis=0)
        return _float_key(sc + 0.0)

    for i in range(pps):
        key_scr[j * pps + i] = score_keys(kp_refs[i][...])

    @pl.when(j == 0)
    def _():
        key = score_keys(kinew_ref[...])
        t = lax.broadcasted_iota(I32, key.shape, 0)
        s = lax.broadcasted_iota(I32, key.shape, 1)
        key_scr[npages] = jnp.where(s <= t, key, imin)

    @pl.when(j == pl.num_programs(1) - 1)
    def _():
        keys = key_scr[...]

        def count_ge(thr):
            cnt = jnp.sum((keys >= thr).astype(I32), axis=0)
            return jnp.sum(cnt, axis=1, keepdims=True)

        thr = _bisect_kth(count_ge, topk, (SAMPLE_PAD, 1))
        mask_ref[...] = ((keys >= thr) & (keys != imin)).astype(F32)


def _sample_attn_kernel(pt_ref, *refs, pps, rq):
    kp_refs = refs[:pps]
    vp_refs = refs[pps:2 * pps]
    mask_ref, masknew_ref, q_ref, knew_ref, vnew_ref, o_ref, m_scr, l_scr, acc_scr = refs[2 * pps:]
    j = pl.program_id(1)
    nrow = rq * SAMPLE_PAD

    def process(kpage, vpage, mask8):
        sel = jnp.concatenate([mask8] * rq, axis=0) > 0.0
        for g in range(KV_HEADS):
            cols = slice(g * HEAD_DIM, (g + 1) * HEAD_DIM)
            qg = (q_ref[g] * (HEAD_DIM ** -0.5)).astype(BF16)
            s = _nt(qg, kpage[:, cols].astype(BF16))
            m_prev = m_scr[g]
            m_new = jnp.maximum(m_prev, jnp.max(jnp.where(sel, s, NEG_BIG), axis=1, keepdims=True))
            alpha = jnp.exp(m_prev - m_new)
            p = jnp.where(sel, jnp.exp(s - m_new), 0.0)
            l_scr[g] = l_scr[g] * alpha + jnp.sum(p, axis=1, keepdims=True)
            acc_scr[g] = acc_scr[g] * alpha + jnp.dot(p.astype(BF16), vpage[:, cols].astype(BF16),
                                                      preferred_element_type=F32)
            m_scr[g] = m_new

    @pl.when(j == 0)
    def _():
        m_scr[...] = jnp.full(m_scr.shape, NEG_BIG, F32)
        l_scr[...] = jnp.zeros(l_scr.shape, F32)
        acc_scr[...] = jnp.zeros(acc_scr.shape, F32)
        process(knew_ref[...], vnew_ref[...], masknew_ref[0])

    for i in range(pps):
        process(kp_refs[i][...], vp_refs[i][...], mask_ref[i])

    @pl.when(j == pl.num_programs(1) - 1)
    def _():
        o_ref[...] = acc_scr[...] / l_scr[...]


def _attn_sample(q, qi, wi, ki_new, k_new, v_new, cache_k, cache_v, cache_kidx, page_table, *, t_valid):
    nb, npages = page_table.shape
    pps = min(PAGES_PER_STEP, npages)
    assert npages % pps == 0
    nsteps = npages // pps
    past = npages * PAGE_SIZE
    topk = min(IDX_TOPK, (past + t_valid) // 4)
    d = q.shape[-1]
    rq = d // HEAD_DIM // KV_HEADS
    kvw = KV_HEADS * HEAD_DIM
    ck = cache_k.reshape(-1, PAGE_SIZE, kvw)
    cv = cache_v.reshape(-1, PAGE_SIZE, kvw)
    cki = cache_kidx.reshape(-1, PAGE_SIZE, IDX_DIM)

    qi_s = qi.reshape(nb, SAMPLE_PAD, IDX_HEADS, IDX_DIM).transpose(0, 2, 1, 3).reshape(nb, IDX_HEADS * SAMPLE_PAD, IDX_DIM)
    w_s = (wi * (IDX_HEADS ** -0.5 * IDX_DIM ** -0.5)).transpose(0, 2, 1).reshape(nb, IDX_HEADS * SAMPLE_PAD, 1)
    w_b = jnp.broadcast_to(w_s, (nb, IDX_HEADS * SAMPLE_PAD, PAGE_SIZE))
    pad_rows = ((0, 0), (0, PAGE_SIZE - SAMPLE_PAD), (0, 0))
    kinew_p = jnp.pad(ki_new, pad_rows)
    knew_p = jnp.pad(k_new, pad_rows)
    vnew_p = jnp.pad(v_new, pad_rows)
    q_g = q.reshape(nb, SAMPLE_PAD, KV_HEADS, rq, HEAD_DIM).transpose(0, 2, 3, 1, 4).reshape(
        nb, KV_HEADS, rq * SAMPLE_PAD, HEAD_DIM)

    def page_spec(i, width):
        return pl.BlockSpec((None, PAGE_SIZE, width), lambda b, j, pt, i=i: (pt[b, j * pps + i], 0, 0))

    mask = pl.pallas_call(
        functools.partial(_sample_index_kernel, npages=npages, topk=topk, pps=pps),
        grid_spec=pltpu.PrefetchScalarGridSpec(
            num_scalar_prefetch=1,
            grid=(nb, nsteps),
            in_specs=[page_spec(i, IDX_DIM) for i in range(pps)] + [
                pl.BlockSpec((None, IDX_HEADS * SAMPLE_PAD, IDX_DIM), lambda b, j, pt: (b, 0, 0)),
                pl.BlockSpec((None, IDX_HEADS * SAMPLE_PAD, PAGE_SIZE), lambda b, j, pt: (b, 0, 0)),
                pl.BlockSpec((None, PAGE_SIZE, IDX_DIM), lambda b, j, pt: (b, 0, 0)),
            ],
            out_specs=pl.BlockSpec((None, npages + 1, SAMPLE_PAD, PAGE_SIZE), lambda b, j, pt: (b, 0, 0, 0)),
            scratch_shapes=[pltpu.VMEM((npages + 1, SAMPLE_PAD, PAGE_SIZE), I32)],
        ),
        out_shape=jax.ShapeDtypeStruct((nb, npages + 1, SAMPLE_PAD, PAGE_SIZE), F32),
        compiler_params=_cparams(("arbitrary", "arbitrary")),
        name="sample_index",
    )(page_table, *([cki] * pps), qi_s, w_b, kinew_p)

    nrow = rq * SAMPLE_PAD
    o = pl.pallas_call(
        functools.partial(_sample_attn_kernel, pps=pps, rq=rq),
        grid_spec=pltpu.PrefetchScalarGridSpec(
            num_scalar_prefetch=1,
            grid=(nb, nsteps),
            in_specs=[page_spec(i, kvw) for i in range(pps)] + [page_spec(i, kvw) for i in range(pps)] + [
                pl.BlockSpec((None, pps, SAMPLE_PAD, PAGE_SIZE), lambda b, j, pt: (b, j, 0, 0)),
                pl.BlockSpec((None, 1, SAMPLE_PAD, PAGE_SIZE), lambda b, j, pt: (b, npages, 0, 0)),
                pl.BlockSpec((None, KV_HEADS, nrow, HEAD_DIM), lambda b, j, pt: (b, 0, 0, 0)),
                pl.BlockSpec((None, PAGE_SIZE, kvw), lambda b, j, pt: (b, 0, 0)),
                pl.BlockSpec((None, PAGE_SIZE, kvw), lambda b, j, pt: (b, 0, 0)),
            ],
            out_specs=pl.BlockSpec((None, KV_HEADS, nrow, HEAD_DIM), lambda b, j, pt: (b, 0, 0, 0)),
            scratch_shapes=[pltpu.VMEM((KV_HEADS, nrow, 1), F32), pltpu.VMEM((KV_HEADS, nrow, 1), F32),
                            pltpu.VMEM((KV_HEADS, nrow, HEAD_DIM), F32)],
        ),
        out_shape=jax.ShapeDtypeStruct((nb, KV_HEADS, nrow, HEAD_DIM), F32),
        compiler_params=_cparams(("arbitrary", "arbitrary")),
        name="sample_attn",
    )(page_table, *([ck] * pps), *([cv] * pps), mask, mask, q_g, knew_p, vnew_p)
    return o.reshape(nb, KV_HEADS, rq, SAMPLE_PAD, HEAD_DIM).transpose(0, 3, 1, 2, 4).reshape(nb * SAMPLE_PAD, d)


def _outproj_kernel(y_ref, a_ref, ag_ref, w1_ref, w2_ref, x_ref, gate_ref, o_ref, an_scr):
    @pl.when(pl.program_id(1) == 0)
    def _():
        an_scr[...] = _rms(a_ref[...], ag_ref[...]).astype(BF16)

    acc = jnp.dot(y_ref[...], w1_ref[...], preferred_element_type=F32)
    acc = acc + jnp.dot(an_scr[...], w2_ref[...], preferred_element_type=F32)
    o_ref[...] = x_ref[...] + gate_ref[...] * acc


def _outproj(y_n, attn, attn_g, w1, w2, x, gate, *, tm, per_token, rows_per_batch):
    m, d = x.shape
    tn = PROJ_TN
    if per_token:
        gate_spec = pl.BlockSpec((tm, tn), lambda i, j: (i, j))
    else:
        tpb = rows_per_batch // tm
        gate_spec = pl.BlockSpec((None, 1, tn), lambda i, j: (i // tpb, 0, j))
    return pl.pallas_call(
        _outproj_kernel,
        grid=(m // tm, d // tn),
        in_specs=[pl.BlockSpec((tm, d), lambda i, j: (i, 0)),
                  pl.BlockSpec((tm, d), lambda i, j: (i, 0)),
                  pl.BlockSpec((1, d), lambda i, j: (0, 0)),
                  pl.BlockSpec((d, tn), lambda i, j: (0, j)),
                  pl.BlockSpec((d, tn), lambda i, j: (0, j)),
                  pl.BlockSpec((tm, tn), lambda i, j: (i, j)),
                  gate_spec],
        out_specs=pl.BlockSpec((tm, tn), lambda i, j: (i, j)),
        out_shape=jax.ShapeDtypeStruct((m, d), F32),
        scratch_shapes=[pltpu.VMEM((tm, d), BF16)],
        compiler_params=_cparams(("arbitrary", "arbitrary")),
        name="outproj",
    )(y_n, attn, attn_g.reshape(1, d), w1, w2, x, gate)


def _peer_route_kernel(q_ref, k1_ref, k2_ref, s1_ref, s2_ref, tau_ref, cc_ref):
    q = q_ref[...]
    k1 = k1_ref[...].astype(BF16)
    k2 = k2_ref[...].astype(BF16)
    half = PEER_D_KEY // 2
    tm = q.shape[0]
    taus, ccs = [], []
    for h in range(PEER_HEADS):
        base = h * PEER_D_KEY
        s1 = _nt(k1, q[:, base:base + half].astype(BF16))
        s2 = _nt(k2, q[:, base + half:base + PEER_D_KEY].astype(BF16))
        s1_ref[h * PEER_KEYS:(h + 1) * PEER_KEYS, :] = s1
        s2_ref[h * PEER_KEYS:(h + 1) * PEER_KEYS, :] = s2

        def top_vals(x):
            vals = []
            for _ in range(PEER_TOPK):
                m = jnp.max(x, axis=0, keepdims=True)
                vals.append(m)
                x = jnp.where(x == m, -jnp.inf, x)
            return vals

        v1 = top_vals(s1)
        v2 = jnp.concatenate(top_vals(s2), axis=0)
        cand = jnp.concatenate([v + v2 for v in v1], axis=0) + 0.0
        ckey = _float_key(cand)

        def count_ge(thr, ckey=ckey):
            return jnp.sum((ckey >= thr).astype(I32), axis=0, keepdims=True)

        tau = _key_float(_bisect_kth(count_ge, PEER_TOPK, (1, tm)))
        cmax = cand[0:1, :]
        zsum = jnp.sum(jnp.where(cand >= tau, jnp.exp(cand - cmax), 0.0), axis=0, keepdims=True)
        taus.append(tau)
        ccs.append(cmax + jnp.log(zsum))
    tau_ref[...] = jnp.concatenate(taus, axis=0)
    cc_ref[...] = jnp.concatenate(ccs, axis=0)


def _peer_route(q, k1, k2, *, tm):
    t = q.shape[0]
    rows = PEER_HEADS * PEER_KEYS
    half = PEER_D_KEY // 2
    return pl.pallas_call(
        _peer_route_kernel,
        grid=(t // tm,),
        in_specs=[pl.BlockSpec((tm, PEER_HEADS * PEER_D_KEY), lambda i: (i, 0)),
                  pl.BlockSpec((PEER_KEYS, half), lambda i: (0, 0)),
                  pl.BlockSpec((PEER_KEYS, half), lambda i: (0, 0))],
        out_specs=[pl.BlockSpec((rows, tm), lambda i: (0, i)),
                   pl.BlockSpec((rows, tm), lambda i: (0, i)),
                   pl.BlockSpec((PEER_HEADS, tm), lambda i: (0, i)),
                   pl.BlockSpec((PEER_HEADS, tm), lambda i: (0, i))],
        out_shape=[jax.ShapeDtypeStruct((rows, t), F32), jax.ShapeDtypeStruct((rows, t), F32),
                   jax.ShapeDtypeStruct((PEER_HEADS, t), F32), jax.ShapeDtypeStruct((PEER_HEADS, t), F32)],
        compiler_params=_cparams(("arbitrary",)),
        name="peer_route",
    )(q, k1, k2)


def _peer_dense_kernel(xb_ref, s1_ref, s2_ref, tau_ref, cc_ref, u_ref, v_ref, o_ref, *, te):
    e = pl.program_id(1)

    @pl.when(e == 0)
    def _():
        o_ref[...] = jnp.zeros(o_ref.shape, F32)

    ut = _nt(u_ref[...], xb_ref[...])
    nsub = te // PEER_KEYS
    coefs = []
    for i in range(nsub):
        i1 = e * nsub + i
        gate = jnp.zeros((PEER_KEYS, ut.shape[1]), F32)
        for h in range(PEER_HEADS):
            s1row = s1_ref[pl.ds(h * PEER_KEYS + i1, 1), :]
            sm = s1row + s2_ref[h * PEER_KEYS:(h + 1) * PEER_KEYS, :]
            gate = gate + jnp.where(sm >= tau_ref[h:h + 1, :], jnp.exp(sm - cc_ref[h:h + 1, :]), 0.0)
        coefs.append(gate * jax.nn.gelu(ut[i * PEER_KEYS:(i + 1) * PEER_KEYS, :]))
    coef = jnp.concatenate(coefs, axis=0).T.astype(BF16)
    o_ref[...] += jnp.dot(coef, v_ref[...], preferred_element_type=F32)


def _peer_dense(xb, s1, s2, tau, cc, u_b, v_b, *, tm):
    t, d = xb.shape
    ne = u_b.shape[0]
    te = min(PEER_TE, ne)
    rows = PEER_HEADS * PEER_KEYS
    return pl.pallas_call(
        functools.partial(_peer_dense_kernel, te=te),
        grid=(t // tm, ne // te),
        in_specs=[pl.BlockSpec((tm, d), lambda i, e: (i, 0)),
                  pl.BlockSpec((rows, tm), lambda i, e: (0, i)),
                  pl.BlockSpec((rows, tm), lambda i, e: (0, i)),
                  pl.BlockSpec((PEER_HEADS, tm), lambda i, e: (0, i)),
                  pl.BlockSpec((PEER_HEADS, tm), lambda i, e: (0, i)),
                  pl.BlockSpec((te, d), lambda i, e: (e, 0)),
                  pl.BlockSpec((te, d), lambda i, e: (e, 0))],
        out_specs=pl.BlockSpec((tm, d), lambda i, e: (i, 0)),
        out_shape=jax.ShapeDtypeStruct((t, d), F32),
        compiler_params=_cparams(("arbitrary", "arbitrary")),
        name="peer_dense",
    )(xb, s1, s2, tau, cc, u_b, v_b)


def _final_kernel(x_ref, f_ref, gate_ref, g_ref, o_ref):
    o_ref[...] = _rms(x_ref[...] + gate_ref[...] * f_ref[...], g_ref[...])


def _final(x, f, gate, g, *, tm, per_token, rows_per_batch):
    m, d = x.shape
    if per_token:
        gate_spec = pl.BlockSpec((tm, d), lambda i: (i, 0))
    else:
        tpb = rows_per_batch // tm
        gate_spec = pl.BlockSpec((None, 1, d), lambda i: (i // tpb, 0, 0))
    return pl.pallas_call(
        _final_kernel,
        grid=(m // tm,),
        in_specs=[pl.BlockSpec((tm, d), lambda i: (i, 0)),
                  pl.BlockSpec((tm, d), lambda i: (i, 0)),
                  gate_spec,
                  pl.BlockSpec((1, d), lambda i: (0, 0))],
        out_specs=pl.BlockSpec((tm, d), lambda i: (i, 0)),
        out_shape=jax.ShapeDtypeStruct((m, d), F32),
        compiler_params=_cparams(("arbitrary",)),
        name="final_norm",
    )(x, f, gate, g.reshape(1, d))


def _layout(d):
    nh = d // SSD_HEADDIM
    kvw = KV_HEADS * HEAD_DIM
    qiw = IDX_HEADS * IDX_DIM
    xbcw = d + 2 * SSD_GROUPS * D_STATE
    lay, off = {}, 0
    for name, w in (("z", d), ("q", d), ("qi", qiw), ("xbc", xbcw), ("k", kvw), ("v", kvw), ("tail", PROJ_TN)):
        assert w % PROJ_TN == 0
        lay[name] = off
        off += w
    lay["total"] = off
    assert IDX_DIM + nh + IDX_HEADS <= PROJ_TN
    return lay


def _group_perm(d):
    xsw = d // SSD_GROUPS
    idx = []
    for g in range(SSD_GROUPS):
        idx += list(range(g * xsw, (g + 1) * xsw))
        idx += list(range(d + g * D_STATE, d + (g + 1) * D_STATE))
        idx += list(range(d + SSD_GROUPS * D_STATE + g * D_STATE, d + SSD_GROUPS * D_STATE + (g + 1) * D_STATE))
    return np.asarray(idx, dtype=np.int32)


def _prep_w_in(w_in, d):
    nh = d // SSD_HEADDIM
    kvw = KV_HEADS * HEAD_DIM
    qiw = IDX_HEADS * IDX_DIM
    xbcw = d + 2 * SSD_GROUPS * D_STATE
    sizes = (d, xbcw, nh, d, kvw, kvw, qiw, IDX_HEADS, IDX_DIM)
    splits = np.cumsum(sizes)[:-1]
    z, xbc, dt, q, k, v, qi, wi, ki = jnp.split(w_in, splits, axis=1)
    xbc = xbc[:, _group_perm(d)]
    pad = jnp.zeros((d, PROJ_TN - IDX_DIM - nh - IDX_HEADS), w_in.dtype)
    return jnp.concatenate([z, q, qi, xbc, k, v, ki, dt, wi, pad], axis=1).astype(BF16)


def _group_major(vec, rpg):
    return jnp.pad(vec.reshape(SSD_GROUPS, 1, rpg), ((0, 0), (0, 0), (0, LANES - rpg)))


def _rope_tables(pos):
    half = HEAD_DIM // 8
    inv = ROPE_THETA ** (-jnp.arange(half, dtype=F32) / half)
    ang = pos.astype(F32)[:, None] * inv[None, :]
    cos, sin = jnp.cos(ang), jnp.sin(ang)
    n = pos.shape[0]
    rest = HEAD_DIM - 2 * half
    return (jnp.concatenate([cos, cos, jnp.ones((n, rest), F32)], axis=1),
            jnp.concatenate([-sin, sin, jnp.zeros((n, rest), F32)], axis=1))


def _layer(x2d, mods, pos, prm, *, nb, seq, per_token, tm, ssd_cfg, attn_fn, peer_tm):
    m, d = x2d.shape
    lay = prm["lay"]
    nh = d // SSD_HEADDIM
    rpg = nh // SSD_GROUPS
    sh1, sc1, g1, sh2, sc2, g2 = mods
    cos, sin = _rope_tables(pos)
    xoff = lay["xbc"]
    rope = (cos, sin, prm["kidx_norm_g"],
            ((lay["q"] // PROJ_TN, lay["xbc"] // PROJ_TN), (lay["k"] // PROJ_TN, lay["v"] // PROJ_TN)),
            lay["tail"] // PROJ_TN, nh)
    proj = _norm_mod_matmul(x2d, prm["norm1_g"], sh1, sc1, prm["w_in"], tm=tm, per_token=per_token,
                            rows_per_batch=seq, rope=rope)

    gw = rpg * SSD_HEADDIM + 2 * D_STATE
    dt_raw = proj[:, lay["tail"] + IDX_DIM: lay["tail"] + IDX_DIM + nh]
    y_ssd, ssm_new = ssd_cfg(proj, dt_raw, xoff // gw)
    tm_e = min(tm, ELEMWISE_TM)
    y_n = _gate_norm(y_ssd, proj, prm["ssd_norm_g"], tm=tm_e)

    o_attn = attn_fn(proj)

    x1 = _outproj(y_n, o_attn, prm["attn_norm_g"], prm["w_out1"], prm["w_out2"], x2d, g1, tm=tm,
                  per_token=per_token, rows_per_batch=seq)

    qp, xb = _norm_mod_matmul(x1, prm["norm2_g"], sh2, sc2, prm["peer_wq"], tm=tm, per_token=per_token,
                              rows_per_batch=seq, emit_h=True)
    s1, s2, tau, cc = _peer_route(qp, prm["peer_k1"], prm["peer_k2"], tm=min(peer_tm, 256))
    ffn = _peer_dense(xb, s1, s2, tau, cc, prm["peer_u"], prm["peer_v"], tm=peer_tm)
    y = _final(x1, ffn, g2, prm["final_norm_g"], tm=tm_e, per_token=per_token, rows_per_batch=seq)
    return y, proj, ssm_new


def kernel(x_prompt, x_sample, c_prompt, c_sample, cache_k, cache_v, cache_kidx, state_ssm, state_conv, page_table, w_mod, b_mod, norm1_g, w_in, conv_w, conv_b, dt_bias, a_log, d_skip, ssd_norm_g, kidx_norm_g, attn_norm_g, w_out, norm2_g, peer_wq, peer_k1, peer_k2, peer_u, peer_v, final_norm_g):
    nbp, seq, d = x_prompt.shape
    nbs, tdec, _ = x_sample.shape
    depth = w_mod.shape[0]
    assert depth == 1
    nh = d // SSD_HEADDIM
    rpg = nh // SSD_GROUPS
    gw = rpg * SSD_HEADDIM + 2 * D_STATE
    lay = _layout(d)
    assert lay["xbc"] % gw == 0
    kvw = KV_HEADS * HEAD_DIM
    xbcw = d + 2 * SSD_GROUPS * D_STATE
    perm = _group_perm(d)
    inv_perm = np.argsort(perm)

    prm = dict(
        lay=lay,
        norm1_g=norm1_g[0], kidx_norm_g=kidx_norm_g[0], ssd_norm_g=ssd_norm_g[0], attn_norm_g=attn_norm_g[0],
        norm2_g=norm2_g[0], final_norm_g=final_norm_g,
        w_in=_prep_w_in(w_in[0], d),
        w_out1=w_out[0, :d].astype(BF16), w_out2=w_out[0, d:].astype(BF16),
        peer_wq=peer_wq[0].astype(BF16), peer_k1=peer_k1[0], peer_k2=peer_k2[0],
        peer_u=peer_u[0].astype(BF16), peer_v=peer_v[0].astype(BF16),
    )
    conv_w_g = conv_w[0][:, perm]
    conv_b_g = conv_b[0][perm].reshape(1, xbcw)
    dtb_g = _group_major(dt_bias[0], rpg)
    alog_g = _group_major(a_log[0], rpg)
    dsk_g = _group_major(d_skip[0], rpg)

    mod = _adaln(jnp.concatenate([c_prompt, c_sample], axis=0), w_mod[0], b_mod[0])
    mod_p = [a.reshape(nbp, 1, d) for a in jnp.split(mod[:nbp], 6, axis=-1)]
    mod_s = [jnp.repeat(a, SAMPLE_PAD, axis=0) for a in jnp.split(mod[nbp:], 6, axis=-1)]

    def dt_group_major(dt_raw):
        rows = dt_raw.shape[0]
        dtg = dt_raw.reshape(rows, SSD_GROUPS, rpg).transpose(1, 0, 2)
        return jnp.pad(dtg, ((0, 0), (0, 0), (0, LANES - rpg)))

    cs_p = min(SSD_CHUNK, seq)
    nc_p = seq // cs_p
    assert seq % cs_p == 0 and seq % Q_BLOCK == 0

    def ssd_prompt(proj, dt_raw, xoff_blocks):
        hist = jnp.zeros((nbp, SUBLANES, xbcw), F32)
        h0 = jnp.zeros((nbp, nh, SSD_HEADDIM, D_STATE), F32)
        return _ssd(proj, xoff_blocks, hist, conv_w_g, conv_b_g, dt_group_major(dt_raw), dtb_g, alog_g, dsk_g, h0,
                    nb=nbp, nc=nc_p, cs=cs_p, d=d, valid_len=cs_p)

    tm_p = min(512, seq)
    pos_p = jnp.tile(jnp.arange(seq), nbp)
    y_p, proj_p, ssm_p = _layer(
        x_prompt.reshape(nbp * seq, d), mod_p, pos_p, prm, nb=nbp, seq=seq, per_token=False, tm=tm_p,
        ssd_cfg=ssd_prompt, attn_fn=functools.partial(_attn_prompt, lay=lay, nb=nbp, seq=seq, d=d),
        peer_tm=min(512, seq))

    npages = page_table.shape[1]
    past = npages * PAGE_SIZE
    ms = nbs * SAMPLE_PAD
    xs_pad = jnp.pad(x_sample, ((0, 0), (0, SAMPLE_PAD - tdec), (0, 0))).reshape(ms, d)
    pos_s = jnp.tile(past + jnp.arange(SAMPLE_PAD), nbs)

    def ssd_sample(proj, dt_raw, xoff_blocks):
        xbc = proj[:, lay["xbc"]:lay["xbc"] + xbcw].reshape(nbs, SAMPLE_PAD, xbcw)
        xbc = jnp.pad(xbc, ((0, 0), (0, SAMPLE_CHUNK - SAMPLE_PAD), (0, 0))).reshape(nbs * SAMPLE_CHUNK, xbcw)
        dtr = jnp.pad(dt_raw.reshape(nbs, SAMPLE_PAD, nh), ((0, 0), (0, SAMPLE_CHUNK - SAMPLE_PAD), (0, 0)))
        hist = jnp.pad(state_conv[0][:, :, perm], ((0, 0), (SUBLANES - (CONV_W - 1), 0), (0, 0)))
        y, hnew = _ssd(xbc, 0, hist, conv_w_g, conv_b_g, dt_group_major(dtr.reshape(nbs * SAMPLE_CHUNK, nh)),
                       dtb_g, alog_g, dsk_g, state_ssm[0], nb=nbs, nc=1, cs=SAMPLE_CHUNK, d=d, valid_len=tdec)
        y = y.reshape(nbs, SAMPLE_CHUNK, d)[:, :SAMPLE_PAD].reshape(ms, d)
        return y, hnew

    def attn_sample(proj):
        def seg(name, w):
            return proj[:, lay[name]:lay[name] + w].reshape(nbs, SAMPLE_PAD, w)
        tail = lay["tail"]
        ki_new = proj[:, tail:tail + IDX_DIM].reshape(nbs, SAMPLE_PAD, IDX_DIM)
        wi = proj[:, tail + IDX_DIM + nh: tail + IDX_DIM + nh + IDX_HEADS].reshape(nbs, SAMPLE_PAD, IDX_HEADS)
        return _attn_sample(seg("q", d), seg("qi", IDX_HEADS * IDX_DIM), wi, ki_new, seg("k", kvw), seg("v", kvw),
                            cache_k[0], cache_v[0], cache_kidx[0], page_table, t_valid=tdec)

    y_s, proj_s, ssm_s = _layer(
        xs_pad, mod_s, pos_s, prm, nb=nbs, seq=SAMPLE_PAD, per_token=True, tm=ms,
        ssd_cfg=ssd_sample, attn_fn=attn_sample, peer_tm=ms)

    def states(proj, nb, rows, valid):
        p3 = proj.reshape(nb, rows, lay["total"])[:, :valid]
        k = p3[:, :, lay["k"]:lay["k"] + kvw].reshape(1, nb, valid, KV_HEADS, HEAD_DIM)
        v = p3[:, :, lay["v"]:lay["v"] + kvw].reshape(1, nb, valid, KV_HEADS, HEAD_DIM)
        ki = p3[:, :, lay["tail"]:lay["tail"] + IDX_DIM].reshape(1, nb, valid, IDX_DIM)
        xbc = p3[:, valid - (CONV_W - 1):valid, lay["xbc"]:lay["xbc"] + xbcw][:, :, inv_perm]
        return k, v, ki, xbc.reshape(1, nb, CONV_W - 1, xbcw)

    k_p, v_p, ki_p, conv_p = states(proj_p, nbp, seq, seq)
    k_s, v_s, ki_s, conv_s = states(proj_s, nbs, SAMPLE_PAD, tdec)
    y_prompt = y_p.reshape(nbp, seq, d)
    y_sample = y_s.reshape(nbs, SAMPLE_PAD, d)[:, :tdec]
    return (y_prompt, y_sample, k_p, v_p, ki_p, ssm_p[None], conv_p,
            k_s, v_s, ki_s, ssm_s[None], conv_s)
```

```python
import functools
import math

import jax
import jax.numpy as jnp
import numpy as np
from jax import lax
from jax.experimental import pallas as pl
from jax.experimental.pallas import tpu as pltpu

F32 = jnp.float32
BF16 = jnp.bfloat16
I32 = jnp.int32

SSD_HEADDIM = 64
SSD_GROUPS = 8
D_STATE = 128
CONV_W = 4
SSD_CHUNK = 256
HEAD_DIM = 128
KV_HEADS = 8
ROPE_THETA = 500000.0
IDX_HEADS = 32
IDX_DIM = 128
IDX_TOPK = 256
Q_BLOCK = 128
PEER_HEADS = 8
PEER_KEYS = 128
PEER_TOPK = 16
PEER_D_KEY = 256
PAGE_SIZE = 128
EPS = 1e-6

LANES = 128
SUBLANES = 8
VMEM_LIMIT_BYTES = 56 * 1024 * 1024

PROJ_TN = 512
ELEMWISE_TM = 256
SAMPLE_PAD = 8
SAMPLE_CHUNK = 128
ATTN_TK = 512
ATTN_GROUPS_PER_STEP = 2
PAGES_PER_STEP = 8
PEER_TE = 512
NEG_BIG = -1e30
INT_MIN = -2 ** 31


def _cparams(sem):
    return pltpu.CompilerParams(dimension_semantics=sem, vmem_limit_bytes=VMEM_LIMIT_BYTES)


def _nt(a, b):
    return lax.dot_general(a, b, (((1,), (1,)), ((), ())), preferred_element_type=F32)


def _tn(a, b):
    return lax.dot_general(a, b, (((0,), (0,)), ((), ())), preferred_element_type=F32)


def _rms(x, g):
    return x * lax.rsqrt(jnp.mean(x * x, axis=-1, keepdims=True) + EPS) * g


def _rope_tile(a, cos, sin):
    half = HEAD_DIM // 8
    lane = lax.broadcasted_iota(I32, a.shape, 1)
    sw = jnp.where(lane < half, pltpu.roll(a, HEAD_DIM - half, 1), pltpu.roll(a, half, 1))
    return a * cos + sw * sin


def _float_key(x):
    bits = pltpu.bitcast(x, I32)
    return jnp.where(bits < 0, bits ^ jnp.int32(0x7FFFFFFF), bits)


def _key_float(key):
    bits = jnp.where(key < 0, key ^ jnp.int32(0x7FFFFFFF), key)
    return pltpu.bitcast(bits, F32)


def _bisect_kth(count_ge, k, shape):
    imin = jnp.int32(INT_MIN)

    def body(i, ans):
        cand_u = ans | lax.shift_left(jnp.int32(1), jnp.int32(31) - i)
        return jnp.where(count_ge(cand_u ^ imin) >= k, cand_u, ans)

    ans = lax.fori_loop(0, 32, body, jnp.zeros(shape, I32))
    return ans ^ imin


def _adaln_kernel(c_ref, w_ref, b_ref, o_ref):
    c = c_ref[...]
    a = (c * jax.nn.sigmoid(c)).astype(BF16)
    o_ref[...] = jnp.dot(a, w_ref[...].astype(BF16), preferred_element_type=F32) + b_ref[...]


def _adaln(c, w_mod, b_mod):
    n, d = c.shape
    npad = -(-n // SUBLANES) * SUBLANES
    cp = jnp.pad(c, ((0, npad - n), (0, 0)))
    nout = w_mod.shape[1]
    tn = PROJ_TN
    out = pl.pallas_call(
        _adaln_kernel,
        grid=(nout // tn,),
        in_specs=[pl.BlockSpec((npad, d), lambda j: (0, 0)),
                  pl.BlockSpec((d, tn), lambda j: (0, j)),
                  pl.BlockSpec((1, tn), lambda j: (0, j))],
        out_specs=pl.BlockSpec((npad, tn), lambda j: (0, j)),
        out_shape=jax.ShapeDtypeStruct((npad, nout), F32),
        compiler_params=_cparams(("arbitrary",)),
        name="adaln",
    )(cp, w_mod, b_mod.reshape(1, nout))
    return out[:n]


def _nmm_kernel(*refs, rope_ranges, tail_tile, emit_h, n_ssd_heads):
    if tail_tile is not None:
        x_ref, g_ref, sh_ref, sc_ref, w_ref, cos_ref, sin_ref, kg_ref = refs[:8]
        rest = refs[8:]
    else:
        x_ref, g_ref, sh_ref, sc_ref, w_ref = refs[:5]
        rest = refs[5:]
    if emit_h:
        o_ref, h_ref, h_scr = rest
    else:
        o_ref, h_scr = rest
    j = pl.program_id(1)

    @pl.when(j == 0)
    def _():
        h = _rms(x_ref[...], g_ref[...]) * (1.0 + sc_ref[...]) + sh_ref[...]
        h_scr[...] = h.astype(BF16)
        if emit_h:
            h_ref[...] = h.astype(BF16)

    def product():
        return jnp.dot(h_scr[...], w_ref[...], preferred_element_type=F32)

    if tail_tile is None:
        o_ref[...] = product()
        return

    is_rope = functools.reduce(jnp.logical_or, [(j >= lo) & (j < hi) for lo, hi in rope_ranges])
    is_tail = j == tail_tile
    tn = w_ref.shape[1]

    @pl.when(jnp.logical_not(is_rope | is_tail))
    def _():
        o_ref[...] = product()

    @pl.when(is_rope)
    def _():
        acc = product()
        cos = cos_ref[...]
        sin = sin_ref[...]
        for t in range(tn // HEAD_DIM):
            sl = slice(t * HEAD_DIM, (t + 1) * HEAD_DIM)
            o_ref[:, sl] = _rope_tile(acc[:, sl], cos, sin)

    @pl.when(is_tail)
    def _():
        acc = product()
        ki = _rms(acc[:, :IDX_DIM], kg_ref[...])
        o_ref[:, :IDX_DIM] = _rope_tile(ki, cos_ref[...], sin_ref[...])
        o_ref[:, IDX_DIM:] = acc[:, IDX_DIM:]


def _norm_mod_matmul(x, g, shift, scale, w, *, tm, per_token, rows_per_batch, rope=None, emit_h=False):
    m, d = x.shape
    n = w.shape[1]
    tn = PROJ_TN
    assert m % tm == 0 and n % tn == 0
    if per_token:
        mod_spec = pl.BlockSpec((tm, d), lambda i, j: (i, 0))
    else:
        tiles_per_batch = rows_per_batch // tm
        mod_spec = pl.BlockSpec((None, 1, d), lambda i, j: (i // tiles_per_batch, 0, 0))
    in_specs = [pl.BlockSpec((tm, d), lambda i, j: (i, 0)),
                pl.BlockSpec((1, d), lambda i, j: (0, 0)),
                mod_spec, mod_spec,
                pl.BlockSpec((d, tn), lambda i, j: (0, j))]
    args = [x, g.reshape(1, d), shift, scale, w]
    rope_ranges, tail_tile, n_ssd_heads = (), None, 0
    if rope is not None:
        cos, sin, kidx_g, rope_ranges, tail_tile, n_ssd_heads = rope
        in_specs += [pl.BlockSpec((tm, HEAD_DIM), lambda i, j: (i, 0)),
                     pl.BlockSpec((tm, HEAD_DIM), lambda i, j: (i, 0)),
                     pl.BlockSpec((1, IDX_DIM), lambda i, j: (0, 0))]
        args += [cos, sin, kidx_g.reshape(1, IDX_DIM)]
    out_specs = [pl.BlockSpec((tm, tn), lambda i, j: (i, j))]
    out_shape = [jax.ShapeDtypeStruct((m, n), F32)]
    if emit_h:
        out_specs.append(pl.BlockSpec((tm, d), lambda i, j: (i, 0)))
        out_shape.append(jax.ShapeDtypeStruct((m, d), BF16))
    res = pl.pallas_call(
        functools.partial(_nmm_kernel, rope_ranges=rope_ranges, tail_tile=tail_tile, emit_h=emit_h,
                          n_ssd_heads=n_ssd_heads),
        grid=(m // tm, n // tn),
        in_specs=in_specs,
        out_specs=out_specs,
        out_shape=out_shape,
        scratch_shapes=[pltpu.VMEM((tm, d), BF16)],
        compiler_params=_cparams(("arbitrary", "arbitrary")),
        name="norm_mod_matmul",
    )(*args)
    return res if emit_h else res[0]


def _ssd_kernel(x_ref, prev_ref, hist_ref, cw_ref, cb_ref, dt_ref, dtb_ref, alog_ref, dsk_ref, h0_ref,
                y_ref, hout_ref, h_scr, *, cs, rpg, valid_len, nc):
    c = pl.program_id(2)
    xs_w = rpg * SSD_HEADDIM

    @pl.when(c == 0)
    def _():
        h_scr[...] = h0_ref[...]

    prev = jnp.where(c == 0, hist_ref[...], prev_ref[...])
    cat = jnp.concatenate([prev, x_ref[...]], axis=0)
    w = cw_ref[...]
    acc = cb_ref[...]
    for j in range(CONV_W):
        lo = SUBLANES - (CONV_W - 1) + j
        acc = acc + cat[lo:lo + cs] * w[j:j + 1]
    xc = acc * jax.nn.sigmoid(acc)
    xs = xc[:, :xs_w]
    bm = xc[:, xs_w:xs_w + D_STATE]
    cm = xc[:, xs_w + D_STATE:]
    bm_b = bm.astype(BF16)
    cm_b = cm.astype(BF16)

    z = dt_ref[...] + dtb_ref[...]
    dt = jnp.maximum(z, 0.0) + jnp.log1p(jnp.exp(-jnp.abs(z)))
    if valid_len < cs:
        row = lax.broadcasted_iota(I32, dt.shape, 0)
        dt = jnp.where(row < valid_len, dt, 0.0)
    a_neg = -jnp.exp(alog_ref[...])
    ii = lax.broadcasted_iota(I32, (cs, cs), 0)
    jj = lax.broadcasted_iota(I32, (cs, cs), 1)
    causal = ii >= jj
    acs = jnp.dot(causal.astype(F32), dt * a_neg, preferred_element_type=F32,
                  precision=lax.Precision.HIGHEST)
    acs_t = acs.T
    a_last = acs[cs - 1:cs, :]
    cb = _nt(cm_b, bm_b)
    dsk = dsk_ref[...]

    ys = []
    for r in range(rpg):
        acol = acs[:, r:r + 1]
        arow = acs_t[r:r + 1, :]
        lm = jnp.exp(jnp.where(causal, acol - arow, -jnp.inf))
        xr = xs[:, r * SSD_HEADDIM:(r + 1) * SSD_HEADDIM]
        xd = xr * dt[:, r:r + 1]
        y = jnp.dot((cb * lm).astype(BF16), xd.astype(BF16), preferred_element_type=F32)
        hr = h_scr[r]
        y = y + jnp.exp(acol) * _nt(cm_b, hr.astype(BF16))
        al = a_last[:, r:r + 1]
        decay = jnp.exp(al - acol)
        h_scr[r] = hr * jnp.exp(al) + _tn((xd * decay).astype(BF16), bm_b)
        ys.append(y + xr * dsk[:, r:r + 1])
    y_ref[...] = jnp.concatenate(ys, axis=1)

    @pl.when(c == nc - 1)
    def _():
        hout_ref[...] = h_scr[...]


def _ssd(xg, xoff_blocks, hist8, conv_w_g, conv_b_g, dt_g, dtb_g, alog_g, dsk_g, h0, *, nb, nc, cs, d, valid_len):
    g_cnt = SSD_GROUPS
    rpg = d // SSD_HEADDIM // g_cnt
    gw = rpg * SSD_HEADDIM + 2 * D_STATE
    m = nb * nc * cs
    cpb = cs // SUBLANES
    kern = functools.partial(_ssd_kernel, cs=cs, rpg=rpg, valid_len=valid_len, nc=nc)
    y, hout = pl.pallas_call(
        kern,
        grid=(nb, g_cnt, nc),
        in_specs=[
            pl.BlockSpec((cs, gw), lambda b, g, c: (b * nc + c, xoff_blocks + g)),
            pl.BlockSpec((SUBLANES, gw), lambda b, g, c: (jnp.maximum((b * nc + c) * cpb - 1, 0), xoff_blocks + g)),
            pl.BlockSpec((None, SUBLANES, gw), lambda b, g, c: (b, 0, g)),
            pl.BlockSpec((CONV_W, gw), lambda b, g, c: (0, g)),
            pl.BlockSpec((1, gw), lambda b, g, c: (0, g)),
            pl.BlockSpec((None, cs, LANES), lambda b, g, c: (g, b * nc + c, 0)),
            pl.BlockSpec((None, 1, LANES), lambda b, g, c: (g, 0, 0)),
            pl.BlockSpec((None, 1, LANES), lambda b, g, c: (g, 0, 0)),
            pl.BlockSpec((None, 1, LANES), lambda b, g, c: (g, 0, 0)),
            pl.BlockSpec((None, rpg, SSD_HEADDIM, D_STATE), lambda b, g, c: (b, g, 0, 0)),
        ],
        out_specs=[
            pl.BlockSpec((cs, rpg * SSD_HEADDIM), lambda b, g, c: (b * nc + c, g)),
            pl.BlockSpec((None, rpg, SSD_HEADDIM, D_STATE), lambda b, g, c: (b, g, 0, 0)),
        ],
        out_shape=[jax.ShapeDtypeStruct((m, d), F32),
                   jax.ShapeDtypeStruct((nb, d // SSD_HEADDIM, SSD_HEADDIM, D_STATE), F32)],
        scratch_shapes=[pltpu.VMEM((rpg, SSD_HEADDIM, D_STATE), F32)],
        compiler_params=_cparams(("arbitrary", "arbitrary", "arbitrary")),
        name="ssd_scan",
    )(xg, xg, hist8, conv_w_g, conv_b_g, dt_g, dtb_g, alog_g, dsk_g, h0)
    return y, hout


def _gate_norm_kernel(y_ref, z_ref, g_ref, o_ref):
    z = z_ref[...]
    o_ref[...] = _rms(y_ref[...] * (z * jax.nn.sigmoid(z)), g_ref[...]).astype(BF16)


def _gate_norm(y, proj, g, *, tm):
    m, d = y.shape
    return pl.pallas_call(
        _gate_norm_kernel,
        grid=(m // tm,),
        in_specs=[pl.BlockSpec((tm, d), lambda i: (i, 0)),
                  pl.BlockSpec((tm, d), lambda i: (i, 0)),
                  pl.BlockSpec((1, d), lambda i: (0, 0))],
        out_specs=pl.BlockSpec((tm, d), lambda i: (i, 0)),
        out_shape=jax.ShapeDtypeStruct((m, d), BF16),
        compiler_params=_cparams(("arbitrary",)),
        name="gate_norm",
    )(y, proj, g.reshape(1, d))


def _attn_prompt_kernel(qi_ref, tail_ref, ki_ref, q_ref, k_ref, v_ref, o_ref, key_scr, thr_scr,
                        *, topk, tk, n_ssd_heads, rq, gps):
    qb = pl.program_id(1)
    g = pl.program_id(2)
    nkt = (qb * Q_BLOCK + Q_BLOCK + tk - 1) // tk
    imin = jnp.int32(INT_MIN)

    @pl.when(g == 0)
    def _():
        qi = qi_ref[...]
        qis = jnp.concatenate([qi[:, h * IDX_DIM:(h + 1) * IDX_DIM] for h in range(IDX_HEADS)],
                              axis=0).astype(BF16)
        w_t = tail_ref[...].T[n_ssd_heads:n_ssd_heads + IDX_HEADS, :] * (IDX_HEADS ** -0.5 * IDX_DIM ** -0.5)
        tpos = qb * Q_BLOCK + lax.broadcasted_iota(I32, (tk, Q_BLOCK), 1)
        kiota = lax.broadcasted_iota(I32, (tk, Q_BLOCK), 0)

        def score_tile(kt, carry):
            rows = pl.ds(pl.multiple_of(kt * tk, tk), tk)
            x = _nt(ki_ref[rows, :].astype(BF16), qis)
            sc = jnp.zeros((tk, Q_BLOCK), F32)
            for h in range(IDX_HEADS):
                sc = sc + jnp.maximum(x[:, h * Q_BLOCK:(h + 1) * Q_BLOCK], 0.0) * w_t[h:h + 1, :]
            key = _float_key(sc + 0.0)
            key_scr[rows, :] = jnp.where(kt * tk + kiota <= tpos, key, imin)
            return carry

        lax.fori_loop(0, nkt, score_tile, 0)

        def count_ge(thr):
            def body(kt, cnt):
                rows = pl.ds(pl.multiple_of(kt * tk, tk), tk)
                ge = (key_scr[rows, :] >= thr).astype(I32)
                return cnt + jnp.sum(ge.reshape(tk // SUBLANES, SUBLANES, Q_BLOCK), axis=0)
            cnt = lax.fori_loop(0, nkt, body, jnp.zeros((SUBLANES, Q_BLOCK), I32))
            return jnp.sum(cnt, axis=0, keepdims=True)

        thr = _bisect_kth(count_ge, topk, (1, Q_BLOCK))
        thr_scr[...] = jnp.broadcast_to(thr, (SUBLANES, Q_BLOCK))

    q = q_ref[...]
    nq = rq * Q_BLOCK
    qs = []
    for u in range(gps):
        heads = [q[:, (u * rq + r) * HEAD_DIM:(u * rq + r + 1) * HEAD_DIM] for r in range(rq)]
        qs.append((jnp.concatenate(heads, axis=0) * (HEAD_DIM ** -0.5)).astype(BF16))
    thr_q = jnp.concatenate([thr_scr[0:1, :]] * rq, axis=1)

    def attend(kt, carry):
        rows = pl.ds(pl.multiple_of(kt * tk, tk), tk)
        key = jnp.concatenate([key_scr[rows, :]] * rq, axis=1)
        sel = (key >= thr_q) & (key != imin)
        k_all = k_ref[rows, :]
        v_all = v_ref[rows, :]
        out = []
        for u in range(gps):
            m, l, acc = carry[u]
            cols = slice(u * HEAD_DIM, (u + 1) * HEAD_DIM)
            s_t = _nt(k_all[:, cols].astype(BF16), qs[u])
            m_new = jnp.maximum(m, jnp.max(jnp.where(sel, s_t, NEG_BIG), axis=0, keepdims=True))
            alpha = jnp.exp(m - m_new)
            p = jnp.where(sel, jnp.exp(s_t - m_new), 0.0)
            l = l * alpha + jnp.sum(p, axis=0, keepdims=True)
            v_t = v_all[:, cols].T.astype(BF16)
            acc = acc * alpha + jnp.dot(v_t, p.astype(BF16), preferred_element_type=F32)
            out.append((m_new, l, acc))
        return tuple(out)

    init = tuple((jnp.full((1, nq), NEG_BIG, F32), jnp.zeros((1, nq), F32), jnp.zeros((HEAD_DIM, nq), F32))
                 for _ in range(gps))
    res = lax.fori_loop(0, nkt, attend, init)
    for u in range(gps):
        _, l, acc = res[u]
        o = (acc / l).T
        for r in range(rq):
            c0 = (u * rq + r) * HEAD_DIM
            o_ref[:, c0:c0 + HEAD_DIM] = o[r * Q_BLOCK:(r + 1) * Q_BLOCK, :]


def _attn_prompt(proj, lay, *, nb, seq, d):
    rq = d // HEAD_DIM // KV_HEADS
    nqb = seq // Q_BLOCK
    topk = min(IDX_TOPK, seq // 4)
    tk = min(ATTN_TK, seq)
    qiw = IDX_HEADS * IDX_DIM
    gps = ATTN_GROUPS_PER_STEP
    qw = gps * rq * HEAD_DIM
    kw = gps * HEAD_DIM
    assert KV_HEADS % gps == 0 and lay["q"] % qw == 0 and lay["k"] % kw == 0 and lay["v"] % kw == 0
    kern = functools.partial(_attn_prompt_kernel, topk=topk, tk=tk, n_ssd_heads=d // SSD_HEADDIM, rq=rq, gps=gps)
    return pl.pallas_call(
        kern,
        grid=(nb, nqb, KV_HEADS // gps),
        in_specs=[
            pl.BlockSpec((Q_BLOCK, qiw), lambda b, i, g: (b * nqb + i, lay["qi"] // qiw)),
            pl.BlockSpec((Q_BLOCK, LANES), lambda b, i, g: (b * nqb + i, lay["tail"] // LANES + 1)),
            pl.BlockSpec((seq, IDX_DIM), lambda b, i, g: (b, lay["tail"] // IDX_DIM)),
            pl.BlockSpec((Q_BLOCK, qw), lambda b, i, g: (b * nqb + i, lay["q"] // qw + g)),
            pl.BlockSpec((seq, kw), lambda b, i, g: (b, lay["k"] // kw + g)),
            pl.BlockSpec((seq, kw), lambda b, i, g: (b, lay["v"] // kw + g)),
        ],
        out_specs=pl.BlockSpec((Q_BLOCK, qw), lambda b, i, g: (b * nqb + i, g)),
        out_shape=jax.ShapeDtypeStruct((nb * seq, d), F32),
        scratch_shapes=[pltpu.VMEM((seq, Q_BLOCK), I32), pltpu.VMEM((SUBLANES, Q_BLOCK), I32)],
        compiler_params=_cparams(("arbitrary", "arbitrary", "arbitrary")),
        name="attn_prompt",
    )(proj, proj, proj, proj, proj, proj)


def _fold_lane_groups(x):
    sh = LANES // 2
    while sh >= SAMPLE_PAD:
        x = x + pltpu.roll(x, sh, 1)
        sh //= 2
    return x


def _sample_scores_kernel(pt_ref, *refs, npages, topk, pps):
    kip_refs = refs[:pps]
    kp_refs = refs[pps:2 * pps]
    qi_ref, w_ref, qbd_ref, kinew_ref, knew_ref, p_ref, key_scr, kc_scr, s_scr, qbd_scr = refs[2 * pps:]
    j = pl.program_id(1)
    imin = jnp.int32(INT_MIN)
    nip = qi_ref.shape[0]
    nlp = qbd_ref.shape[0]
    groups_per_slab = LANES // SAMPLE_PAD
    lane = lax.broadcasted_iota(I32, (PAGE_SIZE, LANES), 1)
    lane_group = lane // SAMPLE_PAD

    @pl.when(j == 0)
    def _():
        qbd_scr[...] = (qbd_ref[...] * (HEAD_DIM ** -0.5)).astype(BF16)
        kc_scr[...] = jnp.full(kc_scr.shape, imin, I32)

    qi = qi_ref[...].astype(BF16)
    w_row = w_ref[...]

    def index_keys(kidx):
        x = _nt(kidx.astype(BF16), qi)
        r = jnp.maximum(x, 0.0) * w_row
        acc = r[:, :LANES]
        for c in range(1, nip // LANES):
            acc = acc + r[:, c * LANES:(c + 1) * LANES]
        return _float_key(_fold_lane_groups(acc) + 0.0)

    def put_page(page, keys, scores):
        rows = pl.ds(pl.multiple_of(page * PAGE_SIZE, PAGE_SIZE), PAGE_SIZE)
        key_scr[rows, :] = keys
        s_scr[rows, :] = scores
        slab = page // groups_per_slab
        kc_scr[slab] = jnp.where(lane_group == page % groups_per_slab, keys, kc_scr[slab])

    kidx_cat = jnp.concatenate([r[...] for r in kip_refs], axis=0)
    keys = index_keys(kidx_cat)
    k2d = jnp.concatenate(
        [jnp.concatenate([kp[:, h, :] for h in range(KV_HEADS)], axis=1) for kp in kp_refs], axis=0).astype(BF16)
    scores = _nt(k2d, qbd_scr[...])
    for i in range(pps):
        sl = slice(i * PAGE_SIZE, (i + 1) * PAGE_SIZE)
        put_page(j * pps + i, keys[sl], scores[sl])

    @pl.when(j == 0)
    def _():
        knew = index_keys(kinew_ref[...])
        srow = lax.broadcasted_iota(I32, (PAGE_SIZE, LANES), 0)
        knew = jnp.where(srow <= lane % SAMPLE_PAD, knew, imin)
        put_page(jnp.int32(npages), knew, _nt(knew_ref[...].astype(BF16), qbd_scr[...]))

    @pl.when(j == pl.num_programs(1) - 1)
    def _():
        kc = kc_scr[...]

        def count_ge(thr):
            ge = (kc >= thr).astype(I32)
            cnt = jnp.sum(ge.reshape(-1, SUBLANES, LANES), axis=0)
            return _fold_lane_groups(jnp.sum(cnt, axis=0, keepdims=True))

        thr = _bisect_kth(count_ge, topk, (1, LANES))
        reps = nlp // LANES
        thr_q = jnp.concatenate([thr] * reps, axis=1)

        def masked(page):
            rows = pl.ds(pl.multiple_of(page * PAGE_SIZE, PAGE_SIZE), PAGE_SIZE)
            key = jnp.concatenate([key_scr[rows, :]] * reps, axis=1)
            return rows, (key >= thr_q) & (key != imin)

        def max_body(page, m):
            rows, sel = masked(page)
            return jnp.maximum(m, jnp.max(jnp.where(sel, s_scr[rows, :], NEG_BIG), axis=0, keepdims=True))

        m = lax.fori_loop(0, npages + 1, max_body, jnp.full((1, nlp), NEG_BIG, F32))

        def exp_body(page, l):
            rows, sel = masked(page)
            e = jnp.where(sel, jnp.exp(s_scr[rows, :] - m), 0.0)
            s_scr[rows, :] = e
            return l + jnp.sum(e, axis=0, keepdims=True)

        l = lax.fori_loop(0, npages + 1, exp_body, jnp.zeros((1, nlp), F32))

        def out_body(page, carry):
            rows = pl.ds(pl.multiple_of(page * PAGE_SIZE, PAGE_SIZE), PAGE_SIZE)
            p_ref[page] = (s_scr[rows, :] / l).T.astype(BF16)
            return carry

        lax.fori_loop(0, npages + 1, out_body, 0)


def _sample_pv_kernel(pt_ref, *refs, pps, rq):
    vp_refs = refs[:pps]
    p_ref, pnew_ref, vnew_ref, o_ref, acc_scr = refs[pps:]
    j = pl.program_id(1)
    nrow = rq * SAMPLE_PAD

    @pl.when(j == 0)
    def _():
        pn = pnew_ref[0]
        vn = vnew_ref[...].astype(BF16)
        for g in range(KV_HEADS):
            acc_scr[g] = jnp.dot(pn[g * nrow:(g + 1) * nrow, :], vn[:, g * HEAD_DIM:(g + 1) * HEAD_DIM],
                                 preferred_element_type=F32)

    pcat = jnp.concatenate([p_ref[i] for i in range(pps)], axis=1)
    for g in range(KV_HEADS):
        vg = jnp.concatenate([vp[:, g, :] for vp in vp_refs], axis=0).astype(BF16)
        acc_scr[g] += jnp.dot(pcat[g * nrow:(g + 1) * nrow, :], vg, preferred_element_type=F32)

    @pl.when(j == pl.num_programs(1) - 1)
    def _():
        o_ref[...] = acc_scr[...]


def _attn_sample(q, qi, wi, ki_new, k_new, v_new, cache_k, cache_v, cache_kidx, page_table, *, t_valid):
    nb, npages = page_table.shape
    pps = min(PAGES_PER_STEP, npages)
    assert npages % pps == 0
    nsteps = npages // pps
    past = npages * PAGE_SIZE
    topk = min(IDX_TOPK, (past + t_valid) // 4)
    d = q.shape[-1]
    rq = d // HEAD_DIM // KV_HEADS
    kvw = KV_HEADS * HEAD_DIM
    nrow = rq * SAMPLE_PAD
    ck = cache_k.reshape(-1, PAGE_SIZE, KV_HEADS, HEAD_DIM)
    cv = cache_v.reshape(-1, PAGE_SIZE, KV_HEADS, HEAD_DIM)
    cki = cache_kidx.reshape(-1, PAGE_SIZE, IDX_DIM)

    def pad_to(a, axis, mult):
        n = a.shape[axis]
        widths = [(0, 0)] * a.ndim
        widths[axis] = (0, -(-n // mult) * mult - n)
        return jnp.pad(a, widths)

    qi_s = pad_to(qi.reshape(nb, SAMPLE_PAD, IDX_HEADS, IDX_DIM).transpose(0, 2, 1, 3)
                  .reshape(nb, IDX_HEADS * SAMPLE_PAD, IDX_DIM), 1, LANES)
    w_row = pad_to((wi * (IDX_HEADS ** -0.5 * IDX_DIM ** -0.5)).transpose(0, 2, 1)
                   .reshape(nb, 1, IDX_HEADS * SAMPLE_PAD), 2, LANES)
    nip = qi_s.shape[1]
    q_g = q.reshape(nb, SAMPLE_PAD, KV_HEADS, rq, HEAD_DIM).transpose(0, 2, 3, 1, 4).reshape(
        nb, KV_HEADS, nrow, HEAD_DIM)
    qbd = pad_to(jnp.einsum("bgrd,gh->bgrhd", q_g, jnp.eye(KV_HEADS, dtype=q.dtype))
                 .reshape(nb, KV_HEADS * nrow, kvw), 1, LANES)
    nlp = qbd.shape[1]
    pad_rows = ((0, 0), (0, PAGE_SIZE - SAMPLE_PAD), (0, 0))
    kinew_p = jnp.pad(ki_new, pad_rows)
    knew_p = jnp.pad(k_new, pad_rows)
    vnew_p = jnp.pad(v_new, pad_rows)
    npg = npages + 1
    nslab = -(-npg // (LANES // SAMPLE_PAD))

    def kidx_spec(i):
        return pl.BlockSpec((None, PAGE_SIZE, IDX_DIM), lambda b, j, pt, i=i: (pt[b, j * pps + i], 0, 0))

    def kv_spec(i):
        return pl.BlockSpec((None, PAGE_SIZE, KV_HEADS, HEAD_DIM),
                            lambda b, j, pt, i=i: (pt[b, j * pps + i], 0, 0, 0))

    probs = pl.pallas_call(
        functools.partial(_sample_scores_kernel, npages=npages, topk=topk, pps=pps),
        grid_spec=pltpu.PrefetchScalarGridSpec(
            num_scalar_prefetch=1,
            grid=(nb, nsteps),
            in_specs=[kidx_spec(i) for i in range(pps)] + [kv_spec(i) for i in range(pps)] + [
                pl.BlockSpec((None, nip, IDX_DIM), lambda b, j, pt: (b, 0, 0)),
                pl.BlockSpec((None, 1, nip), lambda b, j, pt: (b, 0, 0)),
                pl.BlockSpec((None, nlp, kvw), lambda b, j, pt: (b, 0, 0)),
                pl.BlockSpec((None, PAGE_SIZE, IDX_DIM), lambda b, j, pt: (b, 0, 0)),
                pl.BlockSpec((None, PAGE_SIZE, kvw), lambda b, j, pt: (b, 0, 0)),
            ],
            out_specs=pl.BlockSpec((None, npg, nlp, PAGE_SIZE), lambda b, j, pt: (b, 0, 0, 0)),
            scratch_shapes=[pltpu.VMEM((npg * PAGE_SIZE, LANES), I32),
                            pltpu.VMEM((nslab, PAGE_SIZE, LANES), I32),
                            pltpu.VMEM((npg * PAGE_SIZE, nlp), F32),
                            pltpu.VMEM((nlp, kvw), BF16)],
        ),
        out_shape=jax.ShapeDtypeStruct((nb, npg, nlp, PAGE_SIZE), BF16),
        compiler_params=_cparams(("arbitrary", "arbitrary")),
        name="sample_scores",
    )(page_table, *([cki] * pps), *([ck] * pps), qi_s, w_row, qbd, kinew_p, knew_p)

    o = pl.pallas_call(
        functools.partial(_sample_pv_kernel, pps=pps, rq=rq),
        grid_spec=pltpu.PrefetchScalarGridSpec(
            num_scalar_prefetch=1,
            grid=(nb, nsteps),
            in_specs=[kv_spec(i) for i in range(pps)] + [
                pl.BlockSpec((None, pps, nlp, PAGE_SIZE), lambda b, j, pt: (b, j, 0, 0)),
                pl.BlockSpec((None, 1, nlp, PAGE_SIZE), lambda b, j, pt: (b, npages, 0, 0)),
                pl.BlockSpec((None, PAGE_SIZE, kvw), lambda b, j, pt: (b, 0, 0)),
            ],
            out_specs=pl.BlockSpec((None, KV_HEADS, nrow, HEAD_DIM), lambda b, j, pt: (b, 0, 0, 0)),
            scratch_shapes=[pltpu.VMEM((KV_HEADS, nrow, HEAD_DIM), F32)],
        ),
        out_shape=jax.ShapeDtypeStruct((nb, KV_HEADS, nrow, HEAD_DIM), F32),
        compiler_params=_cparams(("arbitrary", "arbitrary")),
        name="sample_pv",
    )(page_table, *([cv] * pps), probs, probs, vnew_p)
    return o.reshape(nb, KV_HEADS, rq, SAMPLE_PAD, HEAD_DIM).transpose(0, 3, 1, 2, 4).reshape(nb * SAMPLE_PAD, d)


def _outproj_kernel(y_ref, a_ref, ag_ref, w1_ref, w2_ref, x_ref, gate_ref, o_ref, an_scr):
    @pl.when(pl.program_id(1) == 0)
    def _():
        an_scr[...] = _rms(a_ref[...], ag_ref[...]).astype(BF16)

    acc = jnp.dot(y_ref[...], w1_ref[...], preferred_element_type=F32)
    acc = acc + jnp.dot(an_scr[...], w2_ref[...], preferred_element_type=F32)
    o_ref[...] = x_ref[...] + gate_ref[...] * acc


def _outproj(y_n, attn, attn_g, w1, w2, x, gate, *, tm, per_token, rows_per_batch):
    m, d = x.shape
    tn = PROJ_TN
    if per_token:
        gate_spec = pl.BlockSpec((tm, tn), lambda i, j: (i, j))
    else:
        tpb = rows_per_batch // tm
        gate_spec = pl.BlockSpec((None, 1, tn), lambda i, j: (i // tpb, 0, j))
    return pl.pallas_call(
        _outproj_kernel,
        grid=(m // tm, d // tn),
        in_specs=[pl.BlockSpec((tm, d), lambda i, j: (i, 0)),
                  pl.BlockSpec((tm, d), lambda i, j: (i, 0)),
                  pl.BlockSpec((1, d), lambda i, j: (0, 0)),
                  pl.BlockSpec((d, tn), lambda i, j: (0, j)),
                  pl.BlockSpec((d, tn), lambda i, j: (0, j)),
                  pl.BlockSpec((tm, tn), lambda i, j: (i, j)),
                  gate_spec],
        out_specs=pl.BlockSpec((tm, tn), lambda i, j: (i, j)),
        out_shape=jax.ShapeDtypeStruct((m, d), F32),
        scratch_shapes=[pltpu.VMEM((tm, d), BF16)],
        compiler_params=_cparams(("arbitrary", "arbitrary")),
        name="outproj",
    )(y_n, attn, attn_g.reshape(1, d), w1, w2, x, gate)


def _peer_route_kernel(q_ref, k1_ref, k2_ref, s1_ref, s2_ref, tau_ref, cc_ref):
    q = q_ref[...]
    k1 = k1_ref[...].astype(BF16)
    k2 = k2_ref[...].astype(BF16)
    half = PEER_D_KEY // 2
    tm = q.shape[0]
    taus, ccs = [], []
    for h in range(PEER_HEADS):
        base = h * PEER_D_KEY
        s1 = _nt(k1, q[:, base:base + half].astype(BF16))
        s2 = _nt(k2, q[:, base + half:base + PEER_D_KEY].astype(BF16))
        s1_ref[h * PEER_KEYS:(h + 1) * PEER_KEYS, :] = s1
        s2_ref[h * PEER_KEYS:(h + 1) * PEER_KEYS, :] = s2

        def top_vals(x):
            vals = []
            for _ in range(PEER_TOPK):
                m = jnp.max(x, axis=0, keepdims=True)
                vals.append(m)
                x = jnp.where(x == m, -jnp.inf, x)
            return vals

        v1 = top_vals(s1)
        v2 = jnp.concatenate(top_vals(s2), axis=0)
        cand = jnp.concatenate([v + v2 for v in v1], axis=0) + 0.0
        ckey = _float_key(cand)

        def count_ge(thr, ckey=ckey):
            return jnp.sum((ckey >= thr).astype(I32), axis=0, keepdims=True)

        tau = _key_float(_bisect_kth(count_ge, PEER_TOPK, (1, tm)))
        cmax = cand[0:1, :]
        zsum = jnp.sum(jnp.where(cand >= tau, jnp.exp(cand - cmax), 0.0), axis=0, keepdims=True)
        taus.append(tau)
        ccs.append(cmax + jnp.log(zsum))
    tau_ref[...] = jnp.concatenate(taus, axis=0)
    cc_ref[...] = jnp.concatenate(ccs, axis=0)


def _peer_route(q, k1, k2, *, tm):
    t = q.shape[0]
    rows = PEER_HEADS * PEER_KEYS
    half = PEER_D_KEY // 2
    return pl.pallas_call(
        _peer_route_kernel,
        grid=(t // tm,),
        in_specs=[pl.BlockSpec((tm, PEER_HEADS * PEER_D_KEY), lambda i: (i, 0)),
                  pl.BlockSpec((PEER_KEYS, half), lambda i: (0, 0)),
                  pl.BlockSpec((PEER_KEYS, half), lambda i: (0, 0))],
        out_specs=[pl.BlockSpec((rows, tm), lambda i: (0, i)),
                   pl.BlockSpec((rows, tm), lambda i: (0, i)),
                   pl.BlockSpec((PEER_HEADS, tm), lambda i: (0, i)),
                   pl.BlockSpec((PEER_HEADS, tm), lambda i: (0, i))],
        out_shape=[jax.ShapeDtypeStruct((rows, t), F32), jax.ShapeDtypeStruct((rows, t), F32),
                   jax.ShapeDtypeStruct((PEER_HEADS, t), F32), jax.ShapeDtypeStruct((PEER_HEADS, t), F32)],
        compiler_params=_cparams(("arbitrary",)),
        name="peer_route",
    )(q, k1, k2)


def _peer_dense_kernel(xb_ref, s1_ref, s2_ref, tau_ref, cc_ref, u_ref, v_ref, o_ref, *, te):
    e = pl.program_id(1)

    @pl.when(e == 0)
    def _():
        o_ref[...] = jnp.zeros(o_ref.shape, F32)

    ut = _nt(u_ref[...], xb_ref[...])
    nsub = te // PEER_KEYS
    coefs = []
    for i in range(nsub):
        i1 = e * nsub + i
        gate = jnp.zeros((PEER_KEYS, ut.shape[1]), F32)
        for h in range(PEER_HEADS):
            s1row = s1_ref[pl.ds(h * PEER_KEYS + i1, 1), :]
            sm = s1row + s2_ref[h * PEER_KEYS:(h + 1) * PEER_KEYS, :]
            gate = gate + jnp.where(sm >= tau_ref[h:h + 1, :], jnp.exp(sm - cc_ref[h:h + 1, :]), 0.0)
        coefs.append(gate * jax.nn.gelu(ut[i * PEER_KEYS:(i + 1) * PEER_KEYS, :]))
    coef = jnp.concatenate(coefs, axis=0).T.astype(BF16)
    o_ref[...] += jnp.dot(coef, v_ref[...], preferred_element_type=F32)


def _peer_dense(xb, s1, s2, tau, cc, u_b, v_b, *, tm):
    t, d = xb.shape
    ne = u_b.shape[0]
    te = min(PEER_TE, ne)
    rows = PEER_HEADS * PEER_KEYS
    return pl.pallas_call(
        functools.partial(_peer_dense_kernel, te=te),
        grid=(t // tm, ne // te),
        in_specs=[pl.BlockSpec((tm, d), lambda i, e: (i, 0)),
                  pl.BlockSpec((rows, tm), lambda i, e: (0, i)),
                  pl.BlockSpec((rows, tm), lambda i, e: (0, i)),
                  pl.BlockSpec((PEER_HEADS, tm), lambda i, e: (0, i)),
                  pl.BlockSpec((PEER_HEADS, tm), lambda i, e: (0, i)),
                  pl.BlockSpec((te, d), lambda i, e: (e, 0)),
                  pl.BlockSpec((te, d), lambda i, e: (e, 0))],
        out_specs=pl.BlockSpec((tm, d), lambda i, e: (i, 0)),
        out_shape=jax.ShapeDtypeStruct((t, d), F32),
        compiler_params=_cparams(("arbitrary", "arbitrary")),
        name="peer_dense",
    )(xb, s1, s2, tau, cc, u_b, v_b)


def _final_kernel(x_ref, f_ref, gate_ref, g_ref, o_ref):
    o_ref[...] = _rms(x_ref[...] + gate_ref[...] * f_ref[...], g_ref[...])


def _final(x, f, gate, g, *, tm, per_token, rows_per_batch):
    m, d = x.shape
    if per_token:
        gate_spec = pl.BlockSpec((tm, d), lambda i: (i, 0))
    else:
        tpb = rows_per_batch // tm
        gate_spec = pl.BlockSpec((None, 1, d), lambda i: (i // tpb, 0, 0))
    return pl.pallas_call(
        _final_kernel,
        grid=(m // tm,),
        in_specs=[pl.BlockSpec((tm, d), lambda i: (i, 0)),
                  pl.BlockSpec((tm, d), lambda i: (i, 0)),
                  gate_spec,
                  pl.BlockSpec((1, d), lambda i: (0, 0))],
        out_specs=pl.BlockSpec((tm, d), lambda i: (i, 0)),
        out_shape=jax.ShapeDtypeStruct((m, d), F32),
        compiler_params=_cparams(("arbitrary",)),
        name="final_norm",
    )(x, f, gate, g.reshape(1, d))


def _layout(d):
    nh = d // SSD_HEADDIM
    kvw = KV_HEADS * HEAD_DIM
    qiw = IDX_HEADS * IDX_DIM
    xbcw = d + 2 * SSD_GROUPS * D_STATE
    lay, off = {}, 0
    for name, w in (("z", d), ("q", d), ("qi", qiw), ("xbc", xbcw), ("k", kvw), ("v", kvw), ("tail", PROJ_TN)):
        assert w % PROJ_TN == 0
        lay[name] = off
        off += w
    lay["total"] = off
    assert IDX_DIM + nh + IDX_HEADS <= PROJ_TN
    return lay


def _group_cols(xbc, d):
    lead = xbc.shape[:-1]
    gn = SSD_GROUPS * D_STATE
    x = xbc[..., :d].reshape(lead + (SSD_GROUPS, d // SSD_GROUPS))
    b = xbc[..., d:d + gn].reshape(lead + (SSD_GROUPS, D_STATE))
    c = xbc[..., d + gn:].reshape(lead + (SSD_GROUPS, D_STATE))
    return jnp.concatenate([x, b, c], axis=-1).reshape(lead + (d + 2 * gn,))


def _ungroup_cols(xg, d):
    lead = xg.shape[:-1]
    gn = SSD_GROUPS * D_STATE
    xsw = d // SSD_GROUPS
    g3 = xg.reshape(lead + (SSD_GROUPS, xsw + 2 * D_STATE))
    return jnp.concatenate([g3[..., :xsw].reshape(lead + (d,)),
                            g3[..., xsw:xsw + D_STATE].reshape(lead + (gn,)),
                            g3[..., xsw + D_STATE:].reshape(lead + (gn,))], axis=-1)


def _prep_w_in(w_in, d):
    nh = d // SSD_HEADDIM
    kvw = KV_HEADS * HEAD_DIM
    qiw = IDX_HEADS * IDX_DIM
    xbcw = d + 2 * SSD_GROUPS * D_STATE
    sizes = (d, xbcw, nh, d, kvw, kvw, qiw, IDX_HEADS, IDX_DIM)
    splits = np.cumsum(sizes)[:-1]
    z, xbc, dt, q, k, v, qi, wi, ki = [a.astype(BF16) for a in jnp.split(w_in, splits, axis=1)]
    pad = jnp.zeros((d, PROJ_TN - IDX_DIM - nh - IDX_HEADS), BF16)
    return jnp.concatenate([z, q, qi, _group_cols(xbc, d), k, v, ki, dt, wi, pad], axis=1)


def _group_major(vec, rpg):
    return jnp.pad(vec.reshape(SSD_GROUPS, 1, rpg), ((0, 0), (0, 0), (0, LANES - rpg)))


def _rope_tables(pos):
    half = HEAD_DIM // 8
    inv = ROPE_THETA ** (-jnp.arange(half, dtype=F32) / half)
    ang = pos.astype(F32)[:, None] * inv[None, :]
    cos, sin = jnp.cos(ang), jnp.sin(ang)
    n = pos.shape[0]
    rest = HEAD_DIM - 2 * half
    return (jnp.concatenate([cos, cos, jnp.ones((n, rest), F32)], axis=1),
            jnp.concatenate([-sin, sin, jnp.zeros((n, rest), F32)], axis=1))


def _layer(x2d, mods, pos, prm, *, nb, seq, per_token, tm, ssd_cfg, attn_fn, peer_tm):
    m, d = x2d.shape
    lay = prm["lay"]
    nh = d // SSD_HEADDIM
    rpg = nh // SSD_GROUPS
    sh1, sc1, g1, sh2, sc2, g2 = mods
    cos, sin = _rope_tables(pos)
    xoff = lay["xbc"]
    rope = (cos, sin, prm["kidx_norm_g"],
            ((lay["q"] // PROJ_TN, lay["xbc"] // PROJ_TN), (lay["k"] // PROJ_TN, lay["v"] // PROJ_TN)),
            lay["tail"] // PROJ_TN, nh)
    proj = _norm_mod_matmul(x2d, prm["norm1_g"], sh1, sc1, prm["w_in"], tm=tm, per_token=per_token,
                            rows_per_batch=seq, rope=rope)

    gw = rpg * SSD_HEADDIM + 2 * D_STATE
    dt_raw = proj[:, lay["tail"] + IDX_DIM: lay["tail"] + IDX_DIM + nh]
    y_ssd, ssm_new = ssd_cfg(proj, dt_raw, xoff // gw)
    tm_e = min(tm, ELEMWISE_TM)
    y_n = _gate_norm(y_ssd, proj, prm["ssd_norm_g"], tm=tm_e)

    o_attn = attn_fn(proj)

    x1 = _outproj(y_n, o_attn, prm["attn_norm_g"], prm["w_out1"], prm["w_out2"], x2d, g1, tm=tm,
                  per_token=per_token, rows_per_batch=seq)

    qp, xb = _norm_mod_matmul(x1, prm["norm2_g"], sh2, sc2, prm["peer_wq"], tm=tm, per_token=per_token,
                              rows_per_batch=seq, emit_h=True)
    s1, s2, tau, cc = _peer_route(qp, prm["peer_k1"], prm["peer_k2"], tm=min(peer_tm, 256))
    ffn = _peer_dense(xb, s1, s2, tau, cc, prm["peer_u"], prm["peer_v"], tm=peer_tm)
    y = _final(x1, ffn, g2, prm["final_norm_g"], tm=tm_e, per_token=per_token, rows_per_batch=seq)
    return y, proj, ssm_new


def kernel(x_prompt, x_sample, c_prompt, c_sample, cache_k, cache_v, cache_kidx, state_ssm, state_conv, page_table, w_mod, b_mod, norm1_g, w_in, conv_w, conv_b, dt_bias, a_log, d_skip, ssd_norm_g, kidx_norm_g, attn_norm_g, w_out, norm2_g, peer_wq, peer_k1, peer_k2, peer_u, peer_v, final_norm_g):
    nbp, seq, d = x_prompt.shape
    nbs, tdec, _ = x_sample.shape
    depth = w_mod.shape[0]
    assert depth == 1
    nh = d // SSD_HEADDIM
    rpg = nh // SSD_GROUPS
    gw = rpg * SSD_HEADDIM + 2 * D_STATE
    lay = _layout(d)
    assert lay["xbc"] % gw == 0
    kvw = KV_HEADS * HEAD_DIM
    xbcw = d + 2 * SSD_GROUPS * D_STATE

    prm = dict(
        lay=lay,
        norm1_g=norm1_g[0], kidx_norm_g=kidx_norm_g[0], ssd_norm_g=ssd_norm_g[0], attn_norm_g=attn_norm_g[0],
        norm2_g=norm2_g[0], final_norm_g=final_norm_g,
        w_in=_prep_w_in(w_in[0], d),
        w_out1=w_out[0, :d].astype(BF16), w_out2=w_out[0, d:].astype(BF16),
        peer_wq=peer_wq[0].astype(BF16), peer_k1=peer_k1[0], peer_k2=peer_k2[0],
        peer_u=peer_u[0].astype(BF16), peer_v=peer_v[0].astype(BF16),
    )
    conv_w_g = _group_cols(conv_w[0], d)
    conv_b_g = _group_cols(conv_b[0], d).reshape(1, xbcw)
    dtb_g = _group_major(dt_bias[0], rpg)
    alog_g = _group_major(a_log[0], rpg)
    dsk_g = _group_major(d_skip[0], rpg)

    mod = _adaln(jnp.concatenate([c_prompt, c_sample], axis=0), w_mod[0], b_mod[0])
    mod_p = [a.reshape(nbp, 1, d) for a in jnp.split(mod[:nbp], 6, axis=-1)]
    mod_s = [jnp.repeat(a, SAMPLE_PAD, axis=0) for a in jnp.split(mod[nbp:], 6, axis=-1)]

    def dt_group_major(dt_raw):
        rows = dt_raw.shape[0]
        dtg = dt_raw.reshape(rows, SSD_GROUPS, rpg).transpose(1, 0, 2)
        return jnp.pad(dtg, ((0, 0), (0, 0), (0, LANES - rpg)))

    cs_p = min(SSD_CHUNK, seq)
    nc_p = seq // cs_p
    assert seq % cs_p == 0 and seq % Q_BLOCK == 0

    def ssd_prompt(proj, dt_raw, xoff_blocks):
        hist = jnp.zeros((nbp, SUBLANES, xbcw), F32)
        h0 = jnp.zeros((nbp, nh, SSD_HEADDIM, D_STATE), F32)
        return _ssd(proj, xoff_blocks, hist, conv_w_g, conv_b_g, dt_group_major(dt_raw), dtb_g, alog_g, dsk_g, h0,
                    nb=nbp, nc=nc_p, cs=cs_p, d=d, valid_len=cs_p)

    tm_p = min(512, seq)
    pos_p = jnp.tile(jnp.arange(seq), nbp)
    y_p, proj_p, ssm_p = _layer(
        x_prompt.reshape(nbp * seq, d), mod_p, pos_p, prm, nb=nbp, seq=seq, per_token=False, tm=tm_p,
        ssd_cfg=ssd_prompt, attn_fn=functools.partial(_attn_prompt, lay=lay, nb=nbp, seq=seq, d=d),
        peer_tm=min(512, seq))

    npages = page_table.shape[1]
    past = npages * PAGE_SIZE
    ms = nbs * SAMPLE_PAD
    xs_pad = jnp.pad(x_sample, ((0, 0), (0, SAMPLE_PAD - tdec), (0, 0))).reshape(ms, d)
    pos_s = jnp.tile(past + jnp.arange(SAMPLE_PAD), nbs)

    def ssd_sample(proj, dt_raw, xoff_blocks):
        xbc = proj[:, lay["xbc"]:lay["xbc"] + xbcw].reshape(nbs, SAMPLE_PAD, xbcw)
        xbc = jnp.pad(xbc, ((0, 0), (0, SAMPLE_CHUNK - SAMPLE_PAD), (0, 0))).reshape(nbs * SAMPLE_CHUNK, xbcw)
        dtr = jnp.pad(dt_raw.reshape(nbs, SAMPLE_PAD, nh), ((0, 0), (0, SAMPLE_CHUNK - SAMPLE_PAD), (0, 0)))
        hist = jnp.pad(_group_cols(state_conv[0], d), ((0, 0), (SUBLANES - (CONV_W - 1), 0), (0, 0)))
        y, hnew = _ssd(xbc, 0, hist, conv_w_g, conv_b_g, dt_group_major(dtr.reshape(nbs * SAMPLE_CHUNK, nh)),
                       dtb_g, alog_g, dsk_g, state_ssm[0], nb=nbs, nc=1, cs=SAMPLE_CHUNK, d=d, valid_len=tdec)
        y = y.reshape(nbs, SAMPLE_CHUNK, d)[:, :SAMPLE_PAD].reshape(ms, d)
        return y, hnew

    def attn_sample(proj):
        def seg(name, w):
            return proj[:, lay[name]:lay[name] + w].reshape(nbs, SAMPLE_PAD, w)
        tail = lay["tail"]
        ki_new = proj[:, tail:tail + IDX_DIM].reshape(nbs, SAMPLE_PAD, IDX_DIM)
        wi = proj[:, tail + IDX_DIM + nh: tail + IDX_DIM + nh + IDX_HEADS].reshape(nbs, SAMPLE_PAD, IDX_HEADS)
        return _attn_sample(seg("q", d), seg("qi", IDX_HEADS * IDX_DIM), wi, ki_new, seg("k", kvw), seg("v", kvw),
                            cache_k, cache_v, cache_kidx, page_table, t_valid=tdec)

    y_s, proj_s, ssm_s = _layer(
        xs_pad, mod_s, pos_s, prm, nb=nbs, seq=SAMPLE_PAD, per_token=True, tm=ms,
        ssd_cfg=ssd_sample, attn_fn=attn_sample, peer_tm=ms)

    def states(proj, nb, rows, valid):
        p3 = proj.reshape(nb, rows, lay["total"])[:, :valid]
        k = p3[:, :, lay["k"]:lay["k"] + kvw].reshape(1, nb, valid, KV_HEADS, HEAD_DIM)
        v = p3[:, :, lay["v"]:lay["v"] + kvw].reshape(1, nb, valid, KV_HEADS, HEAD_DIM)
        ki = p3[:, :, lay["tail"]:lay["tail"] + IDX_DIM].reshape(1, nb, valid, IDX_DIM)
        xbc = _ungroup_cols(p3[:, valid - (CONV_W - 1):valid, lay["xbc"]:lay["xbc"] + xbcw], d)
        return k, v, ki, xbc.reshape(1, nb, CONV_W - 1, xbcw)

    k_p, v_p, ki_p, conv_p = states(proj_p, nbp, seq, seq)
    k_s, v_s, ki_s, conv_s = states(proj_s, nbs, SAMPLE_PAD, tdec)
    y_prompt = y_p.reshape(nbp, seq, d)
    y_sample = y_s.reshape(nbs, SAMPLE_PAD, d)[:, :tdec]
    return (y_prompt, y_sample, k_p, v_p, ki_p, ssm_p[None], conv_p,
            k_s, v_s, ki_s, ssm_s[None], conv_s)
```

```python
import functools
import math

import jax
import jax.numpy as jnp
import numpy as np
from jax import lax
from jax.experimental import pallas as pl
from jax.experimental.pallas import tpu as pltpu

F32 = jnp.float32
BF16 = jnp.bfloat16
I32 = jnp.int32

SSD_HEADDIM = 64
SSD_GROUPS = 8
D_STATE = 128
CONV_W = 4
SSD_CHUNK = 256
HEAD_DIM = 128
KV_HEADS = 8
ROPE_THETA = 500000.0
IDX_HEADS = 32
IDX_DIM = 128
IDX_TOPK = 256
Q_BLOCK = 128
PEER_HEADS = 8
PEER_KEYS = 128
PEER_TOPK = 16
PEER_D_KEY = 256
PAGE_SIZE = 128
EPS = 1e-6

LANES = 128
SUBLANES = 8
VMEM_LIMIT_BYTES = 56 * 1024 * 1024

MXU_TILE = 256
PROJ_TN = MXU_TILE
PROJ_TM = 1024
ELEMWISE_TM = 256
PEER_ROUTE_TM = 256
PEER_TM = 1024
PEER_TE = 512
SAMPLE_PAD = 8
SAMPLE_CHUNK = 128
ATTN_TK = 512
ATTN_GROUPS_PER_STEP = 2
PAGES_PER_STEP = 8
NEG_BIG = -1e30
INT_MIN = -2 ** 31


def _cparams(sem):
    return pltpu.CompilerParams(dimension_semantics=sem, vmem_limit_bytes=VMEM_LIMIT_BYTES)


def _nt(a, b):
    return lax.dot_general(a, b, (((1,), (1,)), ((), ())), preferred_element_type=F32)


def _tn(a, b):
    return lax.dot_general(a, b, (((0,), (0,)), ((), ())), preferred_element_type=F32)


def _rms(x, g):
    return x * lax.rsqrt(jnp.mean(x * x, axis=-1, keepdims=True) + EPS) * g


def _rope_tile(a, cos, sin):
    half = HEAD_DIM // 8
    lane = lax.broadcasted_iota(I32, a.shape, 1)
    sw = jnp.where(lane < half, pltpu.roll(a, HEAD_DIM - half, 1), pltpu.roll(a, half, 1))
    return a * cos + sw * sin


def _float_key(x):
    bits = pltpu.bitcast(x, I32)
    return jnp.where(bits < 0, bits ^ jnp.int32(0x7FFFFFFF), bits)


def _key_float(key):
    bits = jnp.where(key < 0, key ^ jnp.int32(0x7FFFFFFF), key)
    return pltpu.bitcast(bits, F32)


def _bisect_kth(count_ge, k, shape):
    imin = jnp.int32(INT_MIN)

    def body(i, ans):
        cand_u = ans | lax.shift_left(jnp.int32(1), jnp.int32(31) - i)
        return jnp.where(count_ge(cand_u ^ imin) >= k, cand_u, ans)

    ans = lax.fori_loop(0, 32, body, jnp.zeros(shape, I32))
    return ans ^ imin


def _adaln_kernel(c_ref, w_ref, b_ref, o_ref):
    c = c_ref[...]
    a = (c * jax.nn.sigmoid(c)).astype(BF16)
    o_ref[...] = jnp.dot(a, w_ref[...].astype(BF16), preferred_element_type=F32) + b_ref[...]


def _adaln(c, w_mod, b_mod):
    n, d = c.shape
    npad = -(-n // SUBLANES) * SUBLANES
    cp = jnp.pad(c, ((0, npad - n), (0, 0)))
    nout = w_mod.shape[1]
    tn = PROJ_TN
    out = pl.pallas_call(
        _adaln_kernel,
        grid=(nout // tn,),
        in_specs=[pl.BlockSpec((npad, d), lambda j: (0, 0)),
                  pl.BlockSpec((d, tn), lambda j: (0, j)),
                  pl.BlockSpec((1, tn), lambda j: (0, j))],
        out_specs=pl.BlockSpec((npad, tn), lambda j: (0, j)),
        out_shape=jax.ShapeDtypeStruct((npad, nout), F32),
        compiler_params=_cparams(("arbitrary",)),
        name="adaln",
    )(cp, w_mod, b_mod.reshape(1, nout))
    return out[:n]


def _modulate_kernel(x_ref, g_ref, sh_ref, sc_ref, o_ref):
    o_ref[...] = (_rms(x_ref[...], g_ref[...]) * (1.0 + sc_ref[...]) + sh_ref[...]).astype(BF16)


def _modulate(x, g, shift, scale, *, tm, per_token, rows_per_batch):
    m, d = x.shape
    if per_token:
        mod_spec = pl.BlockSpec((tm, d), lambda i: (i, 0))
    else:
        tiles_per_batch = rows_per_batch // tm
        mod_spec = pl.BlockSpec((None, 1, d), lambda i: (i // tiles_per_batch, 0, 0))
    return pl.pallas_call(
        _modulate_kernel,
        grid=(m // tm,),
        in_specs=[pl.BlockSpec((tm, d), lambda i: (i, 0)),
                  pl.BlockSpec((1, d), lambda i: (0, 0)),
                  mod_spec, mod_spec],
        out_specs=pl.BlockSpec((tm, d), lambda i: (i, 0)),
        out_shape=jax.ShapeDtypeStruct((m, d), BF16),
        compiler_params=_cparams(("arbitrary",)),
        name="modulate",
    )(x, g.reshape(1, d), shift, scale)


def _proj_kernel(*refs, rope_ranges, tail_tile):
    if tail_tile is not None:
        h_ref, w_ref, cos_ref, sin_ref, kg_ref, o_ref = refs
    else:
        h_ref, w_ref, o_ref = refs
    j = pl.program_id(1)

    def product():
        return jnp.dot(h_ref[...], w_ref[...], preferred_element_type=F32)

    if tail_tile is None:
        o_ref[...] = product()
        return

    is_rope = functools.reduce(jnp.logical_or, [(j >= lo) & (j < hi) for lo, hi in rope_ranges])
    is_tail = j == tail_tile
    tn = w_ref.shape[1]

    @pl.when(jnp.logical_not(is_rope | is_tail))
    def _():
        o_ref[...] = product()

    @pl.when(is_rope)
    def _():
        acc = product()
        cos = cos_ref[...]
        sin = sin_ref[...]
        for t in range(tn // HEAD_DIM):
            sl = slice(t * HEAD_DIM, (t + 1) * HEAD_DIM)
            o_ref[:, sl] = _rope_tile(acc[:, sl], cos, sin)

    @pl.when(is_tail)
    def _():
        acc = product()
        ki = _rms(acc[:, :IDX_DIM], kg_ref[...])
        o_ref[:, :IDX_DIM] = _rope_tile(ki, cos_ref[...], sin_ref[...])
        o_ref[:, IDX_DIM:] = acc[:, IDX_DIM:]


def _proj(h, w, *, tm, rope=None):
    m, d = h.shape
    n = w.shape[1]
    tn = PROJ_TN
    assert m % tm == 0 and n % tn == 0
    in_specs = [pl.BlockSpec((tm, d), lambda i, j: (i, 0), pipeline_mode=pl.Buffered(1)),
                pl.BlockSpec((d, tn), lambda i, j: (0, j))]
    args = [h, w]
    rope_ranges, tail_tile = (), None
    if rope is not None:
        cos, sin, kidx_g, rope_ranges, tail_tile = rope
        in_specs += [pl.BlockSpec((tm, HEAD_DIM), lambda i, j: (i, 0)),
                     pl.BlockSpec((tm, HEAD_DIM), lambda i, j: (i, 0)),
                     pl.BlockSpec((1, IDX_DIM), lambda i, j: (0, 0))]
        args += [cos, sin, kidx_g.reshape(1, IDX_DIM)]
    return pl.pallas_call(
        functools.partial(_proj_kernel, rope_ranges=rope_ranges, tail_tile=tail_tile),
        grid=(m // tm, n // tn),
        in_specs=in_specs,
        out_specs=pl.BlockSpec((tm, tn), lambda i, j: (i, j)),
        out_shape=jax.ShapeDtypeStruct((m, n), F32),
        compiler_params=_cparams(("arbitrary", "arbitrary")),
        name="proj",
    )(*args)


def _ssd_kernel(x_ref, prev_ref, hist_ref, cw_ref, cb_ref, dt_ref, dtb_ref, alog_ref, dsk_ref, h0_ref,
                y_ref, hout_ref, h_scr, *, cs, rpg, valid_len, nc):
    c = pl.program_id(2)
    xs_w = rpg * SSD_HEADDIM

    @pl.when(c == 0)
    def _():
        h_scr[...] = h0_ref[...]

    prev = jnp.where(c == 0, hist_ref[...], prev_ref[...])
    cat = jnp.concatenate([prev, x_ref[...]], axis=0)
    w = cw_ref[...]
    acc = cb_ref[...]
    for j in range(CONV_W):
        lo = SUBLANES - (CONV_W - 1) + j
        acc = acc + cat[lo:lo + cs] * w[j:j + 1]
    xc = acc * jax.nn.sigmoid(acc)
    xs = xc[:, :xs_w]
    bm = xc[:, xs_w:xs_w + D_STATE]
    cm = xc[:, xs_w + D_STATE:]
    bm_b = bm.astype(BF16)
    cm_b = cm.astype(BF16)

    z = dt_ref[...] + dtb_ref[...]
    dt = jnp.maximum(z, 0.0) + jnp.log1p(jnp.exp(-jnp.abs(z)))
    if valid_len < cs:
        row = lax.broadcasted_iota(I32, dt.shape, 0)
        dt = jnp.where(row < valid_len, dt, 0.0)
    a_neg = -jnp.exp(alog_ref[...])
    ii = lax.broadcasted_iota(I32, (cs, cs), 0)
    jj = lax.broadcasted_iota(I32, (cs, cs), 1)
    causal = ii >= jj
    acs = jnp.dot(causal.astype(F32), dt * a_neg, preferred_element_type=F32,
                  precision=lax.Precision.HIGHEST)
    acs_t = acs.T
    a_last = acs[cs - 1:cs, :]
    cb = _nt(cm_b, bm_b)
    dsk = dsk_ref[...]

    ys = []
    for r in range(rpg):
        acol = acs[:, r:r + 1]
        arow = acs_t[r:r + 1, :]
        lm = jnp.exp(jnp.where(causal, acol - arow, -jnp.inf))
        xr = xs[:, r * SSD_HEADDIM:(r + 1) * SSD_HEADDIM]
        xd = xr * dt[:, r:r + 1]
        y = jnp.dot((cb * lm).astype(BF16), xd.astype(BF16), preferred_element_type=F32)
        hr = h_scr[r]
        y = y + jnp.exp(acol) * _nt(cm_b, hr.astype(BF16))
        al = a_last[:, r:r + 1]
        decay = jnp.exp(al - acol)
        h_scr[r] = hr * jnp.exp(al) + _tn((xd * decay).astype(BF16), bm_b)
        ys.append(y + xr * dsk[:, r:r + 1])
    y_ref[...] = jnp.concatenate(ys, axis=1)

    @pl.when(c == nc - 1)
    def _():
        hout_ref[...] = h_scr[...]


def _ssd(xg, xoff_blocks, hist8, conv_w_g, conv_b_g, dt_g, dtb_g, alog_g, dsk_g, h0, *, nb, nc, cs, d, valid_len):
    g_cnt = SSD_GROUPS
    rpg = d // SSD_HEADDIM // g_cnt
    gw = rpg * SSD_HEADDIM + 2 * D_STATE
    m = nb * nc * cs
    cpb = cs // SUBLANES
    kern = functools.partial(_ssd_kernel, cs=cs, rpg=rpg, valid_len=valid_len, nc=nc)
    y, hout = pl.pallas_call(
        kern,
        grid=(nb, g_cnt, nc),
        in_specs=[
            pl.BlockSpec((cs, gw), lambda b, g, c: (b * nc + c, xoff_blocks + g)),
            pl.BlockSpec((SUBLANES, gw), lambda b, g, c: (jnp.maximum((b * nc + c) * cpb - 1, 0), xoff_blocks + g)),
            pl.BlockSpec((None, SUBLANES, gw), lambda b, g, c: (b, 0, g)),
            pl.BlockSpec((CONV_W, gw), lambda b, g, c: (0, g)),
            pl.BlockSpec((1, gw), lambda b, g, c: (0, g)),
            pl.BlockSpec((None, cs, LANES), lambda b, g, c: (g, b * nc + c, 0)),
            pl.BlockSpec((None, 1, LANES), lambda b, g, c: (g, 0, 0)),
            pl.BlockSpec((None, 1, LANES), lambda b, g, c: (g, 0, 0)),
            pl.BlockSpec((None, 1, LANES), lambda b, g, c: (g, 0, 0)),
            pl.BlockSpec((None, rpg, SSD_HEADDIM, D_STATE), lambda b, g, c: (b, g, 0, 0)),
        ],
        out_specs=[
            pl.BlockSpec((cs, rpg * SSD_HEADDIM), lambda b, g, c: (b * nc + c, g)),
            pl.BlockSpec((None, rpg, SSD_HEADDIM, D_STATE), lambda b, g, c: (b, g, 0, 0)),
        ],
        out_shape=[jax.ShapeDtypeStruct((m, d), F32),
                   jax.ShapeDtypeStruct((nb, d // SSD_HEADDIM, SSD_HEADDIM, D_STATE), F32)],
        scratch_shapes=[pltpu.VMEM((rpg, SSD_HEADDIM, D_STATE), F32)],
        compiler_params=_cparams(("arbitrary", "arbitrary", "arbitrary")),
        name="ssd_scan",
    )(xg, xg, hist8, conv_w_g, conv_b_g, dt_g, dtb_g, alog_g, dsk_g, h0)
    return y, hout


def _gate_norm_kernel(y_ref, z_ref, g_ref, o_ref):
    z = z_ref[...]
    o_ref[...] = _rms(y_ref[...] * (z * jax.nn.sigmoid(z)), g_ref[...]).astype(BF16)


def _gate_norm(y, proj, g, *, tm):
    m, d = y.shape
    return pl.pallas_call(
        _gate_norm_kernel,
        grid=(m // tm,),
        in_specs=[pl.BlockSpec((tm, d), lambda i: (i, 0)),
                  pl.BlockSpec((tm, d), lambda i: (i, 0)),
                  pl.BlockSpec((1, d), lambda i: (0, 0))],
        out_specs=pl.BlockSpec((tm, d), lambda i: (i, 0)),
        out_shape=jax.ShapeDtypeStruct((m, d), BF16),
        compiler_params=_cparams(("arbitrary",)),
        name="gate_norm",
    )(y, proj, g.reshape(1, d))


def _attn_prompt_kernel(qi_ref, tail_ref, ki_ref, q_ref, k_ref, v_ref, o_ref, key_scr, thr_scr,
                        *, topk, tk, n_ssd_heads, rq, gps):
    qb = pl.program_id(1)
    g = pl.program_id(2)
    nkt = (qb * Q_BLOCK + Q_BLOCK + tk - 1) // tk
    imin = jnp.int32(INT_MIN)

    @pl.when(g == 0)
    def _():
        qi = qi_ref[...]
        qis = jnp.concatenate([qi[:, h * IDX_DIM:(h + 1) * IDX_DIM] for h in range(IDX_HEADS)],
                              axis=0).astype(BF16)
        w_t = tail_ref[...].T[n_ssd_heads:n_ssd_heads + IDX_HEADS, :] * (IDX_HEADS ** -0.5 * IDX_DIM ** -0.5)
        tpos = qb * Q_BLOCK + lax.broadcasted_iota(I32, (tk, Q_BLOCK), 1)
        kiota = lax.broadcasted_iota(I32, (tk, Q_BLOCK), 0)

        def score_tile(kt, carry):
            rows = pl.ds(pl.multiple_of(kt * tk, tk), tk)
            x = _nt(ki_ref[rows, :].astype(BF16), qis)
            sc = jnp.zeros((tk, Q_BLOCK), F32)
            for h in range(IDX_HEADS):
                sc = sc + jnp.maximum(x[:, h * Q_BLOCK:(h + 1) * Q_BLOCK], 0.0) * w_t[h:h + 1, :]
            key = _float_key(sc + 0.0)
            key_scr[rows, :] = jnp.where(kt * tk + kiota <= tpos, key, imin)
            return carry

        lax.fori_loop(0, nkt, score_tile, 0)

        def count_ge(thr):
            def body(kt, cnt):
                rows = pl.ds(pl.multiple_of(kt * tk, tk), tk)
                ge = (key_scr[rows, :] >= thr).astype(I32)
                return cnt + jnp.sum(ge.reshape(tk // SUBLANES, SUBLANES, Q_BLOCK), axis=0)
            cnt = lax.fori_loop(0, nkt, body, jnp.zeros((SUBLANES, Q_BLOCK), I32))
            return jnp.sum(cnt, axis=0, keepdims=True)

        thr = _bisect_kth(count_ge, topk, (1, Q_BLOCK))
        thr_scr[...] = jnp.broadcast_to(thr, (SUBLANES, Q_BLOCK))

    q = q_ref[...]
    nq = rq * Q_BLOCK
    qs = []
    for u in range(gps):
        heads = [q[:, (u * rq + r) * HEAD_DIM:(u * rq + r + 1) * HEAD_DIM] for r in range(rq)]
        qs.append((jnp.concatenate(heads, axis=0) * (HEAD_DIM ** -0.5)).astype(BF16))
    thr_q = jnp.concatenate([thr_scr[0:1, :]] * rq, axis=1)

    def attend(kt, carry):
        rows = pl.ds(pl.multiple_of(kt * tk, tk), tk)
        key = jnp.concatenate([key_scr[rows, :]] * rq, axis=1)
        sel = (key >= thr_q) & (key != imin)
        k_all = k_ref[rows, :]
        v_all = v_ref[rows, :]
        out = []
        for u in range(gps):
            m, l, acc = carry[u]
            cols = slice(u * HEAD_DIM, (u + 1) * HEAD_DIM)
            s_t = _nt(k_all[:, cols].astype(BF16), qs[u])
            m_new = jnp.maximum(m, jnp.max(jnp.where(sel, s_t, NEG_BIG), axis=0, keepdims=True))
            alpha = jnp.exp(m - m_new)
            p = jnp.where(sel, jnp.exp(s_t - m_new), 0.0)
            l = l * alpha + jnp.sum(p, axis=0, keepdims=True)
            v_t = v_all[:, cols].T.astype(BF16)
            acc = acc * alpha + jnp.dot(v_t, p.astype(BF16), preferred_element_type=F32)
            out.append((m_new, l, acc))
        return tuple(out)

    init = tuple((jnp.full((1, nq), NEG_BIG, F32), jnp.zeros((1, nq), F32), jnp.zeros((HEAD_DIM, nq), F32))
                 for _ in range(gps))
    res = lax.fori_loop(0, nkt, attend, init)
    for u in range(gps):
        _, l, acc = res[u]
        o = (acc / l).T
        for r in range(rq):
            c0 = (u * rq + r) * HEAD_DIM
            o_ref[:, c0:c0 + HEAD_DIM] = o[r * Q_BLOCK:(r + 1) * Q_BLOCK, :]


def _attn_prompt(proj, lay, *, nb, seq, d):
    rq = d // HEAD_DIM // KV_HEADS
    nqb = seq // Q_BLOCK
    topk = min(IDX_TOPK, seq // 4)
    tk = min(ATTN_TK, seq)
    qiw = IDX_HEADS * IDX_DIM
    gps = ATTN_GROUPS_PER_STEP
    qw = gps * rq * HEAD_DIM
    kw = gps * HEAD_DIM
    assert KV_HEADS % gps == 0 and lay["q"] % qw == 0 and lay["k"] % kw == 0 and lay["v"] % kw == 0
    kern = functools.partial(_attn_prompt_kernel, topk=topk, tk=tk, n_ssd_heads=d // SSD_HEADDIM, rq=rq, gps=gps)
    return pl.pallas_call(
        kern,
        grid=(nb, nqb, KV_HEADS // gps),
        in_specs=[
            pl.BlockSpec((Q_BLOCK, qiw), lambda b, i, g: (b * nqb + i, lay["qi"] // qiw)),
            pl.BlockSpec((Q_BLOCK, LANES), lambda b, i, g: (b * nqb + i, lay["tail"] // LANES + 1)),
            pl.BlockSpec((seq, IDX_DIM), lambda b, i, g: (b, lay["tail"] // IDX_DIM)),
            pl.BlockSpec((Q_BLOCK, qw), lambda b, i, g: (b * nqb + i, lay["q"] // qw + g)),
            pl.BlockSpec((seq, kw), lambda b, i, g: (b, lay["k"] // kw + g)),
            pl.BlockSpec((seq, kw), lambda b, i, g: (b, lay["v"] // kw + g)),
        ],
        out_specs=pl.BlockSpec((Q_BLOCK, qw), lambda b, i, g: (b * nqb + i, g)),
        out_shape=jax.ShapeDtypeStruct((nb * seq, d), F32),
        scratch_shapes=[pltpu.VMEM((seq, Q_BLOCK), I32), pltpu.VMEM((SUBLANES, Q_BLOCK), I32)],
        compiler_params=_cparams(("arbitrary", "arbitrary", "arbitrary")),
        name="attn_prompt",
    )(proj, proj, proj, proj, proj, proj)


def _head_rows(page_ref, head):
    return page_ref[pl.ds(head, PAGE_SIZE, stride=KV_HEADS), :]


def _fold_lane_groups(x):
    sh = LANES // 2
    while sh >= SAMPLE_PAD:
        x = x + pltpu.roll(x, sh, 1)
        sh //= 2
    return x


def _sample_scores_kernel(pt_ref, *refs, npages, topk, pps):
    kip_refs = refs[:pps]
    kp_refs = refs[pps:2 * pps]
    qi_ref, w_ref, qbd_ref, kinew_ref, knew_ref, p_ref, key_scr, kc_scr, s_scr, qbd_scr = refs[2 * pps:]
    j = pl.program_id(1)
    imin = jnp.int32(INT_MIN)
    nip = qi_ref.shape[0]
    nlp = qbd_ref.shape[0]
    groups_per_slab = LANES // SAMPLE_PAD
    lane = lax.broadcasted_iota(I32, (PAGE_SIZE, LANES), 1)
    lane_group = lane // SAMPLE_PAD

    @pl.when(j == 0)
    def _():
        qbd_scr[...] = (qbd_ref[...] * (HEAD_DIM ** -0.5)).astype(BF16)
        kc_scr[...] = jnp.full(kc_scr.shape, imin, I32)

    qi = qi_ref[...].astype(BF16)
    w_row = w_ref[...]

    def index_keys(kidx):
        x = _nt(kidx.astype(BF16), qi)
        r = jnp.maximum(x, 0.0) * w_row
        acc = r[:, :LANES]
        for c in range(1, nip // LANES):
            acc = acc + r[:, c * LANES:(c + 1) * LANES]
        return _float_key(_fold_lane_groups(acc) + 0.0)

    def put_page(page, keys, scores):
        rows = pl.ds(pl.multiple_of(page * PAGE_SIZE, PAGE_SIZE), PAGE_SIZE)
        key_scr[rows, :] = keys
        s_scr[rows, :] = scores
        slab = page // groups_per_slab
        kc_scr[slab] = jnp.where(lane_group == page % groups_per_slab, keys, kc_scr[slab])

    kidx_cat = jnp.concatenate([r[...] for r in kip_refs], axis=0)
    keys = index_keys(kidx_cat)
    k2d = jnp.concatenate(
        [jnp.concatenate([_head_rows(kp, h).astype(BF16) for h in range(KV_HEADS)], axis=1)
         for kp in kp_refs], axis=0)
    scores = _nt(k2d, qbd_scr[...])
    for i in range(pps):
        sl = slice(i * PAGE_SIZE, (i + 1) * PAGE_SIZE)
        put_page(j * pps + i, keys[sl], scores[sl])

    @pl.when(j == 0)
    def _():
        knew = index_keys(kinew_ref[...])
        srow = lax.broadcasted_iota(I32, (PAGE_SIZE, LANES), 0)
        knew = jnp.where(srow <= lane % SAMPLE_PAD, knew, imin)
        put_page(jnp.int32(npages), knew, _nt(knew_ref[...].astype(BF16), qbd_scr[...]))

    @pl.when(j == pl.num_programs(1) - 1)
    def _():
        kc = kc_scr[...]

        def count_ge(thr):
            ge = (kc >= thr).astype(I32)
            cnt = jnp.sum(ge.reshape(-1, SUBLANES, LANES), axis=0)
            return _fold_lane_groups(jnp.sum(cnt, axis=0, keepdims=True))

        thr = _bisect_kth(count_ge, topk, (1, LANES))
        reps = nlp // LANES
        thr_q = jnp.concatenate([thr] * reps, axis=1)

        def masked(page):
            rows = pl.ds(pl.multiple_of(page * PAGE_SIZE, PAGE_SIZE), PAGE_SIZE)
            key = jnp.concatenate([key_scr[rows, :]] * reps, axis=1)
            return rows, (key >= thr_q) & (key != imin)

        def max_body(page, m):
            rows, sel = masked(page)
            return jnp.maximum(m, jnp.max(jnp.where(sel, s_scr[rows, :], NEG_BIG), axis=0, keepdims=True))

        m = lax.fori_loop(0, npages + 1, max_body, jnp.full((1, nlp), NEG_BIG, F32))

        def exp_body(page, l):
            rows, sel = masked(page)
            e = jnp.where(sel, jnp.exp(s_scr[rows, :] - m), 0.0)
            s_scr[rows, :] = e
            return l + jnp.sum(e, axis=0, keepdims=True)

        l = lax.fori_loop(0, npages + 1, exp_body, jnp.zeros((1, nlp), F32))

        def out_body(page, carry):
            rows = pl.ds(pl.multiple_of(page * PAGE_SIZE, PAGE_SIZE), PAGE_SIZE)
            p_ref[page] = (s_scr[rows, :] / l).T.astype(BF16)
            return carry

        lax.fori_loop(0, npages + 1, out_body, 0)


def _sample_pv_kernel(pt_ref, *refs, pps, rq):
    vp_refs = refs[:pps]
    p_ref, pnew_ref, vnew_ref, o_ref, acc_scr = refs[pps:]
    j = pl.program_id(1)
    nrow = rq * SAMPLE_PAD

    @pl.when(j == 0)
    def _():
        pn = pnew_ref[0]
        vn = vnew_ref[...].astype(BF16)
        for g in range(KV_HEADS):
            acc_scr[g] = jnp.dot(pn[g * nrow:(g + 1) * nrow, :], vn[:, g * HEAD_DIM:(g + 1) * HEAD_DIM],
                                 preferred_element_type=F32)

    pcat = jnp.concatenate([p_ref[i] for i in range(pps)], axis=1)
    for g in range(KV_HEADS):
        vg = jnp.concatenate([_head_rows(vp, g).astype(BF16) for vp in vp_refs], axis=0)
        acc_scr[g] += jnp.dot(pcat[g * nrow:(g + 1) * nrow, :], vg, preferred_element_type=F32)

    @pl.when(j == pl.num_programs(1) - 1)
    def _():
        o_ref[...] = acc_scr[...]


def _attn_sample(q, qi, wi, ki_new, k_new, v_new, cache_k, cache_v, cache_kidx, page_table, *, t_valid):
    nb, npages = page_table.shape
    pps = min(PAGES_PER_STEP, npages)
    assert npages % pps == 0
    nsteps = npages // pps
    past = npages * PAGE_SIZE
    topk = min(IDX_TOPK, (past + t_valid) // 4)
    d = q.shape[-1]
    rq = d // HEAD_DIM // KV_HEADS
    kvw = KV_HEADS * HEAD_DIM
    nrow = rq * SAMPLE_PAD
    ck = cache_k.reshape(-1, PAGE_SIZE * KV_HEADS, HEAD_DIM)
    cv = cache_v.reshape(-1, PAGE_SIZE * KV_HEADS, HEAD_DIM)
    cki = cache_kidx.reshape(-1, PAGE_SIZE, IDX_DIM)

    def pad_to(a, axis, mult):
        n = a.shape[axis]
        widths = [(0, 0)] * a.ndim
        widths[axis] = (0, -(-n // mult) * mult - n)
        return jnp.pad(a, widths)

    qi_s = pad_to(qi.reshape(nb, SAMPLE_PAD, IDX_HEADS, IDX_DIM).transpose(0, 2, 1, 3)
                  .reshape(nb, IDX_HEADS * SAMPLE_PAD, IDX_DIM), 1, LANES)
    w_row = pad_to((wi * (IDX_HEADS ** -0.5 * IDX_DIM ** -0.5)).transpose(0, 2, 1)
                   .reshape(nb, 1, IDX_HEADS * SAMPLE_PAD), 2, LANES)
    nip = qi_s.shape[1]
    q_g = q.reshape(nb, SAMPLE_PAD, KV_HEADS, rq, HEAD_DIM).transpose(0, 2, 3, 1, 4).reshape(
        nb, KV_HEADS, nrow, HEAD_DIM)
    qbd = pad_to(jnp.einsum("bgrd,gh->bgrhd", q_g, jnp.eye(KV_HEADS, dtype=q.dtype))
                 .reshape(nb, KV_HEADS * nrow, kvw), 1, LANES)
    nlp = qbd.shape[1]
    pad_rows = ((0, 0), (0, PAGE_SIZE - SAMPLE_PAD), (0, 0))
    kinew_p = jnp.pad(ki_new, pad_rows)
    knew_p = jnp.pad(k_new, pad_rows)
    vnew_p = jnp.pad(v_new, pad_rows)
    npg = npages + 1
    nslab = -(-npg // (LANES // SAMPLE_PAD))

    def kidx_spec(i):
        return pl.BlockSpec((None, PAGE_SIZE, IDX_DIM), lambda b, j, pt, i=i: (pt[b, j * pps + i], 0, 0))

    def kv_specs():
        return [pl.BlockSpec((None, PAGE_SIZE * KV_HEADS, HEAD_DIM), lambda b, j, pt, i=i: (pt[b, j * pps + i], 0, 0))
                for i in range(pps)]

    probs = pl.pallas_call(
        functools.partial(_sample_scores_kernel, npages=npages, topk=topk, pps=pps),
        grid_spec=pltpu.PrefetchScalarGridSpec(
            num_scalar_prefetch=1,
            grid=(nb, nsteps),
            in_specs=[kidx_spec(i) for i in range(pps)] + kv_specs() + [
                pl.BlockSpec((None, nip, IDX_DIM), lambda b, j, pt: (b, 0, 0)),
                pl.BlockSpec((None, 1, nip), lambda b, j, pt: (b, 0, 0)),
                pl.BlockSpec((None, nlp, kvw), lambda b, j, pt: (b, 0, 0)),
                pl.BlockSpec((None, PAGE_SIZE, IDX_DIM), lambda b, j, pt: (b, 0, 0)),
                pl.BlockSpec((None, PAGE_SIZE, kvw), lambda b, j, pt: (b, 0, 0)),
            ],
            out_specs=pl.BlockSpec((None, npg, nlp, PAGE_SIZE), lambda b, j, pt: (b, 0, 0, 0)),
            scratch_shapes=[pltpu.VMEM((npg * PAGE_SIZE, LANES), I32),
                            pltpu.VMEM((nslab, PAGE_SIZE, LANES), I32),
                            pltpu.VMEM((npg * PAGE_SIZE, nlp), F32),
                            pltpu.VMEM((nlp, kvw), BF16)],
        ),
        out_shape=jax.ShapeDtypeStruct((nb, npg, nlp, PAGE_SIZE), BF16),
        compiler_params=_cparams(("arbitrary", "arbitrary")),
        name="sample_scores",
    )(page_table, *([cki] * pps), *([ck] * pps), qi_s, w_row, qbd, kinew_p, knew_p)

    o = pl.pallas_call(
        functools.partial(_sample_pv_kernel, pps=pps, rq=rq),
        grid_spec=pltpu.PrefetchScalarGridSpec(
            num_scalar_prefetch=1,
            grid=(nb, nsteps),
            in_specs=kv_specs() + [
                pl.BlockSpec((None, pps, nlp, PAGE_SIZE), lambda b, j, pt: (b, j, 0, 0)),
                pl.BlockSpec((None, 1, nlp, PAGE_SIZE), lambda b, j, pt: (b, npages, 0, 0)),
                pl.BlockSpec((None, PAGE_SIZE, kvw), lambda b, j, pt: (b, 0, 0)),
            ],
            out_specs=pl.BlockSpec((None, KV_HEADS, nrow, HEAD_DIM), lambda b, j, pt: (b, 0, 0, 0)),
            scratch_shapes=[pltpu.VMEM((KV_HEADS, nrow, HEAD_DIM), F32)],
        ),
        out_shape=jax.ShapeDtypeStruct((nb, KV_HEADS, nrow, HEAD_DIM), F32),
        compiler_params=_cparams(("arbitrary", "arbitrary")),
        name="sample_pv",
    )(page_table, *([cv] * pps), probs, probs, vnew_p)
    return o.reshape(nb, KV_HEADS, rq, SAMPLE_PAD, HEAD_DIM).transpose(0, 3, 1, 2, 4).reshape(nb * SAMPLE_PAD, d)


def _rms_cast_kernel(x_ref, g_ref, o_ref):
    o_ref[...] = _rms(x_ref[...], g_ref[...]).astype(BF16)


def _rms_cast(x, g, *, tm):
    m, d = x.shape
    return pl.pallas_call(
        _rms_cast_kernel,
        grid=(m // tm,),
        in_specs=[pl.BlockSpec((tm, d), lambda i: (i, 0)), pl.BlockSpec((1, d), lambda i: (0, 0))],
        out_specs=pl.BlockSpec((tm, d), lambda i: (i, 0)),
        out_shape=jax.ShapeDtypeStruct((m, d), BF16),
        compiler_params=_cparams(("arbitrary",)),
        name="rms_cast",
    )(x, g.reshape(1, d))


def _outproj_kernel(y_ref, a_ref, w1_ref, w2_ref, x_ref, gate_ref, o_ref):
    acc = jnp.dot(y_ref[...], w1_ref[...], preferred_element_type=F32)
    acc = acc + jnp.dot(a_ref[...], w2_ref[...], preferred_element_type=F32)
    o_ref[...] = x_ref[...] + gate_ref[...] * acc


def _outproj(y_n, attn_n, w1, w2, x, gate, *, tm, per_token, rows_per_batch):
    m, d = x.shape
    tn = PROJ_TN
    if per_token:
        gate_spec = pl.BlockSpec((tm, tn), lambda i, j: (i, j))
    else:
        tpb = rows_per_batch // tm
        gate_spec = pl.BlockSpec((None, 1, tn), lambda i, j: (i // tpb, 0, j))
    return pl.pallas_call(
        _outproj_kernel,
        grid=(m // tm, d // tn),
        in_specs=[pl.BlockSpec((tm, d), lambda i, j: (i, 0), pipeline_mode=pl.Buffered(1)),
                  pl.BlockSpec((tm, d), lambda i, j: (i, 0), pipeline_mode=pl.Buffered(1)),
                  pl.BlockSpec((d, tn), lambda i, j: (0, j)),
                  pl.BlockSpec((d, tn), lambda i, j: (0, j)),
                  pl.BlockSpec((tm, tn), lambda i, j: (i, j)),
                  gate_spec],
        out_specs=pl.BlockSpec((tm, tn), lambda i, j: (i, j)),
        out_shape=jax.ShapeDtypeStruct((m, d), F32),
        compiler_params=_cparams(("arbitrary", "arbitrary")),
        name="outproj",
    )(y_n, attn_n, w1, w2, x, gate)


def _peer_route_kernel(q_ref, k1_ref, k2_ref, s1_ref, s2_ref, tau_ref, cc_ref):
    q = q_ref[...]
    k1 = k1_ref[...].astype(BF16)
    k2 = k2_ref[...].astype(BF16)
    half = PEER_D_KEY // 2
    tm = q.shape[0]
    taus, ccs = [], []
    for h in range(PEER_HEADS):
        base = h * PEER_D_KEY
        s1 = _nt(k1, q[:, base:base + half].astype(BF16))
        s2 = _nt(k2, q[:, base + half:base + PEER_D_KEY].astype(BF16))
        s1_ref[h * PEER_KEYS:(h + 1) * PEER_KEYS, :] = s1
        s2_ref[h * PEER_KEYS:(h + 1) * PEER_KEYS, :] = s2

        def top_vals(x):
            vals = []
            for _ in range(PEER_TOPK):
                m = jnp.max(x, axis=0, keepdims=True)
                vals.append(m)
                x = jnp.where(x == m, -jnp.inf, x)
            return vals

        v1 = top_vals(s1)
        v2 = jnp.concatenate(top_vals(s2), axis=0)
        cand = jnp.concatenate([v + v2 for v in v1], axis=0) + 0.0
        ckey = _float_key(cand)

        def count_ge(thr, ckey=ckey):
            return jnp.sum((ckey >= thr).astype(I32), axis=0, keepdims=True)

        tau = _key_float(_bisect_kth(count_ge, PEER_TOPK, (1, tm)))
        cmax = cand[0:1, :]
        zsum = jnp.sum(jnp.where(cand >= tau, jnp.exp(cand - cmax), 0.0), axis=0, keepdims=True)
        taus.append(tau)
        ccs.append(cmax + jnp.log(zsum))
    tau_ref[...] = jnp.concatenate(taus, axis=0)
    cc_ref[...] = jnp.concatenate(ccs, axis=0)


def _peer_route(q, k1, k2, *, tm):
    t = q.shape[0]
    rows = PEER_HEADS * PEER_KEYS
    half = PEER_D_KEY // 2
    return pl.pallas_call(
        _peer_route_kernel,
        grid=(t // tm,),
        in_specs=[pl.BlockSpec((tm, PEER_HEADS * PEER_D_KEY), lambda i: (i, 0)),
                  pl.BlockSpec((PEER_KEYS, half), lambda i: (0, 0)),
                  pl.BlockSpec((PEER_KEYS, half), lambda i: (0, 0))],
        out_specs=[pl.BlockSpec((rows, tm), lambda i: (0, i)),
                   pl.BlockSpec((rows, tm), lambda i: (0, i)),
                   pl.BlockSpec((PEER_HEADS, tm), lambda i: (0, i)),
                   pl.BlockSpec((PEER_HEADS, tm), lambda i: (0, i))],
        out_shape=[jax.ShapeDtypeStruct((rows, t), F32), jax.ShapeDtypeStruct((rows, t), F32),
                   jax.ShapeDtypeStruct((PEER_HEADS, t), F32), jax.ShapeDtypeStruct((PEER_HEADS, t), F32)],
        compiler_params=_cparams(("arbitrary",)),
        name="peer_route",
    )(q, k1, k2)


def _peer_dense_kernel(xb_ref, s1_ref, s2_ref, tau_ref, cc_ref, u_ref, v_ref, o_ref, *, te):
    e = pl.program_id(1)

    @pl.when(e == 0)
    def _():
        o_ref[...] = jnp.zeros(o_ref.shape, F32)

    tm = xb_ref.shape[0]
    nsub = MXU_TILE // PEER_KEYS
    coefs = []
    for c in range(te // MXU_TILE):
        ut = _nt(xb_ref[...], u_ref[c * MXU_TILE:(c + 1) * MXU_TILE, :]).T
        for i in range(nsub):
            k = c * nsub + i
            gate = jnp.zeros((PEER_KEYS, tm), F32)
            for h in range(PEER_HEADS):
                sm = s1_ref[k, h:h + 1, :] + s2_ref[h * PEER_KEYS:(h + 1) * PEER_KEYS, :]
                gate = gate + jnp.where(sm >= tau_ref[h:h + 1, :], jnp.exp(sm - cc_ref[h:h + 1, :]), 0.0)
            coefs.append(gate * jax.nn.gelu(ut[i * PEER_KEYS:(i + 1) * PEER_KEYS, :]))
    coef = jnp.concatenate(coefs, axis=0).T.astype(BF16)
    for n in range(o_ref.shape[1] // MXU_TILE):
        cols = slice(n * MXU_TILE, (n + 1) * MXU_TILE)
        o_ref[:, cols] += jnp.dot(coef, v_ref[:, cols], preferred_element_type=F32)


def _peer_dense(xb, s1, s2, tau, cc, u_b, v_b, *, tm):
    t, d = xb.shape
    ne = u_b.shape[0]
    te = min(PEER_TE, ne)
    rows = PEER_HEADS * PEER_KEYS
    once = pl.Buffered(1)
    s1 = s1.reshape(PEER_HEADS, PEER_KEYS, t).transpose(1, 0, 2)
    return pl.pallas_call(
        functools.partial(_peer_dense_kernel, te=te),
        grid=(t // tm, ne // te),
        in_specs=[pl.BlockSpec((tm, d), lambda i, e: (i, 0), pipeline_mode=once),
                  pl.BlockSpec((te // PEER_KEYS, PEER_HEADS, tm), lambda i, e: (e, 0, i)),
                  pl.BlockSpec((rows, tm), lambda i, e: (0, i), pipeline_mode=once),
                  pl.BlockSpec((PEER_HEADS, tm), lambda i, e: (0, i), pipeline_mode=once),
                  pl.BlockSpec((PEER_HEADS, tm), lambda i, e: (0, i), pipeline_mode=once),
                  pl.BlockSpec((te, d), lambda i, e: (e, 0)),
                  pl.BlockSpec((te, d), lambda i, e: (e, 0))],
        out_specs=pl.BlockSpec((tm, d), lambda i, e: (i, 0), pipeline_mode=once),
        out_shape=jax.ShapeDtypeStruct((t, d), F32),
        compiler_params=_cparams(("arbitrary", "arbitrary")),
        name="peer_dense",
    )(xb, s1, s2, tau, cc, u_b, v_b)


def _final_kernel(x_ref, f_ref, gate_ref, g_ref, o_ref):
    o_ref[...] = _rms(x_ref[...] + gate_ref[...] * f_ref[...], g_ref[...])


def _final(x, f, gate, g, *, tm, per_token, rows_per_batch):
    m, d = x.shape
    if per_token:
        gate_spec = pl.BlockSpec((tm, d), lambda i: (i, 0))
    else:
        tpb = rows_per_batch // tm
        gate_spec = pl.BlockSpec((None, 1, d), lambda i: (i // tpb, 0, 0))
    return pl.pallas_call(
        _final_kernel,
        grid=(m // tm,),
        in_specs=[pl.BlockSpec((tm, d), lambda i: (i, 0)),
                  pl.BlockSpec((tm, d), lambda i: (i, 0)),
                  gate_spec,
                  pl.BlockSpec((1, d), lambda i: (0, 0))],
        out_specs=pl.BlockSpec((tm, d), lambda i: (i, 0)),
        out_shape=jax.ShapeDtypeStruct((m, d), F32),
        compiler_params=_cparams(("arbitrary",)),
        name="final_norm",
    )(x, f, gate, g.reshape(1, d))


def _layout(d):
    nh = d // SSD_HEADDIM
    kvw = KV_HEADS * HEAD_DIM
    qiw = IDX_HEADS * IDX_DIM
    xbcw = d + 2 * SSD_GROUPS * D_STATE
    lay, off = {}, 0
    for name, w in (("z", d), ("q", d), ("qi", qiw), ("xbc", xbcw), ("k", kvw), ("v", kvw), ("tail", PROJ_TN)):
        assert w % PROJ_TN == 0
        lay[name] = off
        off += w
    lay["total"] = off
    assert IDX_DIM + nh + IDX_HEADS <= PROJ_TN
    return lay


def _group_cols(xbc, d):
    lead = xbc.shape[:-1]
    gn = SSD_GROUPS * D_STATE
    x = xbc[..., :d].reshape(lead + (SSD_GROUPS, d // SSD_GROUPS))
    b = xbc[..., d:d + gn].reshape(lead + (SSD_GROUPS, D_STATE))
    c = xbc[..., d + gn:].reshape(lead + (SSD_GROUPS, D_STATE))
    return jnp.concatenate([x, b, c], axis=-1).reshape(lead + (d + 2 * gn,))


def _ungroup_cols(xg, d):
    lead = xg.shape[:-1]
    gn = SSD_GROUPS * D_STATE
    xsw = d // SSD_GROUPS
    g3 = xg.reshape(lead + (SSD_GROUPS, xsw + 2 * D_STATE))
    return jnp.concatenate([g3[..., :xsw].reshape(lead + (d,)),
                            g3[..., xsw:xsw + D_STATE].reshape(lead + (gn,)),
                            g3[..., xsw + D_STATE:].reshape(lead + (gn,))], axis=-1)


def _prep_w_in(w_in, d):
    nh = d // SSD_HEADDIM
    kvw = KV_HEADS * HEAD_DIM
    qiw = IDX_HEADS * IDX_DIM
    xbcw = d + 2 * SSD_GROUPS * D_STATE
    sizes = (d, xbcw, nh, d, kvw, kvw, qiw, IDX_HEADS, IDX_DIM)
    splits = np.cumsum(sizes)[:-1]
    z, xbc, dt, q, k, v, qi, wi, ki = [a.astype(BF16) for a in jnp.split(w_in, splits, axis=1)]
    pad = jnp.zeros((d, PROJ_TN - IDX_DIM - nh - IDX_HEADS), BF16)
    return jnp.concatenate([z, q, qi, _group_cols(xbc, d), k, v, ki, dt, wi, pad], axis=1)


def _group_major(vec, rpg):
    return jnp.pad(vec.reshape(SSD_GROUPS, 1, rpg), ((0, 0), (0, 0), (0, LANES - rpg)))


def _rope_tables(pos):
    half = HEAD_DIM // 8
    inv = ROPE_THETA ** (-jnp.arange(half, dtype=F32) / half)
    ang = pos.astype(F32)[:, None] * inv[None, :]
    cos, sin = jnp.cos(ang), jnp.sin(ang)
    n = pos.shape[0]
    rest = HEAD_DIM - 2 * half
    return (jnp.concatenate([cos, cos, jnp.ones((n, rest), F32)], axis=1),
            jnp.concatenate([-sin, sin, jnp.zeros((n, rest), F32)], axis=1))


def _layer(x2d, mods, pos, prm, *, nb, seq, per_token, tm, ssd_cfg, attn_fn, peer_tm):
    m, d = x2d.shape
    lay = prm["lay"]
    nh = d // SSD_HEADDIM
    rpg = nh // SSD_GROUPS
    sh1, sc1, g1, sh2, sc2, g2 = mods
    cos, sin = _rope_tables(pos)
    xoff = lay["xbc"]
    rope = (cos, sin, prm["kidx_norm_g"],
            ((lay["q"] // PROJ_TN, lay["xbc"] // PROJ_TN), (lay["k"] // PROJ_TN, lay["v"] // PROJ_TN)),
            lay["tail"] // PROJ_TN)
    tm_e = min(tm, ELEMWISE_TM)
    tm_p = min(m if per_token else seq, PROJ_TM)
    h1 = _modulate(x2d, prm["norm1_g"], sh1, sc1, tm=tm_e, per_token=per_token, rows_per_batch=seq)
    proj = _proj(h1, prm["w_in"], tm=tm_p, rope=rope)

    gw = rpg * SSD_HEADDIM + 2 * D_STATE
    dt_raw = proj[:, lay["tail"] + IDX_DIM: lay["tail"] + IDX_DIM + nh]
    y_ssd, ssm_new = ssd_cfg(proj, dt_raw, xoff // gw)
    y_n = _gate_norm(y_ssd, proj, prm["ssd_norm_g"], tm=tm_e)

    o_attn = attn_fn(proj)

    a_n = _rms_cast(o_attn, prm["attn_norm_g"], tm=tm_e)
    x1 = _outproj(y_n, a_n, prm["w_out1"], prm["w_out2"], x2d, g1, tm=tm_p, per_token=per_token,
                  rows_per_batch=seq)

    xb = _modulate(x1, prm["norm2_g"], sh2, sc2, tm=tm_e, per_token=per_token, rows_per_batch=seq)
    qp = _proj(xb, prm["peer_wq"], tm=tm_p)
    s1, s2, tau, cc = _peer_route(qp, prm["peer_k1"], prm["peer_k2"], tm=min(peer_tm, PEER_ROUTE_TM))
    ffn = _peer_dense(xb, s1, s2, tau, cc, prm["peer_u"], prm["peer_v"], tm=peer_tm)
    y = _final(x1, ffn, g2, prm["final_norm_g"], tm=tm_e, per_token=per_token, rows_per_batch=seq)
    return y, proj, ssm_new


def kernel(x_prompt, x_sample, c_prompt, c_sample, cache_k, cache_v, cache_kidx, state_ssm, state_conv, page_table, w_mod, b_mod, norm1_g, w_in, conv_w, conv_b, dt_bias, a_log, d_skip, ssd_norm_g, kidx_norm_g, attn_norm_g, w_out, norm2_g, peer_wq, peer_k1, peer_k2, peer_u, peer_v, final_norm_g):
    nbp, seq, d = x_prompt.shape
    nbs, tdec, _ = x_sample.shape
    depth = w_mod.shape[0]
    assert depth == 1
    nh = d // SSD_HEADDIM
    rpg = nh // SSD_GROUPS
    gw = rpg * SSD_HEADDIM + 2 * D_STATE
    lay = _layout(d)
    assert lay["xbc"] % gw == 0
    kvw = KV_HEADS * HEAD_DIM
    xbcw = d + 2 * SSD_GROUPS * D_STATE

    prm = dict(
        lay=lay,
        norm1_g=norm1_g[0], kidx_norm_g=kidx_norm_g[0], ssd_norm_g=ssd_norm_g[0], attn_norm_g=attn_norm_g[0],
        norm2_g=norm2_g[0], final_norm_g=final_norm_g,
        w_in=_prep_w_in(w_in[0], d),
        w_out1=w_out[0, :d].astype(BF16), w_out2=w_out[0, d:].astype(BF16),
        peer_wq=peer_wq[0].astype(BF16), peer_k1=peer_k1[0], peer_k2=peer_k2[0],
        peer_u=peer_u[0].astype(BF16), peer_v=peer_v[0].astype(BF16),
    )
    conv_w_g = _group_cols(conv_w[0], d)
    conv_b_g = _group_cols(conv_b[0], d).reshape(1, xbcw)
    dtb_g = _group_major(dt_bias[0], rpg)
    alog_g = _group_major(a_log[0], rpg)
    dsk_g = _group_major(d_skip[0], rpg)

    mod = _adaln(jnp.concatenate([c_prompt, c_sample], axis=0), w_mod[0], b_mod[0])
    mod_p = [a.reshape(nbp, 1, d) for a in jnp.split(mod[:nbp], 6, axis=-1)]
    mod_s = [jnp.repeat(a, SAMPLE_PAD, axis=0) for a in jnp.split(mod[nbp:], 6, axis=-1)]

    def dt_group_major(dt_raw):
        rows = dt_raw.shape[0]
        dtg = dt_raw.reshape(rows, SSD_GROUPS, rpg).transpose(1, 0, 2)
        return jnp.pad(dtg, ((0, 0), (0, 0), (0, LANES - rpg)))

    cs_p = min(SSD_CHUNK, seq)
    nc_p = seq // cs_p
    assert seq % cs_p == 0 and seq % Q_BLOCK == 0

    def ssd_prompt(proj, dt_raw, xoff_blocks):
        hist = jnp.zeros((nbp, SUBLANES, xbcw), F32)
        h0 = jnp.zeros((nbp, nh, SSD_HEADDIM, D_STATE), F32)
        return _ssd(proj, xoff_blocks, hist, conv_w_g, conv_b_g, dt_group_major(dt_raw), dtb_g, alog_g, dsk_g, h0,
                    nb=nbp, nc=nc_p, cs=cs_p, d=d, valid_len=cs_p)

    tm_p = min(512, seq)
    pos_p = jnp.tile(jnp.arange(seq), nbp)
    y_p, proj_p, ssm_p = _layer(
        x_prompt.reshape(nbp * seq, d), mod_p, pos_p, prm, nb=nbp, seq=seq, per_token=False, tm=tm_p,
        ssd_cfg=ssd_prompt, attn_fn=functools.partial(_attn_prompt, lay=lay, nb=nbp, seq=seq, d=d),
        peer_tm=min(PEER_TM, nbp * seq))

    npages = page_table.shape[1]
    past = npages * PAGE_SIZE
    ms = nbs * SAMPLE_PAD
    xs_pad = jnp.pad(x_sample, ((0, 0), (0, SAMPLE_PAD - tdec), (0, 0))).reshape(ms, d)
    pos_s = jnp.tile(past + jnp.arange(SAMPLE_PAD), nbs)

    def ssd_sample(proj, dt_raw, xoff_blocks):
        xbc = proj[:, lay["xbc"]:lay["xbc"] + xbcw].reshape(nbs, SAMPLE_PAD, xbcw)
        xbc = jnp.pad(xbc, ((0, 0), (0, SAMPLE_CHUNK - SAMPLE_PAD), (0, 0))).reshape(nbs * SAMPLE_CHUNK, xbcw)
        dtr = jnp.pad(dt_raw.reshape(nbs, SAMPLE_PAD, nh), ((0, 0), (0, SAMPLE_CHUNK - SAMPLE_PAD), (0, 0)))
        hist = jnp.pad(_group_cols(state_conv[0], d), ((0, 0), (SUBLANES - (CONV_W - 1), 0), (0, 0)))
        y, hnew = _ssd(xbc, 0, hist, conv_w_g, conv_b_g, dt_group_major(dtr.reshape(nbs * SAMPLE_CHUNK, nh)),
                       dtb_g, alog_g, dsk_g, state_ssm[0], nb=nbs, nc=1, cs=SAMPLE_CHUNK, d=d, valid_len=tdec)
        y = y.reshape(nbs, SAMPLE_CHUNK, d)[:, :SAMPLE_PAD].reshape(ms, d)
        return y, hnew

    def attn_sample(proj):
        def seg(name, w):
            return proj[:, lay[name]:lay[name] + w].reshape(nbs, SAMPLE_PAD, w)
        tail = lay["tail"]
        ki_new = proj[:, tail:tail + IDX_DIM].reshape(nbs, SAMPLE_PAD, IDX_DIM)
        wi = proj[:, tail + IDX_DIM + nh: tail + IDX_DIM + nh + IDX_HEADS].reshape(nbs, SAMPLE_PAD, IDX_HEADS)
        return _attn_sample(seg("q", d), seg("qi", IDX_HEADS * IDX_DIM), wi, ki_new, seg("k", kvw), seg("v", kvw),
                            cache_k, cache_v, cache_kidx, page_table, t_valid=tdec)

    y_s, proj_s, ssm_s = _layer(
        xs_pad, mod_s, pos_s, prm, nb=nbs, seq=SAMPLE_PAD, per_token=True, tm=ms,
        ssd_cfg=ssd_sample, attn_fn=attn_sample, peer_tm=ms)

    def states(proj, nb, rows, valid):
        p3 = proj.reshape(nb, rows, lay["total"])[:, :valid]
        k = p3[:, :, lay["k"]:lay["k"] + kvw].reshape(1, nb, valid, KV_HEADS, HEAD_DIM)
        v = p3[:, :, lay["v"]:lay["v"] + kvw].reshape(1, nb, valid, KV_HEADS, HEAD_DIM)
        ki = p3[:, :, lay["tail"]:lay["tail"] + IDX_DIM].reshape(1, nb, valid, IDX_DIM)
        xbc = _ungroup_cols(p3[:, valid - (CONV_W - 1):valid, lay["xbc"]:lay["xbc"] + xbcw], d)
        return k, v, ki, xbc.reshape(1, nb, CONV_W - 1, xbcw)

    k_p, v_p, ki_p, conv_p = states(proj_p, nbp, seq, seq)
    k_s, v_s, ki_s, conv_s = states(proj_s, nbs, SAMPLE_PAD, tdec)
    y_prompt = y_p.reshape(nbp, seq, d)
    y_sample = y_s.reshape(nbs, SAMPLE_PAD, d)[:, :tdec]
    return (y_prompt, y_sample, k_p, v_p, ki_p, ssm_p[None], conv_p,
            k_s, v_s, ki_s, ssm_s[None], conv_s)
```

```python
import functools
import math

import jax
import jax.numpy as jnp
import numpy as np
from jax import lax
from jax.experimental import pallas as pl
from jax.experimental.pallas import tpu as pltpu

F32 = jnp.float32
BF16 = jnp.bfloat16
I32 = jnp.int32

SSD_HEADDIM = 64
SSD_GROUPS = 8
D_STATE = 128
CONV_W = 4
SSD_CHUNK = 256
HEAD_DIM = 128
KV_HEADS = 8
ROPE_THETA = 500000.0
IDX_HEADS = 32
IDX_DIM = 128
IDX_TOPK = 256
Q_BLOCK = 128
PEER_HEADS = 8
PEER_KEYS = 128
PEER_TOPK = 16
PEER_D_KEY = 256
PAGE_SIZE = 128
EPS = 1e-6

LANES = 128
SUBLANES = 8
VMEM_LIMIT_BYTES = 56 * 1024 * 1024

MXU_TILE = 256
PROJ_TN = MXU_TILE
PROJ_TM = 1024
ELEMWISE_TM = 256
PEER_ROUTE_TM = 256
PEER_TM = 1024
PEER_TE = 512
SAMPLE_PAD = 8
SAMPLE_CHUNK = 128
ATTN_TK = 512
ATTN_GROUPS_PER_STEP = 2
PAGES_PER_STEP = 8
NEG_BIG = -1e30
M_INIT = -1e29
INT_MIN = -2 ** 31


def _cparams(sem):
    return pltpu.CompilerParams(dimension_semantics=sem, vmem_limit_bytes=VMEM_LIMIT_BYTES)


def _nt(a, b):
    return lax.dot_general(a, b, (((1,), (1,)), ((), ())), preferred_element_type=F32)


def _tn(a, b):
    return lax.dot_general(a, b, (((0,), (0,)), ((), ())), preferred_element_type=F32)


def _rms(x, g):
    return x * lax.rsqrt(jnp.mean(x * x, axis=-1, keepdims=True) + EPS) * g


def _rope_tile(a, cos, sin):
    half = HEAD_DIM // 8
    lane = lax.broadcasted_iota(I32, a.shape, 1)
    sw = jnp.where(lane < half, pltpu.roll(a, HEAD_DIM - half, 1), pltpu.roll(a, half, 1))
    return a * cos + sw * sin


def _float_key(x):
    bits = pltpu.bitcast(x, I32)
    return jnp.where(bits < 0, bits ^ jnp.int32(0x7FFFFFFF), bits)


def _key_float(key):
    bits = jnp.where(key < 0, key ^ jnp.int32(0x7FFFFFFF), key)
    return pltpu.bitcast(bits, F32)


def _bisect_kth(count_ge, k, shape):
    imin = jnp.int32(INT_MIN)

    def body(i, ans):
        cand_u = ans | lax.shift_left(jnp.int32(1), jnp.int32(31) - i)
        return jnp.where(count_ge(cand_u ^ imin) >= k, cand_u, ans)

    ans = lax.fori_loop(0, 32, body, jnp.zeros(shape, I32))
    return ans ^ imin


def _adaln_kernel(c_ref, w_ref, b_ref, o_ref):
    c = c_ref[...]
    a = (c * jax.nn.sigmoid(c)).astype(BF16)
    o_ref[...] = jnp.dot(a, w_ref[...].astype(BF16), preferred_element_type=F32) + b_ref[...]


def _adaln(c, w_mod, b_mod):
    n, d = c.shape
    npad = -(-n // SUBLANES) * SUBLANES
    cp = jnp.pad(c, ((0, npad - n), (0, 0)))
    nout = w_mod.shape[1]
    tn = PROJ_TN
    out = pl.pallas_call(
        _adaln_kernel,
        grid=(nout // tn,),
        in_specs=[pl.BlockSpec((npad, d), lambda j: (0, 0)),
                  pl.BlockSpec((d, tn), lambda j: (0, j)),
                  pl.BlockSpec((1, tn), lambda j: (0, j))],
        out_specs=pl.BlockSpec((npad, tn), lambda j: (0, j)),
        out_shape=jax.ShapeDtypeStruct((npad, nout), F32),
        compiler_params=_cparams(("arbitrary",)),
        name="adaln",
    )(cp, w_mod, b_mod.reshape(1, nout))
    return out[:n]


def _modulate_kernel(x_ref, g_ref, sh_ref, sc_ref, o_ref):
    o_ref[...] = (_rms(x_ref[...], g_ref[...]) * (1.0 + sc_ref[...]) + sh_ref[...]).astype(BF16)


def _modulate(x, g, shift, scale, *, tm, per_token, rows_per_batch):
    m, d = x.shape
    if per_token:
        mod_spec = pl.BlockSpec((tm, d), lambda i: (i, 0))
    else:
        tiles_per_batch = rows_per_batch // tm
        mod_spec = pl.BlockSpec((None, 1, d), lambda i: (i // tiles_per_batch, 0, 0))
    return pl.pallas_call(
        _modulate_kernel,
        grid=(m // tm,),
        in_specs=[pl.BlockSpec((tm, d), lambda i: (i, 0)),
                  pl.BlockSpec((1, d), lambda i: (0, 0)),
                  mod_spec, mod_spec],
        out_specs=pl.BlockSpec((tm, d), lambda i: (i, 0)),
        out_shape=jax.ShapeDtypeStruct((m, d), BF16),
        compiler_params=_cparams(("arbitrary",)),
        name="modulate",
    )(x, g.reshape(1, d), shift, scale)


def _proj_kernel(*refs, rope_ranges, tail_tile):
    if tail_tile is not None:
        h_ref, w_ref, cos_ref, sin_ref, kg_ref, o_ref = refs
    else:
        h_ref, w_ref, o_ref = refs
    j = pl.program_id(1)

    def product():
        return jnp.dot(h_ref[...], w_ref[...], preferred_element_type=F32)

    if tail_tile is None:
        o_ref[...] = product()
        return

    is_rope = functools.reduce(jnp.logical_or, [(j >= lo) & (j < hi) for lo, hi in rope_ranges])
    is_tail = j == tail_tile
    tn = w_ref.shape[1]

    @pl.when(jnp.logical_not(is_rope | is_tail))
    def _():
        o_ref[...] = product()

    @pl.when(is_rope)
    def _():
        acc = product()
        cos = cos_ref[...]
        sin = sin_ref[...]
        for t in range(tn // HEAD_DIM):
            sl = slice(t * HEAD_DIM, (t + 1) * HEAD_DIM)
            o_ref[:, sl] = _rope_tile(acc[:, sl], cos, sin)

    @pl.when(is_tail)
    def _():
        acc = product()
        ki = _rms(acc[:, :IDX_DIM], kg_ref[...])
        o_ref[:, :IDX_DIM] = _rope_tile(ki, cos_ref[...], sin_ref[...])
        o_ref[:, IDX_DIM:] = acc[:, IDX_DIM:]


def _proj(h, w, *, tm, rope=None):
    m, d = h.shape
    n = w.shape[1]
    tn = PROJ_TN
    assert m % tm == 0 and n % tn == 0
    in_specs = [pl.BlockSpec((tm, d), lambda i, j: (i, 0), pipeline_mode=pl.Buffered(1)),
                pl.BlockSpec((d, tn), lambda i, j: (0, j))]
    args = [h, w]
    rope_ranges, tail_tile = (), None
    if rope is not None:
        cos, sin, kidx_g, rope_ranges, tail_tile = rope
        in_specs += [pl.BlockSpec((tm, HEAD_DIM), lambda i, j: (i, 0)),
                     pl.BlockSpec((tm, HEAD_DIM), lambda i, j: (i, 0)),
                     pl.BlockSpec((1, IDX_DIM), lambda i, j: (0, 0))]
        args += [cos, sin, kidx_g.reshape(1, IDX_DIM)]
    return pl.pallas_call(
        functools.partial(_proj_kernel, rope_ranges=rope_ranges, tail_tile=tail_tile),
        grid=(m // tm, n // tn),
        in_specs=in_specs,
        out_specs=pl.BlockSpec((tm, tn), lambda i, j: (i, j)),
        out_shape=jax.ShapeDtypeStruct((m, n), F32),
        compiler_params=_cparams(("arbitrary", "arbitrary")),
        name="proj",
    )(*args)


def _ssd_kernel(x_ref, prev_ref, hist_ref, cw_ref, cb_ref, dt_ref, dtb_ref, alog_ref, dsk_ref, h0_ref,
                y_ref, hout_ref, h_scr, *, cs, rpg, valid_len, nc):
    c = pl.program_id(2)
    xs_w = rpg * SSD_HEADDIM

    @pl.when(c == 0)
    def _():
        h_scr[...] = h0_ref[...]

    prev = jnp.where(c == 0, hist_ref[...], prev_ref[...])
    cat = jnp.concatenate([prev, x_ref[...]], axis=0)
    w = cw_ref[...]
    acc = cb_ref[...]
    for j in range(CONV_W):
        lo = SUBLANES - (CONV_W - 1) + j
        acc = acc + cat[lo:lo + cs] * w[j:j + 1]
    xc = acc * jax.nn.sigmoid(acc)
    xs = xc[:, :xs_w]
    bm = xc[:, xs_w:xs_w + D_STATE]
    cm = xc[:, xs_w + D_STATE:]
    bm_b = bm.astype(BF16)
    cm_b = cm.astype(BF16)

    z = dt_ref[...] + dtb_ref[...]
    dt = jnp.maximum(z, 0.0) + jnp.log1p(jnp.exp(-jnp.abs(z)))
    if valid_len < cs:
        row = lax.broadcasted_iota(I32, dt.shape, 0)
        dt = jnp.where(row < valid_len, dt, 0.0)
    a_neg = -jnp.exp(alog_ref[...])
    ii = lax.broadcasted_iota(I32, (cs, cs), 0)
    jj = lax.broadcasted_iota(I32, (cs, cs), 1)
    causal = ii >= jj
    acs = jnp.dot(causal.astype(F32), dt * a_neg, preferred_element_type=F32,
                  precision=lax.Precision.HIGHEST)
    acs_t = acs.T
    a_last = acs[cs - 1:cs, :]
    cb = _nt(cm_b, bm_b)
    dsk = dsk_ref[...]

    ys = []
    for r in range(rpg):
        acol = acs[:, r:r + 1]
        arow = acs_t[r:r + 1, :]
        lm = jnp.exp(jnp.where(causal, acol - arow, -jnp.inf))
        xr = xs[:, r * SSD_HEADDIM:(r + 1) * SSD_HEADDIM]
        xd = xr * dt[:, r:r + 1]
        y = jnp.dot((cb * lm).astype(BF16), xd.astype(BF16), preferred_element_type=F32)
        hr = h_scr[r]
        y = y + jnp.exp(acol) * _nt(cm_b, hr.astype(BF16))
        al = a_last[:, r:r + 1]
        decay = jnp.exp(al - acol)
        h_scr[r] = hr * jnp.exp(al) + _tn((xd * decay).astype(BF16), bm_b)
        ys.append(y + xr * dsk[:, r:r + 1])
    y_ref[...] = jnp.concatenate(ys, axis=1)

    @pl.when(c == nc - 1)
    def _():
        hout_ref[...] = h_scr[...]


def _ssd(xg, xoff_blocks, hist8, conv_w_g, conv_b_g, dt_g, dtb_g, alog_g, dsk_g, h0, *, nb, nc, cs, d, valid_len):
    g_cnt = SSD_GROUPS
    rpg = d // SSD_HEADDIM // g_cnt
    gw = rpg * SSD_HEADDIM + 2 * D_STATE
    m = nb * nc * cs
    cpb = cs // SUBLANES
    kern = functools.partial(_ssd_kernel, cs=cs, rpg=rpg, valid_len=valid_len, nc=nc)
    y, hout = pl.pallas_call(
        kern,
        grid=(nb, g_cnt, nc),
        in_specs=[
            pl.BlockSpec((cs, gw), lambda b, g, c: (b * nc + c, xoff_blocks + g)),
            pl.BlockSpec((SUBLANES, gw), lambda b, g, c: (jnp.maximum((b * nc + c) * cpb - 1, 0), xoff_blocks + g)),
            pl.BlockSpec((None, SUBLANES, gw), lambda b, g, c: (b, 0, g)),
            pl.BlockSpec((CONV_W, gw), lambda b, g, c: (0, g)),
            pl.BlockSpec((1, gw), lambda b, g, c: (0, g)),
            pl.BlockSpec((None, cs, LANES), lambda b, g, c: (g, b * nc + c, 0)),
            pl.BlockSpec((None, 1, LANES), lambda b, g, c: (g, 0, 0)),
            pl.BlockSpec((None, 1, LANES), lambda b, g, c: (g, 0, 0)),
            pl.BlockSpec((None, 1, LANES), lambda b, g, c: (g, 0, 0)),
            pl.BlockSpec((None, rpg, SSD_HEADDIM, D_STATE), lambda b, g, c: (b, g, 0, 0)),
        ],
        out_specs=[
            pl.BlockSpec((cs, rpg * SSD_HEADDIM), lambda b, g, c: (b * nc + c, g)),
            pl.BlockSpec((None, rpg, SSD_HEADDIM, D_STATE), lambda b, g, c: (b, g, 0, 0)),
        ],
        out_shape=[jax.ShapeDtypeStruct((m, d), F32),
                   jax.ShapeDtypeStruct((nb, d // SSD_HEADDIM, SSD_HEADDIM, D_STATE), F32)],
        scratch_shapes=[pltpu.VMEM((rpg, SSD_HEADDIM, D_STATE), F32)],
        compiler_params=_cparams(("arbitrary", "arbitrary", "arbitrary")),
        name="ssd_scan",
    )(xg, xg, hist8, conv_w_g, conv_b_g, dt_g, dtb_g, alog_g, dsk_g, h0)
    return y, hout


def _gate_norm_kernel(y_ref, z_ref, g_ref, o_ref):
    z = z_ref[...]
    o_ref[...] = _rms(y_ref[...] * (z * jax.nn.sigmoid(z)), g_ref[...]).astype(BF16)


def _gate_norm(y, proj, g, *, tm):
    m, d = y.shape
    return pl.pallas_call(
        _gate_norm_kernel,
        grid=(m // tm,),
        in_specs=[pl.BlockSpec((tm, d), lambda i: (i, 0)),
                  pl.BlockSpec((tm, d), lambda i: (i, 0)),
                  pl.BlockSpec((1, d), lambda i: (0, 0))],
        out_specs=pl.BlockSpec((tm, d), lambda i: (i, 0)),
        out_shape=jax.ShapeDtypeStruct((m, d), BF16),
        compiler_params=_cparams(("arbitrary",)),
        name="gate_norm",
    )(y, proj, g.reshape(1, d))


def _attn_prompt_kernel(qi_ref, tail_ref, ki_ref, q_ref, k_ref, v_ref, o_ref, key_scr, thr_scr,
                        *, topk, tk, n_ssd_heads, rq, gps):
    qb = pl.program_id(1)
    g = pl.program_id(2)
    nkt = (qb * Q_BLOCK + Q_BLOCK + tk - 1) // tk
    imin = jnp.int32(INT_MIN)

    @pl.when(g == 0)
    def _():
        qi = qi_ref[...]
        qis = jnp.concatenate([qi[:, h * IDX_DIM:(h + 1) * IDX_DIM] for h in range(IDX_HEADS)],
                              axis=0).astype(BF16)
        w_t = tail_ref[...].T[n_ssd_heads:n_ssd_heads + IDX_HEADS, :] * (IDX_HEADS ** -0.5 * IDX_DIM ** -0.5)
        tpos = qb * Q_BLOCK + lax.broadcasted_iota(I32, (tk, Q_BLOCK), 1)
        kiota = lax.broadcasted_iota(I32, (tk, Q_BLOCK), 0)

        def score_tile(kt, carry):
            rows = pl.ds(pl.multiple_of(kt * tk, tk), tk)
            x = _nt(ki_ref[rows, :].astype(BF16), qis)
            sc = jnp.zeros((tk, Q_BLOCK), F32)
            for h in range(IDX_HEADS):
                sc = sc + jnp.maximum(x[:, h * Q_BLOCK:(h + 1) * Q_BLOCK], 0.0) * w_t[h:h + 1, :]
            key = _float_key(sc + 0.0)
            key_scr[rows, :] = jnp.where(kt * tk + kiota <= tpos, key, imin)
            return carry

        lax.fori_loop(0, nkt, score_tile, 0)

        def count_ge(thr):
            def body(kt, cnt):
                rows = pl.ds(pl.multiple_of(kt * tk, tk), tk)
                ge = (key_scr[rows, :] >= thr).astype(I32)
                return cnt + jnp.sum(ge.reshape(tk // SUBLANES, SUBLANES, Q_BLOCK), axis=0)
            cnt = lax.fori_loop(0, nkt, body, jnp.zeros((SUBLANES, Q_BLOCK), I32))
            return jnp.sum(cnt, axis=0, keepdims=True)

        thr = jnp.maximum(_bisect_kth(count_ge, topk, (1, Q_BLOCK)), jnp.int32(INT_MIN + 1))
        thr_scr[...] = jnp.broadcast_to(thr, (SUBLANES, Q_BLOCK))

    q = q_ref[...]
    nq = rq * Q_BLOCK
    qs = []
    for u in range(gps):
        heads = [q[:, (u * rq + r) * HEAD_DIM:(u * rq + r + 1) * HEAD_DIM] for r in range(rq)]
        qs.append((jnp.concatenate(heads, axis=0) * (HEAD_DIM ** -0.5)).astype(BF16))
    thr_q = jnp.concatenate([thr_scr[0:1, :]] * rq, axis=1)

    def attend(kt, carry):
        rows = pl.ds(pl.multiple_of(kt * tk, tk), tk)
        sel = jnp.concatenate([key_scr[rows, :]] * rq, axis=1) >= thr_q
        k_all = k_ref[rows, :]
        v_all = v_ref[rows, :]
        s_ts = [_nt(k_all[:, u * HEAD_DIM:(u + 1) * HEAD_DIM].astype(BF16), qs[u]) for u in range(gps)]
        v_ts = [v_all[:, u * HEAD_DIM:(u + 1) * HEAD_DIM].T.astype(BF16) for u in range(gps)]
        out = []
        for u in range(gps):
            m, l, acc = carry[u]
            s_t = jnp.where(sel, s_ts[u], NEG_BIG)
            m_new = jnp.maximum(m, jnp.max(s_t, axis=0, keepdims=True))
            alpha = jnp.exp(m - m_new)
            p = jnp.exp(s_t - m_new)
            l = l * alpha + jnp.sum(p, axis=0, keepdims=True)
            acc = acc * alpha + jnp.dot(v_ts[u], p.astype(BF16), preferred_element_type=F32)
            out.append((m_new, l, acc))
        return tuple(out)

    init = tuple((jnp.full((1, nq), M_INIT, F32), jnp.zeros((1, nq), F32), jnp.zeros((HEAD_DIM, nq), F32))
                 for _ in range(gps))
    res = lax.fori_loop(0, nkt, attend, init)
    for u in range(gps):
        _, l, acc = res[u]
        o = (acc / l).T
        for r in range(rq):
            c0 = (u * rq + r) * HEAD_DIM
            o_ref[:, c0:c0 + HEAD_DIM] = o[r * Q_BLOCK:(r + 1) * Q_BLOCK, :]


def _attn_prompt(proj, lay, *, nb, seq, d):
    rq = d // HEAD_DIM // KV_HEADS
    nqb = seq // Q_BLOCK
    topk = min(IDX_TOPK, seq // 4)
    tk = min(ATTN_TK, seq)
    qiw = IDX_HEADS * IDX_DIM
    gps = ATTN_GROUPS_PER_STEP
    qw = gps * rq * HEAD_DIM
    kw = gps * HEAD_DIM
    assert KV_HEADS % gps == 0 and lay["q"] % qw == 0 and lay["k"] % kw == 0 and lay["v"] % kw == 0
    kern = functools.partial(_attn_prompt_kernel, topk=topk, tk=tk, n_ssd_heads=d // SSD_HEADDIM, rq=rq, gps=gps)
    return pl.pallas_call(
        kern,
        grid=(nb, nqb, KV_HEADS // gps),
        in_specs=[
            pl.BlockSpec((Q_BLOCK, qiw), lambda b, i, g: (b * nqb + i, lay["qi"] // qiw)),
            pl.BlockSpec((Q_BLOCK, LANES), lambda b, i, g: (b * nqb + i, lay["tail"] // LANES + 1)),
            pl.BlockSpec((seq, IDX_DIM), lambda b, i, g: (b, lay["tail"] // IDX_DIM)),
            pl.BlockSpec((Q_BLOCK, qw), lambda b, i, g: (b * nqb + i, lay["q"] // qw + g)),
            pl.BlockSpec((seq, kw), lambda b, i, g: (b, lay["k"] // kw + g)),
            pl.BlockSpec((seq, kw), lambda b, i, g: (b, lay["v"] // kw + g)),
        ],
        out_specs=pl.BlockSpec((Q_BLOCK, qw), lambda b, i, g: (b * nqb + i, g)),
        out_shape=jax.ShapeDtypeStruct((nb * seq, d), F32),
        scratch_shapes=[pltpu.VMEM((seq, Q_BLOCK), I32), pltpu.VMEM((SUBLANES, Q_BLOCK), I32)],
        compiler_params=_cparams(("arbitrary", "arbitrary", "arbitrary")),
        name="attn_prompt",
    )(proj, proj, proj, proj, proj, proj)


def _head_rows(page_ref, head):
    return page_ref[pl.ds(head, PAGE_SIZE, stride=KV_HEADS), :]


def _fold_lane_groups(x):
    sh = LANES // 2
    while sh >= SAMPLE_PAD:
        x = x + pltpu.roll(x, sh, 1)
        sh //= 2
    return x


def _sample_scores_kernel(pt_ref, *refs, npages, topk, pps):
    kip_refs = refs[:pps]
    kp_refs = refs[pps:2 * pps]
    qi_ref, w_ref, qbd_ref, kinew_ref, knew_ref, p_ref, key_scr, kc_scr, s_scr, qbd_scr = refs[2 * pps:]
    j = pl.program_id(1)
    imin = jnp.int32(INT_MIN)
    nip = qi_ref.shape[0]
    nlp = qbd_ref.shape[0]
    groups_per_slab = LANES // SAMPLE_PAD
    lane = lax.broadcasted_iota(I32, (PAGE_SIZE, LANES), 1)
    lane_group = lane // SAMPLE_PAD

    @pl.when(j == 0)
    def _():
        qbd_scr[...] = (qbd_ref[...] * (HEAD_DIM ** -0.5)).astype(BF16)
        kc_scr[...] = jnp.full(kc_scr.shape, imin, I32)

    qi = qi_ref[...].astype(BF16)
    w_row = w_ref[...]

    def index_keys(kidx):
        x = _nt(kidx.astype(BF16), qi)
        r = jnp.maximum(x, 0.0) * w_row
        acc = r[:, :LANES]
        for c in range(1, nip // LANES):
            acc = acc + r[:, c * LANES:(c + 1) * LANES]
        return _float_key(_fold_lane_groups(acc) + 0.0)

    def put_page(page, keys, scores):
        rows = pl.ds(pl.multiple_of(page * PAGE_SIZE, PAGE_SIZE), PAGE_SIZE)
        key_scr[rows, :] = keys
        s_scr[rows, :] = scores
        slab = page // groups_per_slab
        kc_scr[slab] = jnp.where(lane_group == page % groups_per_slab, keys, kc_scr[slab])

    kidx_cat = jnp.concatenate([r[...] for r in kip_refs], axis=0)
    keys = index_keys(kidx_cat)
    k2d = jnp.concatenate(
        [jnp.concatenate([_head_rows(kp, h).astype(BF16) for h in range(KV_HEADS)], axis=1)
         for kp in kp_refs], axis=0)
    scores = _nt(k2d, qbd_scr[...])
    for i in range(pps):
        sl = slice(i * PAGE_SIZE, (i + 1) * PAGE_SIZE)
        put_page(j * pps + i, keys[sl], scores[sl])

    @pl.when(j == 0)
    def _():
        knew = index_keys(kinew_ref[...])
        srow = lax.broadcasted_iota(I32, (PAGE_SIZE, LANES), 0)
        knew = jnp.where(srow <= lane % SAMPLE_PAD, knew, imin)
        put_page(jnp.int32(npages), knew, _nt(knew_ref[...].astype(BF16), qbd_scr[...]))

    @pl.when(j == pl.num_programs(1) - 1)
    def _():
        kc = kc_scr[...]

        def count_ge(thr):
            ge = (kc >= thr).astype(I32)
            cnt = jnp.sum(ge.reshape(-1, SUBLANES, LANES), axis=0)
            return _fold_lane_groups(jnp.sum(cnt, axis=0, keepdims=True))

        thr = _bisect_kth(count_ge, topk, (1, LANES))
        reps = nlp // LANES
        thr_q = jnp.concatenate([thr] * reps, axis=1)

        def masked(page):
            rows = pl.ds(pl.multiple_of(page * PAGE_SIZE, PAGE_SIZE), PAGE_SIZE)
            key = jnp.concatenate([key_scr[rows, :]] * reps, axis=1)
            return rows, (key >= thr_q) & (key != imin)

        def max_body(page, m):
            rows, sel = masked(page)
            return jnp.maximum(m, jnp.max(jnp.where(sel, s_scr[rows, :], NEG_BIG), axis=0, keepdims=True))

        m = lax.fori_loop(0, npages + 1, max_body, jnp.full((1, nlp), NEG_BIG, F32))

        def exp_body(page, l):
            rows, sel = masked(page)
            e = jnp.where(sel, jnp.exp(s_scr[rows, :] - m), 0.0)
            s_scr[rows, :] = e
            return l + jnp.sum(e, axis=0, keepdims=True)

        l = lax.fori_loop(0, npages + 1, exp_body, jnp.zeros((1, nlp), F32))

        def out_body(page, carry):
            rows = pl.ds(pl.multiple_of(page * PAGE_SIZE, PAGE_SIZE), PAGE_SIZE)
            p_ref[page] = (s_scr[rows, :] / l).T.astype(BF16)
            return carry

        lax.fori_loop(0, npages + 1, out_body, 0)


def _sample_pv_kernel(pt_ref, *refs, pps, rq):
    vp_refs = refs[:pps]
    p_ref, pnew_ref, vnew_ref, o_ref, acc_scr = refs[pps:]
    j = pl.program_id(1)
    nrow = rq * SAMPLE_PAD

    @pl.when(j == 0)
    def _():
        pn = pnew_ref[0]
        vn = vnew_ref[...].astype(BF16)
        for g in range(KV_HEADS):
            acc_scr[g] = jnp.dot(pn[g * nrow:(g + 1) * nrow, :], vn[:, g * HEAD_DIM:(g + 1) * HEAD_DIM],
                                 preferred_element_type=F32)

    pcat = jnp.concatenate([p_ref[i] for i in range(pps)], axis=1)
    for g in range(KV_HEADS):
        vg = jnp.concatenate([_head_rows(vp, g).astype(BF16) for vp in vp_refs], axis=0)
        acc_scr[g] += jnp.dot(pcat[g * nrow:(g + 1) * nrow, :], vg, preferred_element_type=F32)

    @pl.when(j == pl.num_programs(1) - 1)
    def _():
        o_ref[...] = acc_scr[...]


def _attn_sample(q, qi, wi, ki_new, k_new, v_new, cache_k, cache_v, cache_kidx, page_table, *, t_valid):
    nb, npages = page_table.shape
    pps = min(PAGES_PER_STEP, npages)
    assert npages % pps == 0
    nsteps = npages // pps
    past = npages * PAGE_SIZE
    topk = min(IDX_TOPK, (past + t_valid) // 4)
    d = q.shape[-1]
    rq = d // HEAD_DIM // KV_HEADS
    kvw = KV_HEADS * HEAD_DIM
    nrow = rq * SAMPLE_PAD
    ck = cache_k.reshape(-1, PAGE_SIZE * KV_HEADS, HEAD_DIM)
    cv = cache_v.reshape(-1, PAGE_SIZE * KV_HEADS, HEAD_DIM)
    cki = cache_kidx.reshape(-1, PAGE_SIZE, IDX_DIM)

    def pad_to(a, axis, mult):
        n = a.shape[axis]
        widths = [(0, 0)] * a.ndim
        widths[axis] = (0, -(-n // mult) * mult - n)
        return jnp.pad(a, widths)

    qi_s = pad_to(qi.reshape(nb, SAMPLE_PAD, IDX_HEADS, IDX_DIM).transpose(0, 2, 1, 3)
                  .reshape(nb, IDX_HEADS * SAMPLE_PAD, IDX_DIM), 1, LANES)
    w_row = pad_to((wi * (IDX_HEADS ** -0.5 * IDX_DIM ** -0.5)).transpose(0, 2, 1)
                   .reshape(nb, 1, IDX_HEADS * SAMPLE_PAD), 2, LANES)
    nip = qi_s.shape[1]
    q_g = q.reshape(nb, SAMPLE_PAD, KV_HEADS, rq, HEAD_DIM).transpose(0, 2, 3, 1, 4).reshape(
        nb, KV_HEADS, nrow, HEAD_DIM)
    qbd = pad_to(jnp.einsum("bgrd,gh->bgrhd", q_g, jnp.eye(KV_HEADS, dtype=q.dtype))
                 .reshape(nb, KV_HEADS * nrow, kvw), 1, LANES)
    nlp = qbd.shape[1]
    pad_rows = ((0, 0), (0, PAGE_SIZE - SAMPLE_PAD), (0, 0))
    kinew_p = jnp.pad(ki_new, pad_rows)
    knew_p = jnp.pad(k_new, pad_rows)
    vnew_p = jnp.pad(v_new, pad_rows)
    npg = npages + 1
    nslab = -(-npg // (LANES // SAMPLE_PAD))

    def kidx_spec(i):
        return pl.BlockSpec((None, PAGE_SIZE, IDX_DIM), lambda b, j, pt, i=i: (pt[b, j * pps + i], 0, 0))

    def kv_specs():
        return [pl.BlockSpec((None, PAGE_SIZE * KV_HEADS, HEAD_DIM), lambda b, j, pt, i=i: (pt[b, j * pps + i], 0, 0))
                for i in range(pps)]

    probs = pl.pallas_call(
        functools.partial(_sample_scores_kernel, npages=npages, topk=topk, pps=pps),
        grid_spec=pltpu.PrefetchScalarGridSpec(
            num_scalar_prefetch=1,
            grid=(nb, nsteps),
            in_specs=[kidx_spec(i) for i in range(pps)] + kv_specs() + [
                pl.BlockSpec((None, nip, IDX_DIM), lambda b, j, pt: (b, 0, 0)),
                pl.BlockSpec((None, 1, nip), lambda b, j, pt: (b, 0, 0)),
                pl.BlockSpec((None, nlp, kvw), lambda b, j, pt: (b, 0, 0)),
                pl.BlockSpec((None, PAGE_SIZE, IDX_DIM), lambda b, j, pt: (b, 0, 0)),
                pl.BlockSpec((None, PAGE_SIZE, kvw), lambda b, j, pt: (b, 0, 0)),
            ],
            out_specs=pl.BlockSpec((None, npg, nlp, PAGE_SIZE), lambda b, j, pt: (b, 0, 0, 0)),
            scratch_shapes=[pltpu.VMEM((npg * PAGE_SIZE, LANES), I32),
                            pltpu.VMEM((nslab, PAGE_SIZE, LANES), I32),
                            pltpu.VMEM((npg * PAGE_SIZE, nlp), F32),
                            pltpu.VMEM((nlp, kvw), BF16)],
        ),
        out_shape=jax.ShapeDtypeStruct((nb, npg, nlp, PAGE_SIZE), BF16),
        compiler_params=_cparams(("arbitrary", "arbitrary")),
        name="sample_scores",
    )(page_table, *([cki] * pps), *([ck] * pps), qi_s, w_row, qbd, kinew_p, knew_p)

    o = pl.pallas_call(
        functools.partial(_sample_pv_kernel, pps=pps, rq=rq),
        grid_spec=pltpu.PrefetchScalarGridSpec(
            num_scalar_prefetch=1,
            grid=(nb, nsteps),
            in_specs=kv_specs() + [
                pl.BlockSpec((None, pps, nlp, PAGE_SIZE), lambda b, j, pt: (b, j, 0, 0)),
                pl.BlockSpec((None, 1, nlp, PAGE_SIZE), lambda b, j, pt: (b, npages, 0, 0)),
                pl.BlockSpec((None, PAGE_SIZE, kvw), lambda b, j, pt: (b, 0, 0)),
            ],
            out_specs=pl.BlockSpec((None, KV_HEADS, nrow, HEAD_DIM), lambda b, j, pt: (b, 0, 0, 0)),
            scratch_shapes=[pltpu.VMEM((KV_HEADS, nrow, HEAD_DIM), F32)],
        ),
        out_shape=jax.ShapeDtypeStruct((nb, KV_HEADS, nrow, HEAD_DIM), F32),
        compiler_params=_cparams(("arbitrary", "arbitrary")),
        name="sample_pv",
    )(page_table, *([cv] * pps), probs, probs, vnew_p)
    return o.reshape(nb, KV_HEADS, rq, SAMPLE_PAD, HEAD_DIM).transpose(0, 3, 1, 2, 4).reshape(nb * SAMPLE_PAD, d)


def _rms_cast_kernel(x_ref, g_ref, o_ref):
    o_ref[...] = _rms(x_ref[...], g_ref[...]).astype(BF16)


def _rms_cast(x, g, *, tm):
    m, d = x.shape
    return pl.pallas_call(
        _rms_cast_kernel,
        grid=(m // tm,),
        in_specs=[pl.BlockSpec((tm, d), lambda i: (i, 0)), pl.BlockSpec((1, d), lambda i: (0, 0))],
        out_specs=pl.BlockSpec((tm, d), lambda i: (i, 0)),
        out_shape=jax.ShapeDtypeStruct((m, d), BF16),
        compiler_params=_cparams(("arbitrary",)),
        name="rms_cast",
    )(x, g.reshape(1, d))


def _outproj_kernel(y_ref, a_ref, w1_ref, w2_ref, x_ref, gate_ref, o_ref):
    acc = jnp.dot(y_ref[...], w1_ref[...], preferred_element_type=F32)
    acc = acc + jnp.dot(a_ref[...], w2_ref[...], preferred_element_type=F32)
    o_ref[...] = x_ref[...] + gate_ref[...] * acc


def _outproj(y_n, attn_n, w1, w2, x, gate, *, tm, per_token, rows_per_batch):
    m, d = x.shape
    tn = PROJ_TN
    if per_token:
        gate_spec = pl.BlockSpec((tm, tn), lambda i, j: (i, j))
    else:
        tpb = rows_per_batch // tm
        gate_spec = pl.BlockSpec((None, 1, tn), lambda i, j: (i // tpb, 0, j))
    return pl.pallas_call(
        _outproj_kernel,
        grid=(m // tm, d // tn),
        in_specs=[pl.BlockSpec((tm, d), lambda i, j: (i, 0), pipeline_mode=pl.Buffered(1)),
                  pl.BlockSpec((tm, d), lambda i, j: (i, 0), pipeline_mode=pl.Buffered(1)),
                  pl.BlockSpec((d, tn), lambda i, j: (0, j)),
                  pl.BlockSpec((d, tn), lambda i, j: (0, j)),
                  pl.BlockSpec((tm, tn), lambda i, j: (i, j)),
                  gate_spec],
        out_specs=pl.BlockSpec((tm, tn), lambda i, j: (i, j)),
        out_shape=jax.ShapeDtypeStruct((m, d), F32),
        compiler_params=_cparams(("arbitrary", "arbitrary")),
        name="outproj",
    )(y_n, attn_n, w1, w2, x, gate)


def _peer_route_kernel(q_ref, k1_ref, k2_ref, s1_ref, s2_ref, tau_ref, cc_ref):
    q = q_ref[...]
    k1 = k1_ref[...].astype(BF16)
    k2 = k2_ref[...].astype(BF16)
    half = PEER_D_KEY // 2
    tm = q.shape[0]
    taus, ccs = [], []
    for h in range(PEER_HEADS):
        base = h * PEER_D_KEY
        s1 = _nt(k1, q[:, base:base + half].astype(BF16))
        s2 = _nt(k2, q[:, base + half:base + PEER_D_KEY].astype(BF16))
        s1_ref[h * PEER_KEYS:(h + 1) * PEER_KEYS, :] = s1
        s2_ref[h * PEER_KEYS:(h + 1) * PEER_KEYS, :] = s2

        def top_vals(x):
            vals = []
            for _ in range(PEER_TOPK):
                m = jnp.max(x, axis=0, keepdims=True)
                vals.append(m)
                x = jnp.where(x == m, -jnp.inf, x)
            return vals

        v1 = top_vals(s1)
        v2 = top_vals(s2)
        v1a = jnp.concatenate(v1, axis=0)
        v2a = jnp.concatenate(v2, axis=0)
        half_k = PEER_TOPK // 2
        cand = jnp.concatenate([v + v2a[:half_k] for v in v1[:half_k]]
                               + [v1[0] + v2a[half_k:], v1a[half_k:] + v2[0]], axis=0) + 0.0
        ckey = _float_key(cand)

        def count_ge(thr, ckey=ckey):
            return jnp.sum((ckey >= thr).astype(I32), axis=0, keepdims=True)

        tau = _key_float(_bisect_kth(count_ge, PEER_TOPK, (1, tm)))
        cmax = cand[0:1, :]
        zsum = jnp.sum(jnp.where(cand >= tau, jnp.exp(cand - cmax), 0.0), axis=0, keepdims=True)
        taus.append(tau)
        ccs.append(cmax + jnp.log(zsum))
    tau_ref[...] = jnp.concatenate(taus, axis=0)
    cc_ref[...] = jnp.concatenate(ccs, axis=0)


def _peer_route(q, k1, k2, *, tm):
    t = q.shape[0]
    rows = PEER_HEADS * PEER_KEYS
    half = PEER_D_KEY // 2
    return pl.pallas_call(
        _peer_route_kernel,
        grid=(t // tm,),
        in_specs=[pl.BlockSpec((tm, PEER_HEADS * PEER_D_KEY), lambda i: (i, 0)),
                  pl.BlockSpec((PEER_KEYS, half), lambda i: (0, 0)),
                  pl.BlockSpec((PEER_KEYS, half), lambda i: (0, 0))],
        out_specs=[pl.BlockSpec((rows, tm), lambda i: (0, i)),
                   pl.BlockSpec((rows, tm), lambda i: (0, i)),
                   pl.BlockSpec((PEER_HEADS, tm), lambda i: (0, i)),
                   pl.BlockSpec((PEER_HEADS, tm), lambda i: (0, i))],
        out_shape=[jax.ShapeDtypeStruct((rows, t), F32), jax.ShapeDtypeStruct((rows, t), F32),
                   jax.ShapeDtypeStruct((PEER_HEADS, t), F32), jax.ShapeDtypeStruct((PEER_HEADS, t), F32)],
        compiler_params=_cparams(("arbitrary",)),
        name="peer_route",
    )(q, k1, k2)


def _peer_dense_kernel(xb_ref, s1_ref, s2_ref, tau_ref, cc_ref, u_ref, v_ref, o_ref, *, te):
    e = pl.program_id(1)

    @pl.when(e == 0)
    def _():
        o_ref[...] = jnp.zeros(o_ref.shape, F32)

    tm = xb_ref.shape[0]
    nsub = MXU_TILE // PEER_KEYS
    coefs = []
    for c in range(te // MXU_TILE):
        ut = _nt(xb_ref[...], u_ref[c * MXU_TILE:(c + 1) * MXU_TILE, :]).T
        for i in range(nsub):
            k = c * nsub + i
            gate = jnp.zeros((PEER_KEYS, tm), F32)
            for h in range(PEER_HEADS):
                sm = s1_ref[k, h:h + 1, :] + s2_ref[h * PEER_KEYS:(h + 1) * PEER_KEYS, :]
                gate = gate + jnp.where(sm >= tau_ref[h:h + 1, :], jnp.exp(sm - cc_ref[h:h + 1, :]), 0.0)
            coefs.append(gate * jax.nn.gelu(ut[i * PEER_KEYS:(i + 1) * PEER_KEYS, :]))
    coef = jnp.concatenate(coefs, axis=0).T.astype(BF16)
    for n in range(o_ref.shape[1] // MXU_TILE):
        cols = slice(n * MXU_TILE, (n + 1) * MXU_TILE)
        o_ref[:, cols] += jnp.dot(coef, v_ref[:, cols], preferred_element_type=F32)


def _peer_dense(xb, s1, s2, tau, cc, u_b, v_b, *, tm):
    t, d = xb.shape
    ne = u_b.shape[0]
    te = min(PEER_TE, ne)
    rows = PEER_HEADS * PEER_KEYS
    once = pl.Buffered(1)
    s1 = s1.reshape(PEER_HEADS, PEER_KEYS, t).transpose(1, 0, 2)
    return pl.pallas_call(
        functools.partial(_peer_dense_kernel, te=te),
        grid=(t // tm, ne // te),
        in_specs=[pl.BlockSpec((tm, d), lambda i, e: (i, 0), pipeline_mode=once),
                  pl.BlockSpec((te // PEER_KEYS, PEER_HEADS, tm), lambda i, e: (e, 0, i)),
                  pl.BlockSpec((rows, tm), lambda i, e: (0, i), pipeline_mode=once),
                  pl.BlockSpec((PEER_HEADS, tm), lambda i, e: (0, i), pipeline_mode=once),
                  pl.BlockSpec((PEER_HEADS, tm), lambda i, e: (0, i), pipeline_mode=once),
                  pl.BlockSpec((te, d), lambda i, e: (e, 0)),
                  pl.BlockSpec((te, d), lambda i, e: (e, 0))],
        out_specs=pl.BlockSpec((tm, d), lambda i, e: (i, 0), pipeline_mode=once),
        out_shape=jax.ShapeDtypeStruct((t, d), F32),
        compiler_params=_cparams(("arbitrary", "arbitrary")),
        name="peer_dense",
    )(xb, s1, s2, tau, cc, u_b, v_b)


def _final_kernel(x_ref, f_ref, gate_ref, g_ref, o_ref):
    o_ref[...] = _rms(x_ref[...] + gate_ref[...] * f_ref[...], g_ref[...])


def _final(x, f, gate, g, *, tm, per_token, rows_per_batch):
    m, d = x.shape
    if per_token:
        gate_spec = pl.BlockSpec((tm, d), lambda i: (i, 0))
    else:
        tpb = rows_per_batch // tm
        gate_spec = pl.BlockSpec((None, 1, d), lambda i: (i // tpb, 0, 0))
    return pl.pallas_call(
        _final_kernel,
        grid=(m // tm,),
        in_specs=[pl.BlockSpec((tm, d), lambda i: (i, 0)),
                  pl.BlockSpec((tm, d), lambda i: (i, 0)),
                  gate_spec,
                  pl.BlockSpec((1, d), lambda i: (0, 0))],
        out_specs=pl.BlockSpec((tm, d), lambda i: (i, 0)),
        out_shape=jax.ShapeDtypeStruct((m, d), F32),
        compiler_params=_cparams(("arbitrary",)),
        name="final_norm",
    )(x, f, gate, g.reshape(1, d))


def _layout(d):
    nh = d // SSD_HEADDIM
    kvw = KV_HEADS * HEAD_DIM
    qiw = IDX_HEADS * IDX_DIM
    xbcw = d + 2 * SSD_GROUPS * D_STATE
    lay, off = {}, 0
    for name, w in (("z", d), ("q", d), ("qi", qiw), ("xbc", xbcw), ("k", kvw), ("v", kvw), ("tail", PROJ_TN)):
        assert w % PROJ_TN == 0
        lay[name] = off
        off += w
    lay["total"] = off
    assert IDX_DIM + nh + IDX_HEADS <= PROJ_TN
    return lay


def _group_cols(xbc, d):
    lead = xbc.shape[:-1]
    gn = SSD_GROUPS * D_STATE
    x = xbc[..., :d].reshape(lead + (SSD_GROUPS, d // SSD_GROUPS))
    b = xbc[..., d:d + gn].reshape(lead + (SSD_GROUPS, D_STATE))
    c = xbc[..., d + gn:].reshape(lead + (SSD_GROUPS, D_STATE))
    return jnp.concatenate([x, b, c], axis=-1).reshape(lead + (d + 2 * gn,))


def _ungroup_cols(xg, d):
    lead = xg.shape[:-1]
    gn = SSD_GROUPS * D_STATE
    xsw = d // SSD_GROUPS
    g3 = xg.reshape(lead + (SSD_GROUPS, xsw + 2 * D_STATE))
    return jnp.concatenate([g3[..., :xsw].reshape(lead + (d,)),
                            g3[..., xsw:xsw + D_STATE].reshape(lead + (gn,)),
                            g3[..., xsw + D_STATE:].reshape(lead + (gn,))], axis=-1)


def _prep_w_in(w_in, d):
    nh = d // SSD_HEADDIM
    kvw = KV_HEADS * HEAD_DIM
    qiw = IDX_HEADS * IDX_DIM
    xbcw = d + 2 * SSD_GROUPS * D_STATE
    sizes = (d, xbcw, nh, d, kvw, kvw, qiw, IDX_HEADS, IDX_DIM)
    splits = np.cumsum(sizes)[:-1]
    z, xbc, dt, q, k, v, qi, wi, ki = [a.astype(BF16) for a in jnp.split(w_in, splits, axis=1)]
    pad = jnp.zeros((d, PROJ_TN - IDX_DIM - nh - IDX_HEADS), BF16)
    return jnp.concatenate([z, q, qi, _group_cols(xbc, d), k, v, ki, dt, wi, pad], axis=1)


def _group_major(vec, rpg):
    return jnp.pad(vec.reshape(SSD_GROUPS, 1, rpg), ((0, 0), (0, 0), (0, LANES - rpg)))


def _rope_tables(pos):
    half = HEAD_DIM // 8
    inv = ROPE_THETA ** (-jnp.arange(half, dtype=F32) / half)
    ang = pos.astype(F32)[:, None] * inv[None, :]
    cos, sin = jnp.cos(ang), jnp.sin(ang)
    n = pos.shape[0]
    rest = HEAD_DIM - 2 * half
    return (jnp.concatenate([cos, cos, jnp.ones((n, rest), F32)], axis=1),
            jnp.concatenate([-sin, sin, jnp.zeros((n, rest), F32)], axis=1))


def _layer(x2d, mods, pos, prm, *, nb, seq, per_token, tm, ssd_cfg, attn_fn, peer_tm):
    m, d = x2d.shape
    lay = prm["lay"]
    nh = d // SSD_HEADDIM
    rpg = nh // SSD_GROUPS
    sh1, sc1, g1, sh2, sc2, g2 = mods
    cos, sin = _rope_tables(pos)
    xoff = lay["xbc"]
    rope = (cos, sin, prm["kidx_norm_g"],
            ((lay["q"] // PROJ_TN, lay["xbc"] // PROJ_TN), (lay["k"] // PROJ_TN, lay["v"] // PROJ_TN)),
            lay["tail"] // PROJ_TN)
    tm_e = min(tm, ELEMWISE_TM)
    tm_p = min(m if per_token else seq, PROJ_TM)
    h1 = _modulate(x2d, prm["norm1_g"], sh1, sc1, tm=tm_e, per_token=per_token, rows_per_batch=seq)
    proj = _proj(h1, prm["w_in"], tm=tm_p, rope=rope)

    gw = rpg * SSD_HEADDIM + 2 * D_STATE
    dt_raw = proj[:, lay["tail"] + IDX_DIM: lay["tail"] + IDX_DIM + nh]
    y_ssd, ssm_new = ssd_cfg(proj, dt_raw, xoff // gw)
    y_n = _gate_norm(y_ssd, proj, prm["ssd_norm_g"], tm=tm_e)

    o_attn = attn_fn(proj)

    a_n = _rms_cast(o_attn, prm["attn_norm_g"], tm=tm_e)
    x1 = _outproj(y_n, a_n, prm["w_out1"], prm["w_out2"], x2d, g1, tm=tm_p, per_token=per_token,
                  rows_per_batch=seq)

    xb = _modulate(x1, prm["norm2_g"], sh2, sc2, tm=tm_e, per_token=per_token, rows_per_batch=seq)
    qp = _proj(xb, prm["peer_wq"], tm=tm_p)
    s1, s2, tau, cc = _peer_route(qp, prm["peer_k1"], prm["peer_k2"], tm=min(peer_tm, PEER_ROUTE_TM))
    ffn = _peer_dense(xb, s1, s2, tau, cc, prm["peer_u"], prm["peer_v"], tm=peer_tm)
    y = _final(x1, ffn, g2, prm["final_norm_g"], tm=tm_e, per_token=per_token, rows_per_batch=seq)
    return y, proj, ssm_new


def kernel(x_prompt, x_sample, c_prompt, c_sample, cache_k, cache_v, cache_kidx, state_ssm, state_conv, page_table, w_mod, b_mod, norm1_g, w_in, conv_w, conv_b, dt_bias, a_log, d_skip, ssd_norm_g, kidx_norm_g, attn_norm_g, w_out, norm2_g, peer_wq, peer_k1, peer_k2, peer_u, peer_v, final_norm_g):
    nbp, seq, d = x_prompt.shape
    nbs, tdec, _ = x_sample.shape
    depth = w_mod.shape[0]
    assert depth == 1
    nh = d // SSD_HEADDIM
    rpg = nh // SSD_GROUPS
    gw = rpg * SSD_HEADDIM + 2 * D_STATE
    lay = _layout(d)
    assert lay["xbc"] % gw == 0
    kvw = KV_HEADS * HEAD_DIM
    xbcw = d + 2 * SSD_GROUPS * D_STATE

    prm = dict(
        lay=lay,
        norm1_g=norm1_g[0], kidx_norm_g=kidx_norm_g[0], ssd_norm_g=ssd_norm_g[0], attn_norm_g=attn_norm_g[0],
        norm2_g=norm2_g[0], final_norm_g=final_norm_g,
        w_in=_prep_w_in(w_in[0], d),
        w_out1=w_out[0, :d].astype(BF16), w_out2=w_out[0, d:].astype(BF16),
        peer_wq=peer_wq[0].astype(BF16), peer_k1=peer_k1[0], peer_k2=peer_k2[0],
        peer_u=peer_u[0].astype(BF16), peer_v=peer_v[0].astype(BF16),
    )
    conv_w_g = _group_cols(conv_w[0], d)
    conv_b_g = _group_cols(conv_b[0], d).reshape(1, xbcw)
    dtb_g = _group_major(dt_bias[0], rpg)
    alog_g = _group_major(a_log[0], rpg)
    dsk_g = _group_major(d_skip[0], rpg)

    mod = _adaln(jnp.concatenate([c_prompt, c_sample], axis=0), w_mod[0], b_mod[0])
    mod_p = [a.reshape(nbp, 1, d) for a in jnp.split(mod[:nbp], 6, axis=-1)]
    mod_s = [jnp.repeat(a, SAMPLE_PAD, axis=0) for a in jnp.split(mod[nbp:], 6, axis=-1)]

    def dt_group_major(dt_raw):
        rows = dt_raw.shape[0]
        dtg = dt_raw.reshape(rows, SSD_GROUPS, rpg).transpose(1, 0, 2)
        return jnp.pad(dtg, ((0, 0), (0, 0), (0, LANES - rpg)))

    cs_p = min(SSD_CHUNK, seq)
    nc_p = seq // cs_p
    assert seq % cs_p == 0 and seq % Q_BLOCK == 0

    def ssd_prompt(proj, dt_raw, xoff_blocks):
        hist = jnp.zeros((nbp, SUBLANES, xbcw), F32)
        h0 = jnp.zeros((nbp, nh, SSD_HEADDIM, D_STATE), F32)
        return _ssd(proj, xoff_blocks, hist, conv_w_g, conv_b_g, dt_group_major(dt_raw), dtb_g, alog_g, dsk_g, h0,
                    nb=nbp, nc=nc_p, cs=cs_p, d=d, valid_len=cs_p)

    tm_p = min(512, seq)
    pos_p = jnp.tile(jnp.arange(seq), nbp)
    y_p, proj_p, ssm_p = _layer(
        x_prompt.reshape(nbp * seq, d), mod_p, pos_p, prm, nb=nbp, seq=seq, per_token=False, tm=tm_p,
        ssd_cfg=ssd_prompt, attn_fn=functools.partial(_attn_prompt, lay=lay, nb=nbp, seq=seq, d=d),
        peer_tm=min(PEER_TM, nbp * seq))

    npages = page_table.shape[1]
    past = npages * PAGE_SIZE
    ms = nbs * SAMPLE_PAD
    xs_pad = jnp.pad(x_sample, ((0, 0), (0, SAMPLE_PAD - tdec), (0, 0))).reshape(ms, d)
    pos_s = jnp.tile(past + jnp.arange(SAMPLE_PAD), nbs)

    def ssd_sample(proj, dt_raw, xoff_blocks):
        xbc = proj[:, lay["xbc"]:lay["xbc"] + xbcw].reshape(nbs, SAMPLE_PAD, xbcw)
        xbc = jnp.pad(xbc, ((0, 0), (0, SAMPLE_CHUNK - SAMPLE_PAD), (0, 0))).reshape(nbs * SAMPLE_CHUNK, xbcw)
        dtr = jnp.pad(dt_raw.reshape(nbs, SAMPLE_PAD, nh), ((0, 0), (0, SAMPLE_CHUNK - SAMPLE_PAD), (0, 0)))
        hist = jnp.pad(_group_cols(state_conv[0], d), ((0, 0), (SUBLANES - (CONV_W - 1), 0), (0, 0)))
        y, hnew = _ssd(xbc, 0, hist, conv_w_g, conv_b_g, dt_group_major(dtr.reshape(nbs * SAMPLE_CHUNK, nh)),
                       dtb_g, alog_g, dsk_g, state_ssm[0], nb=nbs, nc=1, cs=SAMPLE_CHUNK, d=d, valid_len=tdec)
        y = y.reshape(nbs, SAMPLE_CHUNK, d)[:, :SAMPLE_PAD].reshape(ms, d)
        return y, hnew

    def attn_sample(proj):
        def seg(name, w):
            return proj[:, lay[name]:lay[name] + w].reshape(nbs, SAMPLE_PAD, w)
        tail = lay["tail"]
        ki_new = proj[:, tail:tail + IDX_DIM].reshape(nbs, SAMPLE_PAD, IDX_DIM)
        wi = proj[:, tail + IDX_DIM + nh: tail + IDX_DIM + nh + IDX_HEADS].reshape(nbs, SAMPLE_PAD, IDX_HEADS)
        return _attn_sample(seg("q", d), seg("qi", IDX_HEADS * IDX_DIM), wi, ki_new, seg("k", kvw), seg("v", kvw),
                            cache_k, cache_v, cache_kidx, page_table, t_valid=tdec)

    y_s, proj_s, ssm_s = _layer(
        xs_pad, mod_s, pos_s, prm, nb=nbs, seq=SAMPLE_PAD, per_token=True, tm=ms,
        ssd_cfg=ssd_sample, attn_fn=attn_sample, peer_tm=ms)

    def states(proj, nb, rows, valid):
        p3 = proj.reshape(nb, rows, lay["total"])[:, :valid]
        k = p3[:, :, lay["k"]:lay["k"] + kvw].reshape(1, nb, valid, KV_HEADS, HEAD_DIM)
        v = p3[:, :, lay["v"]:lay["v"] + kvw].reshape(1, nb, valid, KV_HEADS, HEAD_DIM)
        ki = p3[:, :, lay["tail"]:lay["tail"] + IDX_DIM].reshape(1, nb, valid, IDX_DIM)
        xbc = _ungroup_cols(p3[:, valid - (CONV_W - 1):valid, lay["xbc"]:lay["xbc"] + xbcw], d)
        return k, v, ki, xbc.reshape(1, nb, CONV_W - 1, xbcw)

    k_p, v_p, ki_p, conv_p = states(proj_p, nbp, seq, seq)
    k_s, v_s, ki_s, conv_s = states(proj_s, nbs, SAMPLE_PAD, tdec)
    y_prompt = y_p.reshape(nbp, seq, d)
    y_sample = y_s.reshape(nbs, SAMPLE_PAD, d)[:, :tdec]
    return (y_prompt, y_sample, k_p, v_p, ki_p, ssm_p[None], conv_p,
            k_s, v_s, ki_s, ssm_s[None], conv_s)
```

```python
import functools
import math

import jax
import jax.numpy as jnp
import numpy as np
from jax import lax
from jax.experimental import pallas as pl
from jax.experimental.pallas import tpu as pltpu

F32 = jnp.float32
BF16 = jnp.bfloat16
I32 = jnp.int32

SSD_HEADDIM = 64
SSD_GROUPS = 8
D_STATE = 128
CONV_W = 4
SSD_CHUNK = 256
HEAD_DIM = 128
KV_HEADS = 8
ROPE_THETA = 500000.0
IDX_HEADS = 32
IDX_DIM = 128
IDX_TOPK = 256
Q_BLOCK = 128
PEER_HEADS = 8
PEER_KEYS = 128
PEER_TOPK = 16
PEER_D_KEY = 256
PAGE_SIZE = 128
EPS = 1e-6

LANES = 128
SUBLANES = 8
VMEM_LIMIT_BYTES = 56 * 1024 * 1024

MXU_TILE = 256
PROJ_TN = MXU_TILE
PROJ_TM = 1024
ELEMWISE_TM = 256
CAST_ROWS = 512
PEER_ROUTE_TM = 256
PEER_TM = 1024
PEER_TE = 512
SAMPLE_PAD = 8
SAMPLE_CHUNK = 128
ATTN_TK = 512
ATTN_GROUPS_PER_STEP = 2
PAGES_PER_STEP = 8
NEG_BIG = -1e30
M_INIT = -1e29
INT_MIN = -2 ** 31


def _cparams(sem):
    return pltpu.CompilerParams(dimension_semantics=sem, vmem_limit_bytes=VMEM_LIMIT_BYTES)


def _nt(a, b):
    return lax.dot_general(a, b, (((1,), (1,)), ((), ())), preferred_element_type=F32)


def _tn(a, b):
    return lax.dot_general(a, b, (((0,), (0,)), ((), ())), preferred_element_type=F32)


def _rms(x, g):
    return x * lax.rsqrt(jnp.mean(x * x, axis=-1, keepdims=True) + EPS) * g


def _rope_tile(a, cos, sin):
    half = HEAD_DIM // 8
    lane = lax.broadcasted_iota(I32, a.shape, 1)
    sw = jnp.where(lane < half, pltpu.roll(a, HEAD_DIM - half, 1), pltpu.roll(a, half, 1))
    return a * cos + sw * sin


def _float_key(x):
    bits = pltpu.bitcast(x, I32)
    return jnp.where(bits < 0, bits ^ jnp.int32(0x7FFFFFFF), bits)


def _key_float(key):
    bits = jnp.where(key < 0, key ^ jnp.int32(0x7FFFFFFF), key)
    return pltpu.bitcast(bits, F32)


def _bisect_kth(count_ge, k, shape):
    imin = jnp.int32(INT_MIN)

    def body(i, ans):
        cand_u = ans | lax.shift_left(jnp.int32(1), jnp.int32(31) - i)
        return jnp.where(count_ge(cand_u ^ imin) >= k, cand_u, ans)

    ans = lax.fori_loop(0, 32, body, jnp.zeros(shape, I32))
    return ans ^ imin


def _cast_kernel(x_ref, o_ref):
    o_ref[...] = x_ref[...].astype(BF16)


def _cast_bf16(w):
    r, c = w.shape
    tr = min(r, CAST_ROWS)
    assert r % tr == 0
    return pl.pallas_call(
        _cast_kernel,
        grid=(r // tr,),
        in_specs=[pl.BlockSpec((tr, c), lambda i: (i, 0))],
        out_specs=pl.BlockSpec((tr, c), lambda i: (i, 0)),
        out_shape=jax.ShapeDtypeStruct((r, c), BF16),
        compiler_params=_cparams(("arbitrary",)),
        name="cast_bf16",
    )(w)


def _adaln_kernel(c_ref, w_ref, b_ref, o_ref):
    c = c_ref[...]
    a = (c * jax.nn.sigmoid(c)).astype(BF16)
    o_ref[...] = jnp.dot(a, w_ref[...].astype(BF16), preferred_element_type=F32) + b_ref[...]


def _adaln(c, w_mod, b_mod):
    n, d = c.shape
    npad = -(-n // SUBLANES) * SUBLANES
    cp = jnp.pad(c, ((0, npad - n), (0, 0)))
    nout = w_mod.shape[1]
    tn = PROJ_TN
    out = pl.pallas_call(
        _adaln_kernel,
        grid=(nout // tn,),
        in_specs=[pl.BlockSpec((npad, d), lambda j: (0, 0)),
                  pl.BlockSpec((d, tn), lambda j: (0, j)),
                  pl.BlockSpec((1, tn), lambda j: (0, j))],
        out_specs=pl.BlockSpec((npad, tn), lambda j: (0, j)),
        out_shape=jax.ShapeDtypeStruct((npad, nout), F32),
        compiler_params=_cparams(("arbitrary",)),
        name="adaln",
    )(cp, w_mod, b_mod.reshape(1, nout))
    return out[:n]


def _modulate_kernel(x_ref, g_ref, sh_ref, sc_ref, o_ref):
    o_ref[...] = (_rms(x_ref[...], g_ref[...]) * (1.0 + sc_ref[...]) + sh_ref[...]).astype(BF16)


def _modulate(x, g, shift, scale, *, tm, per_token, rows_per_batch):
    m, d = x.shape
    if per_token:
        mod_spec = pl.BlockSpec((tm, d), lambda i: (i, 0))
    else:
        tiles_per_batch = rows_per_batch // tm
        mod_spec = pl.BlockSpec((None, 1, d), lambda i: (i // tiles_per_batch, 0, 0))
    return pl.pallas_call(
        _modulate_kernel,
        grid=(m // tm,),
        in_specs=[pl.BlockSpec((tm, d), lambda i: (i, 0)),
                  pl.BlockSpec((1, d), lambda i: (0, 0)),
                  mod_spec, mod_spec],
        out_specs=pl.BlockSpec((tm, d), lambda i: (i, 0)),
        out_shape=jax.ShapeDtypeStruct((m, d), BF16),
        compiler_params=_cparams(("arbitrary",)),
        name="modulate",
    )(x, g.reshape(1, d), shift, scale)


def _proj_kernel(*refs, rope_ranges, tail_tile):
    if tail_tile is not None:
        h_ref, w_ref, cos_ref, sin_ref, kg_ref, o_ref = refs
    else:
        h_ref, w_ref, o_ref = refs
    j = pl.program_id(1)

    def product():
        return jnp.dot(h_ref[...], w_ref[...], preferred_element_type=F32)

    if tail_tile is None:
        o_ref[...] = product()
        return

    is_rope = functools.reduce(jnp.logical_or, [(j >= lo) & (j < hi) for lo, hi in rope_ranges])
    is_tail = j == tail_tile
    tn = w_ref.shape[1]

    @pl.when(jnp.logical_not(is_rope | is_tail))
    def _():
        o_ref[...] = product()

    @pl.when(is_rope)
    def _():
        acc = product()
        cos = cos_ref[...]
        sin = sin_ref[...]
        for t in range(tn // HEAD_DIM):
            sl = slice(t * HEAD_DIM, (t + 1) * HEAD_DIM)
            o_ref[:, sl] = _rope_tile(acc[:, sl], cos, sin)

    @pl.when(is_tail)
    def _():
        acc = product()
        ki = _rms(acc[:, :IDX_DIM], kg_ref[...])
        o_ref[:, :IDX_DIM] = _rope_tile(ki, cos_ref[...], sin_ref[...])
        o_ref[:, IDX_DIM:] = acc[:, IDX_DIM:]


def _proj(h, w, *, tm, rope=None):
    m, d = h.shape
    n = w.shape[1]
    tn = PROJ_TN
    assert m % tm == 0 and n % tn == 0
    in_specs = [pl.BlockSpec((tm, d), lambda i, j: (i, 0), pipeline_mode=pl.Buffered(1)),
                pl.BlockSpec((d, tn), lambda i, j: (0, j))]
    args = [h, w]
    rope_ranges, tail_tile = (), None
    if rope is not None:
        cos, sin, kidx_g, rope_ranges, tail_tile = rope
        in_specs += [pl.BlockSpec((tm, HEAD_DIM), lambda i, j: (i, 0)),
                     pl.BlockSpec((tm, HEAD_DIM), lambda i, j: (i, 0)),
                     pl.BlockSpec((1, IDX_DIM), lambda i, j: (0, 0))]
        args += [cos, sin, kidx_g.reshape(1, IDX_DIM)]
    return pl.pallas_call(
        functools.partial(_proj_kernel, rope_ranges=rope_ranges, tail_tile=tail_tile),
        grid=(m // tm, n // tn),
        in_specs=in_specs,
        out_specs=pl.BlockSpec((tm, tn), lambda i, j: (i, j)),
        out_shape=jax.ShapeDtypeStruct((m, n), F32),
        compiler_params=_cparams(("arbitrary", "arbitrary")),
        name="proj",
    )(*args)


def _ssd_kernel(x_ref, prev_ref, hist_ref, cw_ref, cb_ref, dt_ref, dtb_ref, alog_ref, dsk_ref, h0_ref,
                y_ref, hout_ref, h_scr, *, cs, rpg, valid_len, nc):
    c = pl.program_id(2)
    xs_w = rpg * SSD_HEADDIM

    @pl.when(c == 0)
    def _():
        h_scr[...] = h0_ref[...]

    prev = jnp.where(c == 0, hist_ref[...], prev_ref[...])
    cat = jnp.concatenate([prev, x_ref[...]], axis=0)
    w = cw_ref[...]
    acc = cb_ref[...]
    for j in range(CONV_W):
        lo = SUBLANES - (CONV_W - 1) + j
        acc = acc + cat[lo:lo + cs] * w[j:j + 1]
    xc = acc * jax.nn.sigmoid(acc)
    xs = xc[:, :xs_w]
    bm = xc[:, xs_w:xs_w + D_STATE]
    cm = xc[:, xs_w + D_STATE:]
    bm_b = bm.astype(BF16)
    cm_b = cm.astype(BF16)

    z = dt_ref[...] + dtb_ref[...]
    dt = jnp.maximum(z, 0.0) + jnp.log1p(jnp.exp(-jnp.abs(z)))
    if valid_len < cs:
        row = lax.broadcasted_iota(I32, dt.shape, 0)
        dt = jnp.where(row < valid_len, dt, 0.0)
    a_neg = -jnp.exp(alog_ref[...])
    ii = lax.broadcasted_iota(I32, (cs, cs), 0)
    jj = lax.broadcasted_iota(I32, (cs, cs), 1)
    causal = ii >= jj
    acs = jnp.dot(causal.astype(F32), dt * a_neg, preferred_element_type=F32,
                  precision=lax.Precision.HIGHEST)
    acs_t = acs.T
    a_last = acs[cs - 1:cs, :]
    cb = _nt(cm_b, bm_b)
    dsk = dsk_ref[...]

    first = lax.broadcasted_iota(I32, (cs, LANES), 1) < SSD_HEADDIM
    first_row = lax.broadcasted_iota(I32, (LANES, D_STATE), 0) < SSD_HEADDIM
    ys = []
    for pr in range(rpg // 2):
        r0, r1 = 2 * pr, 2 * pr + 1
        a0, a1 = acs[:, r0:r0 + 1], acs[:, r1:r1 + 1]
        xp = xs[:, pr * LANES:(pr + 1) * LANES]
        xd = xp * jnp.where(first, dt[:, r0:r0 + 1], dt[:, r1:r1 + 1])
        y = jnp.zeros((cs, LANES), F32)
        for r, keep in ((r0, first), (r1, jnp.logical_not(first))):
            lm = jnp.exp(jnp.where(causal, acs[:, r:r + 1] - acs_t[r:r + 1, :], -jnp.inf))
            y = y + jnp.dot((cb * lm).astype(BF16), jnp.where(keep, xd, 0.0).astype(BF16),
                            preferred_element_type=F32)
        hp = h_scr[pr]
        y = y + jnp.where(first, jnp.exp(a0), jnp.exp(a1)) * _nt(cm_b, hp.astype(BF16))
        al0, al1 = a_last[:, r0:r0 + 1], a_last[:, r1:r1 + 1]
        decay = jnp.where(first, jnp.exp(al0 - a0), jnp.exp(al1 - a1))
        h_scr[pr] = (hp * jnp.where(first_row, jnp.exp(al0), jnp.exp(al1))
                     + _tn((xd * decay).astype(BF16), bm_b))
        ys.append(y + xp * jnp.where(first[0:1, :], dsk[:, r0:r0 + 1], dsk[:, r1:r1 + 1]))
    y_ref[...] = jnp.concatenate(ys, axis=1)

    @pl.when(c == nc - 1)
    def _():
        hout_ref[...] = h_scr[...]


def _ssd(xg, xoff_blocks, hist8, conv_w_g, conv_b_g, dt_g, dtb_g, alog_g, dsk_g, h0, *, nb, nc, cs, d, valid_len):
    g_cnt = SSD_GROUPS
    rpg = d // SSD_HEADDIM // g_cnt
    gw = rpg * SSD_HEADDIM + 2 * D_STATE
    m = nb * nc * cs
    cpb = cs // SUBLANES
    nh = d // SSD_HEADDIM
    npair = rpg // 2
    assert rpg % 2 == 0 and 2 * SSD_HEADDIM == LANES
    kern = functools.partial(_ssd_kernel, cs=cs, rpg=rpg, valid_len=valid_len, nc=nc)
    y, hout = pl.pallas_call(
        kern,
        grid=(nb, g_cnt, nc),
        in_specs=[
            pl.BlockSpec((cs, gw), lambda b, g, c: (b * nc + c, xoff_blocks + g)),
            pl.BlockSpec((SUBLANES, gw), lambda b, g, c: (jnp.maximum((b * nc + c) * cpb - 1, 0), xoff_blocks + g)),
            pl.BlockSpec((None, SUBLANES, gw), lambda b, g, c: (b, 0, g)),
            pl.BlockSpec((CONV_W, gw), lambda b, g, c: (0, g)),
            pl.BlockSpec((1, gw), lambda b, g, c: (0, g)),
            pl.BlockSpec((None, cs, LANES), lambda b, g, c: (g, b * nc + c, 0)),
            pl.BlockSpec((None, 1, LANES), lambda b, g, c: (g, 0, 0)),
            pl.BlockSpec((None, 1, LANES), lambda b, g, c: (g, 0, 0)),
            pl.BlockSpec((None, 1, LANES), lambda b, g, c: (g, 0, 0)),
            pl.BlockSpec((None, npair, LANES, D_STATE), lambda b, g, c: (b, g, 0, 0)),
        ],
        out_specs=[
            pl.BlockSpec((cs, rpg * SSD_HEADDIM), lambda b, g, c: (b * nc + c, g)),
            pl.BlockSpec((None, npair, LANES, D_STATE), lambda b, g, c: (b, g, 0, 0)),
        ],
        out_shape=[jax.ShapeDtypeStruct((m, d), F32),
                   jax.ShapeDtypeStruct((nb, nh // 2, LANES, D_STATE), F32)],
        scratch_shapes=[pltpu.VMEM((npair, LANES, D_STATE), F32)],
        compiler_params=_cparams(("arbitrary", "arbitrary", "arbitrary")),
        name="ssd_scan",
    )(xg, xg, hist8, conv_w_g, conv_b_g, dt_g, dtb_g, alog_g, dsk_g, h0.reshape(nb, nh // 2, LANES, D_STATE))
    return y, hout.reshape(nb, nh, SSD_HEADDIM, D_STATE)


def _gate_norm_kernel(y_ref, z_ref, g_ref, o_ref):
    z = z_ref[...]
    o_ref[...] = _rms(y_ref[...] * (z * jax.nn.sigmoid(z)), g_ref[...]).astype(BF16)


def _gate_norm(y, proj, g, *, tm):
    m, d = y.shape
    return pl.pallas_call(
        _gate_norm_kernel,
        grid=(m // tm,),
        in_specs=[pl.BlockSpec((tm, d), lambda i: (i, 0)),
                  pl.BlockSpec((tm, d), lambda i: (i, 0)),
                  pl.BlockSpec((1, d), lambda i: (0, 0))],
        out_specs=pl.BlockSpec((tm, d), lambda i: (i, 0)),
        out_shape=jax.ShapeDtypeStruct((m, d), BF16),
        compiler_params=_cparams(("arbitrary",)),
        name="gate_norm",
    )(y, proj, g.reshape(1, d))


def _attn_prompt_kernel(qi_ref, tail_ref, ki_ref, q_ref, k_ref, v_ref, o_ref, key_scr, thr_scr,
                        *, topk, tk, n_ssd_heads, rq, gps):
    qb = pl.program_id(1)
    g = pl.program_id(2)
    nkt = (qb * Q_BLOCK + Q_BLOCK + tk - 1) // tk
    imin = jnp.int32(INT_MIN)

    @pl.when(g == 0)
    def _():
        qi = qi_ref[...]
        qis = jnp.concatenate([qi[:, h * IDX_DIM:(h + 1) * IDX_DIM] for h in range(IDX_HEADS)],
                              axis=0).astype(BF16)
        w_t = tail_ref[...].T[n_ssd_heads:n_ssd_heads + IDX_HEADS, :] * (IDX_HEADS ** -0.5 * IDX_DIM ** -0.5)
        tpos = qb * Q_BLOCK + lax.broadcasted_iota(I32, (tk, Q_BLOCK), 1)
        kiota = lax.broadcasted_iota(I32, (tk, Q_BLOCK), 0)

        def score_tile(kt, carry):
            rows = pl.ds(pl.multiple_of(kt * tk, tk), tk)
            x = _nt(ki_ref[rows, :].astype(BF16), qis)
            sc = jnp.zeros((tk, Q_BLOCK), F32)
            for h in range(IDX_HEADS):
                sc = sc + jnp.maximum(x[:, h * Q_BLOCK:(h + 1) * Q_BLOCK], 0.0) * w_t[h:h + 1, :]
            key = _float_key(sc + 0.0)
            key_scr[rows, :] = jnp.where(kt * tk + kiota <= tpos, key, imin)
            return carry

        lax.fori_loop(0, nkt, score_tile, 0)

        def count_ge(thr):
            def body(kt, cnt):
                rows = pl.ds(pl.multiple_of(kt * tk, tk), tk)
                ge = (key_scr[rows, :] >= thr).astype(I32)
                return cnt + jnp.sum(ge.reshape(tk // SUBLANES, SUBLANES, Q_BLOCK), axis=0)
            cnt = lax.fori_loop(0, nkt, body, jnp.zeros((SUBLANES, Q_BLOCK), I32))
            return jnp.sum(cnt, axis=0, keepdims=True)

        thr = jnp.maximum(_bisect_kth(count_ge, topk, (1, Q_BLOCK)), jnp.int32(INT_MIN + 1))
        thr_scr[...] = jnp.broadcast_to(thr, (SUBLANES, Q_BLOCK))

    q = q_ref[...]
    nq = rq * Q_BLOCK
    qs = []
    for u in range(gps):
        heads = [q[:, (u * rq + r) * HEAD_DIM:(u * rq + r + 1) * HEAD_DIM] for r in range(rq)]
        qs.append((jnp.concatenate(heads, axis=0) * (HEAD_DIM ** -0.5)).astype(BF16))
    thr_q = jnp.concatenate([thr_scr[0:1, :]] * rq, axis=1)

    def attend(kt, carry):
        rows = pl.ds(pl.multiple_of(kt * tk, tk), tk)
        sel = jnp.concatenate([key_scr[rows, :]] * rq, axis=1) >= thr_q
        k_all = k_ref[rows, :]
        v_all = v_ref[rows, :]
        s_ts = [_nt(k_all[:, u * HEAD_DIM:(u + 1) * HEAD_DIM].astype(BF16), qs[u]) for u in range(gps)]
        v_ts = [v_all[:, u * HEAD_DIM:(u + 1) * HEAD_DIM].T.astype(BF16) for u in range(gps)]
        out = []
        for u in range(gps):
            m, l, acc = carry[u]
            s_t = jnp.where(sel, s_ts[u], NEG_BIG)
            m_new = jnp.maximum(m, jnp.max(s_t, axis=0, keepdims=True))
            alpha = jnp.exp(m - m_new)
            p = jnp.exp(s_t - m_new)
            l = l * alpha + jnp.sum(p, axis=0, keepdims=True)
            acc = acc * alpha + jnp.dot(v_ts[u], p.astype(BF16), preferred_element_type=F32)
            out.append((m_new, l, acc))
        return tuple(out)

    init = tuple((jnp.full((1, nq), M_INIT, F32), jnp.zeros((1, nq), F32), jnp.zeros((HEAD_DIM, nq), F32))
                 for _ in range(gps))
    res = lax.fori_loop(0, nkt, attend, init)
    for u in range(gps):
        _, l, acc = res[u]
        o = (acc / l).T
        for r in range(rq):
            c0 = (u * rq + r) * HEAD_DIM
            o_ref[:, c0:c0 + HEAD_DIM] = o[r * Q_BLOCK:(r + 1) * Q_BLOCK, :]


def _attn_prompt(proj, lay, *, nb, seq, d):
    rq = d // HEAD_DIM // KV_HEADS
    nqb = seq // Q_BLOCK
    topk = min(IDX_TOPK, seq // 4)
    tk = min(ATTN_TK, seq)
    qiw = IDX_HEADS * IDX_DIM
    gps = ATTN_GROUPS_PER_STEP
    qw = gps * rq * HEAD_DIM
    kw = gps * HEAD_DIM
    assert KV_HEADS % gps == 0 and lay["q"] % qw == 0 and lay["k"] % kw == 0 and lay["v"] % kw == 0
    kern = functools.partial(_attn_prompt_kernel, topk=topk, tk=tk, n_ssd_heads=d // SSD_HEADDIM, rq=rq, gps=gps)
    return pl.pallas_call(
        kern,
        grid=(nb, nqb, KV_HEADS // gps),
        in_specs=[
            pl.BlockSpec((Q_BLOCK, qiw), lambda b, i, g: (b * nqb + i, lay["qi"] // qiw)),
            pl.BlockSpec((Q_BLOCK, LANES), lambda b, i, g: (b * nqb + i, lay["tail"] // LANES + 1)),
            pl.BlockSpec((seq, IDX_DIM), lambda b, i, g: (b, lay["tail"] // IDX_DIM)),
            pl.BlockSpec((Q_BLOCK, qw), lambda b, i, g: (b * nqb + i, lay["q"] // qw + g)),
            pl.BlockSpec((seq, kw), lambda b, i, g: (b, lay["k"] // kw + g)),
            pl.BlockSpec((seq, kw), lambda b, i, g: (b, lay["v"] // kw + g)),
        ],
        out_specs=pl.BlockSpec((Q_BLOCK, qw), lambda b, i, g: (b * nqb + i, g)),
        out_shape=jax.ShapeDtypeStruct((nb * seq, d), F32),
        scratch_shapes=[pltpu.VMEM((seq, Q_BLOCK), I32), pltpu.VMEM((SUBLANES, Q_BLOCK), I32)],
        compiler_params=_cparams(("arbitrary", "arbitrary", "arbitrary")),
        name="attn_prompt",
    )(proj, proj, proj, proj, proj, proj)


def _head_rows(page_ref, head):
    return page_ref[pl.ds(head, PAGE_SIZE, stride=KV_HEADS), :]


def _fold_lane_groups(x):
    sh = LANES // 2
    while sh >= SAMPLE_PAD:
        x = x + pltpu.roll(x, sh, 1)
        sh //= 2
    return x


def _sample_scores_kernel(pt_ref, *refs, npages, topk, pps):
    kip_refs = refs[:pps]
    kp_refs = refs[pps:2 * pps]
    qi_ref, w_ref, qbd_ref, kinew_ref, knew_ref, p_ref, key_scr, kc_scr, s_scr, qbd_scr = refs[2 * pps:]
    j = pl.program_id(1)
    imin = jnp.int32(INT_MIN)
    nip = qi_ref.shape[0]
    nlp = qbd_ref.shape[0]
    groups_per_slab = LANES // SAMPLE_PAD
    lane = lax.broadcasted_iota(I32, (PAGE_SIZE, LANES), 1)
    lane_group = lane // SAMPLE_PAD

    @pl.when(j == 0)
    def _():
        qbd_scr[...] = (qbd_ref[...] * (HEAD_DIM ** -0.5)).astype(BF16)
        kc_scr[...] = jnp.full(kc_scr.shape, imin, I32)

    qi = qi_ref[...].astype(BF16)
    w_row = w_ref[...]

    def index_keys(kidx):
        x = _nt(kidx.astype(BF16), qi)
        r = jnp.maximum(x, 0.0) * w_row
        acc = r[:, :LANES]
        for c in range(1, nip // LANES):
            acc = acc + r[:, c * LANES:(c + 1) * LANES]
        return _float_key(_fold_lane_groups(acc) + 0.0)

    def put_page(page, keys, scores):
        rows = pl.ds(pl.multiple_of(page * PAGE_SIZE, PAGE_SIZE), PAGE_SIZE)
        key_scr[rows, :] = keys
        s_scr[rows, :] = scores
        slab = page // groups_per_slab
        kc_scr[slab] = jnp.where(lane_group == page % groups_per_slab, keys, kc_scr[slab])

    kidx_cat = jnp.concatenate([r[...] for r in kip_refs], axis=0)
    keys = index_keys(kidx_cat)
    k2d = jnp.concatenate(
        [jnp.concatenate([_head_rows(kp, h).astype(BF16) for h in range(KV_HEADS)], axis=1)
         for kp in kp_refs], axis=0)
    scores = _nt(k2d, qbd_scr[...])
    for i in range(pps):
        sl = slice(i * PAGE_SIZE, (i + 1) * PAGE_SIZE)
        put_page(j * pps + i, keys[sl], scores[sl])

    @pl.when(j == 0)
    def _():
        knew = index_keys(kinew_ref[...])
        srow = lax.broadcasted_iota(I32, (PAGE_SIZE, LANES), 0)
        knew = jnp.where(srow <= lane % SAMPLE_PAD, knew, imin)
        put_page(jnp.int32(npages), knew, _nt(knew_ref[...].astype(BF16), qbd_scr[...]))

    @pl.when(j == pl.num_programs(1) - 1)
    def _():
        kc = kc_scr[...]

        tok = lane[0:1, :] % SAMPLE_PAD

        def count_ge(thr):
            ge = (kc >= thr).astype(F32)
            cnt = jnp.sum(jnp.sum(ge.reshape(-1, SUBLANES, LANES), axis=0), axis=0, keepdims=True)
            tot = jnp.zeros((1, LANES), F32)
            for t in range(SAMPLE_PAD):
                tot = jnp.where(tok == t, jnp.sum(jnp.where(tok == t, cnt, 0.0), axis=1, keepdims=True), tot)
            return tot

        thr = _bisect_kth(count_ge, topk, (1, LANES))
        reps = nlp // LANES
        thr_q = jnp.concatenate([thr] * reps, axis=1)

        def masked(page):
            rows = pl.ds(pl.multiple_of(page * PAGE_SIZE, PAGE_SIZE), PAGE_SIZE)
            key = jnp.concatenate([key_scr[rows, :]] * reps, axis=1)
            return rows, (key >= thr_q) & (key != imin)

        def max_body(page, m):
            rows, sel = masked(page)
            return jnp.maximum(m, jnp.max(jnp.where(sel, s_scr[rows, :], NEG_BIG), axis=0, keepdims=True))

        m = lax.fori_loop(0, npages + 1, max_body, jnp.full((1, nlp), NEG_BIG, F32))

        def exp_body(page, l):
            rows, sel = masked(page)
            e = jnp.where(sel, jnp.exp(s_scr[rows, :] - m), 0.0)
            s_scr[rows, :] = e
            return l + jnp.sum(e, axis=0, keepdims=True)

        l = lax.fori_loop(0, npages + 1, exp_body, jnp.zeros((1, nlp), F32))

        def out_body(page, carry):
            rows = pl.ds(pl.multiple_of(page * PAGE_SIZE, PAGE_SIZE), PAGE_SIZE)
            p_ref[page] = (s_scr[rows, :] / l).T.astype(BF16)
            return carry

        lax.fori_loop(0, npages + 1, out_body, 0)


def _sample_pv_kernel(pt_ref, *refs, pps, rq):
    vp_refs = refs[:pps]
    p_ref, pnew_ref, vnew_ref, o_ref, acc_scr = refs[pps:]
    j = pl.program_id(1)
    nrow = rq * SAMPLE_PAD

    @pl.when(j == 0)
    def _():
        pn = pnew_ref[0]
        vn = vnew_ref[...].astype(BF16)
        for g in range(KV_HEADS):
            acc_scr[g] = jnp.dot(pn[g * nrow:(g + 1) * nrow, :], vn[:, g * HEAD_DIM:(g + 1) * HEAD_DIM],
                                 preferred_element_type=F32)

    pcat = jnp.concatenate([p_ref[i] for i in range(pps)], axis=1)
    for g in range(KV_HEADS):
        vg = jnp.concatenate([_head_rows(vp, g).astype(BF16) for vp in vp_refs], axis=0)
        acc_scr[g] += jnp.dot(pcat[g * nrow:(g + 1) * nrow, :], vg, preferred_element_type=F32)

    @pl.when(j == pl.num_programs(1) - 1)
    def _():
        o_ref[...] = acc_scr[...]


def _attn_sample(q, qi, wi, ki_new, k_new, v_new, cache_k, cache_v, cache_kidx, page_table, *, t_valid):
    nb, npages = page_table.shape
    pps = min(PAGES_PER_STEP, npages)
    assert npages % pps == 0
    nsteps = npages // pps
    past = npages * PAGE_SIZE
    topk = min(IDX_TOPK, (past + t_valid) // 4)
    d = q.shape[-1]
    rq = d // HEAD_DIM // KV_HEADS
    kvw = KV_HEADS * HEAD_DIM
    nrow = rq * SAMPLE_PAD
    ck = cache_k.reshape(-1, PAGE_SIZE * KV_HEADS, HEAD_DIM)
    cv = cache_v.reshape(-1, PAGE_SIZE * KV_HEADS, HEAD_DIM)
    cki = cache_kidx.reshape(-1, PAGE_SIZE, IDX_DIM)

    def pad_to(a, axis, mult):
        n = a.shape[axis]
        widths = [(0, 0)] * a.ndim
        widths[axis] = (0, -(-n // mult) * mult - n)
        return jnp.pad(a, widths)

    qi_s = pad_to(qi.reshape(nb, SAMPLE_PAD, IDX_HEADS, IDX_DIM).transpose(0, 2, 1, 3)
                  .reshape(nb, IDX_HEADS * SAMPLE_PAD, IDX_DIM), 1, LANES)
    w_row = pad_to((wi * (IDX_HEADS ** -0.5 * IDX_DIM ** -0.5)).transpose(0, 2, 1)
                   .reshape(nb, 1, IDX_HEADS * SAMPLE_PAD), 2, LANES)
    nip = qi_s.shape[1]
    q_g = q.reshape(nb, SAMPLE_PAD, KV_HEADS, rq, HEAD_DIM).transpose(0, 2, 3, 1, 4).reshape(
        nb, KV_HEADS, nrow, HEAD_DIM)
    qbd = pad_to(jnp.einsum("bgrd,gh->bgrhd", q_g, jnp.eye(KV_HEADS, dtype=q.dtype))
                 .reshape(nb, KV_HEADS * nrow, kvw), 1, LANES)
    nlp = qbd.shape[1]
    pad_rows = ((0, 0), (0, PAGE_SIZE - SAMPLE_PAD), (0, 0))
    kinew_p = jnp.pad(ki_new, pad_rows)
    knew_p = jnp.pad(k_new, pad_rows)
    vnew_p = jnp.pad(v_new, pad_rows)
    npg = npages + 1
    nslab = -(-npg // (LANES // SAMPLE_PAD))

    def kidx_spec(i):
        return pl.BlockSpec((None, PAGE_SIZE, IDX_DIM), lambda b, j, pt, i=i: (pt[b, j * pps + i], 0, 0))

    def kv_specs():
        return [pl.BlockSpec((None, PAGE_SIZE * KV_HEADS, HEAD_DIM), lambda b, j, pt, i=i: (pt[b, j * pps + i], 0, 0))
                for i in range(pps)]

    probs = pl.pallas_call(
        functools.partial(_sample_scores_kernel, npages=npages, topk=topk, pps=pps),
        grid_spec=pltpu.PrefetchScalarGridSpec(
            num_scalar_prefetch=1,
            grid=(nb, nsteps),
            in_specs=[kidx_spec(i) for i in range(pps)] + kv_specs() + [
                pl.BlockSpec((None, nip, IDX_DIM), lambda b, j, pt: (b, 0, 0)),
                pl.BlockSpec((None, 1, nip), lambda b, j, pt: (b, 0, 0)),
                pl.BlockSpec((None, nlp, kvw), lambda b, j, pt: (b, 0, 0)),
                pl.BlockSpec((None, PAGE_SIZE, IDX_DIM), lambda b, j, pt: (b, 0, 0)),
                pl.BlockSpec((None, PAGE_SIZE, kvw), lambda b, j, pt: (b, 0, 0)),
            ],
            out_specs=pl.BlockSpec((None, npg, nlp, PAGE_SIZE), lambda b, j, pt: (b, 0, 0, 0)),
            scratch_shapes=[pltpu.VMEM((npg * PAGE_SIZE, LANES), I32),
                            pltpu.VMEM((nslab, PAGE_SIZE, LANES), I32),
                            pltpu.VMEM((npg * PAGE_SIZE, nlp), F32),
                            pltpu.VMEM((nlp, kvw), BF16)],
        ),
        out_shape=jax.ShapeDtypeStruct((nb, npg, nlp, PAGE_SIZE), BF16),
        compiler_params=_cparams(("arbitrary", "arbitrary")),
        name="sample_scores",
    )(page_table, *([cki] * pps), *([ck] * pps), qi_s, w_row, qbd, kinew_p, knew_p)

    o = pl.pallas_call(
        functools.partial(_sample_pv_kernel, pps=pps, rq=rq),
        grid_spec=pltpu.PrefetchScalarGridSpec(
            num_scalar_prefetch=1,
            grid=(nb, nsteps),
            in_specs=kv_specs() + [
                pl.BlockSpec((None, pps, nlp, PAGE_SIZE), lambda b, j, pt: (b, j, 0, 0)),
                pl.BlockSpec((None, 1, nlp, PAGE_SIZE), lambda b, j, pt: (b, npages, 0, 0)),
                pl.BlockSpec((None, PAGE_SIZE, kvw), lambda b, j, pt: (b, 0, 0)),
            ],
            out_specs=pl.BlockSpec((None, KV_HEADS, nrow, HEAD_DIM), lambda b, j, pt: (b, 0, 0, 0)),
            scratch_shapes=[pltpu.VMEM((KV_HEADS, nrow, HEAD_DIM), F32)],
        ),
        out_shape=jax.ShapeDtypeStruct((nb, KV_HEADS, nrow, HEAD_DIM), F32),
        compiler_params=_cparams(("arbitrary", "arbitrary")),
        name="sample_pv",
    )(page_table, *([cv] * pps), probs, probs, vnew_p)
    return o.reshape(nb, KV_HEADS, rq, SAMPLE_PAD, HEAD_DIM).transpose(0, 3, 1, 2, 4).reshape(nb * SAMPLE_PAD, d)


def _rms_cast_kernel(x_ref, g_ref, o_ref):
    o_ref[...] = _rms(x_ref[...], g_ref[...]).astype(BF16)


def _rms_cast(x, g, *, tm):
    m, d = x.shape
    return pl.pallas_call(
        _rms_cast_kernel,
        grid=(m // tm,),
        in_specs=[pl.BlockSpec((tm, d), lambda i: (i, 0)), pl.BlockSpec((1, d), lambda i: (0, 0))],
        out_specs=pl.BlockSpec((tm, d), lambda i: (i, 0)),
        out_shape=jax.ShapeDtypeStruct((m, d), BF16),
        compiler_params=_cparams(("arbitrary",)),
        name="rms_cast",
    )(x, g.reshape(1, d))


def _outproj_kernel(y_ref, a_ref, w1_ref, w2_ref, x_ref, gate_ref, o_ref):
    acc = jnp.dot(y_ref[...], w1_ref[...], preferred_element_type=F32)
    acc = acc + jnp.dot(a_ref[...], w2_ref[...], preferred_element_type=F32)
    o_ref[...] = x_ref[...] + gate_ref[...] * acc


def _outproj(y_n, attn_n, w, x, gate, *, tm, per_token, rows_per_batch):
    m, d = x.shape
    tn = PROJ_TN
    if per_token:
        gate_spec = pl.BlockSpec((tm, tn), lambda i, j: (i, j))
    else:
        tpb = rows_per_batch // tm
        gate_spec = pl.BlockSpec((None, 1, tn), lambda i, j: (i // tpb, 0, j))
    return pl.pallas_call(
        _outproj_kernel,
        grid=(m // tm, d // tn),
        in_specs=[pl.BlockSpec((tm, d), lambda i, j: (i, 0), pipeline_mode=pl.Buffered(1)),
                  pl.BlockSpec((tm, d), lambda i, j: (i, 0), pipeline_mode=pl.Buffered(1)),
                  pl.BlockSpec((d, tn), lambda i, j: (0, j)),
                  pl.BlockSpec((d, tn), lambda i, j: (1, j)),
                  pl.BlockSpec((tm, tn), lambda i, j: (i, j)),
                  gate_spec],
        out_specs=pl.BlockSpec((tm, tn), lambda i, j: (i, j)),
        out_shape=jax.ShapeDtypeStruct((m, d), F32),
        compiler_params=_cparams(("arbitrary", "arbitrary")),
        name="outproj",
    )(y_n, attn_n, w, w, x, gate)


def _peer_route_kernel(q_ref, k1_ref, k2_ref, s1_ref, s2_ref, tau_ref, cc_ref):
    q = q_ref[...]
    k1 = k1_ref[...].astype(BF16)
    k2 = k2_ref[...].astype(BF16)
    half = PEER_D_KEY // 2
    tm = q.shape[0]
    taus, ccs = [], []
    for h in range(PEER_HEADS):
        base = h * PEER_D_KEY
        s1 = _nt(k1, q[:, base:base + half].astype(BF16))
        s2 = _nt(k2, q[:, base + half:base + PEER_D_KEY].astype(BF16))
        s1_ref[h * PEER_KEYS:(h + 1) * PEER_KEYS, :] = s1
        s2_ref[h * PEER_KEYS:(h + 1) * PEER_KEYS, :] = s2

        def top_vals(x):
            vals = []
            for _ in range(PEER_TOPK):
                m = jnp.max(x, axis=0, keepdims=True)
                vals.append(m)
                x = jnp.where(x == m, -jnp.inf, x)
            return vals

        v1 = top_vals(s1)
        v2 = top_vals(s2)
        v1a = jnp.concatenate(v1, axis=0)
        v2a = jnp.concatenate(v2, axis=0)
        half_k = PEER_TOPK // 2
        cand = jnp.concatenate([v + v2a[:half_k] for v in v1[:half_k]]
                               + [v1[0] + v2a[half_k:], v1a[half_k:] + v2[0]], axis=0) + 0.0
        ckey = _float_key(cand)

        def count_ge(thr, ckey=ckey):
            return jnp.sum((ckey >= thr).astype(I32), axis=0, keepdims=True)

        tau = _key_float(_bisect_kth(count_ge, PEER_TOPK, (1, tm)))
        cmax = cand[0:1, :]
        zsum = jnp.sum(jnp.where(cand >= tau, jnp.exp(cand - cmax), 0.0), axis=0, keepdims=True)
        taus.append(tau)
        ccs.append(cmax + jnp.log(zsum))
    tau_ref[...] = jnp.concatenate(taus, axis=0)
    cc_ref[...] = jnp.concatenate(ccs, axis=0)


def _peer_route(q, k1, k2, *, tm):
    t = q.shape[0]
    rows = PEER_HEADS * PEER_KEYS
    half = PEER_D_KEY // 2
    return pl.pallas_call(
        _peer_route_kernel,
        grid=(t // tm,),
        in_specs=[pl.BlockSpec((tm, PEER_HEADS * PEER_D_KEY), lambda i: (i, 0)),
                  pl.BlockSpec((PEER_KEYS, half), lambda i: (0, 0)),
                  pl.BlockSpec((PEER_KEYS, half), lambda i: (0, 0))],
        out_specs=[pl.BlockSpec((rows, tm), lambda i: (0, i)),
                   pl.BlockSpec((rows, tm), lambda i: (0, i)),
                   pl.BlockSpec((PEER_HEADS, tm), lambda i: (0, i)),
                   pl.BlockSpec((PEER_HEADS, tm), lambda i: (0, i))],
        out_shape=[jax.ShapeDtypeStruct((rows, t), F32), jax.ShapeDtypeStruct((rows, t), F32),
                   jax.ShapeDtypeStruct((PEER_HEADS, t), F32), jax.ShapeDtypeStruct((PEER_HEADS, t), F32)],
        compiler_params=_cparams(("arbitrary",)),
        name="peer_route",
    )(q, k1, k2)


def _peer_dense_kernel(xb_ref, s1_ref, s2_ref, tau_ref, cc_ref, u_ref, v_ref, o_ref, *, te):
    e = pl.program_id(1)

    @pl.when(e == 0)
    def _():
        o_ref[...] = jnp.zeros(o_ref.shape, F32)

    tm = xb_ref.shape[0]
    nsub = MXU_TILE // PEER_KEYS
    coefs = []
    for c in range(te // MXU_TILE):
        ut = _nt(xb_ref[...], u_ref[c * MXU_TILE:(c + 1) * MXU_TILE, :]).T
        for i in range(nsub):
            k = c * nsub + i
            gate = jnp.zeros((PEER_KEYS, tm), F32)
            for h in range(PEER_HEADS):
                sm = s1_ref[k, h:h + 1, :] + s2_ref[h * PEER_KEYS:(h + 1) * PEER_KEYS, :]
                gate = gate + jnp.where(sm >= tau_ref[h:h + 1, :], jnp.exp(sm - cc_ref[h:h + 1, :]), 0.0)
            coefs.append(gate * jax.nn.gelu(ut[i * PEER_KEYS:(i + 1) * PEER_KEYS, :]))
    coef = jnp.concatenate(coefs, axis=0).T.astype(BF16)
    for n in range(o_ref.shape[1] // MXU_TILE):
        cols = slice(n * MXU_TILE, (n + 1) * MXU_TILE)
        o_ref[:, cols] += jnp.dot(coef, v_ref[:, cols], preferred_element_type=F32)


def _peer_dense(xb, s1, s2, tau, cc, u_b, v_b, *, tm):
    t, d = xb.shape
    ne = u_b.shape[0]
    te = min(PEER_TE, ne)
    rows = PEER_HEADS * PEER_KEYS
    once = pl.Buffered(1)
    s1 = s1.reshape(PEER_HEADS, PEER_KEYS, t).transpose(1, 0, 2)
    return pl.pallas_call(
        functools.partial(_peer_dense_kernel, te=te),
        grid=(t // tm, ne // te),
        in_specs=[pl.BlockSpec((tm, d), lambda i, e: (i, 0), pipeline_mode=once),
                  pl.BlockSpec((te // PEER_KEYS, PEER_HEADS, tm), lambda i, e: (e, 0, i)),
                  pl.BlockSpec((rows, tm), lambda i, e: (0, i), pipeline_mode=once),
                  pl.BlockSpec((PEER_HEADS, tm), lambda i, e: (0, i), pipeline_mode=once),
                  pl.BlockSpec((PEER_HEADS, tm), lambda i, e: (0, i), pipeline_mode=once),
                  pl.BlockSpec((te, d), lambda i, e: (e, 0)),
                  pl.BlockSpec((te, d), lambda i, e: (e, 0))],
        out_specs=pl.BlockSpec((tm, d), lambda i, e: (i, 0), pipeline_mode=once),
        out_shape=jax.ShapeDtypeStruct((t, d), F32),
        compiler_params=_cparams(("arbitrary", "arbitrary")),
        name="peer_dense",
    )(xb, s1, s2, tau, cc, u_b, v_b)


def _final_kernel(x_ref, f_ref, gate_ref, g_ref, o_ref):
    o_ref[...] = _rms(x_ref[...] + gate_ref[...] * f_ref[...], g_ref[...])


def _final(x, f, gate, g, *, tm, per_token, rows_per_batch):
    m, d = x.shape
    if per_token:
        gate_spec = pl.BlockSpec((tm, d), lambda i: (i, 0))
    else:
        tpb = rows_per_batch // tm
        gate_spec = pl.BlockSpec((None, 1, d), lambda i: (i // tpb, 0, 0))
    return pl.pallas_call(
        _final_kernel,
        grid=(m // tm,),
        in_specs=[pl.BlockSpec((tm, d), lambda i: (i, 0)),
                  pl.BlockSpec((tm, d), lambda i: (i, 0)),
                  gate_spec,
                  pl.BlockSpec((1, d), lambda i: (0, 0))],
        out_specs=pl.BlockSpec((tm, d), lambda i: (i, 0)),
        out_shape=jax.ShapeDtypeStruct((m, d), F32),
        compiler_params=_cparams(("arbitrary",)),
        name="final_norm",
    )(x, f, gate, g.reshape(1, d))


def _layout(d):
    nh = d // SSD_HEADDIM
    kvw = KV_HEADS * HEAD_DIM
    qiw = IDX_HEADS * IDX_DIM
    xbcw = d + 2 * SSD_GROUPS * D_STATE
    lay, off = {}, 0
    for name, w in (("z", d), ("q", d), ("qi", qiw), ("xbc", xbcw), ("k", kvw), ("v", kvw), ("tail", PROJ_TN)):
        assert w % PROJ_TN == 0
        lay[name] = off
        off += w
    lay["total"] = off
    assert IDX_DIM + nh + IDX_HEADS <= PROJ_TN
    return lay


def _group_cols(xbc, d):
    lead = xbc.shape[:-1]
    gn = SSD_GROUPS * D_STATE
    x = xbc[..., :d].reshape(lead + (SSD_GROUPS, d // SSD_GROUPS))
    b = xbc[..., d:d + gn].reshape(lead + (SSD_GROUPS, D_STATE))
    c = xbc[..., d + gn:].reshape(lead + (SSD_GROUPS, D_STATE))
    return jnp.concatenate([x, b, c], axis=-1).reshape(lead + (d + 2 * gn,))


def _ungroup_cols(xg, d):
    lead = xg.shape[:-1]
    gn = SSD_GROUPS * D_STATE
    xsw = d // SSD_GROUPS
    g3 = xg.reshape(lead + (SSD_GROUPS, xsw + 2 * D_STATE))
    return jnp.concatenate([g3[..., :xsw].reshape(lead + (d,)),
                            g3[..., xsw:xsw + D_STATE].reshape(lead + (gn,)),
                            g3[..., xsw + D_STATE:].reshape(lead + (gn,))], axis=-1)


def _prep_w_in(w_in, d):
    nh = d // SSD_HEADDIM
    kvw = KV_HEADS * HEAD_DIM
    qiw = IDX_HEADS * IDX_DIM
    xbcw = d + 2 * SSD_GROUPS * D_STATE
    sizes = (d, xbcw, nh, d, kvw, kvw, qiw, IDX_HEADS, IDX_DIM)
    splits = np.cumsum(sizes)[:-1]
    z, xbc, dt, q, k, v, qi, wi, ki = [a.astype(BF16) for a in jnp.split(w_in, splits, axis=1)]
    pad = jnp.zeros((d, PROJ_TN - IDX_DIM - nh - IDX_HEADS), BF16)
    return jnp.concatenate([z, q, qi, _group_cols(xbc, d), k, v, ki, dt, wi, pad], axis=1)


def _group_major(vec, rpg):
    return jnp.pad(vec.reshape(SSD_GROUPS, 1, rpg), ((0, 0), (0, 0), (0, LANES - rpg)))


def _rope_tables(pos):
    half = HEAD_DIM // 8
    inv = ROPE_THETA ** (-jnp.arange(half, dtype=F32) / half)
    ang = pos.astype(F32)[:, None] * inv[None, :]
    cos, sin = jnp.cos(ang), jnp.sin(ang)
    n = pos.shape[0]
    rest = HEAD_DIM - 2 * half
    return (jnp.concatenate([cos, cos, jnp.ones((n, rest), F32)], axis=1),
            jnp.concatenate([-sin, sin, jnp.zeros((n, rest), F32)], axis=1))


def _layer(x2d, mods, pos, prm, *, nb, seq, per_token, tm, ssd_cfg, attn_fn, peer_tm):
    m, d = x2d.shape
    lay = prm["lay"]
    nh = d // SSD_HEADDIM
    rpg = nh // SSD_GROUPS
    sh1, sc1, g1, sh2, sc2, g2 = mods
    cos, sin = _rope_tables(pos)
    xoff = lay["xbc"]
    rope = (cos, sin, prm["kidx_norm_g"],
            ((lay["q"] // PROJ_TN, lay["xbc"] // PROJ_TN), (lay["k"] // PROJ_TN, lay["v"] // PROJ_TN)),
            lay["tail"] // PROJ_TN)
    tm_e = min(tm, ELEMWISE_TM)
    tm_p = min(m if per_token else seq, PROJ_TM)
    h1 = _modulate(x2d, prm["norm1_g"], sh1, sc1, tm=tm_e, per_token=per_token, rows_per_batch=seq)
    proj = _proj(h1, prm["w_in"], tm=tm_p, rope=rope)

    gw = rpg * SSD_HEADDIM + 2 * D_STATE
    dt_raw = proj[:, lay["tail"] + IDX_DIM: lay["tail"] + IDX_DIM + nh]
    y_ssd, ssm_new = ssd_cfg(proj, dt_raw, xoff // gw)
    y_n = _gate_norm(y_ssd, proj, prm["ssd_norm_g"], tm=tm_e)

    o_attn = attn_fn(proj)

    a_n = _rms_cast(o_attn, prm["attn_norm_g"], tm=tm_e)
    x1 = _outproj(y_n, a_n, prm["w_out"], x2d, g1, tm=tm_p, per_token=per_token,
                  rows_per_batch=seq)

    xb = _modulate(x1, prm["norm2_g"], sh2, sc2, tm=tm_e, per_token=per_token, rows_per_batch=seq)
    qp = _proj(xb, prm["peer_wq"], tm=tm_p)
    s1, s2, tau, cc = _peer_route(qp, prm["peer_k1"], prm["peer_k2"], tm=min(peer_tm, PEER_ROUTE_TM))
    ffn = _peer_dense(xb, s1, s2, tau, cc, prm["peer_u"], prm["peer_v"], tm=peer_tm)
    y = _final(x1, ffn, g2, prm["final_norm_g"], tm=tm_e, per_token=per_token, rows_per_batch=seq)
    return y, proj, ssm_new


def kernel(x_prompt, x_sample, c_prompt, c_sample, cache_k, cache_v, cache_kidx, state_ssm, state_conv, page_table, w_mod, b_mod, norm1_g, w_in, conv_w, conv_b, dt_bias, a_log, d_skip, ssd_norm_g, kidx_norm_g, attn_norm_g, w_out, norm2_g, peer_wq, peer_k1, peer_k2, peer_u, peer_v, final_norm_g):
    nbp, seq, d = x_prompt.shape
    nbs, tdec, _ = x_sample.shape
    depth = w_mod.shape[0]
    assert depth == 1
    nh = d // SSD_HEADDIM
    rpg = nh // SSD_GROUPS
    gw = rpg * SSD_HEADDIM + 2 * D_STATE
    lay = _layout(d)
    assert lay["xbc"] % gw == 0
    kvw = KV_HEADS * HEAD_DIM
    xbcw = d + 2 * SSD_GROUPS * D_STATE

    prm = dict(
        lay=lay,
        norm1_g=norm1_g[0], kidx_norm_g=kidx_norm_g[0], ssd_norm_g=ssd_norm_g[0], attn_norm_g=attn_norm_g[0],
        norm2_g=norm2_g[0], final_norm_g=final_norm_g,
        w_in=_prep_w_in(w_in[0], d),
        w_out=_cast_bf16(w_out[0]),
        peer_wq=_cast_bf16(peer_wq[0]), peer_k1=peer_k1[0], peer_k2=peer_k2[0],
        peer_u=_cast_bf16(peer_u[0]), peer_v=_cast_bf16(peer_v[0]),
    )
    conv_w_g = _group_cols(conv_w[0], d)
    conv_b_g = _group_cols(conv_b[0], d).reshape(1, xbcw)
    dtb_g = _group_major(dt_bias[0], rpg)
    alog_g = _group_major(a_log[0], rpg)
    dsk_g = _group_major(d_skip[0], rpg)

    mod = _adaln(jnp.concatenate([c_prompt, c_sample], axis=0), w_mod[0], b_mod[0])
    mod_p = [a.reshape(nbp, 1, d) for a in jnp.split(mod[:nbp], 6, axis=-1)]
    mod_s = [jnp.repeat(a, SAMPLE_PAD, axis=0) for a in jnp.split(mod[nbp:], 6, axis=-1)]

    def dt_group_major(dt_raw):
        rows = dt_raw.shape[0]
        dtg = dt_raw.reshape(rows, SSD_GROUPS, rpg).transpose(1, 0, 2)
        return jnp.pad(dtg, ((0, 0), (0, 0), (0, LANES - rpg)))

    cs_p = min(SSD_CHUNK, seq)
    nc_p = seq // cs_p
    assert seq % cs_p == 0 and seq % Q_BLOCK == 0

    def ssd_prompt(proj, dt_raw, xoff_blocks):
        hist = jnp.zeros((nbp, SUBLANES, xbcw), F32)
        h0 = jnp.zeros((nbp, nh, SSD_HEADDIM, D_STATE), F32)
        return _ssd(proj, xoff_blocks, hist, conv_w_g, conv_b_g, dt_group_major(dt_raw), dtb_g, alog_g, dsk_g, h0,
                    nb=nbp, nc=nc_p, cs=cs_p, d=d, valid_len=cs_p)

    tm_p = min(512, seq)
    pos_p = jnp.tile(jnp.arange(seq), nbp)
    y_p, proj_p, ssm_p = _layer(
        x_prompt.reshape(nbp * seq, d), mod_p, pos_p, prm, nb=nbp, seq=seq, per_token=False, tm=tm_p,
        ssd_cfg=ssd_prompt, attn_fn=functools.partial(_attn_prompt, lay=lay, nb=nbp, seq=seq, d=d),
        peer_tm=min(PEER_TM, nbp * seq))

    npages = page_table.shape[1]
    past = npages * PAGE_SIZE
    ms = nbs * SAMPLE_PAD
    xs_pad = jnp.pad(x_sample, ((0, 0), (0, SAMPLE_PAD - tdec), (0, 0))).reshape(ms, d)
    pos_s = jnp.tile(past + jnp.arange(SAMPLE_PAD), nbs)

    def ssd_sample(proj, dt_raw, xoff_blocks):
        xbc = proj[:, lay["xbc"]:lay["xbc"] + xbcw].reshape(nbs, SAMPLE_PAD, xbcw)
        xbc = jnp.pad(xbc, ((0, 0), (0, SAMPLE_CHUNK - SAMPLE_PAD), (0, 0))).reshape(nbs * SAMPLE_CHUNK, xbcw)
        dtr = jnp.pad(dt_raw.reshape(nbs, SAMPLE_PAD, nh), ((0, 0), (0, SAMPLE_CHUNK - SAMPLE_PAD), (0, 0)))
        hist = jnp.pad(_group_cols(state_conv[0], d), ((0, 0), (SUBLANES - (CONV_W - 1), 0), (0, 0)))
        y, hnew = _ssd(xbc, 0, hist, conv_w_g, conv_b_g, dt_group_major(dtr.reshape(nbs * SAMPLE_CHUNK, nh)),
                       dtb_g, alog_g, dsk_g, state_ssm[0], nb=nbs, nc=1, cs=SAMPLE_CHUNK, d=d, valid_len=tdec)
        y = y.reshape(nbs, SAMPLE_CHUNK, d)[:, :SAMPLE_PAD].reshape(ms, d)
        return y, hnew

    def attn_sample(proj):
        def seg(name, w):
            return proj[:, lay[name]:lay[name] + w].reshape(nbs, SAMPLE_PAD, w)
        tail = lay["tail"]
        ki_new = proj[:, tail:tail + IDX_DIM].reshape(nbs, SAMPLE_PAD, IDX_DIM)
        wi = proj[:, tail + IDX_DIM + nh: tail + IDX_DIM + nh + IDX_HEADS].reshape(nbs, SAMPLE_PAD, IDX_HEADS)
        return _attn_sample(seg("q", d), seg("qi", IDX_HEADS * IDX_DIM), wi, ki_new, seg("k", kvw), seg("v", kvw),
                            cache_k, cache_v, cache_kidx, page_table, t_valid=tdec)

    y_s, proj_s, ssm_s = _layer(
        xs_pad, mod_s, pos_s, prm, nb=nbs, seq=SAMPLE_PAD, per_token=True, tm=ms,
        ssd_cfg=ssd_sample, attn_fn=attn_sample, peer_tm=ms)

    def states(proj, nb, rows, valid):
        p3 = proj.reshape(nb, rows, lay["total"])[:, :valid]
        k = p3[:, :, lay["k"]:lay["k"] + kvw].reshape(1, nb, valid, KV_HEADS, HEAD_DIM)
        v = p3[:, :, lay["v"]:lay["v"] + kvw].reshape(1, nb, valid, KV_HEADS, HEAD_DIM)
        ki = p3[:, :, lay["tail"]:lay["tail"] + IDX_DIM].reshape(1, nb, valid, IDX_DIM)
        xbc = _ungroup_cols(p3[:, valid - (CONV_W - 1):valid, lay["xbc"]:lay["xbc"] + xbcw], d)
        return k, v, ki, xbc.reshape(1, nb, CONV_W - 1, xbcw)

    k_p, v_p, ki_p, conv_p = states(proj_p, nbp, seq, seq)
    k_s, v_s, ki_s, conv_s = states(proj_s, nbs, SAMPLE_PAD, tdec)
    y_prompt = y_p.reshape(nbp, seq, d)
    y_sample = y_s.reshape(nbs, SAMPLE_PAD, d)[:, :tdec]
    return (y_prompt, y_sample, k_p, v_p, ki_p, ssm_p[None], conv_p,
            k_s, v_s, ki_s, ssm_s[None], conv_s)
```

```python
import functools
import math

import jax
import jax.numpy as jnp
import numpy as np
from jax import lax
from jax.experimental import pallas as pl
from jax.experimental.pallas import tpu as pltpu

F32 = jnp.float32
BF16 = jnp.bfloat16
I32 = jnp.int32

SSD_HEADDIM = 64
SSD_GROUPS = 8
D_STATE = 128
CONV_W = 4
SSD_CHUNK = 256
HEAD_DIM = 128
KV_HEADS = 8
ROPE_THETA = 500000.0
IDX_HEADS = 32
IDX_DIM = 128
IDX_TOPK = 256
Q_BLOCK = 128
PEER_HEADS = 8
PEER_KEYS = 128
PEER_TOPK = 16
PEER_D_KEY = 256
PAGE_SIZE = 128
EPS = 1e-6

LANES = 128
SUBLANES = 8
VMEM_LIMIT_BYTES = 56 * 1024 * 1024

MXU_TILE = 256
PROJ_TN = MXU_TILE
PROJ_TM = 1024
ELEMWISE_TM = 256
CAST_ROWS = 512
W_IN_PREP_ROWS = 128
PEER_ROUTE_TM = 256
PEER_TM = 1024
PEER_TE = 512
SAMPLE_PAD = 8
SAMPLE_CHUNK = 128
ATTN_TK = 512
ATTN_GROUPS_PER_STEP = 2
PAGES_PER_STEP = 8
NEG_BIG = -1e30
M_INIT = -1e29
INT_MIN = -2 ** 31


def _cparams(sem):
    return pltpu.CompilerParams(dimension_semantics=sem, vmem_limit_bytes=VMEM_LIMIT_BYTES)


def _nt(a, b):
    return lax.dot_general(a, b, (((1,), (1,)), ((), ())), preferred_element_type=F32)


def _tn(a, b):
    return lax.dot_general(a, b, (((0,), (0,)), ((), ())), preferred_element_type=F32)


def _rms(x, g):
    return x * lax.rsqrt(jnp.mean(x * x, axis=-1, keepdims=True) + EPS) * g


def _rope_tile(a, cos, sin):
    half = HEAD_DIM // 8
    lane = lax.broadcasted_iota(I32, a.shape, 1)
    sw = jnp.where(lane < half, pltpu.roll(a, HEAD_DIM - half, 1), pltpu.roll(a, half, 1))
    return a * cos + sw * sin


def _float_key(x):
    bits = pltpu.bitcast(x, I32)
    return jnp.where(bits < 0, bits ^ jnp.int32(0x7FFFFFFF), bits)


def _key_float(key):
    bits = jnp.where(key < 0, key ^ jnp.int32(0x7FFFFFFF), key)
    return pltpu.bitcast(bits, F32)


def _bisect_kth(count_ge, k, shape):
    imin = jnp.int32(INT_MIN)

    def body(i, ans):
        cand_u = ans | lax.shift_left(jnp.int32(1), jnp.int32(31) - i)
        return jnp.where(count_ge(cand_u ^ imin) >= k, cand_u, ans)

    ans = lax.fori_loop(0, 32, body, jnp.zeros(shape, I32))
    return ans ^ imin


def _cast_kernel(x_ref, o_ref):
    o_ref[...] = x_ref[...].astype(BF16)


def _cast_bf16(w):
    r, c = w.shape
    tr = min(r, CAST_ROWS)
    assert r % tr == 0
    return pl.pallas_call(
        _cast_kernel,
        grid=(r // tr,),
        in_specs=[pl.BlockSpec((tr, c), lambda i: (i, 0))],
        out_specs=pl.BlockSpec((tr, c), lambda i: (i, 0)),
        out_shape=jax.ShapeDtypeStruct((r, c), BF16),
        compiler_params=_cparams(("arbitrary",)),
        name="cast_bf16",
    )(w)


def _adaln_kernel(c_ref, w_ref, b_ref, o_ref):
    c = c_ref[...]
    a = (c * jax.nn.sigmoid(c)).astype(BF16)
    o_ref[...] = jnp.dot(a, w_ref[...].astype(BF16), preferred_element_type=F32) + b_ref[...]


def _adaln(c, w_mod, b_mod):
    n, d = c.shape
    npad = -(-n // SUBLANES) * SUBLANES
    cp = jnp.pad(c, ((0, npad - n), (0, 0)))
    nout = w_mod.shape[1]
    tn = PROJ_TN
    out = pl.pallas_call(
        _adaln_kernel,
        grid=(nout // tn,),
        in_specs=[pl.BlockSpec((npad, d), lambda j: (0, 0)),
                  pl.BlockSpec((d, tn), lambda j: (0, j)),
                  pl.BlockSpec((1, tn), lambda j: (0, j))],
        out_specs=pl.BlockSpec((npad, tn), lambda j: (0, j)),
        out_shape=jax.ShapeDtypeStruct((npad, nout), F32),
        compiler_params=_cparams(("arbitrary",)),
        name="adaln",
    )(cp, w_mod, b_mod.reshape(1, nout))
    return out[:n]


def _modulate_kernel(x_ref, g_ref, sh_ref, sc_ref, o_ref):
    o_ref[...] = (_rms(x_ref[...], g_ref[...]) * (1.0 + sc_ref[...]) + sh_ref[...]).astype(BF16)


def _modulate(x, g, shift, scale, *, tm, per_token, rows_per_batch):
    m, d = x.shape
    if per_token:
        mod_spec = pl.BlockSpec((tm, d), lambda i: (i, 0))
    else:
        tiles_per_batch = rows_per_batch // tm
        mod_spec = pl.BlockSpec((None, 1, d), lambda i: (i // tiles_per_batch, 0, 0))
    return pl.pallas_call(
        _modulate_kernel,
        grid=(m // tm,),
        in_specs=[pl.BlockSpec((tm, d), lambda i: (i, 0)),
                  pl.BlockSpec((1, d), lambda i: (0, 0)),
                  mod_spec, mod_spec],
        out_specs=pl.BlockSpec((tm, d), lambda i: (i, 0)),
        out_shape=jax.ShapeDtypeStruct((m, d), BF16),
        compiler_params=_cparams(("arbitrary",)),
        name="modulate",
    )(x, g.reshape(1, d), shift, scale)


def _proj_kernel(*refs, rope_ranges, tail_tile):
    if tail_tile is not None:
        h_ref, w_ref, cos_ref, sin_ref, kg_ref, o_ref = refs
    else:
        h_ref, w_ref, o_ref = refs
    j = pl.program_id(1)

    def product():
        return jnp.dot(h_ref[...], w_ref[...], preferred_element_type=F32)

    if tail_tile is None:
        o_ref[...] = product()
        return

    is_rope = functools.reduce(jnp.logical_or, [(j >= lo) & (j < hi) for lo, hi in rope_ranges])
    is_tail = j == tail_tile
    tn = w_ref.shape[1]

    @pl.when(jnp.logical_not(is_rope | is_tail))
    def _():
        o_ref[...] = product()

    @pl.when(is_rope)
    def _():
        acc = product()
        cos = cos_ref[...]
        sin = sin_ref[...]
        for t in range(tn // HEAD_DIM):
            sl = slice(t * HEAD_DIM, (t + 1) * HEAD_DIM)
            o_ref[:, sl] = _rope_tile(acc[:, sl], cos, sin)

    @pl.when(is_tail)
    def _():
        acc = product()
        ki = _rms(acc[:, :IDX_DIM], kg_ref[...])
        o_ref[:, :IDX_DIM] = _rope_tile(ki, cos_ref[...], sin_ref[...])
        o_ref[:, IDX_DIM:] = acc[:, IDX_DIM:]


def _proj(h, w, *, tm, rope=None):
    m, d = h.shape
    n = w.shape[1]
    tn = PROJ_TN
    assert m % tm == 0 and n % tn == 0
    in_specs = [pl.BlockSpec((tm, d), lambda i, j: (i, 0), pipeline_mode=pl.Buffered(1)),
                pl.BlockSpec((d, tn), lambda i, j: (0, j))]
    args = [h, w]
    rope_ranges, tail_tile = (), None
    if rope is not None:
        cos, sin, kidx_g, rope_ranges, tail_tile = rope
        in_specs += [pl.BlockSpec((tm, HEAD_DIM), lambda i, j: (i, 0)),
                     pl.BlockSpec((tm, HEAD_DIM), lambda i, j: (i, 0)),
                     pl.BlockSpec((1, IDX_DIM), lambda i, j: (0, 0))]
        args += [cos, sin, kidx_g.reshape(1, IDX_DIM)]
    return pl.pallas_call(
        functools.partial(_proj_kernel, rope_ranges=rope_ranges, tail_tile=tail_tile),
        grid=(m // tm, n // tn),
        in_specs=in_specs,
        out_specs=pl.BlockSpec((tm, tn), lambda i, j: (i, j)),
        out_shape=jax.ShapeDtypeStruct((m, n), F32),
        compiler_params=_cparams(("arbitrary", "arbitrary")),
        name="proj",
    )(*args)


def _ssd_kernel(x_ref, prev_ref, hist_ref, cw_ref, cb_ref, dt_ref, dtb_ref, alog_ref, dsk_ref, h0_ref,
                y_ref, hout_ref, h_scr, *, cs, rpg, valid_len, nc):
    c = pl.program_id(2)
    xs_w = rpg * SSD_HEADDIM

    @pl.when(c == 0)
    def _():
        h_scr[...] = h0_ref[...]

    prev = jnp.where(c == 0, hist_ref[...], prev_ref[...])
    cat = jnp.concatenate([prev, x_ref[...]], axis=0)
    w = cw_ref[...]
    acc = cb_ref[...]
    for j in range(CONV_W):
        lo = SUBLANES - (CONV_W - 1) + j
        acc = acc + cat[lo:lo + cs] * w[j:j + 1]
    xc = acc * jax.nn.sigmoid(acc)
    xs = xc[:, :xs_w]
    bm = xc[:, xs_w:xs_w + D_STATE]
    cm = xc[:, xs_w + D_STATE:]
    bm_b = bm.astype(BF16)
    cm_b = cm.astype(BF16)

    z = dt_ref[...] + dtb_ref[...]
    dt = jnp.maximum(z, 0.0) + jnp.log1p(jnp.exp(-jnp.abs(z)))
    if valid_len < cs:
        row = lax.broadcasted_iota(I32, dt.shape, 0)
        dt = jnp.where(row < valid_len, dt, 0.0)
    a_neg = -jnp.exp(alog_ref[...])
    ii = lax.broadcasted_iota(I32, (cs, cs), 0)
    jj = lax.broadcasted_iota(I32, (cs, cs), 1)
    causal = ii >= jj
    acs = jnp.dot(causal.astype(F32), dt * a_neg, preferred_element_type=F32,
                  precision=lax.Precision.HIGHEST)
    acs_t = acs.T
    a_last = acs[cs - 1:cs, :]
    cb = _nt(cm_b, bm_b)
    dsk = dsk_ref[...]

    first = lax.broadcasted_iota(I32, (cs, LANES), 1) < SSD_HEADDIM
    first_row = lax.broadcasted_iota(I32, (LANES, D_STATE), 0) < SSD_HEADDIM
    ys = []
    for pr in range(rpg // 2):
        r0, r1 = 2 * pr, 2 * pr + 1
        a0, a1 = acs[:, r0:r0 + 1], acs[:, r1:r1 + 1]
        xp = xs[:, pr * LANES:(pr + 1) * LANES]
        xd = xp * jnp.where(first, dt[:, r0:r0 + 1], dt[:, r1:r1 + 1])
        y = jnp.zeros((cs, LANES), F32)
        for r, keep in ((r0, first), (r1, jnp.logical_not(first))):
            lm = jnp.exp(jnp.where(causal, acs[:, r:r + 1] - acs_t[r:r + 1, :], -jnp.inf))
            y = y + jnp.dot((cb * lm).astype(BF16), jnp.where(keep, xd, 0.0).astype(BF16),
                            preferred_element_type=F32)
        hp = h_scr[pr]
        y = y + jnp.where(first, jnp.exp(a0), jnp.exp(a1)) * _nt(cm_b, hp.astype(BF16))
        al0, al1 = a_last[:, r0:r0 + 1], a_last[:, r1:r1 + 1]
        decay = jnp.where(first, jnp.exp(al0 - a0), jnp.exp(al1 - a1))
        h_scr[pr] = (hp * jnp.where(first_row, jnp.exp(al0), jnp.exp(al1))
                     + _tn((xd * decay).astype(BF16), bm_b))
        ys.append(y + xp * jnp.where(first[0:1, :], dsk[:, r0:r0 + 1], dsk[:, r1:r1 + 1]))
    y_ref[...] = jnp.concatenate(ys, axis=1)

    @pl.when(c == nc - 1)
    def _():
        hout_ref[...] = h_scr[...]


def _ssd(xg, xoff_blocks, hist8, conv_w_g, conv_b_g, dt_g, dtb_g, alog_g, dsk_g, h0, *, nb, nc, cs, d, valid_len):
    g_cnt = SSD_GROUPS
    rpg = d // SSD_HEADDIM // g_cnt
    gw = rpg * SSD_HEADDIM + 2 * D_STATE
    m = nb * nc * cs
    cpb = cs // SUBLANES
    nh = d // SSD_HEADDIM
    npair = rpg // 2
    assert rpg % 2 == 0 and 2 * SSD_HEADDIM == LANES
    kern = functools.partial(_ssd_kernel, cs=cs, rpg=rpg, valid_len=valid_len, nc=nc)
    y, hout = pl.pallas_call(
        kern,
        grid=(nb, g_cnt, nc),
        in_specs=[
            pl.BlockSpec((cs, gw), lambda b, g, c: (b * nc + c, xoff_blocks + g)),
            pl.BlockSpec((SUBLANES, gw), lambda b, g, c: (jnp.maximum((b * nc + c) * cpb - 1, 0), xoff_blocks + g)),
            pl.BlockSpec((None, SUBLANES, gw), lambda b, g, c: (b, 0, g)),
            pl.BlockSpec((CONV_W, gw), lambda b, g, c: (0, g)),
            pl.BlockSpec((1, gw), lambda b, g, c: (0, g)),
            pl.BlockSpec((None, cs, LANES), lambda b, g, c: (g, b * nc + c, 0)),
            pl.BlockSpec((None, 1, LANES), lambda b, g, c: (g, 0, 0)),
            pl.BlockSpec((None, 1, LANES), lambda b, g, c: (g, 0, 0)),
            pl.BlockSpec((None, 1, LANES), lambda b, g, c: (g, 0, 0)),
            pl.BlockSpec((None, npair, LANES, D_STATE), lambda b, g, c: (b, g, 0, 0)),
        ],
        out_specs=[
            pl.BlockSpec((cs, rpg * SSD_HEADDIM), lambda b, g, c: (b * nc + c, g)),
            pl.BlockSpec((None, npair, LANES, D_STATE), lambda b, g, c: (b, g, 0, 0)),
        ],
        out_shape=[jax.ShapeDtypeStruct((m, d), F32),
                   jax.ShapeDtypeStruct((nb, nh // 2, LANES, D_STATE), F32)],
        scratch_shapes=[pltpu.VMEM((npair, LANES, D_STATE), F32)],
        compiler_params=_cparams(("arbitrary", "arbitrary", "arbitrary")),
        name="ssd_scan",
    )(xg, xg, hist8, conv_w_g, conv_b_g, dt_g, dtb_g, alog_g, dsk_g, h0.reshape(nb, nh // 2, LANES, D_STATE))
    return y, hout.reshape(nb, nh, SSD_HEADDIM, D_STATE)


def _gate_norm_kernel(y_ref, z_ref, g_ref, o_ref):
    z = z_ref[...]
    o_ref[...] = _rms(y_ref[...] * (z * jax.nn.sigmoid(z)), g_ref[...]).astype(BF16)


def _gate_norm(y, proj, g, *, tm):
    m, d = y.shape
    return pl.pallas_call(
        _gate_norm_kernel,
        grid=(m // tm,),
        in_specs=[pl.BlockSpec((tm, d), lambda i: (i, 0)),
                  pl.BlockSpec((tm, d), lambda i: (i, 0)),
                  pl.BlockSpec((1, d), lambda i: (0, 0))],
        out_specs=pl.BlockSpec((tm, d), lambda i: (i, 0)),
        out_shape=jax.ShapeDtypeStruct((m, d), BF16),
        compiler_params=_cparams(("arbitrary",)),
        name="gate_norm",
    )(y, proj, g.reshape(1, d))


def _attn_prompt_kernel(qi_ref, tail_ref, ki_ref, q_ref, k_ref, v_ref, o_ref, key_scr, thr_scr,
                        *, topk, tk, n_ssd_heads, rq, gps):
    qb = pl.program_id(1)
    g = pl.program_id(2)
    nkt = (qb * Q_BLOCK + Q_BLOCK + tk - 1) // tk
    imin = jnp.int32(INT_MIN)

    @pl.when(g == 0)
    def _():
        qi = qi_ref[...]
        qis = jnp.concatenate([qi[:, h * IDX_DIM:(h + 1) * IDX_DIM] for h in range(IDX_HEADS)],
                              axis=0).astype(BF16)
        w_t = tail_ref[...].T[n_ssd_heads:n_ssd_heads + IDX_HEADS, :] * (IDX_HEADS ** -0.5 * IDX_DIM ** -0.5)
        tpos = qb * Q_BLOCK + lax.broadcasted_iota(I32, (tk, Q_BLOCK), 1)
        kiota = lax.broadcasted_iota(I32, (tk, Q_BLOCK), 0)

        def score_tile(kt, carry):
            rows = pl.ds(pl.multiple_of(kt * tk, tk), tk)
            x = _nt(ki_ref[rows, :].astype(BF16), qis)
            sc = jnp.zeros((tk, Q_BLOCK), F32)
            for h in range(IDX_HEADS):
                sc = sc + jnp.maximum(x[:, h * Q_BLOCK:(h + 1) * Q_BLOCK], 0.0) * w_t[h:h + 1, :]
            key = _float_key(sc + 0.0)
            key_scr[rows, :] = jnp.where(kt * tk + kiota <= tpos, key, imin)
            return carry

        lax.fori_loop(0, nkt, score_tile, 0)

        def count_ge(thr):
            def body(kt, cnt):
                rows = pl.ds(pl.multiple_of(kt * tk, tk), tk)
                ge = (key_scr[rows, :] >= thr).astype(I32)
                return cnt + jnp.sum(ge.reshape(tk // SUBLANES, SUBLANES, Q_BLOCK), axis=0)
            cnt = lax.fori_loop(0, nkt, body, jnp.zeros((SUBLANES, Q_BLOCK), I32))
            return jnp.sum(cnt, axis=0, keepdims=True)

        thr = jnp.maximum(_bisect_kth(count_ge, topk, (1, Q_BLOCK)), jnp.int32(INT_MIN + 1))
        thr_scr[...] = jnp.broadcast_to(thr, (SUBLANES, Q_BLOCK))

    q = q_ref[...]
    nq = rq * Q_BLOCK
    qs = []
    for u in range(gps):
        heads = [q[:, (u * rq + r) * HEAD_DIM:(u * rq + r + 1) * HEAD_DIM] for r in range(rq)]
        qs.append((jnp.concatenate(heads, axis=0) * (HEAD_DIM ** -0.5)).astype(BF16))
    thr_q = jnp.concatenate([thr_scr[0:1, :]] * rq, axis=1)

    def attend(kt, carry):
        rows = pl.ds(pl.multiple_of(kt * tk, tk), tk)
        sel = jnp.concatenate([key_scr[rows, :]] * rq, axis=1) >= thr_q
        k_all = k_ref[rows, :]
        v_all = v_ref[rows, :]
        s_ts = [_nt(k_all[:, u * HEAD_DIM:(u + 1) * HEAD_DIM].astype(BF16), qs[u]) for u in range(gps)]
        v_ts = [v_all[:, u * HEAD_DIM:(u + 1) * HEAD_DIM].T.astype(BF16) for u in range(gps)]
        out = []
        for u in range(gps):
            m, l, acc = carry[u]
            s_t = jnp.where(sel, s_ts[u], NEG_BIG)
            m_new = jnp.maximum(m, jnp.max(s_t, axis=0, keepdims=True))
            alpha = jnp.exp(m - m_new)
            p = jnp.exp(s_t - m_new)
            l = l * alpha + jnp.sum(p, axis=0, keepdims=True)
            acc = acc * alpha + jnp.dot(v_ts[u], p.astype(BF16), preferred_element_type=F32)
            out.append((m_new, l, acc))
        return tuple(out)

    init = tuple((jnp.full((1, nq), M_INIT, F32), jnp.zeros((1, nq), F32), jnp.zeros((HEAD_DIM, nq), F32))
                 for _ in range(gps))
    res = lax.fori_loop(0, nkt, attend, init)
    for u in range(gps):
        _, l, acc = res[u]
        o = (acc / l).T
        for r in range(rq):
            c0 = (u * rq + r) * HEAD_DIM
            o_ref[:, c0:c0 + HEAD_DIM] = o[r * Q_BLOCK:(r + 1) * Q_BLOCK, :]


def _attn_prompt(proj, lay, *, nb, seq, d):
    rq = d // HEAD_DIM // KV_HEADS
    nqb = seq // Q_BLOCK
    topk = min(IDX_TOPK, seq // 4)
    tk = min(ATTN_TK, seq)
    qiw = IDX_HEADS * IDX_DIM
    gps = ATTN_GROUPS_PER_STEP
    qw = gps * rq * HEAD_DIM
    kw = gps * HEAD_DIM
    assert KV_HEADS % gps == 0 and lay["q"] % qw == 0 and lay["k"] % kw == 0 and lay["v"] % kw == 0
    kern = functools.partial(_attn_prompt_kernel, topk=topk, tk=tk, n_ssd_heads=d // SSD_HEADDIM, rq=rq, gps=gps)
    return pl.pallas_call(
        kern,
        grid=(nb, nqb, KV_HEADS // gps),
        in_specs=[
            pl.BlockSpec((Q_BLOCK, qiw), lambda b, i, g: (b * nqb + i, lay["qi"] // qiw)),
            pl.BlockSpec((Q_BLOCK, LANES), lambda b, i, g: (b * nqb + i, lay["tail"] // LANES + 1)),
            pl.BlockSpec((seq, IDX_DIM), lambda b, i, g: (b, lay["tail"] // IDX_DIM)),
            pl.BlockSpec((Q_BLOCK, qw), lambda b, i, g: (b * nqb + i, lay["q"] // qw + g)),
            pl.BlockSpec((seq, kw), lambda b, i, g: (b, lay["k"] // kw + g)),
            pl.BlockSpec((seq, kw), lambda b, i, g: (b, lay["v"] // kw + g)),
        ],
        out_specs=pl.BlockSpec((Q_BLOCK, qw), lambda b, i, g: (b * nqb + i, g)),
        out_shape=jax.ShapeDtypeStruct((nb * seq, d), F32),
        scratch_shapes=[pltpu.VMEM((seq, Q_BLOCK), I32), pltpu.VMEM((SUBLANES, Q_BLOCK), I32)],
        compiler_params=_cparams(("arbitrary", "arbitrary", "arbitrary")),
        name="attn_prompt",
    )(proj, proj, proj, proj, proj, proj)


def _head_rows(page_ref, head):
    return page_ref[pl.ds(head, PAGE_SIZE, stride=KV_HEADS), :]


def _fold_lane_groups(x):
    sh = LANES // 2
    while sh >= SAMPLE_PAD:
        x = x + pltpu.roll(x, sh, 1)
        sh //= 2
    return x


def _sample_scores_kernel(pt_ref, *refs, npages, topk, pps):
    kip_refs = refs[:pps]
    kp_refs = refs[pps:2 * pps]
    qi_ref, w_ref, qbd_ref, kinew_ref, knew_ref, p_ref, key_scr, kc_scr, s_scr, qbd_scr = refs[2 * pps:]
    j = pl.program_id(1)
    imin = jnp.int32(INT_MIN)
    nip = qi_ref.shape[0]
    nlp = qbd_ref.shape[0]
    groups_per_slab = LANES // SAMPLE_PAD
    lane = lax.broadcasted_iota(I32, (PAGE_SIZE, LANES), 1)
    lane_group = lane // SAMPLE_PAD

    @pl.when(j == 0)
    def _():
        qbd_scr[...] = (qbd_ref[...] * (HEAD_DIM ** -0.5)).astype(BF16)
        kc_scr[...] = jnp.full(kc_scr.shape, imin, I32)

    qi = qi_ref[...].astype(BF16)
    w_row = w_ref[...]

    def index_keys(kidx):
        x = _nt(kidx.astype(BF16), qi)
        r = jnp.maximum(x, 0.0) * w_row
        acc = r[:, :LANES]
        for c in range(1, nip // LANES):
            acc = acc + r[:, c * LANES:(c + 1) * LANES]
        return _float_key(_fold_lane_groups(acc) + 0.0)

    def put_page(page, keys, scores):
        rows = pl.ds(pl.multiple_of(page * PAGE_SIZE, PAGE_SIZE), PAGE_SIZE)
        key_scr[rows, :] = keys
        s_scr[rows, :] = scores
        slab = page // groups_per_slab
        kc_scr[slab] = jnp.where(lane_group == page % groups_per_slab, keys, kc_scr[slab])

    kidx_cat = jnp.concatenate([r[...] for r in kip_refs], axis=0)
    keys = index_keys(kidx_cat)
    k2d = jnp.concatenate(
        [jnp.concatenate([_head_rows(kp, h).astype(BF16) for h in range(KV_HEADS)], axis=1)
         for kp in kp_refs], axis=0)
    scores = _nt(k2d, qbd_scr[...])
    for i in range(pps):
        sl = slice(i * PAGE_SIZE, (i + 1) * PAGE_SIZE)
        put_page(j * pps + i, keys[sl], scores[sl])

    @pl.when(j == 0)
    def _():
        knew = index_keys(kinew_ref[...])
        srow = lax.broadcasted_iota(I32, (PAGE_SIZE, LANES), 0)
        knew = jnp.where(srow <= lane % SAMPLE_PAD, knew, imin)
        put_page(jnp.int32(npages), knew, _nt(knew_ref[...].astype(BF16), qbd_scr[...]))

    @pl.when(j == pl.num_programs(1) - 1)
    def _():
        kc = kc_scr[...]

        tok = lane[0:1, :] % SAMPLE_PAD

        def count_ge(thr):
            ge = (kc >= thr).astype(F32)
            cnt = jnp.sum(jnp.sum(ge.reshape(-1, SUBLANES, LANES), axis=0), axis=0, keepdims=True)
            tot = jnp.zeros((1, LANES), F32)
            for t in range(SAMPLE_PAD):
                tot = jnp.where(tok == t, jnp.sum(jnp.where(tok == t, cnt, 0.0), axis=1, keepdims=True), tot)
            return tot

        thr = _bisect_kth(count_ge, topk, (1, LANES))
        reps = nlp // LANES
        thr_q = jnp.concatenate([thr] * reps, axis=1)

        def masked(page):
            rows = pl.ds(pl.multiple_of(page * PAGE_SIZE, PAGE_SIZE), PAGE_SIZE)
            key = jnp.concatenate([key_scr[rows, :]] * reps, axis=1)
            return rows, (key >= thr_q) & (key != imin)

        def max_body(page, m):
            rows, sel = masked(page)
            return jnp.maximum(m, jnp.max(jnp.where(sel, s_scr[rows, :], NEG_BIG), axis=0, keepdims=True))

        m = lax.fori_loop(0, npages + 1, max_body, jnp.full((1, nlp), NEG_BIG, F32))

        def exp_body(page, l):
            rows, sel = masked(page)
            e = jnp.where(sel, jnp.exp(s_scr[rows, :] - m), 0.0)
            s_scr[rows, :] = e
            return l + jnp.sum(e, axis=0, keepdims=True)

        l = lax.fori_loop(0, npages + 1, exp_body, jnp.zeros((1, nlp), F32))

        def out_body(page, carry):
            rows = pl.ds(pl.multiple_of(page * PAGE_SIZE, PAGE_SIZE), PAGE_SIZE)
            p_ref[page] = (s_scr[rows, :] / l).T.astype(BF16)
            return carry

        lax.fori_loop(0, npages + 1, out_body, 0)


def _sample_pv_kernel(pt_ref, *refs, pps, rq):
    vp_refs = refs[:pps]
    p_ref, pnew_ref, vnew_ref, o_ref, acc_scr = refs[pps:]
    j = pl.program_id(1)
    nrow = rq * SAMPLE_PAD

    @pl.when(j == 0)
    def _():
        pn = pnew_ref[0]
        vn = vnew_ref[...].astype(BF16)
        for g in range(KV_HEADS):
            acc_scr[g] = jnp.dot(pn[g * nrow:(g + 1) * nrow, :], vn[:, g * HEAD_DIM:(g + 1) * HEAD_DIM],
                                 preferred_element_type=F32)

    pcat = jnp.concatenate([p_ref[i] for i in range(pps)], axis=1)
    for g in range(KV_HEADS):
        vg = jnp.concatenate([_head_rows(vp, g).astype(BF16) for vp in vp_refs], axis=0)
        acc_scr[g] += jnp.dot(pcat[g * nrow:(g + 1) * nrow, :], vg, preferred_element_type=F32)

    @pl.when(j == pl.num_programs(1) - 1)
    def _():
        o_ref[...] = acc_scr[...]


def _attn_sample(q, qi, wi, ki_new, k_new, v_new, cache_k, cache_v, cache_kidx, page_table, *, t_valid):
    nb, npages = page_table.shape
    pps = min(PAGES_PER_STEP, npages)
    assert npages % pps == 0
    nsteps = npages // pps
    past = npages * PAGE_SIZE
    topk = min(IDX_TOPK, (past + t_valid) // 4)
    d = q.shape[-1]
    rq = d // HEAD_DIM // KV_HEADS
    kvw = KV_HEADS * HEAD_DIM
    nrow = rq * SAMPLE_PAD
    ck = cache_k.reshape(-1, PAGE_SIZE * KV_HEADS, HEAD_DIM)
    cv = cache_v.reshape(-1, PAGE_SIZE * KV_HEADS, HEAD_DIM)
    cki = cache_kidx.reshape(-1, PAGE_SIZE, IDX_DIM)

    def pad_to(a, axis, mult):
        n = a.shape[axis]
        widths = [(0, 0)] * a.ndim
        widths[axis] = (0, -(-n // mult) * mult - n)
        return jnp.pad(a, widths)

    qi_s = pad_to(qi.reshape(nb, SAMPLE_PAD, IDX_HEADS, IDX_DIM).transpose(0, 2, 1, 3)
                  .reshape(nb, IDX_HEADS * SAMPLE_PAD, IDX_DIM), 1, LANES)
    w_row = pad_to((wi * (IDX_HEADS ** -0.5 * IDX_DIM ** -0.5)).transpose(0, 2, 1)
                   .reshape(nb, 1, IDX_HEADS * SAMPLE_PAD), 2, LANES)
    nip = qi_s.shape[1]
    q_g = q.reshape(nb, SAMPLE_PAD, KV_HEADS, rq, HEAD_DIM).transpose(0, 2, 3, 1, 4).reshape(
        nb, KV_HEADS, nrow, HEAD_DIM)
    qbd = pad_to(jnp.einsum("bgrd,gh->bgrhd", q_g, jnp.eye(KV_HEADS, dtype=q.dtype))
                 .reshape(nb, KV_HEADS * nrow, kvw), 1, LANES)
    nlp = qbd.shape[1]
    pad_rows = ((0, 0), (0, PAGE_SIZE - SAMPLE_PAD), (0, 0))
    kinew_p = jnp.pad(ki_new, pad_rows)
    knew_p = jnp.pad(k_new, pad_rows)
    vnew_p = jnp.pad(v_new, pad_rows)
    npg = npages + 1
    nslab = -(-npg // (LANES // SAMPLE_PAD))

    def kidx_spec(i):
        return pl.BlockSpec((None, PAGE_SIZE, IDX_DIM), lambda b, j, pt, i=i: (pt[b, j * pps + i], 0, 0))

    def kv_specs():
        return [pl.BlockSpec((None, PAGE_SIZE * KV_HEADS, HEAD_DIM), lambda b, j, pt, i=i: (pt[b, j * pps + i], 0, 0))
                for i in range(pps)]

    probs = pl.pallas_call(
        functools.partial(_sample_scores_kernel, npages=npages, topk=topk, pps=pps),
        grid_spec=pltpu.PrefetchScalarGridSpec(
            num_scalar_prefetch=1,
            grid=(nb, nsteps),
            in_specs=[kidx_spec(i) for i in range(pps)] + kv_specs() + [
                pl.BlockSpec((None, nip, IDX_DIM), lambda b, j, pt: (b, 0, 0)),
                pl.BlockSpec((None, 1, nip), lambda b, j, pt: (b, 0, 0)),
                pl.BlockSpec((None, nlp, kvw), lambda b, j, pt: (b, 0, 0)),
                pl.BlockSpec((None, PAGE_SIZE, IDX_DIM), lambda b, j, pt: (b, 0, 0)),
                pl.BlockSpec((None, PAGE_SIZE, kvw), lambda b, j, pt: (b, 0, 0)),
            ],
            out_specs=pl.BlockSpec((None, npg, nlp, PAGE_SIZE), lambda b, j, pt: (b, 0, 0, 0)),
            scratch_shapes=[pltpu.VMEM((npg * PAGE_SIZE, LANES), I32),
                            pltpu.VMEM((nslab, PAGE_SIZE, LANES), I32),
                            pltpu.VMEM((npg * PAGE_SIZE, nlp), F32),
                            pltpu.VMEM((nlp, kvw), BF16)],
        ),
        out_shape=jax.ShapeDtypeStruct((nb, npg, nlp, PAGE_SIZE), BF16),
        compiler_params=_cparams(("arbitrary", "arbitrary")),
        name="sample_scores",
    )(page_table, *([cki] * pps), *([ck] * pps), qi_s, w_row, qbd, kinew_p, knew_p)

    o = pl.pallas_call(
        functools.partial(_sample_pv_kernel, pps=pps, rq=rq),
        grid_spec=pltpu.PrefetchScalarGridSpec(
            num_scalar_prefetch=1,
            grid=(nb, nsteps),
            in_specs=kv_specs() + [
                pl.BlockSpec((None, pps, nlp, PAGE_SIZE), lambda b, j, pt: (b, j, 0, 0)),
                pl.BlockSpec((None, 1, nlp, PAGE_SIZE), lambda b, j, pt: (b, npages, 0, 0)),
                pl.BlockSpec((None, PAGE_SIZE, kvw), lambda b, j, pt: (b, 0, 0)),
            ],
            out_specs=pl.BlockSpec((None, KV_HEADS, nrow, HEAD_DIM), lambda b, j, pt: (b, 0, 0, 0)),
            scratch_shapes=[pltpu.VMEM((KV_HEADS, nrow, HEAD_DIM), F32)],
        ),
        out_shape=jax.ShapeDtypeStruct((nb, KV_HEADS, nrow, HEAD_DIM), F32),
        compiler_params=_cparams(("arbitrary", "arbitrary")),
        name="sample_pv",
    )(page_table, *([cv] * pps), probs, probs, vnew_p)
    return o.reshape(nb, KV_HEADS, rq, SAMPLE_PAD, HEAD_DIM).transpose(0, 3, 1, 2, 4).reshape(nb * SAMPLE_PAD, d)


def _rms_cast_kernel(x_ref, g_ref, o_ref):
    o_ref[...] = _rms(x_ref[...], g_ref[...]).astype(BF16)


def _rms_cast(x, g, *, tm):
    m, d = x.shape
    return pl.pallas_call(
        _rms_cast_kernel,
        grid=(m // tm,),
        in_specs=[pl.BlockSpec((tm, d), lambda i: (i, 0)), pl.BlockSpec((1, d), lambda i: (0, 0))],
        out_specs=pl.BlockSpec((tm, d), lambda i: (i, 0)),
        out_shape=jax.ShapeDtypeStruct((m, d), BF16),
        compiler_params=_cparams(("arbitrary",)),
        name="rms_cast",
    )(x, g.reshape(1, d))


def _outproj_kernel(y_ref, a_ref, w1_ref, w2_ref, x_ref, gate_ref, o_ref):
    acc = jnp.dot(y_ref[...], w1_ref[...], preferred_element_type=F32)
    acc = acc + jnp.dot(a_ref[...], w2_ref[...], preferred_element_type=F32)
    o_ref[...] = x_ref[...] + gate_ref[...] * acc


def _outproj(y_n, attn_n, w, x, gate, *, tm, per_token, rows_per_batch):
    m, d = x.shape
    tn = PROJ_TN
    if per_token:
        gate_spec = pl.BlockSpec((tm, tn), lambda i, j: (i, j))
    else:
        tpb = rows_per_batch // tm
        gate_spec = pl.BlockSpec((None, 1, tn), lambda i, j: (i // tpb, 0, j))
    return pl.pallas_call(
        _outproj_kernel,
        grid=(m // tm, d // tn),
        in_specs=[pl.BlockSpec((tm, d), lambda i, j: (i, 0), pipeline_mode=pl.Buffered(1)),
                  pl.BlockSpec((tm, d), lambda i, j: (i, 0), pipeline_mode=pl.Buffered(1)),
                  pl.BlockSpec((d, tn), lambda i, j: (0, j)),
                  pl.BlockSpec((d, tn), lambda i, j: (1, j)),
                  pl.BlockSpec((tm, tn), lambda i, j: (i, j)),
                  gate_spec],
        out_specs=pl.BlockSpec((tm, tn), lambda i, j: (i, j)),
        out_shape=jax.ShapeDtypeStruct((m, d), F32),
        compiler_params=_cparams(("arbitrary", "arbitrary")),
        name="outproj",
    )(y_n, attn_n, w, w, x, gate)


def _peer_route_kernel(q_ref, k1_ref, k2_ref, s1_ref, s2_ref, tau_ref, cc_ref):
    q = q_ref[...]
    k1 = k1_ref[...].astype(BF16)
    k2 = k2_ref[...].astype(BF16)
    half = PEER_D_KEY // 2
    tm = q.shape[0]
    taus, ccs = [], []
    for h in range(PEER_HEADS):
        base = h * PEER_D_KEY
        s1 = _nt(k1, q[:, base:base + half].astype(BF16))
        s2 = _nt(k2, q[:, base + half:base + PEER_D_KEY].astype(BF16))
        s1_ref[h * PEER_KEYS:(h + 1) * PEER_KEYS, :] = s1
        s2_ref[h * PEER_KEYS:(h + 1) * PEER_KEYS, :] = s2

        def top_vals(x):
            vals = []
            for _ in range(PEER_TOPK):
                m = jnp.max(x, axis=0, keepdims=True)
                vals.append(m)
                x = jnp.where(x == m, -jnp.inf, x)
            return vals

        v1 = top_vals(s1)
        v2 = top_vals(s2)
        v1a = jnp.concatenate(v1, axis=0)
        v2a = jnp.concatenate(v2, axis=0)
        half_k = PEER_TOPK // 2
        cand = jnp.concatenate([v + v2a[:half_k] for v in v1[:half_k]]
                               + [v1[0] + v2a[half_k:], v1a[half_k:] + v2[0]], axis=0) + 0.0
        ckey = _float_key(cand)

        def count_ge(thr, ckey=ckey):
            return jnp.sum((ckey >= thr).astype(I32), axis=0, keepdims=True)

        tau = _key_float(_bisect_kth(count_ge, PEER_TOPK, (1, tm)))
        cmax = cand[0:1, :]
        zsum = jnp.sum(jnp.where(cand >= tau, jnp.exp(cand - cmax), 0.0), axis=0, keepdims=True)
        taus.append(tau)
        ccs.append(cmax + jnp.log(zsum))
    tau_ref[...] = jnp.concatenate(taus, axis=0)
    cc_ref[...] = jnp.concatenate(ccs, axis=0)


def _peer_route(q, k1, k2, *, tm):
    t = q.shape[0]
    rows = PEER_HEADS * PEER_KEYS
    half = PEER_D_KEY // 2
    return pl.pallas_call(
        _peer_route_kernel,
        grid=(t // tm,),
        in_specs=[pl.BlockSpec((tm, PEER_HEADS * PEER_D_KEY), lambda i: (i, 0)),
                  pl.BlockSpec((PEER_KEYS, half), lambda i: (0, 0)),
                  pl.BlockSpec((PEER_KEYS, half), lambda i: (0, 0))],
        out_specs=[pl.BlockSpec((rows, tm), lambda i: (0, i)),
                   pl.BlockSpec((rows, tm), lambda i: (0, i)),
                   pl.BlockSpec((PEER_HEADS, tm), lambda i: (0, i)),
                   pl.BlockSpec((PEER_HEADS, tm), lambda i: (0, i))],
        out_shape=[jax.ShapeDtypeStruct((rows, t), F32), jax.ShapeDtypeStruct((rows, t), F32),
                   jax.ShapeDtypeStruct((PEER_HEADS, t), F32), jax.ShapeDtypeStruct((PEER_HEADS, t), F32)],
        compiler_params=_cparams(("arbitrary",)),
        name="peer_route",
    )(q, k1, k2)


def _peer_dense_kernel(xb_ref, s1_ref, s2_ref, tau_ref, cc_ref, u_ref, v_ref, o_ref, *, te):
    e = pl.program_id(1)

    @pl.when(e == 0)
    def _():
        o_ref[...] = jnp.zeros(o_ref.shape, F32)

    tm = xb_ref.shape[0]
    nsub = MXU_TILE // PEER_KEYS
    coefs = []
    for c in range(te // MXU_TILE):
        ut = _nt(xb_ref[...], u_ref[c * MXU_TILE:(c + 1) * MXU_TILE, :]).T
        for i in range(nsub):
            k = c * nsub + i
            gate = jnp.zeros((PEER_KEYS, tm), F32)
            for h in range(PEER_HEADS):
                sm = s1_ref[k, h:h + 1, :] + s2_ref[h * PEER_KEYS:(h + 1) * PEER_KEYS, :]
                gate = gate + jnp.where(sm >= tau_ref[h:h + 1, :], jnp.exp(sm - cc_ref[h:h + 1, :]), 0.0)
            coefs.append(gate * jax.nn.gelu(ut[i * PEER_KEYS:(i + 1) * PEER_KEYS, :]))
    coef = jnp.concatenate(coefs, axis=0).T.astype(BF16)
    for n in range(o_ref.shape[1] // MXU_TILE):
        cols = slice(n * MXU_TILE, (n + 1) * MXU_TILE)
        o_ref[:, cols] += jnp.dot(coef, v_ref[:, cols], preferred_element_type=F32)


def _peer_dense(xb, s1, s2, tau, cc, u_b, v_b, *, tm):
    t, d = xb.shape
    ne = u_b.shape[0]
    te = min(PEER_TE, ne)
    rows = PEER_HEADS * PEER_KEYS
    once = pl.Buffered(1)
    s1 = s1.reshape(PEER_HEADS, PEER_KEYS, t).transpose(1, 0, 2)
    return pl.pallas_call(
        functools.partial(_peer_dense_kernel, te=te),
        grid=(t // tm, ne // te),
        in_specs=[pl.BlockSpec((tm, d), lambda i, e: (i, 0), pipeline_mode=once),
                  pl.BlockSpec((te // PEER_KEYS, PEER_HEADS, tm), lambda i, e: (e, 0, i)),
                  pl.BlockSpec((rows, tm), lambda i, e: (0, i), pipeline_mode=once),
                  pl.BlockSpec((PEER_HEADS, tm), lambda i, e: (0, i), pipeline_mode=once),
                  pl.BlockSpec((PEER_HEADS, tm), lambda i, e: (0, i), pipeline_mode=once),
                  pl.BlockSpec((te, d), lambda i, e: (e, 0)),
                  pl.BlockSpec((te, d), lambda i, e: (e, 0))],
        out_specs=pl.BlockSpec((tm, d), lambda i, e: (i, 0), pipeline_mode=once),
        out_shape=jax.ShapeDtypeStruct((t, d), F32),
        compiler_params=_cparams(("arbitrary", "arbitrary")),
        name="peer_dense",
    )(xb, s1, s2, tau, cc, u_b, v_b)


def _final_kernel(x_ref, f_ref, gate_ref, g_ref, o_ref):
    o_ref[...] = _rms(x_ref[...] + gate_ref[...] * f_ref[...], g_ref[...])


def _final(x, f, gate, g, *, tm, per_token, rows_per_batch):
    m, d = x.shape
    if per_token:
        gate_spec = pl.BlockSpec((tm, d), lambda i: (i, 0))
    else:
        tpb = rows_per_batch // tm
        gate_spec = pl.BlockSpec((None, 1, d), lambda i: (i // tpb, 0, 0))
    return pl.pallas_call(
        _final_kernel,
        grid=(m // tm,),
        in_specs=[pl.BlockSpec((tm, d), lambda i: (i, 0)),
                  pl.BlockSpec((tm, d), lambda i: (i, 0)),
                  gate_spec,
                  pl.BlockSpec((1, d), lambda i: (0, 0))],
        out_specs=pl.BlockSpec((tm, d), lambda i: (i, 0)),
        out_shape=jax.ShapeDtypeStruct((m, d), F32),
        compiler_params=_cparams(("arbitrary",)),
        name="final_norm",
    )(x, f, gate, g.reshape(1, d))


def _layout(d):
    nh = d // SSD_HEADDIM
    kvw = KV_HEADS * HEAD_DIM
    qiw = IDX_HEADS * IDX_DIM
    xbcw = d + 2 * SSD_GROUPS * D_STATE
    lay, off = {}, 0
    for name, w in (("z", d), ("q", d), ("qi", qiw), ("xbc", xbcw), ("k", kvw), ("v", kvw), ("tail", PROJ_TN)):
        assert w % PROJ_TN == 0
        lay[name] = off
        off += w
    lay["total"] = off
    assert IDX_DIM + nh + IDX_HEADS <= PROJ_TN
    return lay


def _group_cols(xbc, d):
    lead = xbc.shape[:-1]
    gn = SSD_GROUPS * D_STATE
    x = xbc[..., :d].reshape(lead + (SSD_GROUPS, d // SSD_GROUPS))
    b = xbc[..., d:d + gn].reshape(lead + (SSD_GROUPS, D_STATE))
    c = xbc[..., d + gn:].reshape(lead + (SSD_GROUPS, D_STATE))
    return jnp.concatenate([x, b, c], axis=-1).reshape(lead + (d + 2 * gn,))


def _ungroup_cols(xg, d):
    lead = xg.shape[:-1]
    gn = SSD_GROUPS * D_STATE
    xsw = d // SSD_GROUPS
    g3 = xg.reshape(lead + (SSD_GROUPS, xsw + 2 * D_STATE))
    return jnp.concatenate([g3[..., :xsw].reshape(lead + (d,)),
                            g3[..., xsw:xsw + D_STATE].reshape(lead + (gn,)),
                            g3[..., xsw + D_STATE:].reshape(lead + (gn,))], axis=-1)


def _w_in_moves(d):
    nh = d // SSD_HEADDIM
    kvw = KV_HEADS * HEAD_DIM
    qiw = IDX_HEADS * IDX_DIM
    xbcw = d + 2 * SSD_GROUPS * D_STATE
    names = ("z", "xbc", "dt", "q", "k", "v", "qi", "wi", "ki")
    sizes = (d, xbcw, nh, d, kvw, kvw, qiw, IDX_HEADS, IDX_DIM)
    src = dict(zip(names, np.concatenate([[0], np.cumsum(sizes)[:-1]]).tolist()))
    lay = _layout(d)
    moves = [(lay["z"], src["z"], d), (lay["q"], src["q"], d), (lay["qi"], src["qi"], qiw),
             (lay["k"], src["k"], kvw), (lay["v"], src["v"], kvw)]
    xsw = d // SSD_GROUPS
    gw = xsw + 2 * D_STATE
    gn = SSD_GROUPS * D_STATE
    for g in range(SSD_GROUPS):
        base = lay["xbc"] + g * gw
        moves += [(base, src["xbc"] + g * xsw, xsw),
                  (base + xsw, src["xbc"] + d + g * D_STATE, D_STATE),
                  (base + xsw + D_STATE, src["xbc"] + d + gn + g * D_STATE, D_STATE)]
    tail = [(src["ki"], IDX_DIM), (src["dt"], nh), (src["wi"], IDX_HEADS)]
    return moves, tail, int(sum(sizes))


def _w_in_prep_kernel(x_ref, o_ref, *, moves, tail, ncols):
    def take(src, w):
        lo = src // LANES * LANES
        hi = min(-(-(src + w) // LANES) * LANES, ncols)
        return x_ref[:, lo:hi][:, src - lo:src - lo + w]

    for dst, src, w in moves:
        o_ref[:, dst:dst + w] = take(src, w).astype(BF16)
    tr = x_ref.shape[0]
    pieces = [take(s, w) for s, w in tail]
    pieces.append(jnp.zeros((tr, PROJ_TN - sum(w for _, w in tail)), F32))
    tail_dst = o_ref.shape[1] - PROJ_TN
    o_ref[:, tail_dst:] = jnp.concatenate(pieces, axis=1).astype(BF16)


def _prep_w_in(w_in, d):
    moves, tail, ncols = _w_in_moves(d)
    total = _layout(d)["total"]
    tr = min(d, W_IN_PREP_ROWS)
    return pl.pallas_call(
        functools.partial(_w_in_prep_kernel, moves=tuple(moves), tail=tuple(tail), ncols=ncols),
        grid=(d // tr,),
        in_specs=[pl.BlockSpec((tr, ncols), lambda i: (i, 0))],
        out_specs=pl.BlockSpec((tr, total), lambda i: (i, 0)),
        out_shape=jax.ShapeDtypeStruct((d, total), BF16),
        compiler_params=_cparams(("arbitrary",)),
        name="w_in_prep",
    )(w_in)


def _group_major(vec, rpg):
    return jnp.pad(vec.reshape(SSD_GROUPS, 1, rpg), ((0, 0), (0, 0), (0, LANES - rpg)))


def _rope_tables(pos):
    half = HEAD_DIM // 8
    inv = ROPE_THETA ** (-jnp.arange(half, dtype=F32) / half)
    ang = pos.astype(F32)[:, None] * inv[None, :]
    cos, sin = jnp.cos(ang), jnp.sin(ang)
    n = pos.shape[0]
    rest = HEAD_DIM - 2 * half
    return (jnp.concatenate([cos, cos, jnp.ones((n, rest), F32)], axis=1),
            jnp.concatenate([-sin, sin, jnp.zeros((n, rest), F32)], axis=1))


def _layer(x2d, mods, pos, prm, *, nb, seq, per_token, tm, ssd_cfg, attn_fn, peer_tm):
    m, d = x2d.shape
    lay = prm["lay"]
    nh = d // SSD_HEADDIM
    rpg = nh // SSD_GROUPS
    sh1, sc1, g1, sh2, sc2, g2 = mods
    cos, sin = _rope_tables(pos)
    xoff = lay["xbc"]
    rope = (cos, sin, prm["kidx_norm_g"],
            ((lay["q"] // PROJ_TN, lay["xbc"] // PROJ_TN), (lay["k"] // PROJ_TN, lay["v"] // PROJ_TN)),
            lay["tail"] // PROJ_TN)
    tm_e = min(tm, ELEMWISE_TM)
    tm_p = min(m if per_token else seq, PROJ_TM)
    h1 = _modulate(x2d, prm["norm1_g"], sh1, sc1, tm=tm_e, per_token=per_token, rows_per_batch=seq)
    proj = _proj(h1, prm["w_in"], tm=tm_p, rope=rope)

    gw = rpg * SSD_HEADDIM + 2 * D_STATE
    dt_raw = proj[:, lay["tail"] + IDX_DIM: lay["tail"] + IDX_DIM + nh]
    y_ssd, ssm_new = ssd_cfg(proj, dt_raw, xoff // gw)
    y_n = _gate_norm(y_ssd, proj, prm["ssd_norm_g"], tm=tm_e)

    o_attn = attn_fn(proj)

    a_n = _rms_cast(o_attn, prm["attn_norm_g"], tm=tm_e)
    x1 = _outproj(y_n, a_n, prm["w_out"], x2d, g1, tm=tm_p, per_token=per_token,
                  rows_per_batch=seq)

    xb = _modulate(x1, prm["norm2_g"], sh2, sc2, tm=tm_e, per_token=per_token, rows_per_batch=seq)
    qp = _proj(xb, prm["peer_wq"], tm=tm_p)
    s1, s2, tau, cc = _peer_route(qp, prm["peer_k1"], prm["peer_k2"], tm=min(peer_tm, PEER_ROUTE_TM))
    ffn = _peer_dense(xb, s1, s2, tau, cc, prm["peer_u"], prm["peer_v"], tm=peer_tm)
    y = _final(x1, ffn, g2, prm["final_norm_g"], tm=tm_e, per_token=per_token, rows_per_batch=seq)
    return y, proj, ssm_new


def kernel(x_prompt, x_sample, c_prompt, c_sample, cache_k, cache_v, cache_kidx, state_ssm, state_conv, page_table, w_mod, b_mod, norm1_g, w_in, conv_w, conv_b, dt_bias, a_log, d_skip, ssd_norm_g, kidx_norm_g, attn_norm_g, w_out, norm2_g, peer_wq, peer_k1, peer_k2, peer_u, peer_v, final_norm_g):
    nbp, seq, d = x_prompt.shape
    nbs, tdec, _ = x_sample.shape
    depth = w_mod.shape[0]
    assert depth == 1
    nh = d // SSD_HEADDIM
    rpg = nh // SSD_GROUPS
    gw = rpg * SSD_HEADDIM + 2 * D_STATE
    lay = _layout(d)
    assert lay["xbc"] % gw == 0
    kvw = KV_HEADS * HEAD_DIM
    xbcw = d + 2 * SSD_GROUPS * D_STATE

    prm = dict(
        lay=lay,
        norm1_g=norm1_g[0], kidx_norm_g=kidx_norm_g[0], ssd_norm_g=ssd_norm_g[0], attn_norm_g=attn_norm_g[0],
        norm2_g=norm2_g[0], final_norm_g=final_norm_g,
        w_in=_prep_w_in(w_in[0], d),
        w_out=_cast_bf16(w_out[0]),
        peer_wq=_cast_bf16(peer_wq[0]), peer_k1=peer_k1[0], peer_k2=peer_k2[0],
        peer_u=_cast_bf16(peer_u[0]), peer_v=_cast_bf16(peer_v[0]),
    )
    conv_w_g = _group_cols(conv_w[0], d)
    conv_b_g = _group_cols(conv_b[0], d).reshape(1, xbcw)
    dtb_g = _group_major(dt_bias[0], rpg)
    alog_g = _group_major(a_log[0], rpg)
    dsk_g = _group_major(d_skip[0], rpg)

    mod = _adaln(jnp.concatenate([c_prompt, c_sample], axis=0), w_mod[0], b_mod[0])
    mod_p = [a.reshape(nbp, 1, d) for a in jnp.split(mod[:nbp], 6, axis=-1)]
    mod_s = [jnp.repeat(a, SAMPLE_PAD, axis=0) for a in jnp.split(mod[nbp:], 6, axis=-1)]

    def dt_group_major(dt_raw):
        rows = dt_raw.shape[0]
        dtg = dt_raw.reshape(rows, SSD_GROUPS, rpg).transpose(1, 0, 2)
        return jnp.pad(dtg, ((0, 0), (0, 0), (0, LANES - rpg)))

    cs_p = min(SSD_CHUNK, seq)
    nc_p = seq // cs_p
    assert seq % cs_p == 0 and seq % Q_BLOCK == 0

    def ssd_prompt(proj, dt_raw, xoff_blocks):
        hist = jnp.zeros((nbp, SUBLANES, xbcw), F32)
        h0 = jnp.zeros((nbp, nh, SSD_HEADDIM, D_STATE), F32)
        return _ssd(proj, xoff_blocks, hist, conv_w_g, conv_b_g, dt_group_major(dt_raw), dtb_g, alog_g, dsk_g, h0,
                    nb=nbp, nc=nc_p, cs=cs_p, d=d, valid_len=cs_p)

    tm_p = min(512, seq)
    pos_p = jnp.tile(jnp.arange(seq), nbp)
    y_p, proj_p, ssm_p = _layer(
        x_prompt.reshape(nbp * seq, d), mod_p, pos_p, prm, nb=nbp, seq=seq, per_token=False, tm=tm_p,
        ssd_cfg=ssd_prompt, attn_fn=functools.partial(_attn_prompt, lay=lay, nb=nbp, seq=seq, d=d),
        peer_tm=min(PEER_TM, nbp * seq))

    npages = page_table.shape[1]
    past = npages * PAGE_SIZE
    ms = nbs * SAMPLE_PAD
    xs_pad = jnp.pad(x_sample, ((0, 0), (0, SAMPLE_PAD - tdec), (0, 0))).reshape(ms, d)
    pos_s = jnp.tile(past + jnp.arange(SAMPLE_PAD), nbs)

    def ssd_sample(proj, dt_raw, xoff_blocks):
        xbc = proj[:, lay["xbc"]:lay["xbc"] + xbcw].reshape(nbs, SAMPLE_PAD, xbcw)
        xbc = jnp.pad(xbc, ((0, 0), (0, SAMPLE_CHUNK - SAMPLE_PAD), (0, 0))).reshape(nbs * SAMPLE_CHUNK, xbcw)
        dtr = jnp.pad(dt_raw.reshape(nbs, SAMPLE_PAD, nh), ((0, 0), (0, SAMPLE_CHUNK - SAMPLE_PAD), (0, 0)))
        hist = jnp.pad(_group_cols(state_conv[0], d), ((0, 0), (SUBLANES - (CONV_W - 1), 0), (0, 0)))
        y, hnew = _ssd(xbc, 0, hist, conv_w_g, conv_b_g, dt_group_major(dtr.reshape(nbs * SAMPLE_CHUNK, nh)),
                       dtb_g, alog_g, dsk_g, state_ssm[0], nb=nbs, nc=1, cs=SAMPLE_CHUNK, d=d, valid_len=tdec)
        y = y.reshape(nbs, SAMPLE_CHUNK, d)[:, :SAMPLE_PAD].reshape(ms, d)
        return y, hnew

    def attn_sample(proj):
        def seg(name, w):
            return proj[:, lay[name]:lay[name] + w].reshape(nbs, SAMPLE_PAD, w)
        tail = lay["tail"]
        ki_new = proj[:, tail:tail + IDX_DIM].reshape(nbs, SAMPLE_PAD, IDX_DIM)
        wi = proj[:, tail + IDX_DIM + nh: tail + IDX_DIM + nh + IDX_HEADS].reshape(nbs, SAMPLE_PAD, IDX_HEADS)
        return _attn_sample(seg("q", d), seg("qi", IDX_HEADS * IDX_DIM), wi, ki_new, seg("k", kvw), seg("v", kvw),
                            cache_k, cache_v, cache_kidx, page_table, t_valid=tdec)

    y_s, proj_s, ssm_s = _layer(
        xs_pad, mod_s, pos_s, prm, nb=nbs, seq=SAMPLE_PAD, per_token=True, tm=ms,
        ssd_cfg=ssd_sample, attn_fn=attn_sample, peer_tm=ms)

    def states(proj, nb, rows, valid):
        p3 = proj.reshape(nb, rows, lay["total"])[:, :valid]
        k = p3[:, :, lay["k"]:lay["k"] + kvw].reshape(1, nb, valid, KV_HEADS, HEAD_DIM)
        v = p3[:, :, lay["v"]:lay["v"] + kvw].reshape(1, nb, valid, KV_HEADS, HEAD_DIM)
        ki = p3[:, :, lay["tail"]:lay["tail"] + IDX_DIM].reshape(1, nb, valid, IDX_DIM)
        xbc = _ungroup_cols(p3[:, valid - (CONV_W - 1):valid, lay["xbc"]:lay["xbc"] + xbcw], d)
        return k, v, ki, xbc.reshape(1, nb, CONV_W - 1, xbcw)

    k_p, v_p, ki_p, conv_p = states(proj_p, nbp, seq, seq)
    k_s, v_s, ki_s, conv_s = states(proj_s, nbs, SAMPLE_PAD, tdec)
    y_prompt = y_p.reshape(nbp, seq, d)
    y_sample = y_s.reshape(nbs, SAMPLE_PAD, d)[:, :tdec]
    return (y_prompt, y_sample, k_p, v_p, ki_p, ssm_p[None], conv_p,
            k_s, v_s, ki_s, ssm_s[None], conv_s)
```

```python
import functools
import math

import jax
import jax.numpy as jnp
import numpy as np
from jax import lax
from jax.experimental import pallas as pl
from jax.experimental.pallas import tpu as pltpu

F32 = jnp.float32
BF16 = jnp.bfloat16
I32 = jnp.int32

SSD_HEADDIM = 64
SSD_GROUPS = 8
D_STATE = 128
CONV_W = 4
SSD_CHUNK = 256
HEAD_DIM = 128
KV_HEADS = 8
ROPE_THETA = 500000.0
IDX_HEADS = 32
IDX_DIM = 128
IDX_TOPK = 256
Q_BLOCK = 128
PEER_HEADS = 8
PEER_KEYS = 128
PEER_TOPK = 16
PEER_D_KEY = 256
PAGE_SIZE = 128
EPS = 1e-6

LANES = 128
SUBLANES = 8
VMEM_LIMIT_BYTES = 56 * 1024 * 1024

MXU_TILE = 256
PROJ_TN = MXU_TILE
PROJ_TM = 1024
ELEMWISE_TM = 256
CAST_ROWS = 512
W_IN_PREP_ROWS = 128
PEER_ROUTE_TM = 256
PEER_TM = 1024
PEER_TE = 512
SAMPLE_PAD = 8
SAMPLE_CHUNK = 128
ATTN_TK = 512
ATTN_GROUPS_PER_STEP = 2
PAGES_PER_STEP = 8
NEG_BIG = -1e30
M_INIT = -1e29
INT_MIN = -2 ** 31


def _cparams(sem):
    return pltpu.CompilerParams(dimension_semantics=sem, vmem_limit_bytes=VMEM_LIMIT_BYTES)


def _nt(a, b):
    return lax.dot_general(a, b, (((1,), (1,)), ((), ())), preferred_element_type=F32)


def _tn(a, b):
    return lax.dot_general(a, b, (((0,), (0,)), ((), ())), preferred_element_type=F32)


def _rms(x, g):
    return x * lax.rsqrt(jnp.mean(x * x, axis=-1, keepdims=True) + EPS) * g


def _rope_tile(a, cos, sin):
    half = HEAD_DIM // 8
    lane = lax.broadcasted_iota(I32, a.shape, 1)
    sw = jnp.where(lane < half, pltpu.roll(a, HEAD_DIM - half, 1), pltpu.roll(a, half, 1))
    return a * cos + sw * sin


def _float_key(x):
    bits = pltpu.bitcast(x, I32)
    return jnp.where(bits < 0, bits ^ jnp.int32(0x7FFFFFFF), bits)


def _key_float(key):
    bits = jnp.where(key < 0, key ^ jnp.int32(0x7FFFFFFF), key)
    return pltpu.bitcast(bits, F32)


def _bisect_kth(count_ge, k, shape):
    imin = jnp.int32(INT_MIN)

    def body(i, ans):
        cand_u = ans | lax.shift_left(jnp.int32(1), jnp.int32(31) - i)
        return jnp.where(count_ge(cand_u ^ imin) >= k, cand_u, ans)

    ans = lax.fori_loop(0, 32, body, jnp.zeros(shape, I32))
    return ans ^ imin


def _cast_kernel(x_ref, o_ref):
    o_ref[...] = x_ref[...].astype(BF16)


def _cast_bf16(w):
    r, c = w.shape
    tr = min(r, CAST_ROWS)
    assert r % tr == 0
    return pl.pallas_call(
        _cast_kernel,
        grid=(r // tr,),
        in_specs=[pl.BlockSpec((tr, c), lambda i: (i, 0))],
        out_specs=pl.BlockSpec((tr, c), lambda i: (i, 0)),
        out_shape=jax.ShapeDtypeStruct((r, c), BF16),
        compiler_params=_cparams(("arbitrary",)),
        name="cast_bf16",
    )(w)


def _adaln_kernel(c_ref, w_ref, b_ref, o_ref):
    c = c_ref[...]
    a = (c * jax.nn.sigmoid(c)).astype(BF16)
    o_ref[...] = jnp.dot(a, w_ref[...].astype(BF16), preferred_element_type=F32) + b_ref[...]


def _adaln(c, w_mod, b_mod):
    n, d = c.shape
    npad = -(-n // SUBLANES) * SUBLANES
    cp = jnp.pad(c, ((0, npad - n), (0, 0)))
    nout = w_mod.shape[1]
    tn = PROJ_TN
    out = pl.pallas_call(
        _adaln_kernel,
        grid=(nout // tn,),
        in_specs=[pl.BlockSpec((npad, d), lambda j: (0, 0)),
                  pl.BlockSpec((d, tn), lambda j: (0, j)),
                  pl.BlockSpec((1, tn), lambda j: (0, j))],
        out_specs=pl.BlockSpec((npad, tn), lambda j: (0, j)),
        out_shape=jax.ShapeDtypeStruct((npad, nout), F32),
        compiler_params=_cparams(("arbitrary",)),
        name="adaln",
    )(cp, w_mod, b_mod.reshape(1, nout))
    return out[:n]


def _modulate_kernel(x_ref, g_ref, sh_ref, sc_ref, o_ref):
    o_ref[...] = (_rms(x_ref[...], g_ref[...]) * (1.0 + sc_ref[...]) + sh_ref[...]).astype(BF16)


def _modulate(x, g, shift, scale, *, tm, per_token, rows_per_batch):
    m, d = x.shape
    if per_token:
        mod_spec = pl.BlockSpec((tm, d), lambda i: (i, 0))
    else:
        tiles_per_batch = rows_per_batch // tm
        mod_spec = pl.BlockSpec((None, 1, d), lambda i: (i // tiles_per_batch, 0, 0))
    return pl.pallas_call(
        _modulate_kernel,
        grid=(m // tm,),
        in_specs=[pl.BlockSpec((tm, d), lambda i: (i, 0)),
                  pl.BlockSpec((1, d), lambda i: (0, 0)),
                  mod_spec, mod_spec],
        out_specs=pl.BlockSpec((tm, d), lambda i: (i, 0)),
        out_shape=jax.ShapeDtypeStruct((m, d), BF16),
        compiler_params=_cparams(("arbitrary",)),
        name="modulate",
    )(x, g.reshape(1, d), shift, scale)


def _proj_kernel(*refs, rope_ranges, tail_tile):
    if tail_tile is not None:
        h_ref, w_ref, cos_ref, sin_ref, kg_ref, o_ref = refs
    else:
        h_ref, w_ref, o_ref = refs
    j = pl.program_id(1)

    def product():
        return jnp.dot(h_ref[...], w_ref[...], preferred_element_type=F32)

    if tail_tile is None:
        o_ref[...] = product()
        return

    is_rope = functools.reduce(jnp.logical_or, [(j >= lo) & (j < hi) for lo, hi in rope_ranges])
    is_tail = j == tail_tile
    tn = w_ref.shape[1]

    @pl.when(jnp.logical_not(is_rope | is_tail))
    def _():
        o_ref[...] = product()

    @pl.when(is_rope)
    def _():
        acc = product()
        cos = cos_ref[...]
        sin = sin_ref[...]
        for t in range(tn // HEAD_DIM):
            sl = slice(t * HEAD_DIM, (t + 1) * HEAD_DIM)
            o_ref[:, sl] = _rope_tile(acc[:, sl], cos, sin)

    @pl.when(is_tail)
    def _():
        acc = product()
        ki = _rms(acc[:, :IDX_DIM], kg_ref[...])
        o_ref[:, :IDX_DIM] = _rope_tile(ki, cos_ref[...], sin_ref[...])
        o_ref[:, IDX_DIM:] = acc[:, IDX_DIM:]


def _proj(h, w, *, tm, rope=None):
    m, d = h.shape
    n = w.shape[1]
    tn = PROJ_TN
    assert m % tm == 0 and n % tn == 0
    in_specs = [pl.BlockSpec((tm, d), lambda i, j: (i, 0), pipeline_mode=pl.Buffered(1)),
                pl.BlockSpec((d, tn), lambda i, j: (0, j))]
    args = [h, w]
    rope_ranges, tail_tile = (), None
    if rope is not None:
        cos, sin, kidx_g, rope_ranges, tail_tile = rope
        in_specs += [pl.BlockSpec((tm, HEAD_DIM), lambda i, j: (i, 0)),
                     pl.BlockSpec((tm, HEAD_DIM), lambda i, j: (i, 0)),
                     pl.BlockSpec((1, IDX_DIM), lambda i, j: (0, 0))]
        args += [cos, sin, kidx_g.reshape(1, IDX_DIM)]
    return pl.pallas_call(
        functools.partial(_proj_kernel, rope_ranges=rope_ranges, tail_tile=tail_tile),
        grid=(m // tm, n // tn),
        in_specs=in_specs,
        out_specs=pl.BlockSpec((tm, tn), lambda i, j: (i, j)),
        out_shape=jax.ShapeDtypeStruct((m, n), F32),
        compiler_params=_cparams(("arbitrary", "arbitrary")),
        name="proj",
    )(*args)


def _ssd_kernel(x_ref, prev_ref, hist_ref, cw_ref, cb_ref, dt_ref, dtb_ref, alog_ref, dsk_ref, h0_ref,
                y_ref, hout_ref, h_scr, *, cs, rpg, valid_len, nc):
    c = pl.program_id(2)
    xs_w = rpg * SSD_HEADDIM

    @pl.when(c == 0)
    def _():
        h_scr[...] = h0_ref[...]

    prev = jnp.where(c == 0, hist_ref[...], prev_ref[...])
    cat = jnp.concatenate([prev, x_ref[...]], axis=0)
    w = cw_ref[...]
    acc = cb_ref[...]
    for j in range(CONV_W):
        lo = SUBLANES - (CONV_W - 1) + j
        acc = acc + cat[lo:lo + cs] * w[j:j + 1]
    xc = acc * jax.nn.sigmoid(acc)
    xs = xc[:, :xs_w]
    bm = xc[:, xs_w:xs_w + D_STATE]
    cm = xc[:, xs_w + D_STATE:]
    bm_b = bm.astype(BF16)
    cm_b = cm.astype(BF16)

    z = dt_ref[...] + dtb_ref[...]
    dt = jnp.maximum(z, 0.0) + jnp.log1p(jnp.exp(-jnp.abs(z)))
    if valid_len < cs:
        row = lax.broadcasted_iota(I32, dt.shape, 0)
        dt = jnp.where(row < valid_len, dt, 0.0)
    a_neg = -jnp.exp(alog_ref[...])
    ii = lax.broadcasted_iota(I32, (cs, cs), 0)
    jj = lax.broadcasted_iota(I32, (cs, cs), 1)
    causal = ii >= jj
    acs = jnp.dot(causal.astype(F32), dt * a_neg, preferred_element_type=F32,
                  precision=lax.Precision.HIGHEST)
    acs_t = acs.T
    a_last = acs[cs - 1:cs, :]
    cb = _nt(cm_b, bm_b)
    dsk = dsk_ref[...]

    first = lax.broadcasted_iota(I32, (cs, LANES), 1) < SSD_HEADDIM
    first_row = lax.broadcasted_iota(I32, (LANES, D_STATE), 0) < SSD_HEADDIM
    ys = []
    for pr in range(rpg // 2):
        r0, r1 = 2 * pr, 2 * pr + 1
        a0, a1 = acs[:, r0:r0 + 1], acs[:, r1:r1 + 1]
        xp = xs[:, pr * LANES:(pr + 1) * LANES]
        xd = xp * jnp.where(first, dt[:, r0:r0 + 1], dt[:, r1:r1 + 1])
        y = jnp.zeros((cs, LANES), F32)
        for r, keep in ((r0, first), (r1, jnp.logical_not(first))):
            lm = jnp.exp(jnp.where(causal, acs[:, r:r + 1] - acs_t[r:r + 1, :], -jnp.inf))
            y = y + jnp.dot((cb * lm).astype(BF16), jnp.where(keep, xd, 0.0).astype(BF16),
                            preferred_element_type=F32)
        hp = h_scr[pr]
        y = y + jnp.where(first, jnp.exp(a0), jnp.exp(a1)) * _nt(cm_b, hp.astype(BF16))
        al0, al1 = a_last[:, r0:r0 + 1], a_last[:, r1:r1 + 1]
        decay = jnp.where(first, jnp.exp(al0 - a0), jnp.exp(al1 - a1))
        h_scr[pr] = (hp * jnp.where(first_row, jnp.exp(al0), jnp.exp(al1))
                     + _tn((xd * decay).astype(BF16), bm_b))
        ys.append(y + xp * jnp.where(first[0:1, :], dsk[:, r0:r0 + 1], dsk[:, r1:r1 + 1]))
    y_ref[...] = jnp.concatenate(ys, axis=1)

    @pl.when(c == nc - 1)
    def _():
        hout_ref[...] = h_scr[...]


def _ssd(xg, xoff_blocks, hist8, conv_w_g, conv_b_g, dt_g, dtb_g, alog_g, dsk_g, h0, *, nb, nc, cs, d, valid_len):
    g_cnt = SSD_GROUPS
    rpg = d // SSD_HEADDIM // g_cnt
    gw = rpg * SSD_HEADDIM + 2 * D_STATE
    m = nb * nc * cs
    cpb = cs // SUBLANES
    nh = d // SSD_HEADDIM
    npair = rpg // 2
    assert rpg % 2 == 0 and 2 * SSD_HEADDIM == LANES
    kern = functools.partial(_ssd_kernel, cs=cs, rpg=rpg, valid_len=valid_len, nc=nc)
    y, hout = pl.pallas_call(
        kern,
        grid=(nb, g_cnt, nc),
        in_specs=[
            pl.BlockSpec((cs, gw), lambda b, g, c: (b * nc + c, xoff_blocks + g)),
            pl.BlockSpec((SUBLANES, gw), lambda b, g, c: (jnp.maximum((b * nc + c) * cpb - 1, 0), xoff_blocks + g)),
            pl.BlockSpec((None, SUBLANES, gw), lambda b, g, c: (b, 0, g)),
            pl.BlockSpec((CONV_W, gw), lambda b, g, c: (0, g)),
            pl.BlockSpec((1, gw), lambda b, g, c: (0, g)),
            pl.BlockSpec((None, cs, LANES), lambda b, g, c: (g, b * nc + c, 0)),
            pl.BlockSpec((None, 1, LANES), lambda b, g, c: (g, 0, 0)),
            pl.BlockSpec((None, 1, LANES), lambda b, g, c: (g, 0, 0)),
            pl.BlockSpec((None, 1, LANES), lambda b, g, c: (g, 0, 0)),
            pl.BlockSpec((None, npair, LANES, D_STATE), lambda b, g, c: (b, g, 0, 0)),
        ],
        out_specs=[
            pl.BlockSpec((cs, rpg * SSD_HEADDIM), lambda b, g, c: (b * nc + c, g)),
            pl.BlockSpec((None, npair, LANES, D_STATE), lambda b, g, c: (b, g, 0, 0)),
        ],
        out_shape=[jax.ShapeDtypeStruct((m, d), F32),
                   jax.ShapeDtypeStruct((nb, nh // 2, LANES, D_STATE), F32)],
        scratch_shapes=[pltpu.VMEM((npair, LANES, D_STATE), F32)],
        compiler_params=_cparams(("arbitrary", "arbitrary", "arbitrary")),
        name="ssd_scan",
    )(xg, xg, hist8, conv_w_g, conv_b_g, dt_g, dtb_g, alog_g, dsk_g, h0.reshape(nb, nh // 2, LANES, D_STATE))
    return y, hout.reshape(nb, nh, SSD_HEADDIM, D_STATE)


def _gate_norm_kernel(y_ref, z_ref, g_ref, o_ref):
    z = z_ref[...]
    o_ref[...] = _rms(y_ref[...] * (z * jax.nn.sigmoid(z)), g_ref[...]).astype(BF16)


def _gate_norm(y, proj, g, *, tm):
    m, d = y.shape
    return pl.pallas_call(
        _gate_norm_kernel,
        grid=(m // tm,),
        in_specs=[pl.BlockSpec((tm, d), lambda i: (i, 0)),
                  pl.BlockSpec((tm, d), lambda i: (i, 0)),
                  pl.BlockSpec((1, d), lambda i: (0, 0))],
        out_specs=pl.BlockSpec((tm, d), lambda i: (i, 0)),
        out_shape=jax.ShapeDtypeStruct((m, d), BF16),
        compiler_params=_cparams(("arbitrary",)),
        name="gate_norm",
    )(y, proj, g.reshape(1, d))


def _attn_prompt_kernel(qi_ref, tail_ref, ki_ref, q_ref, k_ref, v_ref, o_ref, key_scr, thr_scr,
                        *, topk, tk, n_ssd_heads, rq, gps):
    qb = pl.program_id(1)
    g = pl.program_id(2)
    nkt = (qb * Q_BLOCK + Q_BLOCK + tk - 1) // tk
    imin = jnp.int32(INT_MIN)

    @pl.when(g == 0)
    def _():
        qi = qi_ref[...]
        qis = jnp.concatenate([qi[:, h * IDX_DIM:(h + 1) * IDX_DIM] for h in range(IDX_HEADS)],
                              axis=0).astype(BF16)
        w_t = tail_ref[...].T[n_ssd_heads:n_ssd_heads + IDX_HEADS, :] * (IDX_HEADS ** -0.5 * IDX_DIM ** -0.5)
        tpos = qb * Q_BLOCK + lax.broadcasted_iota(I32, (tk, Q_BLOCK), 1)
        kiota = lax.broadcasted_iota(I32, (tk, Q_BLOCK), 0)

        def score_tile(kt, carry):
            rows = pl.ds(pl.multiple_of(kt * tk, tk), tk)
            x = _nt(ki_ref[rows, :].astype(BF16), qis)
            sc = jnp.zeros((tk, Q_BLOCK), F32)
            for h in range(IDX_HEADS):
                sc = sc + jnp.maximum(x[:, h * Q_BLOCK:(h + 1) * Q_BLOCK], 0.0) * w_t[h:h + 1, :]
            key = _float_key(sc + 0.0)
            key_scr[rows, :] = jnp.where(kt * tk + kiota <= tpos, key, imin)
            return carry

        lax.fori_loop(0, nkt, score_tile, 0)

        def count_ge(thr):
            def body(kt, cnt):
                rows = pl.ds(pl.multiple_of(kt * tk, tk), tk)
                ge = (key_scr[rows, :] >= thr).astype(I32)
                return cnt + jnp.sum(ge.reshape(tk // SUBLANES, SUBLANES, Q_BLOCK), axis=0)
            cnt = lax.fori_loop(0, nkt, body, jnp.zeros((SUBLANES, Q_BLOCK), I32))
            return jnp.sum(cnt, axis=0, keepdims=True)

        thr = jnp.maximum(_bisect_kth(count_ge, topk, (1, Q_BLOCK)), jnp.int32(INT_MIN + 1))
        thr_scr[...] = jnp.broadcast_to(thr, (SUBLANES, Q_BLOCK))

    q = q_ref[...]
    nq = rq * Q_BLOCK
    qs = []
    for u in range(gps):
        heads = [q[:, (u * rq + r) * HEAD_DIM:(u * rq + r + 1) * HEAD_DIM] for r in range(rq)]
        qs.append((jnp.concatenate(heads, axis=0) * (HEAD_DIM ** -0.5)).astype(BF16))
    thr_q = jnp.concatenate([thr_scr[0:1, :]] * rq, axis=1)

    def attend(kt, carry):
        rows = pl.ds(pl.multiple_of(kt * tk, tk), tk)
        sel = jnp.concatenate([key_scr[rows, :]] * rq, axis=1) >= thr_q
        k_all = k_ref[rows, :]
        v_all = v_ref[rows, :]
        s_ts = [_nt(k_all[:, u * HEAD_DIM:(u + 1) * HEAD_DIM].astype(BF16), qs[u]) for u in range(gps)]
        v_ts = [v_all[:, u * HEAD_DIM:(u + 1) * HEAD_DIM].T.astype(BF16) for u in range(gps)]
        out = []
        for u in range(gps):
            m, l, acc = carry[u]
            s_t = jnp.where(sel, s_ts[u], NEG_BIG)
            m_new = jnp.maximum(m, jnp.max(s_t, axis=0, keepdims=True))
            alpha = jnp.exp(m - m_new)
            p = jnp.exp(s_t - m_new)
            l = l * alpha + jnp.sum(p, axis=0, keepdims=True)
            acc = acc * alpha + jnp.dot(v_ts[u], p.astype(BF16), preferred_element_type=F32)
            out.append((m_new, l, acc))
        return tuple(out)

    init = tuple((jnp.full((1, nq), M_INIT, F32), jnp.zeros((1, nq), F32), jnp.zeros((HEAD_DIM, nq), F32))
                 for _ in range(gps))
    res = lax.fori_loop(0, nkt, attend, init)
    for u in range(gps):
        _, l, acc = res[u]
        o = (acc / l).T
        for r in range(rq):
            c0 = (u * rq + r) * HEAD_DIM
            o_ref[:, c0:c0 + HEAD_DIM] = o[r * Q_BLOCK:(r + 1) * Q_BLOCK, :]


def _attn_prompt(proj, lay, *, nb, seq, d):
    rq = d // HEAD_DIM // KV_HEADS
    nqb = seq // Q_BLOCK
    topk = min(IDX_TOPK, seq // 4)
    tk = min(ATTN_TK, seq)
    qiw = IDX_HEADS * IDX_DIM
    gps = ATTN_GROUPS_PER_STEP
    qw = gps * rq * HEAD_DIM
    kw = gps * HEAD_DIM
    assert KV_HEADS % gps == 0 and lay["q"] % qw == 0 and lay["k"] % kw == 0 and lay["v"] % kw == 0
    kern = functools.partial(_attn_prompt_kernel, topk=topk, tk=tk, n_ssd_heads=d // SSD_HEADDIM, rq=rq, gps=gps)
    return pl.pallas_call(
        kern,
        grid=(nb, nqb, KV_HEADS // gps),
        in_specs=[
            pl.BlockSpec((Q_BLOCK, qiw), lambda b, i, g: (b * nqb + i, lay["qi"] // qiw)),
            pl.BlockSpec((Q_BLOCK, LANES), lambda b, i, g: (b * nqb + i, lay["tail"] // LANES + 1)),
            pl.BlockSpec((seq, IDX_DIM), lambda b, i, g: (b, lay["tail"] // IDX_DIM)),
            pl.BlockSpec((Q_BLOCK, qw), lambda b, i, g: (b * nqb + i, lay["q"] // qw + g)),
            pl.BlockSpec((seq, kw), lambda b, i, g: (b, lay["k"] // kw + g)),
            pl.BlockSpec((seq, kw), lambda b, i, g: (b, lay["v"] // kw + g)),
        ],
        out_specs=pl.BlockSpec((Q_BLOCK, qw), lambda b, i, g: (b * nqb + i, g)),
        out_shape=jax.ShapeDtypeStruct((nb * seq, d), F32),
        scratch_shapes=[pltpu.VMEM((seq, Q_BLOCK), I32), pltpu.VMEM((SUBLANES, Q_BLOCK), I32)],
        compiler_params=_cparams(("arbitrary", "arbitrary", "arbitrary")),
        name="attn_prompt",
    )(proj, proj, proj, proj, proj, proj)


def _head_rows(page_ref, head):
    return page_ref[pl.ds(head, PAGE_SIZE, stride=KV_HEADS), :]


def _fold_lane_groups(x):
    sh = LANES // 2
    while sh >= SAMPLE_PAD:
        x = x + pltpu.roll(x, sh, 1)
        sh //= 2
    return x


def _sample_scores_kernel(pt_ref, *refs, npages, topk, pps):
    kip_refs = refs[:pps]
    kp_refs = refs[pps:2 * pps]
    qi_ref, w_ref, qbd_ref, kinew_ref, knew_ref, p_ref, key_scr, kc_scr, s_scr, qbd_scr = refs[2 * pps:]
    j = pl.program_id(1)
    imin = jnp.int32(INT_MIN)
    nip = qi_ref.shape[0]
    nlp = qbd_ref.shape[0]
    groups_per_slab = LANES // SAMPLE_PAD
    lane = lax.broadcasted_iota(I32, (PAGE_SIZE, LANES), 1)
    lane_group = lane // SAMPLE_PAD

    @pl.when(j == 0)
    def _():
        qbd_scr[...] = (qbd_ref[...] * (HEAD_DIM ** -0.5)).astype(BF16)
        kc_scr[...] = jnp.full(kc_scr.shape, imin, I32)

    qi = qi_ref[...].astype(BF16)
    w_row = w_ref[...]

    def index_keys(kidx):
        x = _nt(kidx.astype(BF16), qi)
        r = jnp.maximum(x, 0.0) * w_row
        acc = r[:, :LANES]
        for c in range(1, nip // LANES):
            acc = acc + r[:, c * LANES:(c + 1) * LANES]
        return _float_key(_fold_lane_groups(acc) + 0.0)

    def put_page(page, keys, scores):
        rows = pl.ds(pl.multiple_of(page * PAGE_SIZE, PAGE_SIZE), PAGE_SIZE)
        key_scr[rows, :] = keys
        s_scr[rows, :] = scores
        slab = page // groups_per_slab
        kc_scr[slab] = jnp.where(lane_group == page % groups_per_slab, keys, kc_scr[slab])

    kidx_cat = jnp.concatenate([r[...] for r in kip_refs], axis=0)
    keys = index_keys(kidx_cat)
    k2d = jnp.concatenate(
        [jnp.concatenate([_head_rows(kp, h).astype(BF16) for h in range(KV_HEADS)], axis=1)
         for kp in kp_refs], axis=0)
    scores = _nt(k2d, qbd_scr[...])
    for i in range(pps):
        sl = slice(i * PAGE_SIZE, (i + 1) * PAGE_SIZE)
        put_page(j * pps + i, keys[sl], scores[sl])

    @pl.when(j == 0)
    def _():
        knew = index_keys(kinew_ref[...])
        srow = lax.broadcasted_iota(I32, (PAGE_SIZE, LANES), 0)
        knew = jnp.where(srow <= lane % SAMPLE_PAD, knew, imin)
        put_page(jnp.int32(npages), knew, _nt(knew_ref[...].astype(BF16), qbd_scr[...]))

    @pl.when(j == pl.num_programs(1) - 1)
    def _():
        kc = kc_scr[...]

        tok = lane[0:1, :] % SAMPLE_PAD

        def count_ge(thr):
            ge = (kc >= thr).astype(F32)
            cnt = jnp.sum(jnp.sum(ge.reshape(-1, SUBLANES, LANES), axis=0), axis=0, keepdims=True)
            tot = jnp.zeros((1, LANES), F32)
            for t in range(SAMPLE_PAD):
                tot = jnp.where(tok == t, jnp.sum(jnp.where(tok == t, cnt, 0.0), axis=1, keepdims=True), tot)
            return tot

        thr = _bisect_kth(count_ge, topk, (1, LANES))
        reps = nlp // LANES
        thr_q = jnp.concatenate([thr] * reps, axis=1)

        def masked(page):
            rows = pl.ds(pl.multiple_of(page * PAGE_SIZE, PAGE_SIZE), PAGE_SIZE)
            key = jnp.concatenate([key_scr[rows, :]] * reps, axis=1)
            return rows, (key >= thr_q) & (key != imin)

        def max_body(page, m):
            rows, sel = masked(page)
            return jnp.maximum(m, jnp.max(jnp.where(sel, s_scr[rows, :], NEG_BIG), axis=0, keepdims=True))

        unroll = next(u for u in (5, 4, 3, 2, 1) if (npages + 1) % u == 0)
        m = lax.fori_loop(0, npages + 1, max_body, jnp.full((1, nlp), NEG_BIG, F32), unroll=unroll)

        def exp_body(page, l):
            rows, sel = masked(page)
            e = jnp.where(sel, jnp.exp(s_scr[rows, :] - m), 0.0)
            s_scr[rows, :] = e
            return l + jnp.sum(e, axis=0, keepdims=True)

        l = lax.fori_loop(0, npages + 1, exp_body, jnp.zeros((1, nlp), F32), unroll=unroll)

        def out_body(page, carry):
            rows = pl.ds(pl.multiple_of(page * PAGE_SIZE, PAGE_SIZE), PAGE_SIZE)
            p_ref[page] = (s_scr[rows, :] / l).T.astype(BF16)
            return carry

        lax.fori_loop(0, npages + 1, out_body, 0, unroll=unroll)


def _sample_pv_kernel(pt_ref, *refs, pps, rq):
    vp_refs = refs[:pps]
    p_ref, pnew_ref, vnew_ref, o_ref, acc_scr = refs[pps:]
    j = pl.program_id(1)
    nrow = rq * SAMPLE_PAD

    @pl.when(j == 0)
    def _():
        pn = pnew_ref[0]
        vn = vnew_ref[...].astype(BF16)
        for g in range(KV_HEADS):
            acc_scr[g] = jnp.dot(pn[g * nrow:(g + 1) * nrow, :], vn[:, g * HEAD_DIM:(g + 1) * HEAD_DIM],
                                 preferred_element_type=F32)

    pcat = jnp.concatenate([p_ref[i] for i in range(pps)], axis=1)
    for g in range(KV_HEADS):
        vg = jnp.concatenate([_head_rows(vp, g).astype(BF16) for vp in vp_refs], axis=0)
        acc_scr[g] += jnp.dot(pcat[g * nrow:(g + 1) * nrow, :], vg, preferred_element_type=F32)

    @pl.when(j == pl.num_programs(1) - 1)
    def _():
        o_ref[...] = acc_scr[...]


def _attn_sample(q, qi, wi, ki_new, k_new, v_new, cache_k, cache_v, cache_kidx, page_table, *, t_valid):
    nb, npages = page_table.shape
    pps = min(PAGES_PER_STEP, npages)
    assert npages % pps == 0
    nsteps = npages // pps
    past = npages * PAGE_SIZE
    topk = min(IDX_TOPK, (past + t_valid) // 4)
    d = q.shape[-1]
    rq = d // HEAD_DIM // KV_HEADS
    kvw = KV_HEADS * HEAD_DIM
    nrow = rq * SAMPLE_PAD
    ck = cache_k.reshape(-1, PAGE_SIZE * KV_HEADS, HEAD_DIM)
    cv = cache_v.reshape(-1, PAGE_SIZE * KV_HEADS, HEAD_DIM)
    cki = cache_kidx.reshape(-1, PAGE_SIZE, IDX_DIM)

    def pad_to(a, axis, mult):
        n = a.shape[axis]
        widths = [(0, 0)] * a.ndim
        widths[axis] = (0, -(-n // mult) * mult - n)
        return jnp.pad(a, widths)

    qi_s = pad_to(qi.reshape(nb, SAMPLE_PAD, IDX_HEADS, IDX_DIM).transpose(0, 2, 1, 3)
                  .reshape(nb, IDX_HEADS * SAMPLE_PAD, IDX_DIM), 1, LANES)
    w_row = pad_to((wi * (IDX_HEADS ** -0.5 * IDX_DIM ** -0.5)).transpose(0, 2, 1)
                   .reshape(nb, 1, IDX_HEADS * SAMPLE_PAD), 2, LANES)
    nip = qi_s.shape[1]
    q_g = q.reshape(nb, SAMPLE_PAD, KV_HEADS, rq, HEAD_DIM).transpose(0, 2, 3, 1, 4).reshape(
        nb, KV_HEADS, nrow, HEAD_DIM)
    qbd = pad_to(jnp.einsum("bgrd,gh->bgrhd", q_g, jnp.eye(KV_HEADS, dtype=q.dtype))
                 .reshape(nb, KV_HEADS * nrow, kvw), 1, LANES)
    nlp = qbd.shape[1]
    pad_rows = ((0, 0), (0, PAGE_SIZE - SAMPLE_PAD), (0, 0))
    kinew_p = jnp.pad(ki_new, pad_rows)
    knew_p = jnp.pad(k_new, pad_rows)
    vnew_p = jnp.pad(v_new, pad_rows)
    npg = npages + 1
    nslab = -(-npg // (LANES // SAMPLE_PAD))

    def kidx_spec(i):
        return pl.BlockSpec((None, PAGE_SIZE, IDX_DIM), lambda b, j, pt, i=i: (pt[b, j * pps + i], 0, 0))

    def kv_specs():
        return [pl.BlockSpec((None, PAGE_SIZE * KV_HEADS, HEAD_DIM), lambda b, j, pt, i=i: (pt[b, j * pps + i], 0, 0))
                for i in range(pps)]

    probs = pl.pallas_call(
        functools.partial(_sample_scores_kernel, npages=npages, topk=topk, pps=pps),
        grid_spec=pltpu.PrefetchScalarGridSpec(
            num_scalar_prefetch=1,
            grid=(nb, nsteps),
            in_specs=[kidx_spec(i) for i in range(pps)] + kv_specs() + [
                pl.BlockSpec((None, nip, IDX_DIM), lambda b, j, pt: (b, 0, 0)),
                pl.BlockSpec((None, 1, nip), lambda b, j, pt: (b, 0, 0)),
                pl.BlockSpec((None, nlp, kvw), lambda b, j, pt: (b, 0, 0)),
                pl.BlockSpec((None, PAGE_SIZE, IDX_DIM), lambda b, j, pt: (b, 0, 0)),
                pl.BlockSpec((None, PAGE_SIZE, kvw), lambda b, j, pt: (b, 0, 0)),
            ],
            out_specs=pl.BlockSpec((None, npg, nlp, PAGE_SIZE), lambda b, j, pt: (b, 0, 0, 0)),
            scratch_shapes=[pltpu.VMEM((npg * PAGE_SIZE, LANES), I32),
                            pltpu.VMEM((nslab, PAGE_SIZE, LANES), I32),
                            pltpu.VMEM((npg * PAGE_SIZE, nlp), F32),
                            pltpu.VMEM((nlp, kvw), BF16)],
        ),
        out_shape=jax.ShapeDtypeStruct((nb, npg, nlp, PAGE_SIZE), BF16),
        compiler_params=_cparams(("arbitrary", "arbitrary")),
        name="sample_scores",
    )(page_table, *([cki] * pps), *([ck] * pps), qi_s, w_row, qbd, kinew_p, knew_p)

    o = pl.pallas_call(
        functools.partial(_sample_pv_kernel, pps=pps, rq=rq),
        grid_spec=pltpu.PrefetchScalarGridSpec(
            num_scalar_prefetch=1,
            grid=(nb, nsteps),
            in_specs=kv_specs() + [
                pl.BlockSpec((None, pps, nlp, PAGE_SIZE), lambda b, j, pt: (b, j, 0, 0)),
                pl.BlockSpec((None, 1, nlp, PAGE_SIZE), lambda b, j, pt: (b, npages, 0, 0)),
                pl.BlockSpec((None, PAGE_SIZE, kvw), lambda b, j, pt: (b, 0, 0)),
            ],
            out_specs=pl.BlockSpec((None, KV_HEADS, nrow, HEAD_DIM), lambda b, j, pt: (b, 0, 0, 0)),
            scratch_shapes=[pltpu.VMEM((KV_HEADS, nrow, HEAD_DIM), F32)],
        ),
        out_shape=jax.ShapeDtypeStruct((nb, KV_HEADS, nrow, HEAD_DIM), F32),
        compiler_params=_cparams(("arbitrary", "arbitrary")),
        name="sample_pv",
    )(page_table, *([cv] * pps), probs, probs, vnew_p)
    return o.reshape(nb, KV_HEADS, rq, SAMPLE_PAD, HEAD_DIM).transpose(0, 3, 1, 2, 4).reshape(nb * SAMPLE_PAD, d)


def _rms_cast_kernel(x_ref, g_ref, o_ref):
    o_ref[...] = _rms(x_ref[...], g_ref[...]).astype(BF16)


def _rms_cast(x, g, *, tm):
    m, d = x.shape
    return pl.pallas_call(
        _rms_cast_kernel,
        grid=(m // tm,),
        in_specs=[pl.BlockSpec((tm, d), lambda i: (i, 0)), pl.BlockSpec((1, d), lambda i: (0, 0))],
        out_specs=pl.BlockSpec((tm, d), lambda i: (i, 0)),
        out_shape=jax.ShapeDtypeStruct((m, d), BF16),
        compiler_params=_cparams(("arbitrary",)),
        name="rms_cast",
    )(x, g.reshape(1, d))


def _outproj_kernel(y_ref, a_ref, w1_ref, w2_ref, x_ref, gate_ref, o_ref):
    acc = jnp.dot(y_ref[...], w1_ref[...], preferred_element_type=F32)
    acc = acc + jnp.dot(a_ref[...], w2_ref[...], preferred_element_type=F32)
    o_ref[...] = x_ref[...] + gate_ref[...] * acc


def _outproj(y_n, attn_n, w, x, gate, *, tm, per_token, rows_per_batch):
    m, d = x.shape
    tn = PROJ_TN
    if per_token:
        gate_spec = pl.BlockSpec((tm, tn), lambda i, j: (i, j))
    else:
        tpb = rows_per_batch // tm
        gate_spec = pl.BlockSpec((None, 1, tn), lambda i, j: (i // tpb, 0, j))
    return pl.pallas_call(
        _outproj_kernel,
        grid=(m // tm, d // tn),
        in_specs=[pl.BlockSpec((tm, d), lambda i, j: (i, 0), pipeline_mode=pl.Buffered(1)),
                  pl.BlockSpec((tm, d), lambda i, j: (i, 0), pipeline_mode=pl.Buffered(1)),
                  pl.BlockSpec((d, tn), lambda i, j: (0, j)),
                  pl.BlockSpec((d, tn), lambda i, j: (1, j)),
                  pl.BlockSpec((tm, tn), lambda i, j: (i, j)),
                  gate_spec],
        out_specs=pl.BlockSpec((tm, tn), lambda i, j: (i, j)),
        out_shape=jax.ShapeDtypeStruct((m, d), F32),
        compiler_params=_cparams(("arbitrary", "arbitrary")),
        name="outproj",
    )(y_n, attn_n, w, w, x, gate)


def _peer_route_kernel(q_ref, k1_ref, k2_ref, s1_ref, s2_ref, tau_ref, cc_ref):
    q = q_ref[...]
    k1 = k1_ref[...].astype(BF16)
    k2 = k2_ref[...].astype(BF16)
    half = PEER_D_KEY // 2
    tm = q.shape[0]
    taus, ccs = [], []
    for h in range(PEER_HEADS):
        base = h * PEER_D_KEY
        s1 = _nt(k1, q[:, base:base + half].astype(BF16))
        s2 = _nt(k2, q[:, base + half:base + PEER_D_KEY].astype(BF16))
        s1_ref[h * PEER_KEYS:(h + 1) * PEER_KEYS, :] = s1
        s2_ref[h * PEER_KEYS:(h + 1) * PEER_KEYS, :] = s2

        def top_vals(x):
            vals = []
            for _ in range(PEER_TOPK):
                m = jnp.max(x, axis=0, keepdims=True)
                vals.append(m)
                x = jnp.where(x == m, -jnp.inf, x)
            return vals

        v1 = top_vals(s1)
        v2 = top_vals(s2)
        v1a = jnp.concatenate(v1, axis=0)
        v2a = jnp.concatenate(v2, axis=0)
        half_k = PEER_TOPK // 2
        cand = jnp.concatenate([v + v2a[:half_k] for v in v1[:half_k]]
                               + [v1[0] + v2a[half_k:], v1a[half_k:] + v2[0]], axis=0) + 0.0
        ckey = _float_key(cand)

        def count_ge(thr, ckey=ckey):
            return jnp.sum((ckey >= thr).astype(I32), axis=0, keepdims=True)

        tau = _key_float(_bisect_kth(count_ge, PEER_TOPK, (1, tm)))
        cmax = cand[0:1, :]
        zsum = jnp.sum(jnp.where(cand >= tau, jnp.exp(cand - cmax), 0.0), axis=0, keepdims=True)
        taus.append(tau)
        ccs.append(cmax + jnp.log(zsum))
    tau_ref[...] = jnp.concatenate(taus, axis=0)
    cc_ref[...] = jnp.concatenate(ccs, axis=0)


def _peer_route(q, k1, k2, *, tm):
    t = q.shape[0]
    rows = PEER_HEADS * PEER_KEYS
    half = PEER_D_KEY // 2
    return pl.pallas_call(
        _peer_route_kernel,
        grid=(t // tm,),
        in_specs=[pl.BlockSpec((tm, PEER_HEADS * PEER_D_KEY), lambda i: (i, 0)),
                  pl.BlockSpec((PEER_KEYS, half), lambda i: (0, 0)),
                  pl.BlockSpec((PEER_KEYS, half), lambda i: (0, 0))],
        out_specs=[pl.BlockSpec((rows, tm), lambda i: (0, i)),
                   pl.BlockSpec((rows, tm), lambda i: (0, i)),
                   pl.BlockSpec((PEER_HEADS, tm), lambda i: (0, i)),
                   pl.BlockSpec((PEER_HEADS, tm), lambda i: (0, i))],
        out_shape=[jax.ShapeDtypeStruct((rows, t), F32), jax.ShapeDtypeStruct((rows, t), F32),
                   jax.ShapeDtypeStruct((PEER_HEADS, t), F32), jax.ShapeDtypeStruct((PEER_HEADS, t), F32)],
        compiler_params=_cparams(("arbitrary",)),
        name="peer_route",
    )(q, k1, k2)


def _peer_dense_kernel(xb_ref, s1_ref, s2_ref, tau_ref, cc_ref, u_ref, v_ref, o_ref, *, te):
    e = pl.program_id(1)

    @pl.when(e == 0)
    def _():
        o_ref[...] = jnp.zeros(o_ref.shape, F32)

    tm = xb_ref.shape[0]
    nsub = MXU_TILE // PEER_KEYS
    coefs = []
    for c in range(te // MXU_TILE):
        ut = _nt(xb_ref[...], u_ref[c * MXU_TILE:(c + 1) * MXU_TILE, :]).T
        for i in range(nsub):
            k = c * nsub + i
            gate = jnp.zeros((PEER_KEYS, tm), F32)
            for h in range(PEER_HEADS):
                sm = s1_ref[k, h:h + 1, :] + s2_ref[h * PEER_KEYS:(h + 1) * PEER_KEYS, :]
                gate = gate + jnp.where(sm >= tau_ref[h:h + 1, :], jnp.exp(sm - cc_ref[h:h + 1, :]), 0.0)
            coefs.append(gate * jax.nn.gelu(ut[i * PEER_KEYS:(i + 1) * PEER_KEYS, :]))
    coef = jnp.concatenate(coefs, axis=0).T.astype(BF16)
    for n in range(o_ref.shape[1] // MXU_TILE):
        cols = slice(n * MXU_TILE, (n + 1) * MXU_TILE)
        o_ref[:, cols] += jnp.dot(coef, v_ref[:, cols], preferred_element_type=F32)


def _peer_dense(xb, s1, s2, tau, cc, u_b, v_b, *, tm):
    t, d = xb.shape
    ne = u_b.shape[0]
    te = min(PEER_TE, ne)
    rows = PEER_HEADS * PEER_KEYS
    once = pl.Buffered(1)
    s1 = s1.reshape(PEER_HEADS, PEER_KEYS, t).transpose(1, 0, 2)
    return pl.pallas_call(
        functools.partial(_peer_dense_kernel, te=te),
        grid=(t // tm, ne // te),
        in_specs=[pl.BlockSpec((tm, d), lambda i, e: (i, 0), pipeline_mode=once),
                  pl.BlockSpec((te // PEER_KEYS, PEER_HEADS, tm), lambda i, e: (e, 0, i)),
                  pl.BlockSpec((rows, tm), lambda i, e: (0, i), pipeline_mode=once),
                  pl.BlockSpec((PEER_HEADS, tm), lambda i, e: (0, i), pipeline_mode=once),
                  pl.BlockSpec((PEER_HEADS, tm), lambda i, e: (0, i), pipeline_mode=once),
                  pl.BlockSpec((te, d), lambda i, e: (e, 0)),
                  pl.BlockSpec((te, d), lambda i, e: (e, 0))],
        out_specs=pl.BlockSpec((tm, d), lambda i, e: (i, 0), pipeline_mode=once),
        out_shape=jax.ShapeDtypeStruct((t, d), F32),
        compiler_params=_cparams(("arbitrary", "arbitrary")),
        name="peer_dense",
    )(xb, s1, s2, tau, cc, u_b, v_b)


def _final_kernel(x_ref, f_ref, gate_ref, g_ref, o_ref):
    o_ref[...] = _rms(x_ref[...] + gate_ref[...] * f_ref[...], g_ref[...])


def _final(x, f, gate, g, *, tm, per_token, rows_per_batch):
    m, d = x.shape
    if per_token:
        gate_spec = pl.BlockSpec((tm, d), lambda i: (i, 0))
    else:
        tpb = rows_per_batch // tm
        gate_spec = pl.BlockSpec((None, 1, d), lambda i: (i // tpb, 0, 0))
    return pl.pallas_call(
        _final_kernel,
        grid=(m // tm,),
        in_specs=[pl.BlockSpec((tm, d), lambda i: (i, 0)),
                  pl.BlockSpec((tm, d), lambda i: (i, 0)),
                  gate_spec,
                  pl.BlockSpec((1, d), lambda i: (0, 0))],
        out_specs=pl.BlockSpec((tm, d), lambda i: (i, 0)),
        out_shape=jax.ShapeDtypeStruct((m, d), F32),
        compiler_params=_cparams(("arbitrary",)),
        name="final_norm",
    )(x, f, gate, g.reshape(1, d))


def _layout(d):
    nh = d // SSD_HEADDIM
    kvw = KV_HEADS * HEAD_DIM
    qiw = IDX_HEADS * IDX_DIM
    xbcw = d + 2 * SSD_GROUPS * D_STATE
    lay, off = {}, 0
    for name, w in (("z", d), ("q", d), ("qi", qiw), ("xbc", xbcw), ("k", kvw), ("v", kvw), ("tail", PROJ_TN)):
        assert w % PROJ_TN == 0
        lay[name] = off
        off += w
    lay["total"] = off
    assert IDX_DIM + nh + IDX_HEADS <= PROJ_TN
    return lay


def _group_cols(xbc, d):
    lead = xbc.shape[:-1]
    gn = SSD_GROUPS * D_STATE
    x = xbc[..., :d].reshape(lead + (SSD_GROUPS, d // SSD_GROUPS))
    b = xbc[..., d:d + gn].reshape(lead + (SSD_GROUPS, D_STATE))
    c = xbc[..., d + gn:].reshape(lead + (SSD_GROUPS, D_STATE))
    return jnp.concatenate([x, b, c], axis=-1).reshape(lead + (d + 2 * gn,))


def _ungroup_cols(xg, d):
    lead = xg.shape[:-1]
    gn = SSD_GROUPS * D_STATE
    xsw = d // SSD_GROUPS
    g3 = xg.reshape(lead + (SSD_GROUPS, xsw + 2 * D_STATE))
    return jnp.concatenate([g3[..., :xsw].reshape(lead + (d,)),
                            g3[..., xsw:xsw + D_STATE].reshape(lead + (gn,)),
                            g3[..., xsw + D_STATE:].reshape(lead + (gn,))], axis=-1)


def _w_in_moves(d):
    nh = d // SSD_HEADDIM
    kvw = KV_HEADS * HEAD_DIM
    qiw = IDX_HEADS * IDX_DIM
    xbcw = d + 2 * SSD_GROUPS * D_STATE
    names = ("z", "xbc", "dt", "q", "k", "v", "qi", "wi", "ki")
    sizes = (d, xbcw, nh, d, kvw, kvw, qiw, IDX_HEADS, IDX_DIM)
    src = dict(zip(names, np.concatenate([[0], np.cumsum(sizes)[:-1]]).tolist()))
    lay = _layout(d)
    moves = [(lay["z"], src["z"], d), (lay["q"], src["q"], d), (lay["qi"], src["qi"], qiw),
             (lay["k"], src["k"], kvw), (lay["v"], src["v"], kvw)]
    xsw = d // SSD_GROUPS
    gw = xsw + 2 * D_STATE
    gn = SSD_GROUPS * D_STATE
    for g in range(SSD_GROUPS):
        base = lay["xbc"] + g * gw
        moves += [(base, src["xbc"] + g * xsw, xsw),
                  (base + xsw, src["xbc"] + d + g * D_STATE, D_STATE),
                  (base + xsw + D_STATE, src["xbc"] + d + gn + g * D_STATE, D_STATE)]
    tail = [(src["ki"], IDX_DIM), (src["dt"], nh), (src["wi"], IDX_HEADS)]
    return moves, tail, int(sum(sizes))


def _w_in_prep_kernel(x_ref, o_ref, *, moves, tail, ncols):
    def take(src, w):
        lo = src // LANES * LANES
        hi = min(-(-(src + w) // LANES) * LANES, ncols)
        return x_ref[:, lo:hi][:, src - lo:src - lo + w]

    for dst, src, w in moves:
        o_ref[:, dst:dst + w] = take(src, w).astype(BF16)
    tr = x_ref.shape[0]
    pieces = [take(s, w) for s, w in tail]
    pieces.append(jnp.zeros((tr, PROJ_TN - sum(w for _, w in tail)), F32))
    tail_dst = o_ref.shape[1] - PROJ_TN
    o_ref[:, tail_dst:] = jnp.concatenate(pieces, axis=1).astype(BF16)


def _prep_w_in(w_in, d):
    moves, tail, ncols = _w_in_moves(d)
    total = _layout(d)["total"]
    tr = min(d, W_IN_PREP_ROWS)
    return pl.pallas_call(
        functools.partial(_w_in_prep_kernel, moves=tuple(moves), tail=tuple(tail), ncols=ncols),
        grid=(d // tr,),
        in_specs=[pl.BlockSpec((None, tr, ncols), lambda i: (0, i, 0))],
        out_specs=pl.BlockSpec((tr, total), lambda i: (i, 0)),
        out_shape=jax.ShapeDtypeStruct((d, total), BF16),
        compiler_params=_cparams(("arbitrary",)),
        name="w_in_prep",
    )(w_in)


def _group_major(vec, rpg):
    return jnp.pad(vec.reshape(SSD_GROUPS, 1, rpg), ((0, 0), (0, 0), (0, LANES - rpg)))


def _rope_tables(pos):
    half = HEAD_DIM // 8
    inv = ROPE_THETA ** (-jnp.arange(half, dtype=F32) / half)
    ang = pos.astype(F32)[:, None] * inv[None, :]
    cos, sin = jnp.cos(ang), jnp.sin(ang)
    n = pos.shape[0]
    rest = HEAD_DIM - 2 * half
    return (jnp.concatenate([cos, cos, jnp.ones((n, rest), F32)], axis=1),
            jnp.concatenate([-sin, sin, jnp.zeros((n, rest), F32)], axis=1))


def _layer(x2d, mods, pos, prm, *, nb, seq, per_token, tm, ssd_cfg, attn_fn, peer_tm):
    m, d = x2d.shape
    lay = prm["lay"]
    nh = d // SSD_HEADDIM
    rpg = nh // SSD_GROUPS
    sh1, sc1, g1, sh2, sc2, g2 = mods
    cos, sin = _rope_tables(pos)
    xoff = lay["xbc"]
    rope = (cos, sin, prm["kidx_norm_g"],
            ((lay["q"] // PROJ_TN, lay["xbc"] // PROJ_TN), (lay["k"] // PROJ_TN, lay["v"] // PROJ_TN)),
            lay["tail"] // PROJ_TN)
    tm_e = min(tm, ELEMWISE_TM)
    tm_p = min(m if per_token else seq, PROJ_TM)
    h1 = _modulate(x2d, prm["norm1_g"], sh1, sc1, tm=tm_e, per_token=per_token, rows_per_batch=seq)
    proj = _proj(h1, prm["w_in"], tm=tm_p, rope=rope)

    gw = rpg * SSD_HEADDIM + 2 * D_STATE
    dt_raw = proj[:, lay["tail"] + IDX_DIM: lay["tail"] + IDX_DIM + nh]
    y_ssd, ssm_new = ssd_cfg(proj, dt_raw, xoff // gw)
    y_n = _gate_norm(y_ssd, proj, prm["ssd_norm_g"], tm=tm_e)

    o_attn = attn_fn(proj)

    a_n = _rms_cast(o_attn, prm["attn_norm_g"], tm=tm_e)
    x1 = _outproj(y_n, a_n, prm["w_out"], x2d, g1, tm=tm_p, per_token=per_token,
                  rows_per_batch=seq)

    xb = _modulate(x1, prm["norm2_g"], sh2, sc2, tm=tm_e, per_token=per_token, rows_per_batch=seq)
    qp = _proj(xb, prm["peer_wq"], tm=tm_p)
    s1, s2, tau, cc = _peer_route(qp, prm["peer_k1"], prm["peer_k2"], tm=min(peer_tm, PEER_ROUTE_TM))
    ffn = _peer_dense(xb, s1, s2, tau, cc, prm["peer_u"], prm["peer_v"], tm=peer_tm)
    y = _final(x1, ffn, g2, prm["final_norm_g"], tm=tm_e, per_token=per_token, rows_per_batch=seq)
    return y, proj, ssm_new


def kernel(x_prompt, x_sample, c_prompt, c_sample, cache_k, cache_v, cache_kidx, state_ssm, state_conv, page_table, w_mod, b_mod, norm1_g, w_in, conv_w, conv_b, dt_bias, a_log, d_skip, ssd_norm_g, kidx_norm_g, attn_norm_g, w_out, norm2_g, peer_wq, peer_k1, peer_k2, peer_u, peer_v, final_norm_g):
    nbp, seq, d = x_prompt.shape
    nbs, tdec, _ = x_sample.shape
    depth = w_mod.shape[0]
    assert depth == 1
    nh = d // SSD_HEADDIM
    rpg = nh // SSD_GROUPS
    gw = rpg * SSD_HEADDIM + 2 * D_STATE
    lay = _layout(d)
    assert lay["xbc"] % gw == 0
    kvw = KV_HEADS * HEAD_DIM
    xbcw = d + 2 * SSD_GROUPS * D_STATE

    prm = dict(
        lay=lay,
        norm1_g=norm1_g[0], kidx_norm_g=kidx_norm_g[0], ssd_norm_g=ssd_norm_g[0], attn_norm_g=attn_norm_g[0],
        norm2_g=norm2_g[0], final_norm_g=final_norm_g,
        w_in=_prep_w_in(w_in, d),
        w_out=_cast_bf16(w_out[0]),
        peer_wq=_cast_bf16(peer_wq[0]), peer_k1=peer_k1[0], peer_k2=peer_k2[0],
        peer_u=_cast_bf16(peer_u[0]), peer_v=_cast_bf16(peer_v[0]),
    )
    conv_w_g = _group_cols(conv_w[0], d)
    conv_b_g = _group_cols(conv_b[0], d).reshape(1, xbcw)
    dtb_g = _group_major(dt_bias[0], rpg)
    alog_g = _group_major(a_log[0], rpg)
    dsk_g = _group_major(d_skip[0], rpg)

    mod = _adaln(jnp.concatenate([c_prompt, c_sample], axis=0), w_mod[0], b_mod[0])
    mod_p = [a.reshape(nbp, 1, d) for a in jnp.split(mod[:nbp], 6, axis=-1)]
    mod_s = [jnp.repeat(a, SAMPLE_PAD, axis=0) for a in jnp.split(mod[nbp:], 6, axis=-1)]

    def dt_group_major(dt_raw):
        rows = dt_raw.shape[0]
        dtg = dt_raw.reshape(rows, SSD_GROUPS, rpg).transpose(1, 0, 2)
        return jnp.pad(dtg, ((0, 0), (0, 0), (0, LANES - rpg)))

    cs_p = min(SSD_CHUNK, seq)
    nc_p = seq // cs_p
    assert seq % cs_p == 0 and seq % Q_BLOCK == 0

    def ssd_prompt(proj, dt_raw, xoff_blocks):
        hist = jnp.zeros((nbp, SUBLANES, xbcw), F32)
        h0 = jnp.zeros((nbp, nh, SSD_HEADDIM, D_STATE), F32)
        return _ssd(proj, xoff_blocks, hist, conv_w_g, conv_b_g, dt_group_major(dt_raw), dtb_g, alog_g, dsk_g, h0,
                    nb=nbp, nc=nc_p, cs=cs_p, d=d, valid_len=cs_p)

    tm_p = min(512, seq)
    pos_p = jnp.tile(jnp.arange(seq), nbp)
    y_p, proj_p, ssm_p = _layer(
        x_prompt.reshape(nbp * seq, d), mod_p, pos_p, prm, nb=nbp, seq=seq, per_token=False, tm=tm_p,
        ssd_cfg=ssd_prompt, attn_fn=functools.partial(_attn_prompt, lay=lay, nb=nbp, seq=seq, d=d),
        peer_tm=min(PEER_TM, nbp * seq))

    npages = page_table.shape[1]
    past = npages * PAGE_SIZE
    ms = nbs * SAMPLE_PAD
    xs_pad = jnp.pad(x_sample, ((0, 0), (0, SAMPLE_PAD - tdec), (0, 0))).reshape(ms, d)
    pos_s = jnp.tile(past + jnp.arange(SAMPLE_PAD), nbs)

    def ssd_sample(proj, dt_raw, xoff_blocks):
        xbc = proj[:, lay["xbc"]:lay["xbc"] + xbcw].reshape(nbs, SAMPLE_PAD, xbcw)
        xbc = jnp.pad(xbc, ((0, 0), (0, SAMPLE_CHUNK - SAMPLE_PAD), (0, 0))).reshape(nbs * SAMPLE_CHUNK, xbcw)
        dtr = jnp.pad(dt_raw.reshape(nbs, SAMPLE_PAD, nh), ((0, 0), (0, SAMPLE_CHUNK - SAMPLE_PAD), (0, 0)))
        hist = jnp.pad(_group_cols(state_conv[0], d), ((0, 0), (SUBLANES - (CONV_W - 1), 0), (0, 0)))
        y, hnew = _ssd(xbc, 0, hist, conv_w_g, conv_b_g, dt_group_major(dtr.reshape(nbs * SAMPLE_CHUNK, nh)),
                       dtb_g, alog_g, dsk_g, state_ssm[0], nb=nbs, nc=1, cs=SAMPLE_CHUNK, d=d, valid_len=tdec)
        y = y.reshape(nbs, SAMPLE_CHUNK, d)[:, :SAMPLE_PAD].reshape(ms, d)
        return y, hnew

    def attn_sample(proj):
        def seg(name, w):
            return proj[:, lay[name]:lay[name] + w].reshape(nbs, SAMPLE_PAD, w)
        tail = lay["tail"]
        ki_new = proj[:, tail:tail + IDX_DIM].reshape(nbs, SAMPLE_PAD, IDX_DIM)
        wi = proj[:, tail + IDX_DIM + nh: tail + IDX_DIM + nh + IDX_HEADS].reshape(nbs, SAMPLE_PAD, IDX_HEADS)
        return _attn_sample(seg("q", d), seg("qi", IDX_HEADS * IDX_DIM), wi, ki_new, seg("k", kvw), seg("v", kvw),
                            cache_k, cache_v, cache_kidx, page_table, t_valid=tdec)

    y_s, proj_s, ssm_s = _layer(
        xs_pad, mod_s, pos_s, prm, nb=nbs, seq=SAMPLE_PAD, per_token=True, tm=ms,
        ssd_cfg=ssd_sample, attn_fn=attn_sample, peer_tm=ms)

    def states(proj, nb, rows, valid):
        p3 = proj.reshape(nb, rows, lay["total"])[:, :valid]
        k = p3[:, :, lay["k"]:lay["k"] + kvw].reshape(1, nb, valid, KV_HEADS, HEAD_DIM)
        v = p3[:, :, lay["v"]:lay["v"] + kvw].reshape(1, nb, valid, KV_HEADS, HEAD_DIM)
        ki = p3[:, :, lay["tail"]:lay["tail"] + IDX_DIM].reshape(1, nb, valid, IDX_DIM)
        xbc = _ungroup_cols(p3[:, valid - (CONV_W - 1):valid, lay["xbc"]:lay["xbc"] + xbcw], d)
        return k, v, ki, xbc.reshape(1, nb, CONV_W - 1, xbcw)

    k_p, v_p, ki_p, conv_p = states(proj_p, nbp, seq, seq)
    k_s, v_s, ki_s, conv_s = states(proj_s, nbs, SAMPLE_PAD, tdec)
    y_prompt = y_p.reshape(nbp, seq, d)
    y_sample = y_s.reshape(nbs, SAMPLE_PAD, d)[:, :tdec]
    return (y_prompt, y_sample, k_p, v_p, ki_p, ssm_p[None], conv_p,
            k_s, v_s, ki_s, ssm_s[None], conv_s)
```

```python
import functools

import jax
import jax.numpy as jnp
import numpy as np
from jax import lax
from jax.experimental import pallas as pl
from jax.experimental.pallas import tpu as pltpu

F32 = jnp.float32
BF16 = jnp.bfloat16
I32 = jnp.int32

SSD_HEADDIM = 64
SSD_GROUPS = 8
D_STATE = 128
CONV_W = 4
SSD_CHUNK = 256
HEAD_DIM = 128
KV_HEADS = 8
ROPE_THETA = 500000.0
IDX_HEADS = 32
IDX_DIM = 128
IDX_TOPK = 256
Q_BLOCK = 128
PEER_HEADS = 8
PEER_KEYS = 128
PEER_TOPK = 16
PEER_D_KEY = 256
PAGE_SIZE = 128
EPS = 1e-6

LANES = 128
SUBLANES = 8
VMEM_LIMIT_BYTES = 56 * 1024 * 1024

MXU_TILE = 256
PROJ_TN = MXU_TILE
PROJ_TM = 1024
ELEMWISE_TM = 256
CAST_ROWS = 512
W_IN_PREP_ROWS = 128
PEER_ROUTE_TM = 256
PEER_TM = 1024
PEER_TE = 512
SAMPLE_PAD = 8
SAMPLE_CHUNK = 128
ATTN_TK = 512
ATTN_GROUPS_PER_STEP = 4
PAGES_PER_STEP = 8
MAX_PAGE_UNROLL = 5
NEG_BIG = -1e30
M_INIT = -1e29
INT_MIN = -2 ** 31


def _cparams(sem):
    return pltpu.CompilerParams(dimension_semantics=sem, vmem_limit_bytes=VMEM_LIMIT_BYTES)


def _nt(a, b):
    return lax.dot_general(a, b, (((1,), (1,)), ((), ())), preferred_element_type=F32)


def _tn(a, b):
    return lax.dot_general(a, b, (((0,), (0,)), ((), ())), preferred_element_type=F32)


def _rms(x, g):
    return x * lax.rsqrt(jnp.mean(x * x, axis=-1, keepdims=True) + EPS) * g


def _rope_tile(a, cos, sin):
    half = HEAD_DIM // 8
    lane = lax.broadcasted_iota(I32, a.shape, 1)
    sw = jnp.where(lane < half, pltpu.roll(a, HEAD_DIM - half, 1), pltpu.roll(a, half, 1))
    return a * cos + sw * sin


def _float_key(x):
    bits = pltpu.bitcast(x, I32)
    return jnp.where(bits < 0, bits ^ jnp.int32(0x7FFFFFFF), bits)


def _key_float(key):
    bits = jnp.where(key < 0, key ^ jnp.int32(0x7FFFFFFF), key)
    return pltpu.bitcast(bits, F32)


def _bisect_kth(count_ge, k, shape):
    imin = jnp.int32(INT_MIN)

    def body(i, ans):
        cand_u = ans | lax.shift_left(jnp.int32(1), jnp.int32(31) - i)
        return jnp.where(count_ge(cand_u ^ imin) >= k, cand_u, ans)

    ans = lax.fori_loop(0, 32, body, jnp.zeros(shape, I32))
    return ans ^ imin


def _cast_kernel(x_ref, o_ref):
    o_ref[...] = x_ref[...].astype(BF16)


def _cast_bf16(w):
    r, c = w.shape
    tr = min(r, CAST_ROWS)
    assert r % tr == 0
    return pl.pallas_call(
        _cast_kernel,
        grid=(r // tr,),
        in_specs=[pl.BlockSpec((tr, c), lambda i: (i, 0))],
        out_specs=pl.BlockSpec((tr, c), lambda i: (i, 0)),
        out_shape=jax.ShapeDtypeStruct((r, c), BF16),
        compiler_params=_cparams(("arbitrary",)),
        name="cast_bf16",
    )(w)


def _adaln_kernel(c_ref, w_ref, b_ref, o_ref):
    c = c_ref[...]
    a = (c * jax.nn.sigmoid(c)).astype(BF16)
    o_ref[...] = jnp.dot(a, w_ref[...].astype(BF16), preferred_element_type=F32) + b_ref[...]


def _adaln(c, w_mod, b_mod):
    n, d = c.shape
    npad = -(-n // SUBLANES) * SUBLANES
    cp = jnp.pad(c, ((0, npad - n), (0, 0)))
    nout = w_mod.shape[1]
    tn = PROJ_TN
    out = pl.pallas_call(
        _adaln_kernel,
        grid=(nout // tn,),
        in_specs=[pl.BlockSpec((npad, d), lambda j: (0, 0)),
                  pl.BlockSpec((d, tn), lambda j: (0, j)),
                  pl.BlockSpec((1, tn), lambda j: (0, j))],
        out_specs=pl.BlockSpec((npad, tn), lambda j: (0, j)),
        out_shape=jax.ShapeDtypeStruct((npad, nout), F32),
        compiler_params=_cparams(("arbitrary",)),
        name="adaln",
    )(cp, w_mod, b_mod.reshape(1, nout))
    return out[:n]


def _modulate_kernel(x_ref, g_ref, sh_ref, sc_ref, o_ref):
    o_ref[...] = (_rms(x_ref[...], g_ref[...]) * (1.0 + sc_ref[...]) + sh_ref[...]).astype(BF16)


def _modulate(x, g, shift, scale, *, tm, per_token, rows_per_batch):
    m, d = x.shape
    if per_token:
        mod_spec = pl.BlockSpec((tm, d), lambda i: (i, 0))
    else:
        tiles_per_batch = rows_per_batch // tm
        mod_spec = pl.BlockSpec((None, 1, d), lambda i: (i // tiles_per_batch, 0, 0))
    return pl.pallas_call(
        _modulate_kernel,
        grid=(m // tm,),
        in_specs=[pl.BlockSpec((tm, d), lambda i: (i, 0)),
                  pl.BlockSpec((1, d), lambda i: (0, 0)),
                  mod_spec, mod_spec],
        out_specs=pl.BlockSpec((tm, d), lambda i: (i, 0)),
        out_shape=jax.ShapeDtypeStruct((m, d), BF16),
        compiler_params=_cparams(("arbitrary",)),
        name="modulate",
    )(x, g.reshape(1, d), shift, scale)


def _proj_kernel(*refs, rope_ranges, tail_tile):
    if tail_tile is not None:
        h_ref, w_ref, cos_ref, sin_ref, kg_ref, o_ref = refs
    else:
        h_ref, w_ref, o_ref = refs
    j = pl.program_id(1)

    def product():
        return jnp.dot(h_ref[...], w_ref[...], preferred_element_type=F32)

    if tail_tile is None:
        o_ref[...] = product()
        return

    is_rope = functools.reduce(jnp.logical_or, [(j >= lo) & (j < hi) for lo, hi in rope_ranges])
    is_tail = j == tail_tile
    tn = w_ref.shape[1]

    @pl.when(jnp.logical_not(is_rope | is_tail))
    def _():
        o_ref[...] = product()

    @pl.when(is_rope)
    def _():
        acc = product()
        cos = cos_ref[...]
        sin = sin_ref[...]
        for t in range(tn // HEAD_DIM):
            sl = slice(t * HEAD_DIM, (t + 1) * HEAD_DIM)
            o_ref[:, sl] = _rope_tile(acc[:, sl], cos, sin)

    @pl.when(is_tail)
    def _():
        acc = product()
        ki = _rms(acc[:, :IDX_DIM], kg_ref[...])
        o_ref[:, :IDX_DIM] = _rope_tile(ki, cos_ref[...], sin_ref[...])
        o_ref[:, IDX_DIM:] = acc[:, IDX_DIM:]


def _proj(h, w, *, tm, rope=None):
    m, d = h.shape
    n = w.shape[1]
    tn = PROJ_TN
    assert m % tm == 0 and n % tn == 0
    in_specs = [pl.BlockSpec((tm, d), lambda i, j: (i, 0), pipeline_mode=pl.Buffered(1)),
                pl.BlockSpec((d, tn), lambda i, j: (0, j))]
    args = [h, w]
    rope_ranges, tail_tile = (), None
    if rope is not None:
        cos, sin, kidx_g, rope_ranges, tail_tile = rope
        in_specs += [pl.BlockSpec((tm, HEAD_DIM), lambda i, j: (i, 0)),
                     pl.BlockSpec((tm, HEAD_DIM), lambda i, j: (i, 0)),
                     pl.BlockSpec((1, IDX_DIM), lambda i, j: (0, 0))]
        args += [cos, sin, kidx_g.reshape(1, IDX_DIM)]
    return pl.pallas_call(
        functools.partial(_proj_kernel, rope_ranges=rope_ranges, tail_tile=tail_tile),
        grid=(m // tm, n // tn),
        in_specs=in_specs,
        out_specs=pl.BlockSpec((tm, tn), lambda i, j: (i, j)),
        out_shape=jax.ShapeDtypeStruct((m, n), F32),
        compiler_params=_cparams(("arbitrary", "arbitrary")),
        name="proj",
    )(*args)


def _ssd_kernel(x_ref, prev_ref, hist_ref, cw_ref, cb_ref, dt_ref, dtb_ref, alog_ref, dsk_ref, h0_ref,
                y_ref, hout_ref, h_scr, *, cs, rpg, valid_len, nc):
    c = pl.program_id(2)
    xs_w = rpg * SSD_HEADDIM

    @pl.when(c == 0)
    def _():
        h_scr[...] = h0_ref[...]

    prev = jnp.where(c == 0, hist_ref[...], prev_ref[...])
    cat = jnp.concatenate([prev, x_ref[...]], axis=0)
    w = cw_ref[...]
    acc = cb_ref[...]
    for j in range(CONV_W):
        lo = SUBLANES - (CONV_W - 1) + j
        acc = acc + cat[lo:lo + cs] * w[j:j + 1]
    xc = acc * jax.nn.sigmoid(acc)
    xs = xc[:, :xs_w]
    bm = xc[:, xs_w:xs_w + D_STATE]
    cm = xc[:, xs_w + D_STATE:]
    bm_b = bm.astype(BF16)
    cm_b = cm.astype(BF16)

    z = dt_ref[...] + dtb_ref[...]
    dt = jnp.maximum(z, 0.0) + jnp.log1p(jnp.exp(-jnp.abs(z)))
    if valid_len < cs:
        row = lax.broadcasted_iota(I32, dt.shape, 0)
        dt = jnp.where(row < valid_len, dt, 0.0)
    a_neg = -jnp.exp(alog_ref[...])
    ii = lax.broadcasted_iota(I32, (cs, cs), 0)
    jj = lax.broadcasted_iota(I32, (cs, cs), 1)
    causal = ii >= jj
    acs = jnp.dot(causal.astype(F32), dt * a_neg, preferred_element_type=F32,
                  precision=lax.Precision.HIGHEST)
    acs_t = acs.T
    a_last = acs[cs - 1:cs, :]
    cb = _nt(cm_b, bm_b)
    dsk = dsk_ref[...]

    first = lax.broadcasted_iota(I32, (cs, LANES), 1) < SSD_HEADDIM
    first_row = lax.broadcasted_iota(I32, (LANES, D_STATE), 0) < SSD_HEADDIM
    ys = []
    for pr in range(rpg // 2):
        r0, r1 = 2 * pr, 2 * pr + 1
        a0, a1 = acs[:, r0:r0 + 1], acs[:, r1:r1 + 1]
        xp = xs[:, pr * LANES:(pr + 1) * LANES]
        xd = xp * jnp.where(first, dt[:, r0:r0 + 1], dt[:, r1:r1 + 1])
        y = jnp.zeros((cs, LANES), F32)
        for r, keep in ((r0, first), (r1, jnp.logical_not(first))):
            lm = jnp.exp(jnp.where(causal, acs[:, r:r + 1] - acs_t[r:r + 1, :], -jnp.inf))
            y = y + jnp.dot((cb * lm).astype(BF16), jnp.where(keep, xd, 0.0).astype(BF16),
                            preferred_element_type=F32)
        hp = h_scr[pr]
        y = y + jnp.where(first, jnp.exp(a0), jnp.exp(a1)) * _nt(cm_b, hp.astype(BF16))
        al0, al1 = a_last[:, r0:r0 + 1], a_last[:, r1:r1 + 1]
        decay = jnp.where(first, jnp.exp(al0 - a0), jnp.exp(al1 - a1))
        h_scr[pr] = (hp * jnp.where(first_row, jnp.exp(al0), jnp.exp(al1))
                     + _tn((xd * decay).astype(BF16), bm_b))
        ys.append(y + xp * jnp.where(first[0:1, :], dsk[:, r0:r0 + 1], dsk[:, r1:r1 + 1]))
    y_ref[...] = jnp.concatenate(ys, axis=1)

    @pl.when(c == nc - 1)
    def _():
        hout_ref[...] = h_scr[...]


def _ssd(xg, xoff_blocks, hist8, conv_w_g, conv_b_g, dt_g, dtb_g, alog_g, dsk_g, h0, *, nb, nc, cs, d, valid_len):
    g_cnt = SSD_GROUPS
    rpg = d // SSD_HEADDIM // g_cnt
    gw = rpg * SSD_HEADDIM + 2 * D_STATE
    m = nb * nc * cs
    cpb = cs // SUBLANES
    nh = d // SSD_HEADDIM
    npair = rpg // 2
    assert rpg % 2 == 0 and 2 * SSD_HEADDIM == LANES
    kern = functools.partial(_ssd_kernel, cs=cs, rpg=rpg, valid_len=valid_len, nc=nc)
    y, hout = pl.pallas_call(
        kern,
        grid=(nb, g_cnt, nc),
        in_specs=[
            pl.BlockSpec((cs, gw), lambda b, g, c: (b * nc + c, xoff_blocks + g)),
            pl.BlockSpec((SUBLANES, gw), lambda b, g, c: (jnp.maximum((b * nc + c) * cpb - 1, 0), xoff_blocks + g)),
            pl.BlockSpec((None, SUBLANES, gw), lambda b, g, c: (b, 0, g)),
            pl.BlockSpec((CONV_W, gw), lambda b, g, c: (0, g)),
            pl.BlockSpec((1, gw), lambda b, g, c: (0, g)),
            pl.BlockSpec((None, cs, LANES), lambda b, g, c: (g, b * nc + c, 0)),
            pl.BlockSpec((None, 1, LANES), lambda b, g, c: (g, 0, 0)),
            pl.BlockSpec((None, 1, LANES), lambda b, g, c: (g, 0, 0)),
            pl.BlockSpec((None, 1, LANES), lambda b, g, c: (g, 0, 0)),
            pl.BlockSpec((None, npair, LANES, D_STATE), lambda b, g, c: (b, g, 0, 0)),
        ],
        out_specs=[
            pl.BlockSpec((cs, rpg * SSD_HEADDIM), lambda b, g, c: (b * nc + c, g)),
            pl.BlockSpec((None, npair, LANES, D_STATE), lambda b, g, c: (b, g, 0, 0)),
        ],
        out_shape=[jax.ShapeDtypeStruct((m, d), F32),
                   jax.ShapeDtypeStruct((nb, nh // 2, LANES, D_STATE), F32)],
        scratch_shapes=[pltpu.VMEM((npair, LANES, D_STATE), F32)],
        compiler_params=_cparams(("arbitrary", "arbitrary", "arbitrary")),
        name="ssd_scan",
    )(xg, xg, hist8, conv_w_g, conv_b_g, dt_g, dtb_g, alog_g, dsk_g, h0.reshape(nb, nh // 2, LANES, D_STATE))
    return y, hout.reshape(nb, nh, SSD_HEADDIM, D_STATE)


def _gate_norm_kernel(y_ref, z_ref, g_ref, o_ref):
    z = z_ref[...]
    o_ref[...] = _rms(y_ref[...] * (z * jax.nn.sigmoid(z)), g_ref[...]).astype(BF16)


def _gate_norm(y, proj, g, *, tm):
    m, d = y.shape
    return pl.pallas_call(
        _gate_norm_kernel,
        grid=(m // tm,),
        in_specs=[pl.BlockSpec((tm, d), lambda i: (i, 0)),
                  pl.BlockSpec((tm, d), lambda i: (i, 0)),
                  pl.BlockSpec((1, d), lambda i: (0, 0))],
        out_specs=pl.BlockSpec((tm, d), lambda i: (i, 0)),
        out_shape=jax.ShapeDtypeStruct((m, d), BF16),
        compiler_params=_cparams(("arbitrary",)),
        name="gate_norm",
    )(y, proj, g.reshape(1, d))


def _attn_prompt_kernel(qi_ref, tail_ref, ki_ref, q_ref, k_ref, v_ref, o_ref, key_scr, thr_scr,
                        *, topk, tk, n_ssd_heads, rq, gps):
    qb = pl.program_id(1)
    g = pl.program_id(2)
    nkt = (qb * Q_BLOCK + Q_BLOCK + tk - 1) // tk
    imin = jnp.int32(INT_MIN)

    @pl.when(g == 0)
    def _():
        qi = qi_ref[...]
        qis = jnp.concatenate([qi[:, h * IDX_DIM:(h + 1) * IDX_DIM] for h in range(IDX_HEADS)],
                              axis=0).astype(BF16)
        w_t = tail_ref[...].T[n_ssd_heads:n_ssd_heads + IDX_HEADS, :] * (IDX_HEADS ** -0.5 * IDX_DIM ** -0.5)
        tpos = qb * Q_BLOCK + lax.broadcasted_iota(I32, (tk, Q_BLOCK), 1)
        kiota = lax.broadcasted_iota(I32, (tk, Q_BLOCK), 0)

        def score_tile(kt, carry):
            rows = pl.ds(pl.multiple_of(kt * tk, tk), tk)
            x = _nt(ki_ref[rows, :].astype(BF16), qis)
            sc = jnp.zeros((tk, Q_BLOCK), F32)
            for h in range(IDX_HEADS):
                sc = sc + jnp.maximum(x[:, h * Q_BLOCK:(h + 1) * Q_BLOCK], 0.0) * w_t[h:h + 1, :]
            key = _float_key(sc + 0.0)
            key_scr[rows, :] = jnp.where(kt * tk + kiota <= tpos, key, imin)
            return carry

        lax.fori_loop(0, nkt, score_tile, 0)

        def count_ge(thr):
            def body(kt, cnt):
                rows = pl.ds(pl.multiple_of(kt * tk, tk), tk)
                ge = (key_scr[rows, :] >= thr).astype(I32)
                return cnt + jnp.sum(ge.reshape(tk // SUBLANES, SUBLANES, Q_BLOCK), axis=0)
            cnt = lax.fori_loop(0, nkt, body, jnp.zeros((SUBLANES, Q_BLOCK), I32))
            return jnp.sum(cnt, axis=0, keepdims=True)

        thr = jnp.maximum(_bisect_kth(count_ge, topk, (1, Q_BLOCK)), jnp.int32(INT_MIN + 1))
        thr_scr[...] = jnp.broadcast_to(thr, (SUBLANES, Q_BLOCK))

    q = q_ref[...]
    nq = rq * Q_BLOCK
    qs = []
    for u in range(gps):
        heads = [q[:, (u * rq + r) * HEAD_DIM:(u * rq + r + 1) * HEAD_DIM] for r in range(rq)]
        qs.append((jnp.concatenate(heads, axis=0) * (HEAD_DIM ** -0.5)).astype(BF16))
    thr_q = jnp.concatenate([thr_scr[0:1, :]] * rq, axis=1)

    def attend(kt, carry):
        rows = pl.ds(pl.multiple_of(kt * tk, tk), tk)
        sel = jnp.concatenate([key_scr[rows, :]] * rq, axis=1) >= thr_q
        k_all = k_ref[rows, :]
        v_all = v_ref[rows, :]
        s_ts = [_nt(k_all[:, u * HEAD_DIM:(u + 1) * HEAD_DIM].astype(BF16), qs[u]) for u in range(gps)]
        v_ts = [v_all[:, u * HEAD_DIM:(u + 1) * HEAD_DIM].T.astype(BF16) for u in range(gps)]
        out = []
        for u in range(gps):
            m, l, acc = carry[u]
            s_t = jnp.where(sel, s_ts[u], NEG_BIG)
            m_new = jnp.maximum(m, jnp.max(s_t, axis=0, keepdims=True))
            alpha = jnp.exp(m - m_new)
            p = jnp.exp(s_t - m_new)
            l = l * alpha + jnp.sum(p, axis=0, keepdims=True)
            acc = acc * alpha + jnp.dot(v_ts[u], p.astype(BF16), preferred_element_type=F32)
            out.append((m_new, l, acc))
        return tuple(out)

    init = tuple((jnp.full((1, nq), M_INIT, F32), jnp.zeros((1, nq), F32), jnp.zeros((HEAD_DIM, nq), F32))
                 for _ in range(gps))
    res = lax.fori_loop(0, nkt, attend, init)
    for u in range(gps):
        _, l, acc = res[u]
        o = (acc / l).T
        for r in range(rq):
            c0 = (u * rq + r) * HEAD_DIM
            o_ref[:, c0:c0 + HEAD_DIM] = o[r * Q_BLOCK:(r + 1) * Q_BLOCK, :]


def _attn_prompt(proj, lay, *, nb, seq, d):
    rq = d // HEAD_DIM // KV_HEADS
    nqb = seq // Q_BLOCK
    topk = min(IDX_TOPK, seq // 4)
    tk = min(ATTN_TK, seq)
    qiw = IDX_HEADS * IDX_DIM
    gps = min(ATTN_GROUPS_PER_STEP, KV_HEADS)
    qw = gps * rq * HEAD_DIM
    kw = gps * HEAD_DIM
    assert KV_HEADS % gps == 0 and lay["q"] % qw == 0 and lay["k"] % kw == 0 and lay["v"] % kw == 0
    kern = functools.partial(_attn_prompt_kernel, topk=topk, tk=tk, n_ssd_heads=d // SSD_HEADDIM, rq=rq, gps=gps)
    return pl.pallas_call(
        kern,
        grid=(nb, nqb, KV_HEADS // gps),
        in_specs=[
            pl.BlockSpec((Q_BLOCK, qiw), lambda b, i, g: (b * nqb + i, lay["qi"] // qiw)),
            pl.BlockSpec((Q_BLOCK, LANES), lambda b, i, g: (b * nqb + i, lay["tail"] // LANES + 1)),
            pl.BlockSpec((seq, IDX_DIM), lambda b, i, g: (b, lay["tail"] // IDX_DIM)),
            pl.BlockSpec((Q_BLOCK, qw), lambda b, i, g: (b * nqb + i, lay["q"] // qw + g)),
            pl.BlockSpec((seq, kw), lambda b, i, g: (b, lay["k"] // kw + g)),
            pl.BlockSpec((seq, kw), lambda b, i, g: (b, lay["v"] // kw + g)),
        ],
        out_specs=pl.BlockSpec((Q_BLOCK, qw), lambda b, i, g: (b * nqb + i, g)),
        out_shape=jax.ShapeDtypeStruct((nb * seq, d), F32),
        scratch_shapes=[pltpu.VMEM((seq, Q_BLOCK), I32), pltpu.VMEM((SUBLANES, Q_BLOCK), I32)],
        compiler_params=_cparams(("arbitrary", "arbitrary", "arbitrary")),
        name="attn_prompt",
    )(proj, proj, proj, proj, proj, proj)


def _head_rows(page_ref, head):
    return page_ref[pl.ds(head, PAGE_SIZE, stride=KV_HEADS), :]


def _fold_lane_groups(x):
    sh = LANES // 2
    while sh >= SAMPLE_PAD:
        x = x + pltpu.roll(x, sh, 1)
        sh //= 2
    return x


def _sample_scores_kernel(pt_ref, *refs, npages, topk, pps):
    kip_refs = refs[:pps]
    kp_refs = refs[pps:2 * pps]
    qi_ref, w_ref, qbd_ref, kinew_ref, knew_ref, p_ref, key_scr, kc_scr, s_scr, qbd_scr = refs[2 * pps:]
    j = pl.program_id(1)
    imin = jnp.int32(INT_MIN)
    nip = qi_ref.shape[0]
    nlp = qbd_ref.shape[0]
    groups_per_slab = LANES // SAMPLE_PAD
    lane = lax.broadcasted_iota(I32, (PAGE_SIZE, LANES), 1)
    lane_group = lane // SAMPLE_PAD

    @pl.when(j == 0)
    def _():
        qbd_scr[...] = (qbd_ref[...] * (HEAD_DIM ** -0.5)).astype(BF16)
        kc_scr[...] = jnp.full(kc_scr.shape, imin, I32)

    qi = qi_ref[...].astype(BF16)
    w_row = w_ref[...]

    def index_keys(kidx):
        x = _nt(kidx.astype(BF16), qi)
        r = jnp.maximum(x, 0.0) * w_row
        acc = r[:, :LANES]
        for c in range(1, nip // LANES):
            acc = acc + r[:, c * LANES:(c + 1) * LANES]
        return _float_key(_fold_lane_groups(acc) + 0.0)

    def put_page(page, keys, scores):
        rows = pl.ds(pl.multiple_of(page * PAGE_SIZE, PAGE_SIZE), PAGE_SIZE)
        key_scr[rows, :] = keys
        s_scr[rows, :] = scores
        slab = page // groups_per_slab
        kc_scr[slab] = jnp.where(lane_group == page % groups_per_slab, keys, kc_scr[slab])

    kidx_cat = jnp.concatenate([r[...] for r in kip_refs], axis=0)
    keys = index_keys(kidx_cat)
    k2d = jnp.concatenate(
        [jnp.concatenate([_head_rows(kp, h).astype(BF16) for h in range(KV_HEADS)], axis=1)
         for kp in kp_refs], axis=0)
    scores = _nt(k2d, qbd_scr[...])
    for i in range(pps):
        sl = slice(i * PAGE_SIZE, (i + 1) * PAGE_SIZE)
        put_page(j * pps + i, keys[sl], scores[sl])

    @pl.when(j == 0)
    def _():
        knew = index_keys(kinew_ref[...])
        srow = lax.broadcasted_iota(I32, (PAGE_SIZE, LANES), 0)
        knew = jnp.where(srow <= lane % SAMPLE_PAD, knew, imin)
        put_page(jnp.int32(npages), knew, _nt(knew_ref[...].astype(BF16), qbd_scr[...]))

    @pl.when(j == pl.num_programs(1) - 1)
    def _():
        kc = kc_scr[...]

        tok = lane[0:1, :] % SAMPLE_PAD

        def count_ge(thr):
            ge = (kc >= thr).astype(F32)
            cnt = jnp.sum(jnp.sum(ge.reshape(-1, SUBLANES, LANES), axis=0), axis=0, keepdims=True)
            tot = jnp.zeros((1, LANES), F32)
            for t in range(SAMPLE_PAD):
                tot = jnp.where(tok == t, jnp.sum(jnp.where(tok == t, cnt, 0.0), axis=1, keepdims=True), tot)
            return tot

        thr = _bisect_kth(count_ge, topk, (1, LANES))
        reps = nlp // LANES
        thr_q = jnp.concatenate([thr] * reps, axis=1)

        def masked(page):
            rows = pl.ds(pl.multiple_of(page * PAGE_SIZE, PAGE_SIZE), PAGE_SIZE)
            key = jnp.concatenate([key_scr[rows, :]] * reps, axis=1)
            return rows, (key >= thr_q) & (key != imin)

        def max_body(page, m):
            rows, sel = masked(page)
            return jnp.maximum(m, jnp.max(jnp.where(sel, s_scr[rows, :], NEG_BIG), axis=0, keepdims=True))

        unroll = next(u for u in range(MAX_PAGE_UNROLL, 0, -1) if (npages + 1) % u == 0)
        m = lax.fori_loop(0, npages + 1, max_body, jnp.full((1, nlp), NEG_BIG, F32), unroll=unroll)

        def exp_body(page, l):
            rows, sel = masked(page)
            e = jnp.where(sel, jnp.exp(s_scr[rows, :] - m), 0.0)
            s_scr[rows, :] = e
            return l + jnp.sum(e, axis=0, keepdims=True)

        l = lax.fori_loop(0, npages + 1, exp_body, jnp.zeros((1, nlp), F32), unroll=unroll)

        def out_body(page, carry):
            rows = pl.ds(pl.multiple_of(page * PAGE_SIZE, PAGE_SIZE), PAGE_SIZE)
            p_ref[page] = (s_scr[rows, :] / l).T.astype(BF16)
            return carry

        lax.fori_loop(0, npages + 1, out_body, 0, unroll=unroll)


def _sample_pv_kernel(pt_ref, *refs, pps, rq):
    vp_refs = refs[:pps]
    p_ref, pnew_ref, vnew_ref, o_ref, acc_scr = refs[pps:]
    j = pl.program_id(1)
    nrow = rq * SAMPLE_PAD

    @pl.when(j == 0)
    def _():
        pn = pnew_ref[0]
        vn = vnew_ref[...].astype(BF16)
        for g in range(KV_HEADS):
            acc_scr[g] = jnp.dot(pn[g * nrow:(g + 1) * nrow, :], vn[:, g * HEAD_DIM:(g + 1) * HEAD_DIM],
                                 preferred_element_type=F32)

    pcat = jnp.concatenate([p_ref[i] for i in range(pps)], axis=1)
    for g in range(KV_HEADS):
        vg = jnp.concatenate([_head_rows(vp, g).astype(BF16) for vp in vp_refs], axis=0)
        acc_scr[g] += jnp.dot(pcat[g * nrow:(g + 1) * nrow, :], vg, preferred_element_type=F32)

    @pl.when(j == pl.num_programs(1) - 1)
    def _():
        o_ref[...] = acc_scr[...]


def _attn_sample(q, qi, wi, ki_new, k_new, v_new, cache_k, cache_v, cache_kidx, page_table, *, t_valid):
    nb, npages = page_table.shape
    pps = min(PAGES_PER_STEP, npages)
    assert npages % pps == 0
    nsteps = npages // pps
    past = npages * PAGE_SIZE
    topk = min(IDX_TOPK, (past + t_valid) // 4)
    d = q.shape[-1]
    rq = d // HEAD_DIM // KV_HEADS
    kvw = KV_HEADS * HEAD_DIM
    nrow = rq * SAMPLE_PAD
    ck = cache_k.reshape(-1, PAGE_SIZE * KV_HEADS, HEAD_DIM)
    cv = cache_v.reshape(-1, PAGE_SIZE * KV_HEADS, HEAD_DIM)
    cki = cache_kidx.reshape(-1, PAGE_SIZE, IDX_DIM)

    def pad_to(a, axis, mult):
        n = a.shape[axis]
        widths = [(0, 0)] * a.ndim
        widths[axis] = (0, -(-n // mult) * mult - n)
        return jnp.pad(a, widths)

    qi_s = pad_to(qi.reshape(nb, SAMPLE_PAD, IDX_HEADS, IDX_DIM).transpose(0, 2, 1, 3)
                  .reshape(nb, IDX_HEADS * SAMPLE_PAD, IDX_DIM), 1, LANES)
    w_row = pad_to((wi * (IDX_HEADS ** -0.5 * IDX_DIM ** -0.5)).transpose(0, 2, 1)
                   .reshape(nb, 1, IDX_HEADS * SAMPLE_PAD), 2, LANES)
    nip = qi_s.shape[1]
    q_g = q.reshape(nb, SAMPLE_PAD, KV_HEADS, rq, HEAD_DIM).transpose(0, 2, 3, 1, 4).reshape(
        nb, KV_HEADS, nrow, HEAD_DIM)
    qbd = pad_to(jnp.einsum("bgrd,gh->bgrhd", q_g, jnp.eye(KV_HEADS, dtype=q.dtype))
                 .reshape(nb, KV_HEADS * nrow, kvw), 1, LANES)
    nlp = qbd.shape[1]
    pad_rows = ((0, 0), (0, PAGE_SIZE - SAMPLE_PAD), (0, 0))
    kinew_p = jnp.pad(ki_new, pad_rows)
    knew_p = jnp.pad(k_new, pad_rows)
    vnew_p = jnp.pad(v_new, pad_rows)
    npg = npages + 1
    nslab = -(-npg // (LANES // SAMPLE_PAD))

    def kidx_spec(i):
        return pl.BlockSpec((None, PAGE_SIZE, IDX_DIM), lambda b, j, pt, i=i: (pt[b, j * pps + i], 0, 0))

    def kv_specs():
        return [pl.BlockSpec((None, PAGE_SIZE * KV_HEADS, HEAD_DIM), lambda b, j, pt, i=i: (pt[b, j * pps + i], 0, 0))
                for i in range(pps)]

    probs = pl.pallas_call(
        functools.partial(_sample_scores_kernel, npages=npages, topk=topk, pps=pps),
        grid_spec=pltpu.PrefetchScalarGridSpec(
            num_scalar_prefetch=1,
            grid=(nb, nsteps),
            in_specs=[kidx_spec(i) for i in range(pps)] + kv_specs() + [
                pl.BlockSpec((None, nip, IDX_DIM), lambda b, j, pt: (b, 0, 0)),
                pl.BlockSpec((None, 1, nip), lambda b, j, pt: (b, 0, 0)),
                pl.BlockSpec((None, nlp, kvw), lambda b, j, pt: (b, 0, 0)),
                pl.BlockSpec((None, PAGE_SIZE, IDX_DIM), lambda b, j, pt: (b, 0, 0)),
                pl.BlockSpec((None, PAGE_SIZE, kvw), lambda b, j, pt: (b, 0, 0)),
            ],
            out_specs=pl.BlockSpec((None, npg, nlp, PAGE_SIZE), lambda b, j, pt: (b, 0, 0, 0)),
            scratch_shapes=[pltpu.VMEM((npg * PAGE_SIZE, LANES), I32),
                            pltpu.VMEM((nslab, PAGE_SIZE, LANES), I32),
                            pltpu.VMEM((npg * PAGE_SIZE, nlp), F32),
                            pltpu.VMEM((nlp, kvw), BF16)],
        ),
        out_shape=jax.ShapeDtypeStruct((nb, npg, nlp, PAGE_SIZE), BF16),
        compiler_params=_cparams(("arbitrary", "arbitrary")),
        name="sample_scores",
    )(page_table, *([cki] * pps), *([ck] * pps), qi_s, w_row, qbd, kinew_p, knew_p)

    o = pl.pallas_call(
        functools.partial(_sample_pv_kernel, pps=pps, rq=rq),
        grid_spec=pltpu.PrefetchScalarGridSpec(
            num_scalar_prefetch=1,
            grid=(nb, nsteps),
            in_specs=kv_specs() + [
                pl.BlockSpec((None, pps, nlp, PAGE_SIZE), lambda b, j, pt: (b, j, 0, 0)),
                pl.BlockSpec((None, 1, nlp, PAGE_SIZE), lambda b, j, pt: (b, npages, 0, 0)),
                pl.BlockSpec((None, PAGE_SIZE, kvw), lambda b, j, pt: (b, 0, 0)),
            ],
            out_specs=pl.BlockSpec((None, KV_HEADS, nrow, HEAD_DIM), lambda b, j, pt: (b, 0, 0, 0)),
            scratch_shapes=[pltpu.VMEM((KV_HEADS, nrow, HEAD_DIM), F32)],
        ),
        out_shape=jax.ShapeDtypeStruct((nb, KV_HEADS, nrow, HEAD_DIM), F32),
        compiler_params=_cparams(("arbitrary", "arbitrary")),
        name="sample_pv",
    )(page_table, *([cv] * pps), probs, probs, vnew_p)
    return o.reshape(nb, KV_HEADS, rq, SAMPLE_PAD, HEAD_DIM).transpose(0, 3, 1, 2, 4).reshape(nb * SAMPLE_PAD, d)


def _rms_cast_kernel(x_ref, g_ref, o_ref):
    o_ref[...] = _rms(x_ref[...], g_ref[...]).astype(BF16)


def _rms_cast(x, g, *, tm):
    m, d = x.shape
    return pl.pallas_call(
        _rms_cast_kernel,
        grid=(m // tm,),
        in_specs=[pl.BlockSpec((tm, d), lambda i: (i, 0)), pl.BlockSpec((1, d), lambda i: (0, 0))],
        out_specs=pl.BlockSpec((tm, d), lambda i: (i, 0)),
        out_shape=jax.ShapeDtypeStruct((m, d), BF16),
        compiler_params=_cparams(("arbitrary",)),
        name="rms_cast",
    )(x, g.reshape(1, d))


def _outproj_kernel(y_ref, a_ref, w1_ref, w2_ref, x_ref, gate_ref, o_ref):
    acc = jnp.dot(y_ref[...], w1_ref[...], preferred_element_type=F32)
    acc = acc + jnp.dot(a_ref[...], w2_ref[...], preferred_element_type=F32)
    o_ref[...] = x_ref[...] + gate_ref[...] * acc


def _outproj(y_n, attn_n, w, x, gate, *, tm, per_token, rows_per_batch):
    m, d = x.shape
    tn = PROJ_TN
    if per_token:
        gate_spec = pl.BlockSpec((tm, tn), lambda i, j: (i, j))
    else:
        tpb = rows_per_batch // tm
        gate_spec = pl.BlockSpec((None, 1, tn), lambda i, j: (i // tpb, 0, j))
    return pl.pallas_call(
        _outproj_kernel,
        grid=(m // tm, d // tn),
        in_specs=[pl.BlockSpec((tm, d), lambda i, j: (i, 0), pipeline_mode=pl.Buffered(1)),
                  pl.BlockSpec((tm, d), lambda i, j: (i, 0), pipeline_mode=pl.Buffered(1)),
                  pl.BlockSpec((d, tn), lambda i, j: (0, j)),
                  pl.BlockSpec((d, tn), lambda i, j: (1, j)),
                  pl.BlockSpec((tm, tn), lambda i, j: (i, j)),
                  gate_spec],
        out_specs=pl.BlockSpec((tm, tn), lambda i, j: (i, j)),
        out_shape=jax.ShapeDtypeStruct((m, d), F32),
        compiler_params=_cparams(("arbitrary", "arbitrary")),
        name="outproj",
    )(y_n, attn_n, w, w, x, gate)


def _peer_route_kernel(q_ref, k1_ref, k2_ref, s1_ref, s2_ref, tau_ref, cc_ref):
    q = q_ref[...]
    k1 = k1_ref[...].astype(BF16)
    k2 = k2_ref[...].astype(BF16)
    half = PEER_D_KEY // 2
    tm = q.shape[0]
    taus, ccs = [], []
    for h in range(PEER_HEADS):
        base = h * PEER_D_KEY
        s1 = _nt(k1, q[:, base:base + half].astype(BF16))
        s2 = _nt(k2, q[:, base + half:base + PEER_D_KEY].astype(BF16))
        s1_ref[h * PEER_KEYS:(h + 1) * PEER_KEYS, :] = s1
        s2_ref[h * PEER_KEYS:(h + 1) * PEER_KEYS, :] = s2

        def top_vals(x):
            vals = []
            for _ in range(PEER_TOPK):
                m = jnp.max(x, axis=0, keepdims=True)
                vals.append(m)
                x = jnp.where(x == m, -jnp.inf, x)
            return vals

        v1 = top_vals(s1)
        v2 = top_vals(s2)
        v1a = jnp.concatenate(v1, axis=0)
        v2a = jnp.concatenate(v2, axis=0)
        half_k = PEER_TOPK // 2
        cand = jnp.concatenate([v + v2a[:half_k] for v in v1[:half_k]]
                               + [v1[0] + v2a[half_k:], v1a[half_k:] + v2[0]], axis=0) + 0.0
        ckey = _float_key(cand)

        def count_ge(thr, ckey=ckey):
            return jnp.sum((ckey >= thr).astype(I32), axis=0, keepdims=True)

        tau = _key_float(_bisect_kth(count_ge, PEER_TOPK, (1, tm)))
        cmax = cand[0:1, :]
        zsum = jnp.sum(jnp.where(cand >= tau, jnp.exp(cand - cmax), 0.0), axis=0, keepdims=True)
        taus.append(tau)
        ccs.append(cmax + jnp.log(zsum))
    tau_ref[...] = jnp.concatenate(taus, axis=0)
    cc_ref[...] = jnp.concatenate(ccs, axis=0)


def _peer_route(q, k1, k2, *, tm):
    t = q.shape[0]
    rows = PEER_HEADS * PEER_KEYS
    half = PEER_D_KEY // 2
    return pl.pallas_call(
        _peer_route_kernel,
        grid=(t // tm,),
        in_specs=[pl.BlockSpec((tm, PEER_HEADS * PEER_D_KEY), lambda i: (i, 0)),
                  pl.BlockSpec((PEER_KEYS, half), lambda i: (0, 0)),
                  pl.BlockSpec((PEER_KEYS, half), lambda i: (0, 0))],
        out_specs=[pl.BlockSpec((rows, tm), lambda i: (0, i)),
                   pl.BlockSpec((rows, tm), lambda i: (0, i)),
                   pl.BlockSpec((PEER_HEADS, tm), lambda i: (0, i)),
                   pl.BlockSpec((PEER_HEADS, tm), lambda i: (0, i))],
        out_shape=[jax.ShapeDtypeStruct((rows, t), F32), jax.ShapeDtypeStruct((rows, t), F32),
                   jax.ShapeDtypeStruct((PEER_HEADS, t), F32), jax.ShapeDtypeStruct((PEER_HEADS, t), F32)],
        compiler_params=_cparams(("arbitrary",)),
        name="peer_route",
    )(q, k1, k2)


def _peer_dense_kernel(xb_ref, s1_ref, s2_ref, tau_ref, cc_ref, u_ref, v_ref, o_ref, *, te):
    e = pl.program_id(1)

    @pl.when(e == 0)
    def _():
        o_ref[...] = jnp.zeros(o_ref.shape, F32)

    tm = xb_ref.shape[0]
    nsub = MXU_TILE // PEER_KEYS
    coefs = []
    for c in range(te // MXU_TILE):
        ut = _nt(xb_ref[...], u_ref[c * MXU_TILE:(c + 1) * MXU_TILE, :]).T
        for i in range(nsub):
            k = c * nsub + i
            gate = jnp.zeros((PEER_KEYS, tm), F32)
            for h in range(PEER_HEADS):
                sm = s1_ref[k, h:h + 1, :] + s2_ref[h * PEER_KEYS:(h + 1) * PEER_KEYS, :]
                gate = gate + jnp.where(sm >= tau_ref[h:h + 1, :], jnp.exp(sm - cc_ref[h:h + 1, :]), 0.0)
            coefs.append(gate * jax.nn.gelu(ut[i * PEER_KEYS:(i + 1) * PEER_KEYS, :]))
    coef = jnp.concatenate(coefs, axis=0).T.astype(BF16)
    for n in range(o_ref.shape[1] // MXU_TILE):
        cols = slice(n * MXU_TILE, (n + 1) * MXU_TILE)
        o_ref[:, cols] += jnp.dot(coef, v_ref[:, cols], preferred_element_type=F32)


def _peer_dense(xb, s1, s2, tau, cc, u_b, v_b, *, tm):
    t, d = xb.shape
    ne = u_b.shape[0]
    te = min(PEER_TE, ne)
    rows = PEER_HEADS * PEER_KEYS
    once = pl.Buffered(1)
    s1 = s1.reshape(PEER_HEADS, PEER_KEYS, t).transpose(1, 0, 2)
    return pl.pallas_call(
        functools.partial(_peer_dense_kernel, te=te),
        grid=(t // tm, ne // te),
        in_specs=[pl.BlockSpec((tm, d), lambda i, e: (i, 0), pipeline_mode=once),
                  pl.BlockSpec((te // PEER_KEYS, PEER_HEADS, tm), lambda i, e: (e, 0, i)),
                  pl.BlockSpec((rows, tm), lambda i, e: (0, i), pipeline_mode=once),
                  pl.BlockSpec((PEER_HEADS, tm), lambda i, e: (0, i), pipeline_mode=once),
                  pl.BlockSpec((PEER_HEADS, tm), lambda i, e: (0, i), pipeline_mode=once),
                  pl.BlockSpec((te, d), lambda i, e: (e, 0)),
                  pl.BlockSpec((te, d), lambda i, e: (e, 0))],
        out_specs=pl.BlockSpec((tm, d), lambda i, e: (i, 0), pipeline_mode=once),
        out_shape=jax.ShapeDtypeStruct((t, d), F32),
        compiler_params=_cparams(("arbitrary", "arbitrary")),
        name="peer_dense",
    )(xb, s1, s2, tau, cc, u_b, v_b)


def _final_kernel(x_ref, f_ref, gate_ref, g_ref, o_ref):
    o_ref[...] = _rms(x_ref[...] + gate_ref[...] * f_ref[...], g_ref[...])


def _final(x, f, gate, g, *, tm, per_token, rows_per_batch):
    m, d = x.shape
    if per_token:
        gate_spec = pl.BlockSpec((tm, d), lambda i: (i, 0))
    else:
        tpb = rows_per_batch // tm
        gate_spec = pl.BlockSpec((None, 1, d), lambda i: (i // tpb, 0, 0))
    return pl.pallas_call(
        _final_kernel,
        grid=(m // tm,),
        in_specs=[pl.BlockSpec((tm, d), lambda i: (i, 0)),
                  pl.BlockSpec((tm, d), lambda i: (i, 0)),
                  gate_spec,
                  pl.BlockSpec((1, d), lambda i: (0, 0))],
        out_specs=pl.BlockSpec((tm, d), lambda i: (i, 0)),
        out_shape=jax.ShapeDtypeStruct((m, d), F32),
        compiler_params=_cparams(("arbitrary",)),
        name="final_norm",
    )(x, f, gate, g.reshape(1, d))


def _layout(d):
    nh = d // SSD_HEADDIM
    kvw = KV_HEADS * HEAD_DIM
    qiw = IDX_HEADS * IDX_DIM
    xbcw = d + 2 * SSD_GROUPS * D_STATE
    lay, off = {}, 0
    for name, w in (("z", d), ("q", d), ("qi", qiw), ("xbc", xbcw), ("k", kvw), ("v", kvw), ("tail", PROJ_TN)):
        assert w % PROJ_TN == 0
        lay[name] = off
        off += w
    lay["total"] = off
    assert IDX_DIM + nh + IDX_HEADS <= PROJ_TN
    return lay


def _group_cols(xbc, d):
    lead = xbc.shape[:-1]
    gn = SSD_GROUPS * D_STATE
    x = xbc[..., :d].reshape(lead + (SSD_GROUPS, d // SSD_GROUPS))
    b = xbc[..., d:d + gn].reshape(lead + (SSD_GROUPS, D_STATE))
    c = xbc[..., d + gn:].reshape(lead + (SSD_GROUPS, D_STATE))
    return jnp.concatenate([x, b, c], axis=-1).reshape(lead + (d + 2 * gn,))


def _ungroup_cols(xg, d):
    lead = xg.shape[:-1]
    gn = SSD_GROUPS * D_STATE
    xsw = d // SSD_GROUPS
    g3 = xg.reshape(lead + (SSD_GROUPS, xsw + 2 * D_STATE))
    return jnp.concatenate([g3[..., :xsw].reshape(lead + (d,)),
                            g3[..., xsw:xsw + D_STATE].reshape(lead + (gn,)),
                            g3[..., xsw + D_STATE:].reshape(lead + (gn,))], axis=-1)


def _w_in_moves(d):
    nh = d // SSD_HEADDIM
    kvw = KV_HEADS * HEAD_DIM
    qiw = IDX_HEADS * IDX_DIM
    xbcw = d + 2 * SSD_GROUPS * D_STATE
    names = ("z", "xbc", "dt", "q", "k", "v", "qi", "wi", "ki")
    sizes = (d, xbcw, nh, d, kvw, kvw, qiw, IDX_HEADS, IDX_DIM)
    src = dict(zip(names, np.concatenate([[0], np.cumsum(sizes)[:-1]]).tolist()))
    lay = _layout(d)
    moves = [(lay["z"], src["z"], d), (lay["q"], src["q"], d), (lay["qi"], src["qi"], qiw),
             (lay["k"], src["k"], kvw), (lay["v"], src["v"], kvw)]
    xsw = d // SSD_GROUPS
    gw = xsw + 2 * D_STATE
    gn = SSD_GROUPS * D_STATE
    for g in range(SSD_GROUPS):
        base = lay["xbc"] + g * gw
        moves += [(base, src["xbc"] + g * xsw, xsw),
                  (base + xsw, src["xbc"] + d + g * D_STATE, D_STATE),
                  (base + xsw + D_STATE, src["xbc"] + d + gn + g * D_STATE, D_STATE)]
    tail = [(src["ki"], IDX_DIM), (src["dt"], nh), (src["wi"], IDX_HEADS)]
    return moves, tail, int(sum(sizes))


def _w_in_prep_kernel(x_ref, o_ref, *, moves, tail, ncols):
    def take(src, w):
        lo = src // LANES * LANES
        hi = min(-(-(src + w) // LANES) * LANES, ncols)
        return x_ref[:, lo:hi][:, src - lo:src - lo + w]

    for dst, src, w in moves:
        o_ref[:, dst:dst + w] = take(src, w).astype(BF16)
    tr = x_ref.shape[0]
    pieces = [take(s, w) for s, w in tail]
    pieces.append(jnp.zeros((tr, PROJ_TN - sum(w for _, w in tail)), F32))
    tail_dst = o_ref.shape[1] - PROJ_TN
    o_ref[:, tail_dst:] = jnp.concatenate(pieces, axis=1).astype(BF16)


def _prep_w_in(w_in, d):
    moves, tail, ncols = _w_in_moves(d)
    total = _layout(d)["total"]
    tr = min(d, W_IN_PREP_ROWS)
    return pl.pallas_call(
        functools.partial(_w_in_prep_kernel, moves=tuple(moves), tail=tuple(tail), ncols=ncols),
        grid=(d // tr,),
        in_specs=[pl.BlockSpec((None, tr, ncols), lambda i: (0, i, 0))],
        out_specs=pl.BlockSpec((tr, total), lambda i: (i, 0)),
        out_shape=jax.ShapeDtypeStruct((d, total), BF16),
        compiler_params=_cparams(("arbitrary",)),
        name="w_in_prep",
    )(w_in)


def _group_major(vec, rpg):
    return jnp.pad(vec.reshape(SSD_GROUPS, 1, rpg), ((0, 0), (0, 0), (0, LANES - rpg)))


def _rope_tables(pos):
    half = HEAD_DIM // 8
    inv = ROPE_THETA ** (-jnp.arange(half, dtype=F32) / half)
    ang = pos.astype(F32)[:, None] * inv[None, :]
    cos, sin = jnp.cos(ang), jnp.sin(ang)
    n = pos.shape[0]
    rest = HEAD_DIM - 2 * half
    return (jnp.concatenate([cos, cos, jnp.ones((n, rest), F32)], axis=1),
            jnp.concatenate([-sin, sin, jnp.zeros((n, rest), F32)], axis=1))


def _layer(x2d, mods, pos, prm, *, nb, seq, per_token, tm, ssd_cfg, attn_fn, peer_tm):
    m, d = x2d.shape
    lay = prm["lay"]
    nh = d // SSD_HEADDIM
    rpg = nh // SSD_GROUPS
    sh1, sc1, g1, sh2, sc2, g2 = mods
    cos, sin = _rope_tables(pos)
    xoff = lay["xbc"]
    rope = (cos, sin, prm["kidx_norm_g"],
            ((lay["q"] // PROJ_TN, lay["xbc"] // PROJ_TN), (lay["k"] // PROJ_TN, lay["v"] // PROJ_TN)),
            lay["tail"] // PROJ_TN)
    tm_e = min(tm, ELEMWISE_TM)
    tm_p = min(m if per_token else seq, PROJ_TM)
    h1 = _modulate(x2d, prm["norm1_g"], sh1, sc1, tm=tm_e, per_token=per_token, rows_per_batch=seq)
    proj = _proj(h1, prm["w_in"], tm=tm_p, rope=rope)

    gw = rpg * SSD_HEADDIM + 2 * D_STATE
    dt_raw = proj[:, lay["tail"] + IDX_DIM: lay["tail"] + IDX_DIM + nh]
    y_ssd, ssm_new = ssd_cfg(proj, dt_raw, xoff // gw)
    y_n = _gate_norm(y_ssd, proj, prm["ssd_norm_g"], tm=tm_e)

    o_attn = attn_fn(proj)

    a_n = _rms_cast(o_attn, prm["attn_norm_g"], tm=tm_e)
    x1 = _outproj(y_n, a_n, prm["w_out"], x2d, g1, tm=tm_p, per_token=per_token,
                  rows_per_batch=seq)

    xb = _modulate(x1, prm["norm2_g"], sh2, sc2, tm=tm_e, per_token=per_token, rows_per_batch=seq)
    qp = _proj(xb, prm["peer_wq"], tm=tm_p)
    s1, s2, tau, cc = _peer_route(qp, prm["peer_k1"], prm["peer_k2"], tm=min(peer_tm, PEER_ROUTE_TM))
    ffn = _peer_dense(xb, s1, s2, tau, cc, prm["peer_u"], prm["peer_v"], tm=peer_tm)
    y = _final(x1, ffn, g2, prm["final_norm_g"], tm=tm_e, per_token=per_token, rows_per_batch=seq)
    return y, proj, ssm_new


def kernel(x_prompt, x_sample, c_prompt, c_sample, cache_k, cache_v, cache_kidx, state_ssm, state_conv, page_table, w_mod, b_mod, norm1_g, w_in, conv_w, conv_b, dt_bias, a_log, d_skip, ssd_norm_g, kidx_norm_g, attn_norm_g, w_out, norm2_g, peer_wq, peer_k1, peer_k2, peer_u, peer_v, final_norm_g):
    nbp, seq, d = x_prompt.shape
    nbs, tdec, _ = x_sample.shape
    depth = w_mod.shape[0]
    assert depth == 1
    nh = d // SSD_HEADDIM
    rpg = nh // SSD_GROUPS
    gw = rpg * SSD_HEADDIM + 2 * D_STATE
    lay = _layout(d)
    assert lay["xbc"] % gw == 0
    kvw = KV_HEADS * HEAD_DIM
    xbcw = d + 2 * SSD_GROUPS * D_STATE

    prm = dict(
        lay=lay,
        norm1_g=norm1_g[0], kidx_norm_g=kidx_norm_g[0], ssd_norm_g=ssd_norm_g[0], attn_norm_g=attn_norm_g[0],
        norm2_g=norm2_g[0], final_norm_g=final_norm_g,
        w_in=_prep_w_in(w_in, d),
        w_out=_cast_bf16(w_out[0]),
        peer_wq=_cast_bf16(peer_wq[0]), peer_k1=peer_k1[0], peer_k2=peer_k2[0],
        peer_u=_cast_bf16(peer_u[0]), peer_v=_cast_bf16(peer_v[0]),
    )
    conv_w_g = _group_cols(conv_w[0], d)
    conv_b_g = _group_cols(conv_b[0], d).reshape(1, xbcw)
    dtb_g = _group_major(dt_bias[0], rpg)
    alog_g = _group_major(a_log[0], rpg)
    dsk_g = _group_major(d_skip[0], rpg)

    mod = _adaln(jnp.concatenate([c_prompt, c_sample], axis=0), w_mod[0], b_mod[0])
    mod_p = [a.reshape(nbp, 1, d) for a in jnp.split(mod[:nbp], 6, axis=-1)]
    mod_s = [jnp.repeat(a, SAMPLE_PAD, axis=0) for a in jnp.split(mod[nbp:], 6, axis=-1)]

    def dt_group_major(dt_raw):
        rows = dt_raw.shape[0]
        dtg = dt_raw.reshape(rows, SSD_GROUPS, rpg).transpose(1, 0, 2)
        return jnp.pad(dtg, ((0, 0), (0, 0), (0, LANES - rpg)))

    cs_p = min(SSD_CHUNK, seq)
    nc_p = seq // cs_p
    assert seq % cs_p == 0 and seq % Q_BLOCK == 0

    def ssd_prompt(proj, dt_raw, xoff_blocks):
        hist = jnp.zeros((nbp, SUBLANES, xbcw), F32)
        h0 = jnp.zeros((nbp, nh, SSD_HEADDIM, D_STATE), F32)
        return _ssd(proj, xoff_blocks, hist, conv_w_g, conv_b_g, dt_group_major(dt_raw), dtb_g, alog_g, dsk_g, h0,
                    nb=nbp, nc=nc_p, cs=cs_p, d=d, valid_len=cs_p)

    tm_p = min(512, seq)
    pos_p = jnp.tile(jnp.arange(seq), nbp)
    y_p, proj_p, ssm_p = _layer(
        x_prompt.reshape(nbp * seq, d), mod_p, pos_p, prm, nb=nbp, seq=seq, per_token=False, tm=tm_p,
        ssd_cfg=ssd_prompt, attn_fn=functools.partial(_attn_prompt, lay=lay, nb=nbp, seq=seq, d=d),
        peer_tm=min(PEER_TM, nbp * seq))

    npages = page_table.shape[1]
    past = npages * PAGE_SIZE
    ms = nbs * SAMPLE_PAD
    xs_pad = jnp.pad(x_sample, ((0, 0), (0, SAMPLE_PAD - tdec), (0, 0))).reshape(ms, d)
    pos_s = jnp.tile(past + jnp.arange(SAMPLE_PAD), nbs)

    def ssd_sample(proj, dt_raw, xoff_blocks):
        xbc = proj[:, lay["xbc"]:lay["xbc"] + xbcw].reshape(nbs, SAMPLE_PAD, xbcw)
        xbc = jnp.pad(xbc, ((0, 0), (0, SAMPLE_CHUNK - SAMPLE_PAD), (0, 0))).reshape(nbs * SAMPLE_CHUNK, xbcw)
        dtr = jnp.pad(dt_raw.reshape(nbs, SAMPLE_PAD, nh), ((0, 0), (0, SAMPLE_CHUNK - SAMPLE_PAD), (0, 0)))
        hist = jnp.pad(_group_cols(state_conv[0], d), ((0, 0), (SUBLANES - (CONV_W - 1), 0), (0, 0)))
        y, hnew = _ssd(xbc, 0, hist, conv_w_g, conv_b_g, dt_group_major(dtr.reshape(nbs * SAMPLE_CHUNK, nh)),
                       dtb_g, alog_g, dsk_g, state_ssm[0], nb=nbs, nc=1, cs=SAMPLE_CHUNK, d=d, valid_len=tdec)
        y = y.reshape(nbs, SAMPLE_CHUNK, d)[:, :SAMPLE_PAD].reshape(ms, d)
        return y, hnew

    def attn_sample(proj):
        def seg(name, w):
            return proj[:, lay[name]:lay[name] + w].reshape(nbs, SAMPLE_PAD, w)
        tail = lay["tail"]
        ki_new = proj[:, tail:tail + IDX_DIM].reshape(nbs, SAMPLE_PAD, IDX_DIM)
        wi = proj[:, tail + IDX_DIM + nh: tail + IDX_DIM + nh + IDX_HEADS].reshape(nbs, SAMPLE_PAD, IDX_HEADS)
        return _attn_sample(seg("q", d), seg("qi", IDX_HEADS * IDX_DIM), wi, ki_new, seg("k", kvw), seg("v", kvw),
                            cache_k, cache_v, cache_kidx, page_table, t_valid=tdec)

    y_s, proj_s, ssm_s = _layer(
        xs_pad, mod_s, pos_s, prm, nb=nbs, seq=SAMPLE_PAD, per_token=True, tm=ms,
        ssd_cfg=ssd_sample, attn_fn=attn_sample, peer_tm=ms)

    def states(proj, nb, rows, valid):
        p3 = proj.reshape(nb, rows, lay["total"])[:, :valid]
        k = p3[:, :, lay["k"]:lay["k"] + kvw].reshape(1, nb, valid, KV_HEADS, HEAD_DIM)
        v = p3[:, :, lay["v"]:lay["v"] + kvw].reshape(1, nb, valid, KV_HEADS, HEAD_DIM)
        ki = p3[:, :, lay["tail"]:lay["tail"] + IDX_DIM].reshape(1, nb, valid, IDX_DIM)
        xbc = _ungroup_cols(p3[:, valid - (CONV_W - 1):valid, lay["xbc"]:lay["xbc"] + xbcw], d)
        return k, v, ki, xbc.reshape(1, nb, CONV_W - 1, xbcw)

    k_p, v_p, ki_p, conv_p = states(proj_p, nbp, seq, seq)
    k_s, v_s, ki_s, conv_s = states(proj_s, nbs, SAMPLE_PAD, tdec)
    y_prompt = y_p.reshape(nbp, seq, d)
    y_sample = y_s.reshape(nbs, SAMPLE_PAD, d)[:, :tdec]
    return (y_prompt, y_sample, k_p, v_p, ki_p, ssm_p[None], conv_p,
            k_s, v_s, ki_s, ssm_s[None], conv_s)
```

```python
import functools

import jax
import jax.numpy as jnp
import numpy as np
from jax import lax
from jax.experimental import pallas as pl
from jax.experimental.pallas import tpu as pltpu

F32 = jnp.float32
BF16 = jnp.bfloat16
I32 = jnp.int32

SSD_HEADDIM = 64
SSD_GROUPS = 8
D_STATE = 128
CONV_W = 4
SSD_CHUNK = 256
HEAD_DIM = 128
KV_HEADS = 8
ROPE_THETA = 500000.0
IDX_HEADS = 32
IDX_DIM = 128
IDX_TOPK = 256
Q_BLOCK = 128
PEER_HEADS = 8
PEER_KEYS = 128
PEER_TOPK = 16
PEER_D_KEY = 256
PAGE_SIZE = 128
EPS = 1e-6

LANES = 128
SUBLANES = 8
VMEM_LIMIT_BYTES = 56 * 1024 * 1024

MXU_TILE = 256
PROJ_TN = MXU_TILE
PROJ_TM = 1024
ELEMWISE_TM = 256
CAST_ROWS = 512
PEER_ROUTE_TM = 256
PEER_TM = 1024
PEER_TE = 512
SAMPLE_PAD = 8
SAMPLE_CHUNK = 128
ATTN_TK = 512
ATTN_GROUPS_PER_STEP = 4
PAGES_PER_STEP = 8
MAX_PAGE_UNROLL = 5
NEG_BIG = -1e30
M_INIT = -1e29
INT_MIN = -2 ** 31


def _cparams(sem):
    return pltpu.CompilerParams(dimension_semantics=sem, vmem_limit_bytes=VMEM_LIMIT_BYTES)


def _nt(a, b):
    return lax.dot_general(a, b, (((1,), (1,)), ((), ())), preferred_element_type=F32)


def _tn(a, b):
    return lax.dot_general(a, b, (((0,), (0,)), ((), ())), preferred_element_type=F32)


def _rms(x, g):
    return x * lax.rsqrt(jnp.mean(x * x, axis=-1, keepdims=True) + EPS) * g


def _rope_tile(a, cos, sin):
    half = HEAD_DIM // 8
    lane = lax.broadcasted_iota(I32, a.shape, 1)
    sw = jnp.where(lane < half, pltpu.roll(a, HEAD_DIM - half, 1), pltpu.roll(a, half, 1))
    return a * cos + sw * sin


def _float_key(x):
    bits = pltpu.bitcast(x, I32)
    return jnp.where(bits < 0, bits ^ jnp.int32(0x7FFFFFFF), bits)


def _key_float(key):
    bits = jnp.where(key < 0, key ^ jnp.int32(0x7FFFFFFF), key)
    return pltpu.bitcast(bits, F32)


def _bisect_kth(count_ge, k, shape):
    imin = jnp.int32(INT_MIN)

    def body(i, ans):
        cand_u = ans | lax.shift_left(jnp.int32(1), jnp.int32(31) - i)
        return jnp.where(count_ge(cand_u ^ imin) >= k, cand_u, ans)

    ans = lax.fori_loop(0, 32, body, jnp.zeros(shape, I32))
    return ans ^ imin


def _cast_kernel(x_ref, o_ref):
    o_ref[...] = x_ref[...].astype(BF16)


def _cast_bf16(w):
    r, c = w.shape
    tr = min(r, CAST_ROWS)
    assert r % tr == 0
    return pl.pallas_call(
        _cast_kernel,
        grid=(r // tr,),
        in_specs=[pl.BlockSpec((tr, c), lambda i: (i, 0))],
        out_specs=pl.BlockSpec((tr, c), lambda i: (i, 0)),
        out_shape=jax.ShapeDtypeStruct((r, c), BF16),
        compiler_params=_cparams(("arbitrary",)),
        name="cast_bf16",
    )(w)


def _adaln_kernel(c_ref, w_ref, b_ref, o_ref):
    c = c_ref[...]
    a = (c * jax.nn.sigmoid(c)).astype(BF16)
    o_ref[...] = jnp.dot(a, w_ref[...].astype(BF16), preferred_element_type=F32) + b_ref[...]


def _adaln(c, w_mod, b_mod):
    n, d = c.shape
    npad = -(-n // SUBLANES) * SUBLANES
    cp = jnp.pad(c, ((0, npad - n), (0, 0)))
    nout = w_mod.shape[1]
    tn = PROJ_TN
    out = pl.pallas_call(
        _adaln_kernel,
        grid=(nout // tn,),
        in_specs=[pl.BlockSpec((npad, d), lambda j: (0, 0)),
                  pl.BlockSpec((d, tn), lambda j: (0, j)),
                  pl.BlockSpec((1, tn), lambda j: (0, j))],
        out_specs=pl.BlockSpec((npad, tn), lambda j: (0, j)),
        out_shape=jax.ShapeDtypeStruct((npad, nout), F32),
        compiler_params=_cparams(("arbitrary",)),
        name="adaln",
    )(cp, w_mod, b_mod.reshape(1, nout))
    return out[:n]


def _modulate_kernel(x_ref, g_ref, sh_ref, sc_ref, o_ref):
    o_ref[...] = (_rms(x_ref[...], g_ref[...]) * (1.0 + sc_ref[...]) + sh_ref[...]).astype(BF16)


def _modulate(x, g, shift, scale, *, tm, per_token, rows_per_batch):
    m, d = x.shape
    if per_token:
        mod_spec = pl.BlockSpec((tm, d), lambda i: (i, 0))
    else:
        tiles_per_batch = rows_per_batch // tm
        mod_spec = pl.BlockSpec((None, 1, d), lambda i: (i // tiles_per_batch, 0, 0))
    return pl.pallas_call(
        _modulate_kernel,
        grid=(m // tm,),
        in_specs=[pl.BlockSpec((tm, d), lambda i: (i, 0)),
                  pl.BlockSpec((1, d), lambda i: (0, 0)),
                  mod_spec, mod_spec],
        out_specs=pl.BlockSpec((tm, d), lambda i: (i, 0)),
        out_shape=jax.ShapeDtypeStruct((m, d), BF16),
        compiler_params=_cparams(("arbitrary",)),
        name="modulate",
    )(x, g.reshape(1, d), shift, scale)


def _proj_kernel(*refs, rope_ranges, tail_tile, w_transposed):
    if tail_tile is not None:
        h_ref, w_ref, cos_ref, sin_ref, kg_ref, o_ref = refs
    else:
        h_ref, w_ref, o_ref = refs
    j = pl.program_id(1)

    def product():
        if w_transposed:
            return _nt(h_ref[...], w_ref[...])
        return jnp.dot(h_ref[...], w_ref[...], preferred_element_type=F32)

    if tail_tile is None:
        o_ref[...] = product()
        return

    is_rope = functools.reduce(jnp.logical_or, [(j >= lo) & (j < hi) for lo, hi in rope_ranges])
    is_tail = j == tail_tile
    tn = o_ref.shape[1]

    @pl.when(jnp.logical_not(is_rope | is_tail))
    def _():
        o_ref[...] = product()

    @pl.when(is_rope)
    def _():
        acc = product()
        cos = cos_ref[...]
        sin = sin_ref[...]
        for t in range(tn // HEAD_DIM):
            sl = slice(t * HEAD_DIM, (t + 1) * HEAD_DIM)
            o_ref[:, sl] = _rope_tile(acc[:, sl], cos, sin)

    @pl.when(is_tail)
    def _():
        acc = product()
        ki = _rms(acc[:, :IDX_DIM], kg_ref[...])
        o_ref[:, :IDX_DIM] = _rope_tile(ki, cos_ref[...], sin_ref[...])
        o_ref[:, IDX_DIM:] = acc[:, IDX_DIM:]


def _proj(h, w, *, tm, rope=None, w_transposed=False):
    m, d = h.shape
    n = w.shape[0] if w_transposed else w.shape[1]
    tn = PROJ_TN
    assert m % tm == 0 and n % tn == 0
    w_spec = pl.BlockSpec((tn, d), lambda i, j: (j, 0)) if w_transposed else pl.BlockSpec((d, tn), lambda i, j: (0, j))
    in_specs = [pl.BlockSpec((tm, d), lambda i, j: (i, 0), pipeline_mode=pl.Buffered(1)), w_spec]
    args = [h, w]
    rope_ranges, tail_tile = (), None
    if rope is not None:
        cos, sin, kidx_g, rope_ranges, tail_tile = rope
        in_specs += [pl.BlockSpec((tm, HEAD_DIM), lambda i, j: (i, 0)),
                     pl.BlockSpec((tm, HEAD_DIM), lambda i, j: (i, 0)),
                     pl.BlockSpec((1, IDX_DIM), lambda i, j: (0, 0))]
        args += [cos, sin, kidx_g.reshape(1, IDX_DIM)]
    return pl.pallas_call(
        functools.partial(_proj_kernel, rope_ranges=rope_ranges, tail_tile=tail_tile, w_transposed=w_transposed),
        grid=(m // tm, n // tn),
        in_specs=in_specs,
        out_specs=pl.BlockSpec((tm, tn), lambda i, j: (i, j)),
        out_shape=jax.ShapeDtypeStruct((m, n), F32),
        compiler_params=_cparams(("arbitrary", "arbitrary")),
        name="proj",
    )(*args)


def _ssd_kernel(x_ref, prev_ref, hist_ref, cw_ref, cb_ref, dt_ref, dtb_ref, alog_ref, dsk_ref, h0_ref,
                y_ref, hout_ref, h_scr, *, cs, rpg, valid_len, nc):
    c = pl.program_id(2)
    xs_w = rpg * SSD_HEADDIM

    @pl.when(c == 0)
    def _():
        h_scr[...] = h0_ref[...]

    prev = jnp.where(c == 0, hist_ref[...], prev_ref[...])
    cat = jnp.concatenate([prev, x_ref[...]], axis=0)
    w = cw_ref[...]
    acc = cb_ref[...]
    for j in range(CONV_W):
        lo = SUBLANES - (CONV_W - 1) + j
        acc = acc + cat[lo:lo + cs] * w[j:j + 1]
    xc = acc * jax.nn.sigmoid(acc)
    xs = xc[:, :xs_w]
    bm = xc[:, xs_w:xs_w + D_STATE]
    cm = xc[:, xs_w + D_STATE:]
    bm_b = bm.astype(BF16)
    cm_b = cm.astype(BF16)

    z = dt_ref[...] + dtb_ref[...]
    dt = jnp.maximum(z, 0.0) + jnp.log1p(jnp.exp(-jnp.abs(z)))
    if valid_len < cs:
        row = lax.broadcasted_iota(I32, dt.shape, 0)
        dt = jnp.where(row < valid_len, dt, 0.0)
    a_neg = -jnp.exp(alog_ref[...])
    ii = lax.broadcasted_iota(I32, (cs, cs), 0)
    jj = lax.broadcasted_iota(I32, (cs, cs), 1)
    causal = ii >= jj
    acs = jnp.dot(causal.astype(F32), dt * a_neg, preferred_element_type=F32,
                  precision=lax.Precision.HIGHEST)
    acs_t = acs.T
    a_last = acs[cs - 1:cs, :]
    cb = _nt(cm_b, bm_b)
    dsk = dsk_ref[...]

    first = lax.broadcasted_iota(I32, (cs, LANES), 1) < SSD_HEADDIM
    first_row = lax.broadcasted_iota(I32, (LANES, D_STATE), 0) < SSD_HEADDIM
    ys = []
    for pr in range(rpg // 2):
        r0, r1 = 2 * pr, 2 * pr + 1
        a0, a1 = acs[:, r0:r0 + 1], acs[:, r1:r1 + 1]
        xp = xs[:, pr * LANES:(pr + 1) * LANES]
        xd = xp * jnp.where(first, dt[:, r0:r0 + 1], dt[:, r1:r1 + 1])
        y = jnp.zeros((cs, LANES), F32)
        for r, keep in ((r0, first), (r1, jnp.logical_not(first))):
            lm = jnp.exp(jnp.where(causal, acs[:, r:r + 1] - acs_t[r:r + 1, :], -jnp.inf))
            y = y + jnp.dot((cb * lm).astype(BF16), jnp.where(keep, xd, 0.0).astype(BF16),
                            preferred_element_type=F32)
        hp = h_scr[pr]
        y = y + jnp.where(first, jnp.exp(a0), jnp.exp(a1)) * _nt(cm_b, hp.astype(BF16))
        al0, al1 = a_last[:, r0:r0 + 1], a_last[:, r1:r1 + 1]
        decay = jnp.where(first, jnp.exp(al0 - a0), jnp.exp(al1 - a1))
        h_scr[pr] = (hp * jnp.where(first_row, jnp.exp(al0), jnp.exp(al1))
                     + _tn((xd * decay).astype(BF16), bm_b))
        ys.append(y + xp * jnp.where(first[0:1, :], dsk[:, r0:r0 + 1], dsk[:, r1:r1 + 1]))
    y_ref[...] = jnp.concatenate(ys, axis=1)

    @pl.when(c == nc - 1)
    def _():
        hout_ref[...] = h_scr[...]


def _ssd(xg, xoff_blocks, hist8, conv_w_g, conv_b_g, dt_g, dtb_g, alog_g, dsk_g, h0, *, nb, nc, cs, d, valid_len):
    g_cnt = SSD_GROUPS
    rpg = d // SSD_HEADDIM // g_cnt
    gw = rpg * SSD_HEADDIM + 2 * D_STATE
    m = nb * nc * cs
    cpb = cs // SUBLANES
    nh = d // SSD_HEADDIM
    npair = rpg // 2
    assert rpg % 2 == 0 and 2 * SSD_HEADDIM == LANES
    kern = functools.partial(_ssd_kernel, cs=cs, rpg=rpg, valid_len=valid_len, nc=nc)
    y, hout = pl.pallas_call(
        kern,
        grid=(nb, g_cnt, nc),
        in_specs=[
            pl.BlockSpec((cs, gw), lambda b, g, c: (b * nc + c, xoff_blocks + g)),
            pl.BlockSpec((SUBLANES, gw), lambda b, g, c: (jnp.maximum((b * nc + c) * cpb - 1, 0), xoff_blocks + g)),
            pl.BlockSpec((None, SUBLANES, gw), lambda b, g, c: (b, 0, g)),
            pl.BlockSpec((CONV_W, gw), lambda b, g, c: (0, g)),
            pl.BlockSpec((1, gw), lambda b, g, c: (0, g)),
            pl.BlockSpec((None, cs, LANES), lambda b, g, c: (g, b * nc + c, 0)),
            pl.BlockSpec((None, 1, LANES), lambda b, g, c: (g, 0, 0)),
            pl.BlockSpec((None, 1, LANES), lambda b, g, c: (g, 0, 0)),
            pl.BlockSpec((None, 1, LANES), lambda b, g, c: (g, 0, 0)),
            pl.BlockSpec((None, npair, LANES, D_STATE), lambda b, g, c: (b, g, 0, 0)),
        ],
        out_specs=[
            pl.BlockSpec((cs, rpg * SSD_HEADDIM), lambda b, g, c: (b * nc + c, g)),
            pl.BlockSpec((None, npair, LANES, D_STATE), lambda b, g, c: (b, g, 0, 0)),
        ],
        out_shape=[jax.ShapeDtypeStruct((m, d), F32),
                   jax.ShapeDtypeStruct((nb, nh // 2, LANES, D_STATE), F32)],
        scratch_shapes=[pltpu.VMEM((npair, LANES, D_STATE), F32)],
        compiler_params=_cparams(("arbitrary", "arbitrary", "arbitrary")),
        name="ssd_scan",
    )(xg, xg, hist8, conv_w_g, conv_b_g, dt_g, dtb_g, alog_g, dsk_g, h0.reshape(nb, nh // 2, LANES, D_STATE))
    return y, hout.reshape(nb, nh, SSD_HEADDIM, D_STATE)


def _gate_norm_kernel(y_ref, z_ref, g_ref, o_ref):
    z = z_ref[...]
    o_ref[...] = _rms(y_ref[...] * (z * jax.nn.sigmoid(z)), g_ref[...]).astype(BF16)


def _gate_norm(y, proj, g, *, tm):
    m, d = y.shape
    return pl.pallas_call(
        _gate_norm_kernel,
        grid=(m // tm,),
        in_specs=[pl.BlockSpec((tm, d), lambda i: (i, 0)),
                  pl.BlockSpec((tm, d), lambda i: (i, 0)),
                  pl.BlockSpec((1, d), lambda i: (0, 0))],
        out_specs=pl.BlockSpec((tm, d), lambda i: (i, 0)),
        out_shape=jax.ShapeDtypeStruct((m, d), BF16),
        compiler_params=_cparams(("arbitrary",)),
        name="gate_norm",
    )(y, proj, g.reshape(1, d))


def _attn_prompt_kernel(qi_ref, tail_ref, ki_ref, q_ref, k_ref, v_ref, o_ref, key_scr, thr_scr,
                        *, topk, tk, n_ssd_heads, rq, gps):
    qb = pl.program_id(1)
    g = pl.program_id(2)
    nkt = (qb * Q_BLOCK + Q_BLOCK + tk - 1) // tk
    imin = jnp.int32(INT_MIN)

    @pl.when(g == 0)
    def _():
        qi = qi_ref[...]
        qis = jnp.concatenate([qi[:, h * IDX_DIM:(h + 1) * IDX_DIM] for h in range(IDX_HEADS)],
                              axis=0).astype(BF16)
        w_t = tail_ref[...].T[n_ssd_heads:n_ssd_heads + IDX_HEADS, :] * (IDX_HEADS ** -0.5 * IDX_DIM ** -0.5)
        tpos = qb * Q_BLOCK + lax.broadcasted_iota(I32, (tk, Q_BLOCK), 1)
        kiota = lax.broadcasted_iota(I32, (tk, Q_BLOCK), 0)

        def score_tile(kt, carry):
            rows = pl.ds(pl.multiple_of(kt * tk, tk), tk)
            x = _nt(ki_ref[rows, :].astype(BF16), qis)
            sc = jnp.zeros((tk, Q_BLOCK), F32)
            for h in range(IDX_HEADS):
                sc = sc + jnp.maximum(x[:, h * Q_BLOCK:(h + 1) * Q_BLOCK], 0.0) * w_t[h:h + 1, :]
            key = _float_key(sc + 0.0)
            key_scr[rows, :] = jnp.where(kt * tk + kiota <= tpos, key, imin)
            return carry

        lax.fori_loop(0, nkt, score_tile, 0)

        def count_ge(thr):
            def body(kt, cnt):
                rows = pl.ds(pl.multiple_of(kt * tk, tk), tk)
                ge = (key_scr[rows, :] >= thr).astype(I32)
                return cnt + jnp.sum(ge.reshape(tk // SUBLANES, SUBLANES, Q_BLOCK), axis=0)
            cnt = lax.fori_loop(0, nkt, body, jnp.zeros((SUBLANES, Q_BLOCK), I32))
            return jnp.sum(cnt, axis=0, keepdims=True)

        thr = jnp.maximum(_bisect_kth(count_ge, topk, (1, Q_BLOCK)), jnp.int32(INT_MIN + 1))
        thr_scr[...] = jnp.broadcast_to(thr, (SUBLANES, Q_BLOCK))

    q = q_ref[...]
    nq = rq * Q_BLOCK
    qs = []
    for u in range(gps):
        heads = [q[:, (u * rq + r) * HEAD_DIM:(u * rq + r + 1) * HEAD_DIM] for r in range(rq)]
        qs.append((jnp.concatenate(heads, axis=0) * (HEAD_DIM ** -0.5)).astype(BF16))
    thr_q = jnp.concatenate([thr_scr[0:1, :]] * rq, axis=1)

    def attend(kt, carry):
        rows = pl.ds(pl.multiple_of(kt * tk, tk), tk)
        sel = jnp.concatenate([key_scr[rows, :]] * rq, axis=1) >= thr_q
        k_all = k_ref[rows, :]
        v_all = v_ref[rows, :]
        s_ts = [_nt(k_all[:, u * HEAD_DIM:(u + 1) * HEAD_DIM].astype(BF16), qs[u]) for u in range(gps)]
        v_ts = [v_all[:, u * HEAD_DIM:(u + 1) * HEAD_DIM].T.astype(BF16) for u in range(gps)]
        out = []
        for u in range(gps):
            m, l, acc = carry[u]
            s_t = jnp.where(sel, s_ts[u], NEG_BIG)
            m_new = jnp.maximum(m, jnp.max(s_t, axis=0, keepdims=True))
            alpha = jnp.exp(m - m_new)
            p = jnp.exp(s_t - m_new)
            l = l * alpha + jnp.sum(p, axis=0, keepdims=True)
            acc = acc * alpha + jnp.dot(v_ts[u], p.astype(BF16), preferred_element_type=F32)
            out.append((m_new, l, acc))
        return tuple(out)

    init = tuple((jnp.full((1, nq), M_INIT, F32), jnp.zeros((1, nq), F32), jnp.zeros((HEAD_DIM, nq), F32))
                 for _ in range(gps))
    res = lax.fori_loop(0, nkt, attend, init)
    for u in range(gps):
        _, l, acc = res[u]
        o = (acc / l).T
        for r in range(rq):
            c0 = (u * rq + r) * HEAD_DIM
            o_ref[:, c0:c0 + HEAD_DIM] = o[r * Q_BLOCK:(r + 1) * Q_BLOCK, :]


def _attn_prompt(proj, lay, *, nb, seq, d):
    rq = d // HEAD_DIM // KV_HEADS
    nqb = seq // Q_BLOCK
    topk = min(IDX_TOPK, seq // 4)
    tk = min(ATTN_TK, seq)
    qiw = IDX_HEADS * IDX_DIM
    gps = min(ATTN_GROUPS_PER_STEP, KV_HEADS)
    qw = gps * rq * HEAD_DIM
    kw = gps * HEAD_DIM
    assert KV_HEADS % gps == 0 and lay["q"] % qw == 0 and lay["k"] % kw == 0 and lay["v"] % kw == 0
    kern = functools.partial(_attn_prompt_kernel, topk=topk, tk=tk, n_ssd_heads=d // SSD_HEADDIM, rq=rq, gps=gps)
    return pl.pallas_call(
        kern,
        grid=(nb, nqb, KV_HEADS // gps),
        in_specs=[
            pl.BlockSpec((Q_BLOCK, qiw), lambda b, i, g: (b * nqb + i, lay["qi"] // qiw)),
            pl.BlockSpec((Q_BLOCK, LANES), lambda b, i, g: (b * nqb + i, lay["tail"] // LANES + 1)),
            pl.BlockSpec((seq, IDX_DIM), lambda b, i, g: (b, lay["tail"] // IDX_DIM)),
            pl.BlockSpec((Q_BLOCK, qw), lambda b, i, g: (b * nqb + i, lay["q"] // qw + g)),
            pl.BlockSpec((seq, kw), lambda b, i, g: (b, lay["k"] // kw + g)),
            pl.BlockSpec((seq, kw), lambda b, i, g: (b, lay["v"] // kw + g)),
        ],
        out_specs=pl.BlockSpec((Q_BLOCK, qw), lambda b, i, g: (b * nqb + i, g)),
        out_shape=jax.ShapeDtypeStruct((nb * seq, d), F32),
        scratch_shapes=[pltpu.VMEM((seq, Q_BLOCK), I32), pltpu.VMEM((SUBLANES, Q_BLOCK), I32)],
        compiler_params=_cparams(("arbitrary", "arbitrary", "arbitrary")),
        name="attn_prompt",
    )(proj, proj, proj, proj, proj, proj)


def _head_rows(page_ref, head):
    return page_ref[pl.ds(head, PAGE_SIZE, stride=KV_HEADS), :]


def _fold_lane_groups(x):
    sh = LANES // 2
    while sh >= SAMPLE_PAD:
        x = x + pltpu.roll(x, sh, 1)
        sh //= 2
    return x


def _sample_scores_kernel(pt_ref, *refs, npages, topk, pps):
    kip_refs = refs[:pps]
    kp_refs = refs[pps:2 * pps]
    qi_ref, w_ref, qbd_ref, kinew_ref, knew_ref, p_ref, key_scr, kc_scr, s_scr, qbd_scr = refs[2 * pps:]
    j = pl.program_id(1)
    imin = jnp.int32(INT_MIN)
    nip = qi_ref.shape[0]
    nlp = qbd_ref.shape[0]
    groups_per_slab = LANES // SAMPLE_PAD
    lane = lax.broadcasted_iota(I32, (PAGE_SIZE, LANES), 1)
    lane_group = lane // SAMPLE_PAD

    @pl.when(j == 0)
    def _():
        qbd_scr[...] = (qbd_ref[...] * (HEAD_DIM ** -0.5)).astype(BF16)
        kc_scr[...] = jnp.full(kc_scr.shape, imin, I32)

    qi = qi_ref[...].astype(BF16)
    w_row = w_ref[...]

    def index_keys(kidx):
        x = _nt(kidx.astype(BF16), qi)
        r = jnp.maximum(x, 0.0) * w_row
        acc = r[:, :LANES]
        for c in range(1, nip // LANES):
            acc = acc + r[:, c * LANES:(c + 1) * LANES]
        return _float_key(_fold_lane_groups(acc) + 0.0)

    def put_page(page, keys, scores):
        rows = pl.ds(pl.multiple_of(page * PAGE_SIZE, PAGE_SIZE), PAGE_SIZE)
        key_scr[rows, :] = keys
        s_scr[rows, :] = scores
        slab = page // groups_per_slab
        kc_scr[slab] = jnp.where(lane_group == page % groups_per_slab, keys, kc_scr[slab])

    kidx_cat = jnp.concatenate([r[...] for r in kip_refs], axis=0)
    keys = index_keys(kidx_cat)
    k2d = jnp.concatenate(
        [jnp.concatenate([_head_rows(kp, h).astype(BF16) for h in range(KV_HEADS)], axis=1)
         for kp in kp_refs], axis=0)
    scores = _nt(k2d, qbd_scr[...])
    for i in range(pps):
        sl = slice(i * PAGE_SIZE, (i + 1) * PAGE_SIZE)
        put_page(j * pps + i, keys[sl], scores[sl])

    @pl.when(j == 0)
    def _():
        knew = index_keys(kinew_ref[...])
        srow = lax.broadcasted_iota(I32, (PAGE_SIZE, LANES), 0)
        knew = jnp.where(srow <= lane % SAMPLE_PAD, knew, imin)
        put_page(jnp.int32(npages), knew, _nt(knew_ref[...].astype(BF16), qbd_scr[...]))

    @pl.when(j == pl.num_programs(1) - 1)
    def _():
        kc = kc_scr[...]

        tok = lane[0:1, :] % SAMPLE_PAD

        def count_ge(thr):
            ge = (kc >= thr).astype(F32)
            cnt = jnp.sum(jnp.sum(ge.reshape(-1, SUBLANES, LANES), axis=0), axis=0, keepdims=True)
            tot = jnp.zeros((1, LANES), F32)
            for t in range(SAMPLE_PAD):
                tot = jnp.where(tok == t, jnp.sum(jnp.where(tok == t, cnt, 0.0), axis=1, keepdims=True), tot)
            return tot

        thr = _bisect_kth(count_ge, topk, (1, LANES))
        reps = nlp // LANES
        thr_q = jnp.concatenate([thr] * reps, axis=1)

        def masked(page):
            rows = pl.ds(pl.multiple_of(page * PAGE_SIZE, PAGE_SIZE), PAGE_SIZE)
            key = jnp.concatenate([key_scr[rows, :]] * reps, axis=1)
            return rows, (key >= thr_q) & (key != imin)

        def max_body(page, m):
            rows, sel = masked(page)
            return jnp.maximum(m, jnp.max(jnp.where(sel, s_scr[rows, :], NEG_BIG), axis=0, keepdims=True))

        unroll = next(u for u in range(MAX_PAGE_UNROLL, 0, -1) if (npages + 1) % u == 0)
        m = lax.fori_loop(0, npages + 1, max_body, jnp.full((1, nlp), NEG_BIG, F32), unroll=unroll)

        def exp_body(page, l):
            rows, sel = masked(page)
            e = jnp.where(sel, jnp.exp(s_scr[rows, :] - m), 0.0)
            s_scr[rows, :] = e
            return l + jnp.sum(e, axis=0, keepdims=True)

        l = lax.fori_loop(0, npages + 1, exp_body, jnp.zeros((1, nlp), F32), unroll=unroll)

        def out_body(page, carry):
            rows = pl.ds(pl.multiple_of(page * PAGE_SIZE, PAGE_SIZE), PAGE_SIZE)
            p_ref[page] = (s_scr[rows, :] / l).T.astype(BF16)
            return carry

        lax.fori_loop(0, npages + 1, out_body, 0, unroll=unroll)


def _sample_pv_kernel(pt_ref, *refs, pps, rq):
    vp_refs = refs[:pps]
    p_ref, pnew_ref, vnew_ref, o_ref, acc_scr = refs[pps:]
    j = pl.program_id(1)
    nrow = rq * SAMPLE_PAD

    @pl.when(j == 0)
    def _():
        pn = pnew_ref[0]
        vn = vnew_ref[...].astype(BF16)
        for g in range(KV_HEADS):
            acc_scr[g] = jnp.dot(pn[g * nrow:(g + 1) * nrow, :], vn[:, g * HEAD_DIM:(g + 1) * HEAD_DIM],
                                 preferred_element_type=F32)

    pcat = jnp.concatenate([p_ref[i] for i in range(pps)], axis=1)
    for g in range(KV_HEADS):
        vg = jnp.concatenate([_head_rows(vp, g).astype(BF16) for vp in vp_refs], axis=0)
        acc_scr[g] += jnp.dot(pcat[g * nrow:(g + 1) * nrow, :], vg, preferred_element_type=F32)

    @pl.when(j == pl.num_programs(1) - 1)
    def _():
        o_ref[...] = acc_scr[...]


def _attn_sample(q, qi, wi, ki_new, k_new, v_new, cache_k, cache_v, cache_kidx, page_table, *, t_valid):
    nb, npages = page_table.shape
    pps = min(PAGES_PER_STEP, npages)
    assert npages % pps == 0
    nsteps = npages // pps
    past = npages * PAGE_SIZE
    topk = min(IDX_TOPK, (past + t_valid) // 4)
    d = q.shape[-1]
    rq = d // HEAD_DIM // KV_HEADS
    kvw = KV_HEADS * HEAD_DIM
    nrow = rq * SAMPLE_PAD
    ck = cache_k.reshape(-1, PAGE_SIZE * KV_HEADS, HEAD_DIM)
    cv = cache_v.reshape(-1, PAGE_SIZE * KV_HEADS, HEAD_DIM)
    cki = cache_kidx.reshape(-1, PAGE_SIZE, IDX_DIM)

    def pad_to(a, axis, mult):
        n = a.shape[axis]
        widths = [(0, 0)] * a.ndim
        widths[axis] = (0, -(-n // mult) * mult - n)
        return jnp.pad(a, widths)

    qi_s = pad_to(qi.reshape(nb, SAMPLE_PAD, IDX_HEADS, IDX_DIM).transpose(0, 2, 1, 3)
                  .reshape(nb, IDX_HEADS * SAMPLE_PAD, IDX_DIM), 1, LANES)
    w_row = pad_to((wi * (IDX_HEADS ** -0.5 * IDX_DIM ** -0.5)).transpose(0, 2, 1)
                   .reshape(nb, 1, IDX_HEADS * SAMPLE_PAD), 2, LANES)
    nip = qi_s.shape[1]
    q_g = q.reshape(nb, SAMPLE_PAD, KV_HEADS, rq, HEAD_DIM).transpose(0, 2, 3, 1, 4).reshape(
        nb, KV_HEADS, nrow, HEAD_DIM)
    qbd = pad_to(jnp.einsum("bgrd,gh->bgrhd", q_g, jnp.eye(KV_HEADS, dtype=q.dtype))
                 .reshape(nb, KV_HEADS * nrow, kvw), 1, LANES)
    nlp = qbd.shape[1]
    pad_rows = ((0, 0), (0, PAGE_SIZE - SAMPLE_PAD), (0, 0))
    kinew_p = jnp.pad(ki_new, pad_rows)
    knew_p = jnp.pad(k_new, pad_rows)
    vnew_p = jnp.pad(v_new, pad_rows)
    npg = npages + 1
    nslab = -(-npg // (LANES // SAMPLE_PAD))

    def kidx_spec(i):
        return pl.BlockSpec((None, PAGE_SIZE, IDX_DIM), lambda b, j, pt, i=i: (pt[b, j * pps + i], 0, 0))

    def kv_specs():
        return [pl.BlockSpec((None, PAGE_SIZE * KV_HEADS, HEAD_DIM), lambda b, j, pt, i=i: (pt[b, j * pps + i], 0, 0))
                for i in range(pps)]

    probs = pl.pallas_call(
        functools.partial(_sample_scores_kernel, npages=npages, topk=topk, pps=pps),
        grid_spec=pltpu.PrefetchScalarGridSpec(
            num_scalar_prefetch=1,
            grid=(nb, nsteps),
            in_specs=[kidx_spec(i) for i in range(pps)] + kv_specs() + [
                pl.BlockSpec((None, nip, IDX_DIM), lambda b, j, pt: (b, 0, 0)),
                pl.BlockSpec((None, 1, nip), lambda b, j, pt: (b, 0, 0)),
                pl.BlockSpec((None, nlp, kvw), lambda b, j, pt: (b, 0, 0)),
                pl.BlockSpec((None, PAGE_SIZE, IDX_DIM), lambda b, j, pt: (b, 0, 0)),
                pl.BlockSpec((None, PAGE_SIZE, kvw), lambda b, j, pt: (b, 0, 0)),
            ],
            out_specs=pl.BlockSpec((None, npg, nlp, PAGE_SIZE), lambda b, j, pt: (b, 0, 0, 0)),
            scratch_shapes=[pltpu.VMEM((npg * PAGE_SIZE, LANES), I32),
                            pltpu.VMEM((nslab, PAGE_SIZE, LANES), I32),
                            pltpu.VMEM((npg * PAGE_SIZE, nlp), F32),
                            pltpu.VMEM((nlp, kvw), BF16)],
        ),
        out_shape=jax.ShapeDtypeStruct((nb, npg, nlp, PAGE_SIZE), BF16),
        compiler_params=_cparams(("arbitrary", "arbitrary")),
        name="sample_scores",
    )(page_table, *([cki] * pps), *([ck] * pps), qi_s, w_row, qbd, kinew_p, knew_p)

    o = pl.pallas_call(
        functools.partial(_sample_pv_kernel, pps=pps, rq=rq),
        grid_spec=pltpu.PrefetchScalarGridSpec(
            num_scalar_prefetch=1,
            grid=(nb, nsteps),
            in_specs=kv_specs() + [
                pl.BlockSpec((None, pps, nlp, PAGE_SIZE), lambda b, j, pt: (b, j, 0, 0)),
                pl.BlockSpec((None, 1, nlp, PAGE_SIZE), lambda b, j, pt: (b, npages, 0, 0)),
                pl.BlockSpec((None, PAGE_SIZE, kvw), lambda b, j, pt: (b, 0, 0)),
            ],
            out_specs=pl.BlockSpec((None, KV_HEADS, nrow, HEAD_DIM), lambda b, j, pt: (b, 0, 0, 0)),
            scratch_shapes=[pltpu.VMEM((KV_HEADS, nrow, HEAD_DIM), F32)],
        ),
        out_shape=jax.ShapeDtypeStruct((nb, KV_HEADS, nrow, HEAD_DIM), F32),
        compiler_params=_cparams(("arbitrary", "arbitrary")),
        name="sample_pv",
    )(page_table, *([cv] * pps), probs, probs, vnew_p)
    return o.reshape(nb, KV_HEADS, rq, SAMPLE_PAD, HEAD_DIM).transpose(0, 3, 1, 2, 4).reshape(nb * SAMPLE_PAD, d)


def _rms_cast_kernel(x_ref, g_ref, o_ref):
    o_ref[...] = _rms(x_ref[...], g_ref[...]).astype(BF16)


def _rms_cast(x, g, *, tm):
    m, d = x.shape
    return pl.pallas_call(
        _rms_cast_kernel,
        grid=(m // tm,),
        in_specs=[pl.BlockSpec((tm, d), lambda i: (i, 0)), pl.BlockSpec((1, d), lambda i: (0, 0))],
        out_specs=pl.BlockSpec((tm, d), lambda i: (i, 0)),
        out_shape=jax.ShapeDtypeStruct((m, d), BF16),
        compiler_params=_cparams(("arbitrary",)),
        name="rms_cast",
    )(x, g.reshape(1, d))


def _outproj_kernel(y_ref, a_ref, w1_ref, w2_ref, x_ref, gate_ref, o_ref):
    acc = jnp.dot(y_ref[...], w1_ref[...], preferred_element_type=F32)
    acc = acc + jnp.dot(a_ref[...], w2_ref[...], preferred_element_type=F32)
    o_ref[...] = x_ref[...] + gate_ref[...] * acc


def _outproj(y_n, attn_n, w, x, gate, *, tm, per_token, rows_per_batch):
    m, d = x.shape
    tn = PROJ_TN
    if per_token:
        gate_spec = pl.BlockSpec((tm, tn), lambda i, j: (i, j))
    else:
        tpb = rows_per_batch // tm
        gate_spec = pl.BlockSpec((None, 1, tn), lambda i, j: (i // tpb, 0, j))
    return pl.pallas_call(
        _outproj_kernel,
        grid=(m // tm, d // tn),
        in_specs=[pl.BlockSpec((tm, d), lambda i, j: (i, 0), pipeline_mode=pl.Buffered(1)),
                  pl.BlockSpec((tm, d), lambda i, j: (i, 0), pipeline_mode=pl.Buffered(1)),
                  pl.BlockSpec((d, tn), lambda i, j: (0, j)),
                  pl.BlockSpec((d, tn), lambda i, j: (1, j)),
                  pl.BlockSpec((tm, tn), lambda i, j: (i, j)),
                  gate_spec],
        out_specs=pl.BlockSpec((tm, tn), lambda i, j: (i, j)),
        out_shape=jax.ShapeDtypeStruct((m, d), F32),
        compiler_params=_cparams(("arbitrary", "arbitrary")),
        name="outproj",
    )(y_n, attn_n, w, w, x, gate)


def _peer_route_kernel(q_ref, k1_ref, k2_ref, s1_ref, s2_ref, tau_ref, cc_ref):
    q = q_ref[...]
    k1 = k1_ref[...].astype(BF16)
    k2 = k2_ref[...].astype(BF16)
    half = PEER_D_KEY // 2
    tm = q.shape[0]
    taus, ccs = [], []
    for h in range(PEER_HEADS):
        base = h * PEER_D_KEY
        s1 = _nt(k1, q[:, base:base + half].astype(BF16))
        s2 = _nt(k2, q[:, base + half:base + PEER_D_KEY].astype(BF16))
        s1_ref[h * PEER_KEYS:(h + 1) * PEER_KEYS, :] = s1
        s2_ref[h * PEER_KEYS:(h + 1) * PEER_KEYS, :] = s2

        def top_vals(x):
            vals = []
            for _ in range(PEER_TOPK):
                m = jnp.max(x, axis=0, keepdims=True)
                vals.append(m)
                x = jnp.where(x == m, -jnp.inf, x)
            return vals

        v1 = top_vals(s1)
        v2 = top_vals(s2)
        v1a = jnp.concatenate(v1, axis=0)
        v2a = jnp.concatenate(v2, axis=0)
        half_k = PEER_TOPK // 2
        cand = jnp.concatenate([v + v2a[:half_k] for v in v1[:half_k]]
                               + [v1[0] + v2a[half_k:], v1a[half_k:] + v2[0]], axis=0) + 0.0
        ckey = _float_key(cand)

        def count_ge(thr, ckey=ckey):
            return jnp.sum((ckey >= thr).astype(I32), axis=0, keepdims=True)

        tau = _key_float(_bisect_kth(count_ge, PEER_TOPK, (1, tm)))
        cmax = cand[0:1, :]
        zsum = jnp.sum(jnp.where(cand >= tau, jnp.exp(cand - cmax), 0.0), axis=0, keepdims=True)
        taus.append(tau)
        ccs.append(cmax + jnp.log(zsum))
    tau_ref[...] = jnp.concatenate(taus, axis=0)
    cc_ref[...] = jnp.concatenate(ccs, axis=0)


def _peer_route(q, k1, k2, *, tm):
    t = q.shape[0]
    rows = PEER_HEADS * PEER_KEYS
    half = PEER_D_KEY // 2
    return pl.pallas_call(
        _peer_route_kernel,
        grid=(t // tm,),
        in_specs=[pl.BlockSpec((tm, PEER_HEADS * PEER_D_KEY), lambda i: (i, 0)),
                  pl.BlockSpec((PEER_KEYS, half), lambda i: (0, 0)),
                  pl.BlockSpec((PEER_KEYS, half), lambda i: (0, 0))],
        out_specs=[pl.BlockSpec((rows, tm), lambda i: (0, i)),
                   pl.BlockSpec((rows, tm), lambda i: (0, i)),
                   pl.BlockSpec((PEER_HEADS, tm), lambda i: (0, i)),
                   pl.BlockSpec((PEER_HEADS, tm), lambda i: (0, i))],
        out_shape=[jax.ShapeDtypeStruct((rows, t), F32), jax.ShapeDtypeStruct((rows, t), F32),
                   jax.ShapeDtypeStruct((PEER_HEADS, t), F32), jax.ShapeDtypeStruct((PEER_HEADS, t), F32)],
        compiler_params=_cparams(("arbitrary",)),
        name="peer_route",
    )(q, k1, k2)


def _peer_dense_kernel(xb_ref, s1_ref, s2_ref, tau_ref, cc_ref, u_ref, v_ref, o_ref, *, te):
    e = pl.program_id(1)

    @pl.when(e == 0)
    def _():
        o_ref[...] = jnp.zeros(o_ref.shape, F32)

    tm = xb_ref.shape[0]
    nsub = MXU_TILE // PEER_KEYS
    coefs = []
    for c in range(te // MXU_TILE):
        ut = _nt(xb_ref[...], u_ref[c * MXU_TILE:(c + 1) * MXU_TILE, :]).T
        for i in range(nsub):
            k = c * nsub + i
            gate = jnp.zeros((PEER_KEYS, tm), F32)
            for h in range(PEER_HEADS):
                sm = s1_ref[k, h:h + 1, :] + s2_ref[h * PEER_KEYS:(h + 1) * PEER_KEYS, :]
                gate = gate + jnp.where(sm >= tau_ref[h:h + 1, :], jnp.exp(sm - cc_ref[h:h + 1, :]), 0.0)
            coefs.append(gate * jax.nn.gelu(ut[i * PEER_KEYS:(i + 1) * PEER_KEYS, :]))
    coef = jnp.concatenate(coefs, axis=0).T.astype(BF16)
    for n in range(o_ref.shape[1] // MXU_TILE):
        cols = slice(n * MXU_TILE, (n + 1) * MXU_TILE)
        o_ref[:, cols] += jnp.dot(coef, v_ref[:, cols], preferred_element_type=F32)


def _peer_dense(xb, s1, s2, tau, cc, u_b, v_b, *, tm):
    t, d = xb.shape
    ne = u_b.shape[0]
    te = min(PEER_TE, ne)
    rows = PEER_HEADS * PEER_KEYS
    once = pl.Buffered(1)
    s1 = s1.reshape(PEER_HEADS, PEER_KEYS, t).transpose(1, 0, 2)
    return pl.pallas_call(
        functools.partial(_peer_dense_kernel, te=te),
        grid=(t // tm, ne // te),
        in_specs=[pl.BlockSpec((tm, d), lambda i, e: (i, 0), pipeline_mode=once),
                  pl.BlockSpec((te // PEER_KEYS, PEER_HEADS, tm), lambda i, e: (e, 0, i)),
                  pl.BlockSpec((rows, tm), lambda i, e: (0, i), pipeline_mode=once),
                  pl.BlockSpec((PEER_HEADS, tm), lambda i, e: (0, i), pipeline_mode=once),
                  pl.BlockSpec((PEER_HEADS, tm), lambda i, e: (0, i), pipeline_mode=once),
                  pl.BlockSpec((te, d), lambda i, e: (e, 0)),
                  pl.BlockSpec((te, d), lambda i, e: (e, 0))],
        out_specs=pl.BlockSpec((tm, d), lambda i, e: (i, 0), pipeline_mode=once),
        out_shape=jax.ShapeDtypeStruct((t, d), F32),
        compiler_params=_cparams(("arbitrary", "arbitrary")),
        name="peer_dense",
    )(xb, s1, s2, tau, cc, u_b, v_b)


def _final_kernel(x_ref, f_ref, gate_ref, g_ref, o_ref):
    o_ref[...] = _rms(x_ref[...] + gate_ref[...] * f_ref[...], g_ref[...])


def _final(x, f, gate, g, *, tm, per_token, rows_per_batch):
    m, d = x.shape
    if per_token:
        gate_spec = pl.BlockSpec((tm, d), lambda i: (i, 0))
    else:
        tpb = rows_per_batch // tm
        gate_spec = pl.BlockSpec((None, 1, d), lambda i: (i // tpb, 0, 0))
    return pl.pallas_call(
        _final_kernel,
        grid=(m // tm,),
        in_specs=[pl.BlockSpec((tm, d), lambda i: (i, 0)),
                  pl.BlockSpec((tm, d), lambda i: (i, 0)),
                  gate_spec,
                  pl.BlockSpec((1, d), lambda i: (0, 0))],
        out_specs=pl.BlockSpec((tm, d), lambda i: (i, 0)),
        out_shape=jax.ShapeDtypeStruct((m, d), F32),
        compiler_params=_cparams(("arbitrary",)),
        name="final_norm",
    )(x, f, gate, g.reshape(1, d))


def _layout(d):
    nh = d // SSD_HEADDIM
    kvw = KV_HEADS * HEAD_DIM
    qiw = IDX_HEADS * IDX_DIM
    xbcw = d + 2 * SSD_GROUPS * D_STATE
    lay, off = {}, 0
    for name, w in (("z", d), ("q", d), ("qi", qiw), ("xbc", xbcw), ("k", kvw), ("v", kvw), ("tail", PROJ_TN)):
        assert w % PROJ_TN == 0
        lay[name] = off
        off += w
    lay["total"] = off
    assert IDX_DIM + nh + IDX_HEADS <= PROJ_TN
    return lay


def _group_cols(xbc, d):
    lead = xbc.shape[:-1]
    gn = SSD_GROUPS * D_STATE
    x = xbc[..., :d].reshape(lead + (SSD_GROUPS, d // SSD_GROUPS))
    b = xbc[..., d:d + gn].reshape(lead + (SSD_GROUPS, D_STATE))
    c = xbc[..., d + gn:].reshape(lead + (SSD_GROUPS, D_STATE))
    return jnp.concatenate([x, b, c], axis=-1).reshape(lead + (d + 2 * gn,))


def _ungroup_cols(xg, d):
    lead = xg.shape[:-1]
    gn = SSD_GROUPS * D_STATE
    xsw = d // SSD_GROUPS
    g3 = xg.reshape(lead + (SSD_GROUPS, xsw + 2 * D_STATE))
    return jnp.concatenate([g3[..., :xsw].reshape(lead + (d,)),
                            g3[..., xsw:xsw + D_STATE].reshape(lead + (gn,)),
                            g3[..., xsw + D_STATE:].reshape(lead + (gn,))], axis=-1)


def _w_in_moves(d):
    nh = d // SSD_HEADDIM
    kvw = KV_HEADS * HEAD_DIM
    qiw = IDX_HEADS * IDX_DIM
    xbcw = d + 2 * SSD_GROUPS * D_STATE
    names = ("z", "xbc", "dt", "q", "k", "v", "qi", "wi", "ki")
    sizes = (d, xbcw, nh, d, kvw, kvw, qiw, IDX_HEADS, IDX_DIM)
    src = dict(zip(names, np.concatenate([[0], np.cumsum(sizes)[:-1]]).tolist()))
    lay = _layout(d)
    moves = [(lay["z"], src["z"], d), (lay["q"], src["q"], d), (lay["qi"], src["qi"], qiw),
             (lay["k"], src["k"], kvw), (lay["v"], src["v"], kvw)]
    xsw = d // SSD_GROUPS
    gw = xsw + 2 * D_STATE
    gn = SSD_GROUPS * D_STATE
    for g in range(SSD_GROUPS):
        base = lay["xbc"] + g * gw
        moves += [(base, src["xbc"] + g * xsw, xsw),
                  (base + xsw, src["xbc"] + d + g * D_STATE, D_STATE),
                  (base + xsw + D_STATE, src["xbc"] + d + gn + g * D_STATE, D_STATE)]
    tail = [(src["ki"], IDX_DIM), (src["dt"], nh), (src["wi"], IDX_HEADS)]
    return moves, tail, int(sum(sizes))


def _w_in_prep_kernel(tbl_ref, x_ref, last_ref, o_ref):
    j = pl.program_id(0)
    is_last = j == pl.num_programs(0) - 1

    @pl.when(jnp.logical_not(is_last))
    def _():
        o_ref[...] = x_ref[...].astype(BF16)

    @pl.when(is_last)
    def _():
        o_ref[...] = last_ref[...].astype(BF16)


def _prep_w_in(w_in, d):
    moves, tail, ncols = _w_in_moves(d)
    total = _layout(d)["total"]
    w_t = jnp.swapaxes(w_in[0], 0, 1)
    nblk = total // LANES
    src_row = np.zeros((nblk,), np.int32)
    for dst, src, w in moves + [(total - PROJ_TN, tail[0][0], tail[0][1])]:
        assert dst % LANES == 0 and w % LANES == 0 and src % SUBLANES == 0
        for b in range(w // LANES):
            src_row[dst // LANES + b] = src + b * LANES
    assert PROJ_TN == 2 * LANES and tail[0][1] == LANES
    pieces = [lax.slice_in_dim(w_t, s, s + w, axis=0) for s, w in tail[1:]]
    pieces.append(jnp.zeros((LANES - sum(w for _, w in tail[1:]), d), w_t.dtype))
    last = jnp.concatenate(pieces, axis=0)
    return pl.pallas_call(
        _w_in_prep_kernel,
        grid_spec=pltpu.PrefetchScalarGridSpec(
            num_scalar_prefetch=1,
            grid=(nblk,),
            in_specs=[pl.BlockSpec((pl.Element(LANES), pl.Element(d)),
                                   lambda j, tbl: (tbl[j] * SUBLANES, 0)),
                      pl.BlockSpec((LANES, d), lambda j, tbl: (0, 0))],
            out_specs=pl.BlockSpec((LANES, d), lambda j, tbl: (j, 0)),
        ),
        out_shape=jax.ShapeDtypeStruct((total, d), BF16),
        compiler_params=_cparams(("arbitrary",)),
        name="w_in_prep",
    )(jnp.asarray(src_row // SUBLANES), w_t, last)


def _group_major(vec, rpg):
    return jnp.pad(vec.reshape(SSD_GROUPS, 1, rpg), ((0, 0), (0, 0), (0, LANES - rpg)))


def _rope_tables(pos):
    half = HEAD_DIM // 8
    inv = ROPE_THETA ** (-jnp.arange(half, dtype=F32) / half)
    ang = pos.astype(F32)[:, None] * inv[None, :]
    cos, sin = jnp.cos(ang), jnp.sin(ang)
    n = pos.shape[0]
    rest = HEAD_DIM - 2 * half
    return (jnp.concatenate([cos, cos, jnp.ones((n, rest), F32)], axis=1),
            jnp.concatenate([-sin, sin, jnp.zeros((n, rest), F32)], axis=1))


def _layer(x2d, mods, pos, prm, *, nb, seq, per_token, tm, ssd_cfg, attn_fn, peer_tm):
    m, d = x2d.shape
    lay = prm["lay"]
    nh = d // SSD_HEADDIM
    rpg = nh // SSD_GROUPS
    sh1, sc1, g1, sh2, sc2, g2 = mods
    cos, sin = _rope_tables(pos)
    xoff = lay["xbc"]
    rope = (cos, sin, prm["kidx_norm_g"],
            ((lay["q"] // PROJ_TN, lay["xbc"] // PROJ_TN), (lay["k"] // PROJ_TN, lay["v"] // PROJ_TN)),
            lay["tail"] // PROJ_TN)
    tm_e = min(tm, ELEMWISE_TM)
    tm_p = min(m if per_token else seq, PROJ_TM)
    h1 = _modulate(x2d, prm["norm1_g"], sh1, sc1, tm=tm_e, per_token=per_token, rows_per_batch=seq)
    proj = _proj(h1, prm["w_in"], tm=tm_p, rope=rope, w_transposed=True)

    gw = rpg * SSD_HEADDIM + 2 * D_STATE
    dt_raw = proj[:, lay["tail"] + IDX_DIM: lay["tail"] + IDX_DIM + nh]
    y_ssd, ssm_new = ssd_cfg(proj, dt_raw, xoff // gw)
    y_n = _gate_norm(y_ssd, proj, prm["ssd_norm_g"], tm=tm_e)

    o_attn = attn_fn(proj)

    a_n = _rms_cast(o_attn, prm["attn_norm_g"], tm=tm_e)
    x1 = _outproj(y_n, a_n, prm["w_out"], x2d, g1, tm=tm_p, per_token=per_token,
                  rows_per_batch=seq)

    xb = _modulate(x1, prm["norm2_g"], sh2, sc2, tm=tm_e, per_token=per_token, rows_per_batch=seq)
    qp = _proj(xb, prm["peer_wq"], tm=tm_p)
    s1, s2, tau, cc = _peer_route(qp, prm["peer_k1"], prm["peer_k2"], tm=min(peer_tm, PEER_ROUTE_TM))
    ffn = _peer_dense(xb, s1, s2, tau, cc, prm["peer_u"], prm["peer_v"], tm=peer_tm)
    y = _final(x1, ffn, g2, prm["final_norm_g"], tm=tm_e, per_token=per_token, rows_per_batch=seq)
    return y, proj, ssm_new


def kernel(x_prompt, x_sample, c_prompt, c_sample, cache_k, cache_v, cache_kidx, state_ssm, state_conv, page_table, w_mod, b_mod, norm1_g, w_in, conv_w, conv_b, dt_bias, a_log, d_skip, ssd_norm_g, kidx_norm_g, attn_norm_g, w_out, norm2_g, peer_wq, peer_k1, peer_k2, peer_u, peer_v, final_norm_g):
    nbp, seq, d = x_prompt.shape
    nbs, tdec, _ = x_sample.shape
    depth = w_mod.shape[0]
    assert depth == 1
    nh = d // SSD_HEADDIM
    rpg = nh // SSD_GROUPS
    gw = rpg * SSD_HEADDIM + 2 * D_STATE
    lay = _layout(d)
    assert lay["xbc"] % gw == 0
    kvw = KV_HEADS * HEAD_DIM
    xbcw = d + 2 * SSD_GROUPS * D_STATE

    prm = dict(
        lay=lay,
        norm1_g=norm1_g[0], kidx_norm_g=kidx_norm_g[0], ssd_norm_g=ssd_norm_g[0], attn_norm_g=attn_norm_g[0],
        norm2_g=norm2_g[0], final_norm_g=final_norm_g,
        w_in=_prep_w_in(w_in, d),
        w_out=_cast_bf16(w_out[0]),
        peer_wq=_cast_bf16(peer_wq[0]), peer_k1=peer_k1[0], peer_k2=peer_k2[0],
        peer_u=_cast_bf16(peer_u[0]), peer_v=_cast_bf16(peer_v[0]),
    )
    conv_w_g = _group_cols(conv_w[0], d)
    conv_b_g = _group_cols(conv_b[0], d).reshape(1, xbcw)
    dtb_g = _group_major(dt_bias[0], rpg)
    alog_g = _group_major(a_log[0], rpg)
    dsk_g = _group_major(d_skip[0], rpg)

    mod = _adaln(jnp.concatenate([c_prompt, c_sample], axis=0), w_mod[0], b_mod[0])
    mod_p = [a.reshape(nbp, 1, d) for a in jnp.split(mod[:nbp], 6, axis=-1)]
    mod_s = [jnp.repeat(a, SAMPLE_PAD, axis=0) for a in jnp.split(mod[nbp:], 6, axis=-1)]

    def dt_group_major(dt_raw):
        rows = dt_raw.shape[0]
        dtg = dt_raw.reshape(rows, SSD_GROUPS, rpg).transpose(1, 0, 2)
        return jnp.pad(dtg, ((0, 0), (0, 0), (0, LANES - rpg)))

    cs_p = min(SSD_CHUNK, seq)
    nc_p = seq // cs_p
    assert seq % cs_p == 0 and seq % Q_BLOCK == 0

    def ssd_prompt(proj, dt_raw, xoff_blocks):
        hist = jnp.zeros((nbp, SUBLANES, xbcw), F32)
        h0 = jnp.zeros((nbp, nh, SSD_HEADDIM, D_STATE), F32)
        return _ssd(proj, xoff_blocks, hist, conv_w_g, conv_b_g, dt_group_major(dt_raw), dtb_g, alog_g, dsk_g, h0,
                    nb=nbp, nc=nc_p, cs=cs_p, d=d, valid_len=cs_p)

    tm_p = min(512, seq)
    pos_p = jnp.tile(jnp.arange(seq), nbp)
    y_p, proj_p, ssm_p = _layer(
        x_prompt.reshape(nbp * seq, d), mod_p, pos_p, prm, nb=nbp, seq=seq, per_token=False, tm=tm_p,
        ssd_cfg=ssd_prompt, attn_fn=functools.partial(_attn_prompt, lay=lay, nb=nbp, seq=seq, d=d),
        peer_tm=min(PEER_TM, nbp * seq))

    npages = page_table.shape[1]
    past = npages * PAGE_SIZE
    ms = nbs * SAMPLE_PAD
    xs_pad = jnp.pad(x_sample, ((0, 0), (0, SAMPLE_PAD - tdec), (0, 0))).reshape(ms, d)
    pos_s = jnp.tile(past + jnp.arange(SAMPLE_PAD), nbs)

    def ssd_sample(proj, dt_raw, xoff_blocks):
        xbc = proj[:, lay["xbc"]:lay["xbc"] + xbcw].reshape(nbs, SAMPLE_PAD, xbcw)
        xbc = jnp.pad(xbc, ((0, 0), (0, SAMPLE_CHUNK - SAMPLE_PAD), (0, 0))).reshape(nbs * SAMPLE_CHUNK, xbcw)
        dtr = jnp.pad(dt_raw.reshape(nbs, SAMPLE_PAD, nh), ((0, 0), (0, SAMPLE_CHUNK - SAMPLE_PAD), (0, 0)))
        hist = jnp.pad(_group_cols(state_conv[0], d), ((0, 0), (SUBLANES - (CONV_W - 1), 0), (0, 0)))
        y, hnew = _ssd(xbc, 0, hist, conv_w_g, conv_b_g, dt_group_major(dtr.reshape(nbs * SAMPLE_CHUNK, nh)),
                       dtb_g, alog_g, dsk_g, state_ssm[0], nb=nbs, nc=1, cs=SAMPLE_CHUNK, d=d, valid_len=tdec)
        y = y.reshape(nbs, SAMPLE_CHUNK, d)[:, :SAMPLE_PAD].reshape(ms, d)
        return y, hnew

    def attn_sample(proj):
        def seg(name, w):
            return proj[:, lay[name]:lay[name] + w].reshape(nbs, SAMPLE_PAD, w)
        tail = lay["tail"]
        ki_new = proj[:, tail:tail + IDX_DIM].reshape(nbs, SAMPLE_PAD, IDX_DIM)
        wi = proj[:, tail + IDX_DIM + nh: tail + IDX_DIM + nh + IDX_HEADS].reshape(nbs, SAMPLE_PAD, IDX_HEADS)
        return _attn_sample(seg("q", d), seg("qi", IDX_HEADS * IDX_DIM), wi, ki_new, seg("k", kvw), seg("v", kvw),
                            cache_k, cache_v, cache_kidx, page_table, t_valid=tdec)

    y_s, proj_s, ssm_s = _layer(
        xs_pad, mod_s, pos_s, prm, nb=nbs, seq=SAMPLE_PAD, per_token=True, tm=ms,
        ssd_cfg=ssd_sample, attn_fn=attn_sample, peer_tm=ms)

    def states(proj, nb, rows, valid):
        p3 = proj.reshape(nb, rows, lay["total"])[:, :valid]
        k = p3[:, :, lay["k"]:lay["k"] + kvw].reshape(1, nb, valid, KV_HEADS, HEAD_DIM)
        v = p3[:, :, lay["v"]:lay["v"] + kvw].reshape(1, nb, valid, KV_HEADS, HEAD_DIM)
        ki = p3[:, :, lay["tail"]:lay["tail"] + IDX_DIM].reshape(1, nb, valid, IDX_DIM)
        xbc = _ungroup_cols(p3[:, valid - (CONV_W - 1):valid, lay["xbc"]:lay["xbc"] + xbcw], d)
        return k, v, ki, xbc.reshape(1, nb, CONV_W - 1, xbcw)

    k_p, v_p, ki_p, conv_p = states(proj_p, nbp, seq, seq)
    k_s, v_s, ki_s, conv_s = states(proj_s, nbs, SAMPLE_PAD, tdec)
    y_prompt = y_p.reshape(nbp, seq, d)
    y_sample = y_s.reshape(nbs, SAMPLE_PAD, d)[:, :tdec]
    return (y_prompt, y_sample, k_p, v_p, ki_p, ssm_p[None], conv_p,
            k_s, v_s, ki_s, ssm_s[None], conv_s)
```

```python
import functools

import jax
import jax.numpy as jnp
import numpy as np
from jax import lax
from jax.experimental import pallas as pl
from jax.experimental.pallas import tpu as pltpu

F32 = jnp.float32
BF16 = jnp.bfloat16
I32 = jnp.int32

SSD_HEADDIM = 64
SSD_GROUPS = 8
D_STATE = 128
CONV_W = 4
SSD_CHUNK = 256
HEAD_DIM = 128
KV_HEADS = 8
ROPE_THETA = 500000.0
IDX_HEADS = 32
IDX_DIM = 128
IDX_TOPK = 256
Q_BLOCK = 128
PEER_HEADS = 8
PEER_KEYS = 128
PEER_TOPK = 16
PEER_D_KEY = 256
PAGE_SIZE = 128
EPS = 1e-6

LANES = 128
SUBLANES = 8
VMEM_LIMIT_BYTES = 56 * 1024 * 1024

MXU_TILE = 256
PROJ_TN = MXU_TILE
PROJ_TM = 1024
ELEMWISE_TM = 256
CAST_ROWS = 512
PEER_ROUTE_TM = 256
PEER_TM = 1024
PEER_TE = 512
SAMPLE_PAD = 8
SAMPLE_CHUNK = 128
ATTN_TK = 512
ATTN_GROUPS_PER_STEP = 4
PAGES_PER_STEP = 16
MAX_PAGE_UNROLL = 5
NEG_BIG = -1e30
M_INIT = -1e29
INT_MIN = -2 ** 31


def _cparams(sem):
    return pltpu.CompilerParams(dimension_semantics=sem, vmem_limit_bytes=VMEM_LIMIT_BYTES)


def _nt(a, b):
    return lax.dot_general(a, b, (((1,), (1,)), ((), ())), preferred_element_type=F32)


def _tn(a, b):
    return lax.dot_general(a, b, (((0,), (0,)), ((), ())), preferred_element_type=F32)


def _rms(x, g):
    return x * lax.rsqrt(jnp.mean(x * x, axis=-1, keepdims=True) + EPS) * g


def _rope_tile(a, cos, sin):
    half = HEAD_DIM // 8
    lane = lax.broadcasted_iota(I32, a.shape, 1)
    sw = jnp.where(lane < half, pltpu.roll(a, HEAD_DIM - half, 1), pltpu.roll(a, half, 1))
    return a * cos + sw * sin


def _float_key(x):
    bits = pltpu.bitcast(x, I32)
    return jnp.where(bits < 0, bits ^ jnp.int32(0x7FFFFFFF), bits)


def _key_float(key):
    bits = jnp.where(key < 0, key ^ jnp.int32(0x7FFFFFFF), key)
    return pltpu.bitcast(bits, F32)


def _bisect_kth(count_ge, k, shape):
    imin = jnp.int32(INT_MIN)

    def body(i, ans):
        cand_u = ans | lax.shift_left(jnp.int32(1), jnp.int32(31) - i)
        return jnp.where(count_ge(cand_u ^ imin) >= k, cand_u, ans)

    ans = lax.fori_loop(0, 32, body, jnp.zeros(shape, I32))
    return ans ^ imin


def _cast_kernel(x_ref, o_ref):
    o_ref[...] = x_ref[...].astype(BF16)


def _cast_bf16(w):
    r, c = w.shape
    tr = min(r, CAST_ROWS)
    assert r % tr == 0
    return pl.pallas_call(
        _cast_kernel,
        grid=(r // tr,),
        in_specs=[pl.BlockSpec((tr, c), lambda i: (i, 0))],
        out_specs=pl.BlockSpec((tr, c), lambda i: (i, 0)),
        out_shape=jax.ShapeDtypeStruct((r, c), BF16),
        compiler_params=_cparams(("arbitrary",)),
        name="cast_bf16",
    )(w)


def _adaln_kernel(c_ref, w_ref, b_ref, o_ref):
    c = c_ref[...]
    a = (c * jax.nn.sigmoid(c)).astype(BF16)
    o_ref[...] = jnp.dot(a, w_ref[...].astype(BF16), preferred_element_type=F32) + b_ref[...]


def _adaln(c, w_mod, b_mod):
    n, d = c.shape
    npad = -(-n // SUBLANES) * SUBLANES
    cp = jnp.pad(c, ((0, npad - n), (0, 0)))
    nout = w_mod.shape[1]
    tn = PROJ_TN
    out = pl.pallas_call(
        _adaln_kernel,
        grid=(nout // tn,),
        in_specs=[pl.BlockSpec((npad, d), lambda j: (0, 0)),
                  pl.BlockSpec((d, tn), lambda j: (0, j)),
                  pl.BlockSpec((1, tn), lambda j: (0, j))],
        out_specs=pl.BlockSpec((npad, tn), lambda j: (0, j)),
        out_shape=jax.ShapeDtypeStruct((npad, nout), F32),
        compiler_params=_cparams(("arbitrary",)),
        name="adaln",
    )(cp, w_mod, b_mod.reshape(1, nout))
    return out[:n]


def _modulate_kernel(x_ref, g_ref, sh_ref, sc_ref, o_ref):
    o_ref[...] = (_rms(x_ref[...], g_ref[...]) * (1.0 + sc_ref[...]) + sh_ref[...]).astype(BF16)


def _modulate(x, g, shift, scale, *, tm, per_token, rows_per_batch):
    m, d = x.shape
    if per_token:
        mod_spec = pl.BlockSpec((tm, d), lambda i: (i, 0))
    else:
        tiles_per_batch = rows_per_batch // tm
        mod_spec = pl.BlockSpec((None, 1, d), lambda i: (i // tiles_per_batch, 0, 0))
    return pl.pallas_call(
        _modulate_kernel,
        grid=(m // tm,),
        in_specs=[pl.BlockSpec((tm, d), lambda i: (i, 0)),
                  pl.BlockSpec((1, d), lambda i: (0, 0)),
                  mod_spec, mod_spec],
        out_specs=pl.BlockSpec((tm, d), lambda i: (i, 0)),
        out_shape=jax.ShapeDtypeStruct((m, d), BF16),
        compiler_params=_cparams(("arbitrary",)),
        name="modulate",
    )(x, g.reshape(1, d), shift, scale)


def _proj_kernel(*refs, rope_ranges, tail_tile, w_transposed):
    if tail_tile is not None:
        h_ref, w_ref, cos_ref, sin_ref, kg_ref, o_ref = refs
    else:
        h_ref, w_ref, o_ref = refs
    j = pl.program_id(1)

    def product():
        if w_transposed:
            return _nt(h_ref[...], w_ref[...])
        return jnp.dot(h_ref[...], w_ref[...], preferred_element_type=F32)

    if tail_tile is None:
        o_ref[...] = product()
        return

    is_rope = functools.reduce(jnp.logical_or, [(j >= lo) & (j < hi) for lo, hi in rope_ranges])
    is_tail = j == tail_tile
    tn = o_ref.shape[1]

    @pl.when(jnp.logical_not(is_rope | is_tail))
    def _():
        o_ref[...] = product()

    @pl.when(is_rope)
    def _():
        acc = product()
        cos = cos_ref[...]
        sin = sin_ref[...]
        for t in range(tn // HEAD_DIM):
            sl = slice(t * HEAD_DIM, (t + 1) * HEAD_DIM)
            o_ref[:, sl] = _rope_tile(acc[:, sl], cos, sin)

    @pl.when(is_tail)
    def _():
        acc = product()
        ki = _rms(acc[:, :IDX_DIM], kg_ref[...])
        o_ref[:, :IDX_DIM] = _rope_tile(ki, cos_ref[...], sin_ref[...])
        o_ref[:, IDX_DIM:] = acc[:, IDX_DIM:]


def _proj(h, w, *, tm, rope=None, w_transposed=False):
    m, d = h.shape
    n = w.shape[0] if w_transposed else w.shape[1]
    tn = PROJ_TN
    assert m % tm == 0 and n % tn == 0
    w_spec = pl.BlockSpec((tn, d), lambda i, j: (j, 0)) if w_transposed else pl.BlockSpec((d, tn), lambda i, j: (0, j))
    in_specs = [pl.BlockSpec((tm, d), lambda i, j: (i, 0), pipeline_mode=pl.Buffered(1)), w_spec]
    args = [h, w]
    rope_ranges, tail_tile = (), None
    if rope is not None:
        cos, sin, kidx_g, rope_ranges, tail_tile = rope
        in_specs += [pl.BlockSpec((tm, HEAD_DIM), lambda i, j: (i, 0)),
                     pl.BlockSpec((tm, HEAD_DIM), lambda i, j: (i, 0)),
                     pl.BlockSpec((1, IDX_DIM), lambda i, j: (0, 0))]
        args += [cos, sin, kidx_g.reshape(1, IDX_DIM)]
    return pl.pallas_call(
        functools.partial(_proj_kernel, rope_ranges=rope_ranges, tail_tile=tail_tile, w_transposed=w_transposed),
        grid=(m // tm, n // tn),
        in_specs=in_specs,
        out_specs=pl.BlockSpec((tm, tn), lambda i, j: (i, j)),
        out_shape=jax.ShapeDtypeStruct((m, n), F32),
        compiler_params=_cparams(("arbitrary", "arbitrary")),
        name="proj",
    )(*args)


def _ssd_kernel(x_ref, prev_ref, hist_ref, cw_ref, cb_ref, dt_ref, dtb_ref, alog_ref, dsk_ref, h0_ref,
                y_ref, hout_ref, h_scr, *, cs, rpg, valid_len, nc):
    c = pl.program_id(2)
    xs_w = rpg * SSD_HEADDIM

    @pl.when(c == 0)
    def _():
        h_scr[...] = h0_ref[...]

    prev = jnp.where(c == 0, hist_ref[...], prev_ref[...])
    cat = jnp.concatenate([prev, x_ref[...]], axis=0)
    w = cw_ref[...]
    acc = cb_ref[...]
    for j in range(CONV_W):
        lo = SUBLANES - (CONV_W - 1) + j
        acc = acc + cat[lo:lo + cs] * w[j:j + 1]
    xc = acc * jax.nn.sigmoid(acc)
    xs = xc[:, :xs_w]
    bm = xc[:, xs_w:xs_w + D_STATE]
    cm = xc[:, xs_w + D_STATE:]
    bm_b = bm.astype(BF16)
    cm_b = cm.astype(BF16)

    z = dt_ref[...] + dtb_ref[...]
    dt = jnp.maximum(z, 0.0) + jnp.log1p(jnp.exp(-jnp.abs(z)))
    if valid_len < cs:
        row = lax.broadcasted_iota(I32, dt.shape, 0)
        dt = jnp.where(row < valid_len, dt, 0.0)
    a_neg = -jnp.exp(alog_ref[...])
    ii = lax.broadcasted_iota(I32, (cs, cs), 0)
    jj = lax.broadcasted_iota(I32, (cs, cs), 1)
    causal = ii >= jj
    acs = jnp.dot(causal.astype(F32), dt * a_neg, preferred_element_type=F32,
                  precision=lax.Precision.HIGHEST)
    acs_t = acs.T
    a_last = acs[cs - 1:cs, :]
    cb = _nt(cm_b, bm_b)
    dsk = dsk_ref[...]

    first = lax.broadcasted_iota(I32, (cs, LANES), 1) < SSD_HEADDIM
    first_row = lax.broadcasted_iota(I32, (LANES, D_STATE), 0) < SSD_HEADDIM
    ys = []
    for pr in range(rpg // 2):
        r0, r1 = 2 * pr, 2 * pr + 1
        a0, a1 = acs[:, r0:r0 + 1], acs[:, r1:r1 + 1]
        xp = xs[:, pr * LANES:(pr + 1) * LANES]
        xd = xp * jnp.where(first, dt[:, r0:r0 + 1], dt[:, r1:r1 + 1])
        y = jnp.zeros((cs, LANES), F32)
        for r, keep in ((r0, first), (r1, jnp.logical_not(first))):
            lm = jnp.exp(jnp.where(causal, acs[:, r:r + 1] - acs_t[r:r + 1, :], -jnp.inf))
            y = y + jnp.dot((cb * lm).astype(BF16), jnp.where(keep, xd, 0.0).astype(BF16),
                            preferred_element_type=F32)
        hp = h_scr[pr]
        y = y + jnp.where(first, jnp.exp(a0), jnp.exp(a1)) * _nt(cm_b, hp.astype(BF16))
        al0, al1 = a_last[:, r0:r0 + 1], a_last[:, r1:r1 + 1]
        decay = jnp.where(first, jnp.exp(al0 - a0), jnp.exp(al1 - a1))
        h_scr[pr] = (hp * jnp.where(first_row, jnp.exp(al0), jnp.exp(al1))
                     + _tn((xd * decay).astype(BF16), bm_b))
        ys.append(y + xp * jnp.where(first[0:1, :], dsk[:, r0:r0 + 1], dsk[:, r1:r1 + 1]))
    y_ref[...] = jnp.concatenate(ys, axis=1)

    @pl.when(c == nc - 1)
    def _():
        hout_ref[...] = h_scr[...]


def _ssd(xg, xoff_blocks, hist8, conv_w_g, conv_b_g, dt_g, dtb_g, alog_g, dsk_g, h0, *, nb, nc, cs, d, valid_len):
    g_cnt = SSD_GROUPS
    rpg = d // SSD_HEADDIM // g_cnt
    gw = rpg * SSD_HEADDIM + 2 * D_STATE
    m = nb * nc * cs
    cpb = cs // SUBLANES
    nh = d // SSD_HEADDIM
    npair = rpg // 2
    assert rpg % 2 == 0 and 2 * SSD_HEADDIM == LANES
    kern = functools.partial(_ssd_kernel, cs=cs, rpg=rpg, valid_len=valid_len, nc=nc)
    y, hout = pl.pallas_call(
        kern,
        grid=(nb, g_cnt, nc),
        in_specs=[
            pl.BlockSpec((cs, gw), lambda b, g, c: (b * nc + c, xoff_blocks + g)),
            pl.BlockSpec((SUBLANES, gw), lambda b, g, c: (jnp.maximum((b * nc + c) * cpb - 1, 0), xoff_blocks + g)),
            pl.BlockSpec((None, SUBLANES, gw), lambda b, g, c: (b, 0, g)),
            pl.BlockSpec((CONV_W, gw), lambda b, g, c: (0, g)),
            pl.BlockSpec((1, gw), lambda b, g, c: (0, g)),
            pl.BlockSpec((None, cs, LANES), lambda b, g, c: (g, b * nc + c, 0)),
            pl.BlockSpec((None, 1, LANES), lambda b, g, c: (g, 0, 0)),
            pl.BlockSpec((None, 1, LANES), lambda b, g, c: (g, 0, 0)),
            pl.BlockSpec((None, 1, LANES), lambda b, g, c: (g, 0, 0)),
            pl.BlockSpec((None, npair, LANES, D_STATE), lambda b, g, c: (b, g, 0, 0)),
        ],
        out_specs=[
            pl.BlockSpec((cs, rpg * SSD_HEADDIM), lambda b, g, c: (b * nc + c, g)),
            pl.BlockSpec((None, npair, LANES, D_STATE), lambda b, g, c: (b, g, 0, 0)),
        ],
        out_shape=[jax.ShapeDtypeStruct((m, d), F32),
                   jax.ShapeDtypeStruct((nb, nh // 2, LANES, D_STATE), F32)],
        scratch_shapes=[pltpu.VMEM((npair, LANES, D_STATE), F32)],
        compiler_params=_cparams(("arbitrary", "arbitrary", "arbitrary")),
        name="ssd_scan",
    )(xg, xg, hist8, conv_w_g, conv_b_g, dt_g, dtb_g, alog_g, dsk_g, h0.reshape(nb, nh // 2, LANES, D_STATE))
    return y, hout.reshape(nb, nh, SSD_HEADDIM, D_STATE)


def _gate_norm_kernel(y_ref, z_ref, g_ref, o_ref):
    z = z_ref[...]
    o_ref[...] = _rms(y_ref[...] * (z * jax.nn.sigmoid(z)), g_ref[...]).astype(BF16)


def _gate_norm(y, proj, g, *, tm):
    m, d = y.shape
    return pl.pallas_call(
        _gate_norm_kernel,
        grid=(m // tm,),
        in_specs=[pl.BlockSpec((tm, d), lambda i: (i, 0)),
                  pl.BlockSpec((tm, d), lambda i: (i, 0)),
                  pl.BlockSpec((1, d), lambda i: (0, 0))],
        out_specs=pl.BlockSpec((tm, d), lambda i: (i, 0)),
        out_shape=jax.ShapeDtypeStruct((m, d), BF16),
        compiler_params=_cparams(("arbitrary",)),
        name="gate_norm",
    )(y, proj, g.reshape(1, d))


def _attn_prompt_kernel(qi_ref, tail_ref, ki_ref, q_ref, k_ref, v_ref, o_ref, key_scr, thr_scr,
                        *, topk, tk, n_ssd_heads, rq, gps):
    qb = pl.program_id(1)
    g = pl.program_id(2)
    nkt = (qb * Q_BLOCK + Q_BLOCK + tk - 1) // tk
    imin = jnp.int32(INT_MIN)

    @pl.when(g == 0)
    def _():
        qi = qi_ref[...]
        qis = jnp.concatenate([qi[:, h * IDX_DIM:(h + 1) * IDX_DIM] for h in range(IDX_HEADS)],
                              axis=0).astype(BF16)
        w_t = tail_ref[...].T[n_ssd_heads:n_ssd_heads + IDX_HEADS, :] * (IDX_HEADS ** -0.5 * IDX_DIM ** -0.5)
        tpos = qb * Q_BLOCK + lax.broadcasted_iota(I32, (tk, Q_BLOCK), 1)
        kiota = lax.broadcasted_iota(I32, (tk, Q_BLOCK), 0)

        def score_tile(kt, carry):
            rows = pl.ds(pl.multiple_of(kt * tk, tk), tk)
            x = _nt(ki_ref[rows, :].astype(BF16), qis)
            sc = jnp.zeros((tk, Q_BLOCK), F32)
            for h in range(IDX_HEADS):
                sc = sc + jnp.maximum(x[:, h * Q_BLOCK:(h + 1) * Q_BLOCK], 0.0) * w_t[h:h + 1, :]
            key = _float_key(sc + 0.0)
            key_scr[rows, :] = jnp.where(kt * tk + kiota <= tpos, key, imin)
            return carry

        lax.fori_loop(0, nkt, score_tile, 0)

        def count_ge(thr):
            def body(kt, cnt):
                rows = pl.ds(pl.multiple_of(kt * tk, tk), tk)
                ge = (key_scr[rows, :] >= thr).astype(I32)
                return cnt + jnp.sum(ge.reshape(tk // SUBLANES, SUBLANES, Q_BLOCK), axis=0)
            cnt = lax.fori_loop(0, nkt, body, jnp.zeros((SUBLANES, Q_BLOCK), I32))
            return jnp.sum(cnt, axis=0, keepdims=True)

        thr = jnp.maximum(_bisect_kth(count_ge, topk, (1, Q_BLOCK)), jnp.int32(INT_MIN + 1))
        thr_scr[...] = jnp.broadcast_to(thr, (SUBLANES, Q_BLOCK))

    q = q_ref[...]
    nq = rq * Q_BLOCK
    qs = []
    for u in range(gps):
        heads = [q[:, (u * rq + r) * HEAD_DIM:(u * rq + r + 1) * HEAD_DIM] for r in range(rq)]
        qs.append((jnp.concatenate(heads, axis=0) * (HEAD_DIM ** -0.5)).astype(BF16))
    thr_q = jnp.concatenate([thr_scr[0:1, :]] * rq, axis=1)

    def attend(kt, carry):
        rows = pl.ds(pl.multiple_of(kt * tk, tk), tk)
        sel = jnp.concatenate([key_scr[rows, :]] * rq, axis=1) >= thr_q
        k_all = k_ref[rows, :]
        v_all = v_ref[rows, :]
        s_ts = [_nt(k_all[:, u * HEAD_DIM:(u + 1) * HEAD_DIM].astype(BF16), qs[u]) for u in range(gps)]
        v_ts = [v_all[:, u * HEAD_DIM:(u + 1) * HEAD_DIM].T.astype(BF16) for u in range(gps)]
        out = []
        for u in range(gps):
            m, l, acc = carry[u]
            s_t = jnp.where(sel, s_ts[u], NEG_BIG)
            m_new = jnp.maximum(m, jnp.max(s_t, axis=0, keepdims=True))
            alpha = jnp.exp(m - m_new)
            p = jnp.exp(s_t - m_new)
            l = l * alpha + jnp.sum(p, axis=0, keepdims=True)
            acc = acc * alpha + jnp.dot(v_ts[u], p.astype(BF16), preferred_element_type=F32)
            out.append((m_new, l, acc))
        return tuple(out)

    init = tuple((jnp.full((1, nq), M_INIT, F32), jnp.zeros((1, nq), F32), jnp.zeros((HEAD_DIM, nq), F32))
                 for _ in range(gps))
    res = lax.fori_loop(0, nkt, attend, init)
    for u in range(gps):
        _, l, acc = res[u]
        o = (acc / l).T
        for r in range(rq):
            c0 = (u * rq + r) * HEAD_DIM
            o_ref[:, c0:c0 + HEAD_DIM] = o[r * Q_BLOCK:(r + 1) * Q_BLOCK, :]


def _attn_prompt(proj, lay, *, nb, seq, d):
    rq = d // HEAD_DIM // KV_HEADS
    nqb = seq // Q_BLOCK
    topk = min(IDX_TOPK, seq // 4)
    tk = min(ATTN_TK, seq)
    qiw = IDX_HEADS * IDX_DIM
    gps = min(ATTN_GROUPS_PER_STEP, KV_HEADS)
    qw = gps * rq * HEAD_DIM
    kw = gps * HEAD_DIM
    assert KV_HEADS % gps == 0 and lay["q"] % qw == 0 and lay["k"] % kw == 0 and lay["v"] % kw == 0
    kern = functools.partial(_attn_prompt_kernel, topk=topk, tk=tk, n_ssd_heads=d // SSD_HEADDIM, rq=rq, gps=gps)
    return pl.pallas_call(
        kern,
        grid=(nb, nqb, KV_HEADS // gps),
        in_specs=[
            pl.BlockSpec((Q_BLOCK, qiw), lambda b, i, g: (b * nqb + i, lay["qi"] // qiw)),
            pl.BlockSpec((Q_BLOCK, LANES), lambda b, i, g: (b * nqb + i, lay["tail"] // LANES + 1)),
            pl.BlockSpec((seq, IDX_DIM), lambda b, i, g: (b, lay["tail"] // IDX_DIM)),
            pl.BlockSpec((Q_BLOCK, qw), lambda b, i, g: (b * nqb + i, lay["q"] // qw + g)),
            pl.BlockSpec((seq, kw), lambda b, i, g: (b, lay["k"] // kw + g)),
            pl.BlockSpec((seq, kw), lambda b, i, g: (b, lay["v"] // kw + g)),
        ],
        out_specs=pl.BlockSpec((Q_BLOCK, qw), lambda b, i, g: (b * nqb + i, g)),
        out_shape=jax.ShapeDtypeStruct((nb * seq, d), F32),
        scratch_shapes=[pltpu.VMEM((seq, Q_BLOCK), I32), pltpu.VMEM((SUBLANES, Q_BLOCK), I32)],
        compiler_params=_cparams(("arbitrary", "arbitrary", "arbitrary")),
        name="attn_prompt",
    )(proj, proj, proj, proj, proj, proj)


def _head_rows(page_ref, head):
    return page_ref[pl.ds(head, PAGE_SIZE, stride=KV_HEADS), :]


def _fold_lane_groups(x):
    sh = LANES // 2
    while sh >= SAMPLE_PAD:
        x = x + pltpu.roll(x, sh, 1)
        sh //= 2
    return x


def _sample_scores_kernel(pt_ref, *refs, npages, topk, pps):
    kip_refs = refs[:pps]
    kp_refs = refs[pps:2 * pps]
    qi_ref, w_ref, qbd_ref, kinew_ref, knew_ref, p_ref, key_scr, kc_scr, s_scr, qbd_scr = refs[2 * pps:]
    j = pl.program_id(1)
    imin = jnp.int32(INT_MIN)
    nip = qi_ref.shape[0]
    nlp = qbd_ref.shape[0]
    groups_per_slab = LANES // SAMPLE_PAD
    lane = lax.broadcasted_iota(I32, (PAGE_SIZE, LANES), 1)
    lane_group = lane // SAMPLE_PAD

    @pl.when(j == 0)
    def _():
        qbd_scr[...] = (qbd_ref[...] * (HEAD_DIM ** -0.5)).astype(BF16)
        kc_scr[...] = jnp.full(kc_scr.shape, imin, I32)

    qi = qi_ref[...].astype(BF16)
    w_row = w_ref[...]

    def index_keys(kidx):
        x = _nt(kidx.astype(BF16), qi)
        r = jnp.maximum(x, 0.0) * w_row
        acc = r[:, :LANES]
        for c in range(1, nip // LANES):
            acc = acc + r[:, c * LANES:(c + 1) * LANES]
        return _float_key(_fold_lane_groups(acc) + 0.0)

    def put_page(page, keys, scores):
        rows = pl.ds(pl.multiple_of(page * PAGE_SIZE, PAGE_SIZE), PAGE_SIZE)
        key_scr[rows, :] = keys
        s_scr[rows, :] = scores
        slab = page // groups_per_slab
        kc_scr[slab] = jnp.where(lane_group == page % groups_per_slab, keys, kc_scr[slab])

    kidx_cat = jnp.concatenate([r[...] for r in kip_refs], axis=0)
    keys = index_keys(kidx_cat)
    k2d = jnp.concatenate(
        [jnp.concatenate([_head_rows(kp, h).astype(BF16) for h in range(KV_HEADS)], axis=1)
         for kp in kp_refs], axis=0)
    scores = _nt(k2d, qbd_scr[...])
    for i in range(pps):
        sl = slice(i * PAGE_SIZE, (i + 1) * PAGE_SIZE)
        put_page(j * pps + i, keys[sl], scores[sl])

    @pl.when(j == 0)
    def _():
        knew = index_keys(kinew_ref[...])
        srow = lax.broadcasted_iota(I32, (PAGE_SIZE, LANES), 0)
        knew = jnp.where(srow <= lane % SAMPLE_PAD, knew, imin)
        put_page(jnp.int32(npages), knew, _nt(knew_ref[...].astype(BF16), qbd_scr[...]))

    @pl.when(j == pl.num_programs(1) - 1)
    def _():
        kc = kc_scr[...]

        tok = lane[0:1, :] % SAMPLE_PAD

        def count_ge(thr):
            ge = (kc >= thr).astype(F32)
            cnt = jnp.sum(jnp.sum(ge.reshape(-1, SUBLANES, LANES), axis=0), axis=0, keepdims=True)
            tot = jnp.zeros((1, LANES), F32)
            for t in range(SAMPLE_PAD):
                tot = jnp.where(tok == t, jnp.sum(jnp.where(tok == t, cnt, 0.0), axis=1, keepdims=True), tot)
            return tot

        thr = _bisect_kth(count_ge, topk, (1, LANES))
        reps = nlp // LANES
        thr_q = jnp.concatenate([thr] * reps, axis=1)

        def masked(page):
            rows = pl.ds(pl.multiple_of(page * PAGE_SIZE, PAGE_SIZE), PAGE_SIZE)
            key = jnp.concatenate([key_scr[rows, :]] * reps, axis=1)
            return rows, (key >= thr_q) & (key != imin)

        def max_body(page, m):
            rows, sel = masked(page)
            return jnp.maximum(m, jnp.max(jnp.where(sel, s_scr[rows, :], NEG_BIG), axis=0, keepdims=True))

        unroll = next(u for u in range(MAX_PAGE_UNROLL, 0, -1) if (npages + 1) % u == 0)
        m = lax.fori_loop(0, npages + 1, max_body, jnp.full((1, nlp), NEG_BIG, F32), unroll=unroll)

        def exp_body(page, l):
            rows, sel = masked(page)
            e = jnp.where(sel, jnp.exp(s_scr[rows, :] - m), 0.0)
            s_scr[rows, :] = e
            return l + jnp.sum(e, axis=0, keepdims=True)

        l = lax.fori_loop(0, npages + 1, exp_body, jnp.zeros((1, nlp), F32), unroll=unroll)

        def out_body(page, carry):
            rows = pl.ds(pl.multiple_of(page * PAGE_SIZE, PAGE_SIZE), PAGE_SIZE)
            p_ref[page] = (s_scr[rows, :] / l).T.astype(BF16)
            return carry

        lax.fori_loop(0, npages + 1, out_body, 0, unroll=unroll)


def _sample_pv_kernel(pt_ref, *refs, pps, rq):
    vp_refs = refs[:pps]
    p_ref, pnew_ref, vnew_ref, o_ref, acc_scr = refs[pps:]
    j = pl.program_id(1)
    nrow = rq * SAMPLE_PAD

    @pl.when(j == 0)
    def _():
        pn = pnew_ref[0]
        vn = vnew_ref[...].astype(BF16)
        for g in range(KV_HEADS):
            acc_scr[g] = jnp.dot(pn[g * nrow:(g + 1) * nrow, :], vn[:, g * HEAD_DIM:(g + 1) * HEAD_DIM],
                                 preferred_element_type=F32)

    pcat = jnp.concatenate([p_ref[i] for i in range(pps)], axis=1)
    for g in range(KV_HEADS):
        vg = jnp.concatenate([_head_rows(vp, g).astype(BF16) for vp in vp_refs], axis=0)
        acc_scr[g] += jnp.dot(pcat[g * nrow:(g + 1) * nrow, :], vg, preferred_element_type=F32)

    @pl.when(j == pl.num_programs(1) - 1)
    def _():
        o_ref[...] = acc_scr[...]


def _attn_sample(q, qi, wi, ki_new, k_new, v_new, cache_k, cache_v, cache_kidx, page_table, *, t_valid):
    nb, npages = page_table.shape
    pps = min(PAGES_PER_STEP, npages)
    assert npages % pps == 0
    nsteps = npages // pps
    past = npages * PAGE_SIZE
    topk = min(IDX_TOPK, (past + t_valid) // 4)
    d = q.shape[-1]
    rq = d // HEAD_DIM // KV_HEADS
    kvw = KV_HEADS * HEAD_DIM
    nrow = rq * SAMPLE_PAD
    ck = cache_k.reshape(-1, PAGE_SIZE * KV_HEADS, HEAD_DIM)
    cv = cache_v.reshape(-1, PAGE_SIZE * KV_HEADS, HEAD_DIM)
    cki = cache_kidx.reshape(-1, PAGE_SIZE, IDX_DIM)

    def pad_to(a, axis, mult):
        n = a.shape[axis]
        widths = [(0, 0)] * a.ndim
        widths[axis] = (0, -(-n // mult) * mult - n)
        return jnp.pad(a, widths)

    qi_s = pad_to(qi.reshape(nb, SAMPLE_PAD, IDX_HEADS, IDX_DIM).transpose(0, 2, 1, 3)
                  .reshape(nb, IDX_HEADS * SAMPLE_PAD, IDX_DIM), 1, LANES)
    w_row = pad_to((wi * (IDX_HEADS ** -0.5 * IDX_DIM ** -0.5)).transpose(0, 2, 1)
                   .reshape(nb, 1, IDX_HEADS * SAMPLE_PAD), 2, LANES)
    nip = qi_s.shape[1]
    q_g = q.reshape(nb, SAMPLE_PAD, KV_HEADS, rq, HEAD_DIM).transpose(0, 2, 3, 1, 4).reshape(
        nb, KV_HEADS, nrow, HEAD_DIM)
    qbd = pad_to(jnp.einsum("bgrd,gh->bgrhd", q_g, jnp.eye(KV_HEADS, dtype=q.dtype))
                 .reshape(nb, KV_HEADS * nrow, kvw), 1, LANES)
    nlp = qbd.shape[1]
    pad_rows = ((0, 0), (0, PAGE_SIZE - SAMPLE_PAD), (0, 0))
    kinew_p = jnp.pad(ki_new, pad_rows)
    knew_p = jnp.pad(k_new, pad_rows)
    vnew_p = jnp.pad(v_new, pad_rows)
    npg = npages + 1
    nslab = -(-npg // (LANES // SAMPLE_PAD))

    def kidx_spec(i):
        return pl.BlockSpec((None, PAGE_SIZE, IDX_DIM), lambda b, j, pt, i=i: (pt[b, j * pps + i], 0, 0))

    def kv_specs():
        return [pl.BlockSpec((None, PAGE_SIZE * KV_HEADS, HEAD_DIM), lambda b, j, pt, i=i: (pt[b, j * pps + i], 0, 0))
                for i in range(pps)]

    probs = pl.pallas_call(
        functools.partial(_sample_scores_kernel, npages=npages, topk=topk, pps=pps),
        grid_spec=pltpu.PrefetchScalarGridSpec(
            num_scalar_prefetch=1,
            grid=(nb, nsteps),
            in_specs=[kidx_spec(i) for i in range(pps)] + kv_specs() + [
                pl.BlockSpec((None, nip, IDX_DIM), lambda b, j, pt: (b, 0, 0)),
                pl.BlockSpec((None, 1, nip), lambda b, j, pt: (b, 0, 0)),
                pl.BlockSpec((None, nlp, kvw), lambda b, j, pt: (b, 0, 0)),
                pl.BlockSpec((None, PAGE_SIZE, IDX_DIM), lambda b, j, pt: (b, 0, 0)),
                pl.BlockSpec((None, PAGE_SIZE, kvw), lambda b, j, pt: (b, 0, 0)),
            ],
            out_specs=pl.BlockSpec((None, npg, nlp, PAGE_SIZE), lambda b, j, pt: (b, 0, 0, 0)),
            scratch_shapes=[pltpu.VMEM((npg * PAGE_SIZE, LANES), I32),
                            pltpu.VMEM((nslab, PAGE_SIZE, LANES), I32),
                            pltpu.VMEM((npg * PAGE_SIZE, nlp), F32),
                            pltpu.VMEM((nlp, kvw), BF16)],
        ),
        out_shape=jax.ShapeDtypeStruct((nb, npg, nlp, PAGE_SIZE), BF16),
        compiler_params=_cparams(("arbitrary", "arbitrary")),
        name="sample_scores",
    )(page_table, *([cki] * pps), *([ck] * pps), qi_s, w_row, qbd, kinew_p, knew_p)

    o = pl.pallas_call(
        functools.partial(_sample_pv_kernel, pps=pps, rq=rq),
        grid_spec=pltpu.PrefetchScalarGridSpec(
            num_scalar_prefetch=1,
            grid=(nb, nsteps),
            in_specs=kv_specs() + [
                pl.BlockSpec((None, pps, nlp, PAGE_SIZE), lambda b, j, pt: (b, j, 0, 0)),
                pl.BlockSpec((None, 1, nlp, PAGE_SIZE), lambda b, j, pt: (b, npages, 0, 0)),
                pl.BlockSpec((None, PAGE_SIZE, kvw), lambda b, j, pt: (b, 0, 0)),
            ],
            out_specs=pl.BlockSpec((None, KV_HEADS, nrow, HEAD_DIM), lambda b, j, pt: (b, 0, 0, 0)),
            scratch_shapes=[pltpu.VMEM((KV_HEADS, nrow, HEAD_DIM), F32)],
        ),
        out_shape=jax.ShapeDtypeStruct((nb, KV_HEADS, nrow, HEAD_DIM), F32),
        compiler_params=_cparams(("arbitrary", "arbitrary")),
        name="sample_pv",
    )(page_table, *([cv] * pps), probs, probs, vnew_p)
    return o.reshape(nb, KV_HEADS, rq, SAMPLE_PAD, HEAD_DIM).transpose(0, 3, 1, 2, 4).reshape(nb * SAMPLE_PAD, d)


def _rms_cast_kernel(x_ref, g_ref, o_ref):
    o_ref[...] = _rms(x_ref[...], g_ref[...]).astype(BF16)


def _rms_cast(x, g, *, tm):
    m, d = x.shape
    return pl.pallas_call(
        _rms_cast_kernel,
        grid=(m // tm,),
        in_specs=[pl.BlockSpec((tm, d), lambda i: (i, 0)), pl.BlockSpec((1, d), lambda i: (0, 0))],
        out_specs=pl.BlockSpec((tm, d), lambda i: (i, 0)),
        out_shape=jax.ShapeDtypeStruct((m, d), BF16),
        compiler_params=_cparams(("arbitrary",)),
        name="rms_cast",
    )(x, g.reshape(1, d))


def _outproj_kernel(y_ref, a_ref, w1_ref, w2_ref, x_ref, gate_ref, o_ref):
    acc = jnp.dot(y_ref[...], w1_ref[...], preferred_element_type=F32)
    acc = acc + jnp.dot(a_ref[...], w2_ref[...], preferred_element_type=F32)
    o_ref[...] = x_ref[...] + gate_ref[...] * acc


def _outproj(y_n, attn_n, w, x, gate, *, tm, per_token, rows_per_batch):
    m, d = x.shape
    tn = PROJ_TN
    if per_token:
        gate_spec = pl.BlockSpec((tm, tn), lambda i, j: (i, j))
    else:
        tpb = rows_per_batch // tm
        gate_spec = pl.BlockSpec((None, 1, tn), lambda i, j: (i // tpb, 0, j))
    return pl.pallas_call(
        _outproj_kernel,
        grid=(m // tm, d // tn),
        in_specs=[pl.BlockSpec((tm, d), lambda i, j: (i, 0), pipeline_mode=pl.Buffered(1)),
                  pl.BlockSpec((tm, d), lambda i, j: (i, 0), pipeline_mode=pl.Buffered(1)),
                  pl.BlockSpec((d, tn), lambda i, j: (0, j)),
                  pl.BlockSpec((d, tn), lambda i, j: (1, j)),
                  pl.BlockSpec((tm, tn), lambda i, j: (i, j)),
                  gate_spec],
        out_specs=pl.BlockSpec((tm, tn), lambda i, j: (i, j)),
        out_shape=jax.ShapeDtypeStruct((m, d), F32),
        compiler_params=_cparams(("arbitrary", "arbitrary")),
        name="outproj",
    )(y_n, attn_n, w, w, x, gate)


def _peer_route_kernel(q_ref, k1_ref, k2_ref, s1_ref, s2_ref, tau_ref, cc_ref):
    q = q_ref[...]
    k1 = k1_ref[...].astype(BF16)
    k2 = k2_ref[...].astype(BF16)
    half = PEER_D_KEY // 2
    tm = q.shape[0]
    taus, ccs = [], []
    for h in range(PEER_HEADS):
        base = h * PEER_D_KEY
        s1 = _nt(k1, q[:, base:base + half].astype(BF16))
        s2 = _nt(k2, q[:, base + half:base + PEER_D_KEY].astype(BF16))
        s1_ref[h * PEER_KEYS:(h + 1) * PEER_KEYS, :] = s1
        s2_ref[h * PEER_KEYS:(h + 1) * PEER_KEYS, :] = s2

        def top_vals(x):
            vals = []
            for _ in range(PEER_TOPK):
                m = jnp.max(x, axis=0, keepdims=True)
                vals.append(m)
                x = jnp.where(x == m, -jnp.inf, x)
            return vals

        v1 = top_vals(s1)
        v2 = top_vals(s2)
        v1a = jnp.concatenate(v1, axis=0)
        v2a = jnp.concatenate(v2, axis=0)
        half_k = PEER_TOPK // 2
        cand = jnp.concatenate([v + v2a[:half_k] for v in v1[:half_k]]
                               + [v1[0] + v2a[half_k:], v1a[half_k:] + v2[0]], axis=0) + 0.0
        ckey = _float_key(cand)

        def count_ge(thr, ckey=ckey):
            return jnp.sum((ckey >= thr).astype(I32), axis=0, keepdims=True)

        tau = _key_float(_bisect_kth(count_ge, PEER_TOPK, (1, tm)))
        cmax = cand[0:1, :]
        zsum = jnp.sum(jnp.where(cand >= tau, jnp.exp(cand - cmax), 0.0), axis=0, keepdims=True)
        taus.append(tau)
        ccs.append(cmax + jnp.log(zsum))
    tau_ref[...] = jnp.concatenate(taus, axis=0)
    cc_ref[...] = jnp.concatenate(ccs, axis=0)


def _peer_route(q, k1, k2, *, tm):
    t = q.shape[0]
    rows = PEER_HEADS * PEER_KEYS
    half = PEER_D_KEY // 2
    return pl.pallas_call(
        _peer_route_kernel,
        grid=(t // tm,),
        in_specs=[pl.BlockSpec((tm, PEER_HEADS * PEER_D_KEY), lambda i: (i, 0)),
                  pl.BlockSpec((PEER_KEYS, half), lambda i: (0, 0)),
                  pl.BlockSpec((PEER_KEYS, half), lambda i: (0, 0))],
        out_specs=[pl.BlockSpec((rows, tm), lambda i: (0, i)),
                   pl.BlockSpec((rows, tm), lambda i: (0, i)),
                   pl.BlockSpec((PEER_HEADS, tm), lambda i: (0, i)),
                   pl.BlockSpec((PEER_HEADS, tm), lambda i: (0, i))],
        out_shape=[jax.ShapeDtypeStruct((rows, t), F32), jax.ShapeDtypeStruct((rows, t), F32),
                   jax.ShapeDtypeStruct((PEER_HEADS, t), F32), jax.ShapeDtypeStruct((PEER_HEADS, t), F32)],
        compiler_params=_cparams(("arbitrary",)),
        name="peer_route",
    )(q, k1, k2)


def _peer_dense_kernel(xb_ref, s1_ref, s2_ref, tau_ref, cc_ref, u_ref, v_ref, o_ref, *, te):
    e = pl.program_id(1)

    @pl.when(e == 0)
    def _():
        o_ref[...] = jnp.zeros(o_ref.shape, F32)

    tm = xb_ref.shape[0]
    nsub = MXU_TILE // PEER_KEYS
    coefs = []
    for c in range(te // MXU_TILE):
        ut = _nt(xb_ref[...], u_ref[c * MXU_TILE:(c + 1) * MXU_TILE, :]).T
        for i in range(nsub):
            k = c * nsub + i
            gate = jnp.zeros((PEER_KEYS, tm), F32)
            for h in range(PEER_HEADS):
                sm = s1_ref[k, h:h + 1, :] + s2_ref[h * PEER_KEYS:(h + 1) * PEER_KEYS, :]
                gate = gate + jnp.where(sm >= tau_ref[h:h + 1, :], jnp.exp(sm - cc_ref[h:h + 1, :]), 0.0)
            coefs.append(gate * jax.nn.gelu(ut[i * PEER_KEYS:(i + 1) * PEER_KEYS, :]))
    coef = jnp.concatenate(coefs, axis=0).T.astype(BF16)
    for n in range(o_ref.shape[1] // MXU_TILE):
        cols = slice(n * MXU_TILE, (n + 1) * MXU_TILE)
        o_ref[:, cols] += jnp.dot(coef, v_ref[:, cols], preferred_element_type=F32)


def _peer_dense(xb, s1, s2, tau, cc, u_b, v_b, *, tm):
    t, d = xb.shape
    ne = u_b.shape[0]
    te = min(PEER_TE, ne)
    rows = PEER_HEADS * PEER_KEYS
    once = pl.Buffered(1)
    s1 = s1.reshape(PEER_HEADS, PEER_KEYS, t).transpose(1, 0, 2)
    return pl.pallas_call(
        functools.partial(_peer_dense_kernel, te=te),
        grid=(t // tm, ne // te),
        in_specs=[pl.BlockSpec((tm, d), lambda i, e: (i, 0), pipeline_mode=once),
                  pl.BlockSpec((te // PEER_KEYS, PEER_HEADS, tm), lambda i, e: (e, 0, i)),
                  pl.BlockSpec((rows, tm), lambda i, e: (0, i), pipeline_mode=once),
                  pl.BlockSpec((PEER_HEADS, tm), lambda i, e: (0, i), pipeline_mode=once),
                  pl.BlockSpec((PEER_HEADS, tm), lambda i, e: (0, i), pipeline_mode=once),
                  pl.BlockSpec((te, d), lambda i, e: (e, 0)),
                  pl.BlockSpec((te, d), lambda i, e: (e, 0))],
        out_specs=pl.BlockSpec((tm, d), lambda i, e: (i, 0), pipeline_mode=once),
        out_shape=jax.ShapeDtypeStruct((t, d), F32),
        compiler_params=_cparams(("arbitrary", "arbitrary")),
        name="peer_dense",
    )(xb, s1, s2, tau, cc, u_b, v_b)


def _final_kernel(x_ref, f_ref, gate_ref, g_ref, o_ref):
    o_ref[...] = _rms(x_ref[...] + gate_ref[...] * f_ref[...], g_ref[...])


def _final(x, f, gate, g, *, tm, per_token, rows_per_batch):
    m, d = x.shape
    if per_token:
        gate_spec = pl.BlockSpec((tm, d), lambda i: (i, 0))
    else:
        tpb = rows_per_batch // tm
        gate_spec = pl.BlockSpec((None, 1, d), lambda i: (i // tpb, 0, 0))
    return pl.pallas_call(
        _final_kernel,
        grid=(m // tm,),
        in_specs=[pl.BlockSpec((tm, d), lambda i: (i, 0)),
                  pl.BlockSpec((tm, d), lambda i: (i, 0)),
                  gate_spec,
                  pl.BlockSpec((1, d), lambda i: (0, 0))],
        out_specs=pl.BlockSpec((tm, d), lambda i: (i, 0)),
        out_shape=jax.ShapeDtypeStruct((m, d), F32),
        compiler_params=_cparams(("arbitrary",)),
        name="final_norm",
    )(x, f, gate, g.reshape(1, d))


def _layout(d):
    nh = d // SSD_HEADDIM
    kvw = KV_HEADS * HEAD_DIM
    qiw = IDX_HEADS * IDX_DIM
    xbcw = d + 2 * SSD_GROUPS * D_STATE
    lay, off = {}, 0
    for name, w in (("z", d), ("q", d), ("qi", qiw), ("xbc", xbcw), ("k", kvw), ("v", kvw), ("tail", PROJ_TN)):
        assert w % PROJ_TN == 0
        lay[name] = off
        off += w
    lay["total"] = off
    assert IDX_DIM + nh + IDX_HEADS <= PROJ_TN
    return lay


def _group_cols(xbc, d):
    lead = xbc.shape[:-1]
    gn = SSD_GROUPS * D_STATE
    x = xbc[..., :d].reshape(lead + (SSD_GROUPS, d // SSD_GROUPS))
    b = xbc[..., d:d + gn].reshape(lead + (SSD_GROUPS, D_STATE))
    c = xbc[..., d + gn:].reshape(lead + (SSD_GROUPS, D_STATE))
    return jnp.concatenate([x, b, c], axis=-1).reshape(lead + (d + 2 * gn,))


def _ungroup_cols(xg, d):
    lead = xg.shape[:-1]
    gn = SSD_GROUPS * D_STATE
    xsw = d // SSD_GROUPS
    g3 = xg.reshape(lead + (SSD_GROUPS, xsw + 2 * D_STATE))
    return jnp.concatenate([g3[..., :xsw].reshape(lead + (d,)),
                            g3[..., xsw:xsw + D_STATE].reshape(lead + (gn,)),
                            g3[..., xsw + D_STATE:].reshape(lead + (gn,))], axis=-1)


def _w_in_moves(d):
    nh = d // SSD_HEADDIM
    kvw = KV_HEADS * HEAD_DIM
    qiw = IDX_HEADS * IDX_DIM
    xbcw = d + 2 * SSD_GROUPS * D_STATE
    names = ("z", "xbc", "dt", "q", "k", "v", "qi", "wi", "ki")
    sizes = (d, xbcw, nh, d, kvw, kvw, qiw, IDX_HEADS, IDX_DIM)
    src = dict(zip(names, np.concatenate([[0], np.cumsum(sizes)[:-1]]).tolist()))
    lay = _layout(d)
    moves = [(lay["z"], src["z"], d), (lay["q"], src["q"], d), (lay["qi"], src["qi"], qiw),
             (lay["k"], src["k"], kvw), (lay["v"], src["v"], kvw)]
    xsw = d // SSD_GROUPS
    gw = xsw + 2 * D_STATE
    gn = SSD_GROUPS * D_STATE
    for g in range(SSD_GROUPS):
        base = lay["xbc"] + g * gw
        moves += [(base, src["xbc"] + g * xsw, xsw),
                  (base + xsw, src["xbc"] + d + g * D_STATE, D_STATE),
                  (base + xsw + D_STATE, src["xbc"] + d + gn + g * D_STATE, D_STATE)]
    tail = [(src["ki"], IDX_DIM), (src["dt"], nh), (src["wi"], IDX_HEADS)]
    return moves, tail, int(sum(sizes))


def _w_in_prep_kernel(tbl_ref, x_ref, last_ref, o_ref):
    j = pl.program_id(0)
    is_last = j == pl.num_programs(0) - 1

    @pl.when(jnp.logical_not(is_last))
    def _():
        o_ref[...] = x_ref[...].astype(BF16)

    @pl.when(is_last)
    def _():
        o_ref[...] = last_ref[...].astype(BF16)


def _prep_w_in(w_in, d):
    moves, tail, ncols = _w_in_moves(d)
    total = _layout(d)["total"]
    w_t = jnp.swapaxes(w_in[0], 0, 1)
    nblk = total // LANES
    src_row = np.zeros((nblk,), np.int32)
    for dst, src, w in moves + [(total - PROJ_TN, tail[0][0], tail[0][1])]:
        assert dst % LANES == 0 and w % LANES == 0 and src % SUBLANES == 0
        for b in range(w // LANES):
            src_row[dst // LANES + b] = src + b * LANES
    assert PROJ_TN == 2 * LANES and tail[0][1] == LANES
    pieces = [lax.slice_in_dim(w_t, s, s + w, axis=0) for s, w in tail[1:]]
    pieces.append(jnp.zeros((LANES - sum(w for _, w in tail[1:]), d), w_t.dtype))
    last = jnp.concatenate(pieces, axis=0)
    return pl.pallas_call(
        _w_in_prep_kernel,
        grid_spec=pltpu.PrefetchScalarGridSpec(
            num_scalar_prefetch=1,
            grid=(nblk,),
            in_specs=[pl.BlockSpec((pl.Element(LANES), pl.Element(d)),
                                   lambda j, tbl: (tbl[j] * SUBLANES, 0)),
                      pl.BlockSpec((LANES, d), lambda j, tbl: (0, 0))],
            out_specs=pl.BlockSpec((LANES, d), lambda j, tbl: (j, 0)),
        ),
        out_shape=jax.ShapeDtypeStruct((total, d), BF16),
        compiler_params=_cparams(("arbitrary",)),
        name="w_in_prep",
    )(jnp.asarray(src_row // SUBLANES), w_t, last)


def _group_major(vec, rpg):
    return jnp.pad(vec.reshape(SSD_GROUPS, 1, rpg), ((0, 0), (0, 0), (0, LANES - rpg)))


def _rope_tables(pos):
    half = HEAD_DIM // 8
    inv = ROPE_THETA ** (-jnp.arange(half, dtype=F32) / half)
    ang = pos.astype(F32)[:, None] * inv[None, :]
    cos, sin = jnp.cos(ang), jnp.sin(ang)
    n = pos.shape[0]
    rest = HEAD_DIM - 2 * half
    return (jnp.concatenate([cos, cos, jnp.ones((n, rest), F32)], axis=1),
            jnp.concatenate([-sin, sin, jnp.zeros((n, rest), F32)], axis=1))


def _layer(x2d, mods, pos, prm, *, nb, seq, per_token, tm, ssd_cfg, attn_fn, peer_tm):
    m, d = x2d.shape
    lay = prm["lay"]
    nh = d // SSD_HEADDIM
    rpg = nh // SSD_GROUPS
    sh1, sc1, g1, sh2, sc2, g2 = mods
    cos, sin = _rope_tables(pos)
    xoff = lay["xbc"]
    rope = (cos, sin, prm["kidx_norm_g"],
            ((lay["q"] // PROJ_TN, lay["xbc"] // PROJ_TN), (lay["k"] // PROJ_TN, lay["v"] // PROJ_TN)),
            lay["tail"] // PROJ_TN)
    tm_e = min(tm, ELEMWISE_TM)
    tm_p = min(m if per_token else seq, PROJ_TM)
    h1 = _modulate(x2d, prm["norm1_g"], sh1, sc1, tm=tm_e, per_token=per_token, rows_per_batch=seq)
    proj = _proj(h1, prm["w_in"], tm=tm_p, rope=rope, w_transposed=True)

    gw = rpg * SSD_HEADDIM + 2 * D_STATE
    dt_raw = proj[:, lay["tail"] + IDX_DIM: lay["tail"] + IDX_DIM + nh]
    y_ssd, ssm_new = ssd_cfg(proj, dt_raw, xoff // gw)
    y_n = _gate_norm(y_ssd, proj, prm["ssd_norm_g"], tm=tm_e)

    o_attn = attn_fn(proj)

    a_n = _rms_cast(o_attn, prm["attn_norm_g"], tm=tm_e)
    x1 = _outproj(y_n, a_n, prm["w_out"], x2d, g1, tm=tm_p, per_token=per_token,
                  rows_per_batch=seq)

    xb = _modulate(x1, prm["norm2_g"], sh2, sc2, tm=tm_e, per_token=per_token, rows_per_batch=seq)
    qp = _proj(xb, prm["peer_wq"], tm=tm_p)
    s1, s2, tau, cc = _peer_route(qp, prm["peer_k1"], prm["peer_k2"], tm=min(peer_tm, PEER_ROUTE_TM))
    ffn = _peer_dense(xb, s1, s2, tau, cc, prm["peer_u"], prm["peer_v"], tm=peer_tm)
    y = _final(x1, ffn, g2, prm["final_norm_g"], tm=tm_e, per_token=per_token, rows_per_batch=seq)
    return y, proj, ssm_new


def kernel(x_prompt, x_sample, c_prompt, c_sample, cache_k, cache_v, cache_kidx, state_ssm, state_conv, page_table, w_mod, b_mod, norm1_g, w_in, conv_w, conv_b, dt_bias, a_log, d_skip, ssd_norm_g, kidx_norm_g, attn_norm_g, w_out, norm2_g, peer_wq, peer_k1, peer_k2, peer_u, peer_v, final_norm_g):
    nbp, seq, d = x_prompt.shape
    nbs, tdec, _ = x_sample.shape
    depth = w_mod.shape[0]
    assert depth == 1
    nh = d // SSD_HEADDIM
    rpg = nh // SSD_GROUPS
    gw = rpg * SSD_HEADDIM + 2 * D_STATE
    lay = _layout(d)
    assert lay["xbc"] % gw == 0
    kvw = KV_HEADS * HEAD_DIM
    xbcw = d + 2 * SSD_GROUPS * D_STATE

    prm = dict(
        lay=lay,
        norm1_g=norm1_g[0], kidx_norm_g=kidx_norm_g[0], ssd_norm_g=ssd_norm_g[0], attn_norm_g=attn_norm_g[0],
        norm2_g=norm2_g[0], final_norm_g=final_norm_g,
        w_in=_prep_w_in(w_in, d),
        w_out=_cast_bf16(w_out[0]),
        peer_wq=_cast_bf16(peer_wq[0]), peer_k1=peer_k1[0], peer_k2=peer_k2[0],
        peer_u=_cast_bf16(peer_u[0]), peer_v=_cast_bf16(peer_v[0]),
    )
    conv_w_g = _group_cols(conv_w[0], d)
    conv_b_g = _group_cols(conv_b[0], d).reshape(1, xbcw)
    dtb_g = _group_major(dt_bias[0], rpg)
    alog_g = _group_major(a_log[0], rpg)
    dsk_g = _group_major(d_skip[0], rpg)

    mod = _adaln(jnp.concatenate([c_prompt, c_sample], axis=0), w_mod[0], b_mod[0])
    mod_p = [a.reshape(nbp, 1, d) for a in jnp.split(mod[:nbp], 6, axis=-1)]
    mod_s = [jnp.repeat(a, SAMPLE_PAD, axis=0) for a in jnp.split(mod[nbp:], 6, axis=-1)]

    def dt_group_major(dt_raw):
        rows = dt_raw.shape[0]
        dtg = dt_raw.reshape(rows, SSD_GROUPS, rpg).transpose(1, 0, 2)
        return jnp.pad(dtg, ((0, 0), (0, 0), (0, LANES - rpg)))

    cs_p = min(SSD_CHUNK, seq)
    nc_p = seq // cs_p
    assert seq % cs_p == 0 and seq % Q_BLOCK == 0

    def ssd_prompt(proj, dt_raw, xoff_blocks):
        hist = jnp.zeros((nbp, SUBLANES, xbcw), F32)
        h0 = jnp.zeros((nbp, nh, SSD_HEADDIM, D_STATE), F32)
        return _ssd(proj, xoff_blocks, hist, conv_w_g, conv_b_g, dt_group_major(dt_raw), dtb_g, alog_g, dsk_g, h0,
                    nb=nbp, nc=nc_p, cs=cs_p, d=d, valid_len=cs_p)

    tm_p = min(512, seq)
    pos_p = jnp.tile(jnp.arange(seq), nbp)
    y_p, proj_p, ssm_p = _layer(
        x_prompt.reshape(nbp * seq, d), mod_p, pos_p, prm, nb=nbp, seq=seq, per_token=False, tm=tm_p,
        ssd_cfg=ssd_prompt, attn_fn=functools.partial(_attn_prompt, lay=lay, nb=nbp, seq=seq, d=d),
        peer_tm=min(PEER_TM, nbp * seq))

    npages = page_table.shape[1]
    past = npages * PAGE_SIZE
    ms = nbs * SAMPLE_PAD
    xs_pad = jnp.pad(x_sample, ((0, 0), (0, SAMPLE_PAD - tdec), (0, 0))).reshape(ms, d)
    pos_s = jnp.tile(past + jnp.arange(SAMPLE_PAD), nbs)

    def ssd_sample(proj, dt_raw, xoff_blocks):
        xbc = proj[:, lay["xbc"]:lay["xbc"] + xbcw].reshape(nbs, SAMPLE_PAD, xbcw)
        xbc = jnp.pad(xbc, ((0, 0), (0, SAMPLE_CHUNK - SAMPLE_PAD), (0, 0))).reshape(nbs * SAMPLE_CHUNK, xbcw)
        dtr = jnp.pad(dt_raw.reshape(nbs, SAMPLE_PAD, nh), ((0, 0), (0, SAMPLE_CHUNK - SAMPLE_PAD), (0, 0)))
        hist = jnp.pad(_group_cols(state_conv[0], d), ((0, 0), (SUBLANES - (CONV_W - 1), 0), (0, 0)))
        y, hnew = _ssd(xbc, 0, hist, conv_w_g, conv_b_g, dt_group_major(dtr.reshape(nbs * SAMPLE_CHUNK, nh)),
                       dtb_g, alog_g, dsk_g, state_ssm[0], nb=nbs, nc=1, cs=SAMPLE_CHUNK, d=d, valid_len=tdec)
        y = y.reshape(nbs, SAMPLE_CHUNK, d)[:, :SAMPLE_PAD].reshape(ms, d)
        return y, hnew

    def attn_sample(proj):
        def seg(name, w):
            return proj[:, lay[name]:lay[name] + w].reshape(nbs, SAMPLE_PAD, w)
        tail = lay["tail"]
        ki_new = proj[:, tail:tail + IDX_DIM].reshape(nbs, SAMPLE_PAD, IDX_DIM)
        wi = proj[:, tail + IDX_DIM + nh: tail + IDX_DIM + nh + IDX_HEADS].reshape(nbs, SAMPLE_PAD, IDX_HEADS)
        return _attn_sample(seg("q", d), seg("qi", IDX_HEADS * IDX_DIM), wi, ki_new, seg("k", kvw), seg("v", kvw),
                            cache_k, cache_v, cache_kidx, page_table, t_valid=tdec)

    y_s, proj_s, ssm_s = _layer(
        xs_pad, mod_s, pos_s, prm, nb=nbs, seq=SAMPLE_PAD, per_token=True, tm=ms,
        ssd_cfg=ssd_sample, attn_fn=attn_sample, peer_tm=ms)

    def states(proj, nb, rows, valid):
        p3 = proj.reshape(nb, rows, lay["total"])[:, :valid]
        k = p3[:, :, lay["k"]:lay["k"] + kvw].reshape(1, nb, valid, KV_HEADS, HEAD_DIM)
        v = p3[:, :, lay["v"]:lay["v"] + kvw].reshape(1, nb, valid, KV_HEADS, HEAD_DIM)
        ki = p3[:, :, lay["tail"]:lay["tail"] + IDX_DIM].reshape(1, nb, valid, IDX_DIM)
        xbc = _ungroup_cols(p3[:, valid - (CONV_W - 1):valid, lay["xbc"]:lay["xbc"] + xbcw], d)
        return k, v, ki, xbc.reshape(1, nb, CONV_W - 1, xbcw)

    k_p, v_p, ki_p, conv_p = states(proj_p, nbp, seq, seq)
    k_s, v_s, ki_s, conv_s = states(proj_s, nbs, SAMPLE_PAD, tdec)
    y_prompt = y_p.reshape(nbp, seq, d)
    y_sample = y_s.reshape(nbs, SAMPLE_PAD, d)[:, :tdec]
    return (y_prompt, y_sample, k_p, v_p, ki_p, ssm_p[None], conv_p,
            k_s, v_s, ki_s, ssm_s[None], conv_s)
```

```python
import functools

import jax
import jax.numpy as jnp
import numpy as np
from jax import lax
from jax.experimental import pallas as pl
from jax.experimental.pallas import tpu as pltpu

F32 = jnp.float32
BF16 = jnp.bfloat16
I32 = jnp.int32

SSD_HEADDIM = 64
SSD_GROUPS = 8
D_STATE = 128
CONV_W = 4
SSD_CHUNK = 256
HEAD_DIM = 128
KV_HEADS = 8
ROPE_THETA = 500000.0
IDX_HEADS = 32
IDX_DIM = 128
IDX_TOPK = 256
Q_BLOCK = 128
PEER_HEADS = 8
PEER_KEYS = 128
PEER_TOPK = 16
PEER_D_KEY = 256
PAGE_SIZE = 128
EPS = 1e-6

LANES = 128
SUBLANES = 8
VMEM_LIMIT_BYTES = 56 * 1024 * 1024

MXU_TILE = 256
PROJ_TN = MXU_TILE
PROJ_TM = 1024
ELEMWISE_TM = 256
CAST_ROWS = 512
PEER_ROUTE_TM = 256
PEER_TM = 1024
PEER_TE = 512
SAMPLE_PAD = 8
SAMPLE_CHUNK = 128
ATTN_TK = 512
ATTN_GROUPS_PER_STEP = 4
PAGES_PER_STEP = 16
MAX_PAGE_UNROLL = 5
NEG_BIG = -1e30
M_INIT = -1e29
INT_MIN = -2 ** 31


def _cparams(sem):
    return pltpu.CompilerParams(dimension_semantics=sem, vmem_limit_bytes=VMEM_LIMIT_BYTES)


def _nt(a, b):
    return lax.dot_general(a, b, (((1,), (1,)), ((), ())), preferred_element_type=F32)


def _tn(a, b):
    return lax.dot_general(a, b, (((0,), (0,)), ((), ())), preferred_element_type=F32)


def _rms(x, g):
    return x * lax.rsqrt(jnp.mean(x * x, axis=-1, keepdims=True) + EPS) * g


def _rope_tile(a, cos, sin):
    half = HEAD_DIM // 8
    lane = lax.broadcasted_iota(I32, a.shape, 1)
    sw = jnp.where(lane < half, pltpu.roll(a, HEAD_DIM - half, 1), pltpu.roll(a, half, 1))
    return a * cos + sw * sin


def _float_key(x):
    bits = pltpu.bitcast(x, I32)
    return jnp.where(bits < 0, bits ^ jnp.int32(0x7FFFFFFF), bits)


def _key_float(key):
    bits = jnp.where(key < 0, key ^ jnp.int32(0x7FFFFFFF), key)
    return pltpu.bitcast(bits, F32)


def _bisect_kth(count_ge, k, shape):
    imin = jnp.int32(INT_MIN)

    def body(i, ans):
        cand_u = ans | lax.shift_left(jnp.int32(1), jnp.int32(31) - i)
        return jnp.where(count_ge(cand_u ^ imin) >= k, cand_u, ans)

    ans = lax.fori_loop(0, 32, body, jnp.zeros(shape, I32))
    return ans ^ imin


def _cast_kernel(x_ref, o_ref):
    o_ref[...] = x_ref[...].astype(BF16)


def _cast_bf16(w):
    r, c = w.shape
    tr = min(r, CAST_ROWS)
    assert r % tr == 0
    return pl.pallas_call(
        _cast_kernel,
        grid=(r // tr,),
        in_specs=[pl.BlockSpec((tr, c), lambda i: (i, 0))],
        out_specs=pl.BlockSpec((tr, c), lambda i: (i, 0)),
        out_shape=jax.ShapeDtypeStruct((r, c), BF16),
        compiler_params=_cparams(("arbitrary",)),
        name="cast_bf16",
    )(w)


def _adaln_kernel(c_ref, w_ref, b_ref, o_ref):
    c = c_ref[...]
    a = (c * jax.nn.sigmoid(c)).astype(BF16)
    o_ref[...] = jnp.dot(a, w_ref[...].astype(BF16), preferred_element_type=F32) + b_ref[...]


def _adaln(c, w_mod, b_mod):
    n, d = c.shape
    npad = -(-n // SUBLANES) * SUBLANES
    cp = jnp.pad(c, ((0, npad - n), (0, 0)))
    nout = w_mod.shape[1]
    tn = PROJ_TN
    out = pl.pallas_call(
        _adaln_kernel,
        grid=(nout // tn,),
        in_specs=[pl.BlockSpec((npad, d), lambda j: (0, 0)),
                  pl.BlockSpec((d, tn), lambda j: (0, j)),
                  pl.BlockSpec((1, tn), lambda j: (0, j))],
        out_specs=pl.BlockSpec((npad, tn), lambda j: (0, j)),
        out_shape=jax.ShapeDtypeStruct((npad, nout), F32),
        compiler_params=_cparams(("arbitrary",)),
        name="adaln",
    )(cp, w_mod, b_mod.reshape(1, nout))
    return out[:n]


def _modulate_kernel(x_ref, g_ref, sh_ref, sc_ref, o_ref):
    o_ref[...] = (_rms(x_ref[...], g_ref[...]) * (1.0 + sc_ref[...]) + sh_ref[...]).astype(BF16)


def _modulate(x, g, shift, scale, *, tm, per_token, rows_per_batch):
    m, d = x.shape
    if per_token:
        mod_spec = pl.BlockSpec((tm, d), lambda i: (i, 0))
    else:
        tiles_per_batch = rows_per_batch // tm
        mod_spec = pl.BlockSpec((None, 1, d), lambda i: (i // tiles_per_batch, 0, 0))
    return pl.pallas_call(
        _modulate_kernel,
        grid=(m // tm,),
        in_specs=[pl.BlockSpec((tm, d), lambda i: (i, 0)),
                  pl.BlockSpec((1, d), lambda i: (0, 0)),
                  mod_spec, mod_spec],
        out_specs=pl.BlockSpec((tm, d), lambda i: (i, 0)),
        out_shape=jax.ShapeDtypeStruct((m, d), BF16),
        compiler_params=_cparams(("arbitrary",)),
        name="modulate",
    )(x, g.reshape(1, d), shift, scale)


def _proj_kernel(*refs, rope_ranges, tail_tile, w_transposed):
    if tail_tile is not None:
        h_ref, w_ref, cos_ref, sin_ref, kg_ref, o_ref = refs
    else:
        h_ref, w_ref, o_ref = refs
    j = pl.program_id(1)

    def product():
        if w_transposed:
            return _nt(h_ref[...], w_ref[...])
        return jnp.dot(h_ref[...], w_ref[...], preferred_element_type=F32)

    if tail_tile is None:
        o_ref[...] = product()
        return

    is_rope = functools.reduce(jnp.logical_or, [(j >= lo) & (j < hi) for lo, hi in rope_ranges])
    is_tail = j == tail_tile
    tn = o_ref.shape[1]

    @pl.when(jnp.logical_not(is_rope | is_tail))
    def _():
        o_ref[...] = product()

    @pl.when(is_rope)
    def _():
        acc = product()
        cos = cos_ref[...]
        sin = sin_ref[...]
        for t in range(tn // HEAD_DIM):
            sl = slice(t * HEAD_DIM, (t + 1) * HEAD_DIM)
            o_ref[:, sl] = _rope_tile(acc[:, sl], cos, sin)

    @pl.when(is_tail)
    def _():
        acc = product()
        ki = _rms(acc[:, :IDX_DIM], kg_ref[...])
        o_ref[:, :IDX_DIM] = _rope_tile(ki, cos_ref[...], sin_ref[...])
        o_ref[:, IDX_DIM:] = acc[:, IDX_DIM:]


def _proj(h, w, *, tm, rope=None, w_transposed=False):
    m, d = h.shape
    n = w.shape[0] if w_transposed else w.shape[1]
    tn = PROJ_TN
    assert m % tm == 0 and n % tn == 0
    w_spec = pl.BlockSpec((tn, d), lambda i, j: (j, 0)) if w_transposed else pl.BlockSpec((d, tn), lambda i, j: (0, j))
    in_specs = [pl.BlockSpec((tm, d), lambda i, j: (i, 0), pipeline_mode=pl.Buffered(1)), w_spec]
    args = [h, w]
    rope_ranges, tail_tile = (), None
    if rope is not None:
        cos, sin, kidx_g, rope_ranges, tail_tile = rope
        in_specs += [pl.BlockSpec((tm, HEAD_DIM), lambda i, j: (i, 0)),
                     pl.BlockSpec((tm, HEAD_DIM), lambda i, j: (i, 0)),
                     pl.BlockSpec((1, IDX_DIM), lambda i, j: (0, 0))]
        args += [cos, sin, kidx_g.reshape(1, IDX_DIM)]
    return pl.pallas_call(
        functools.partial(_proj_kernel, rope_ranges=rope_ranges, tail_tile=tail_tile, w_transposed=w_transposed),
        grid=(m // tm, n // tn),
        in_specs=in_specs,
        out_specs=pl.BlockSpec((tm, tn), lambda i, j: (i, j)),
        out_shape=jax.ShapeDtypeStruct((m, n), F32),
        compiler_params=_cparams(("arbitrary", "arbitrary")),
        name="proj",
    )(*args)


def _ssd_kernel(x_ref, prev_ref, hist_ref, cw_ref, cb_ref, dt_ref, dtb_ref, alog_ref, dsk_ref, h0_ref,
                y_ref, hout_ref, h_scr, *, cs, rpg, valid_len, nc):
    c = pl.program_id(2)
    xs_w = rpg * SSD_HEADDIM

    @pl.when(c == 0)
    def _():
        h_scr[...] = h0_ref[...]

    def rows_to_chunk(a):
        if a.shape[0] == cs:
            return a
        return jnp.concatenate([a, jnp.zeros((cs - a.shape[0], a.shape[1]), a.dtype)], axis=0)

    prev = jnp.where(c == 0, hist_ref[...], prev_ref[...])
    cat = jnp.concatenate([prev, rows_to_chunk(x_ref[...])], axis=0)
    w = cw_ref[...]
    acc = cb_ref[...]
    for j in range(CONV_W):
        lo = SUBLANES - (CONV_W - 1) + j
        acc = acc + cat[lo:lo + cs] * w[j:j + 1]
    xc = acc * jax.nn.sigmoid(acc)
    xs = xc[:, :xs_w]
    bm = xc[:, xs_w:xs_w + D_STATE]
    cm = xc[:, xs_w + D_STATE:]
    bm_b = bm.astype(BF16)
    cm_b = cm.astype(BF16)

    z = rows_to_chunk(dt_ref[...]) + dtb_ref[...]
    dt = jnp.maximum(z, 0.0) + jnp.log1p(jnp.exp(-jnp.abs(z)))
    if valid_len < cs:
        row = lax.broadcasted_iota(I32, dt.shape, 0)
        dt = jnp.where(row < valid_len, dt, 0.0)
    a_neg = -jnp.exp(alog_ref[...])
    ii = lax.broadcasted_iota(I32, (cs, cs), 0)
    jj = lax.broadcasted_iota(I32, (cs, cs), 1)
    causal = ii >= jj
    acs = jnp.dot(causal.astype(F32), dt * a_neg, preferred_element_type=F32,
                  precision=lax.Precision.HIGHEST)
    acs_t = acs.T
    a_last = acs[cs - 1:cs, :]
    cb = _nt(cm_b, bm_b)
    dsk = dsk_ref[...]

    first = lax.broadcasted_iota(I32, (cs, LANES), 1) < SSD_HEADDIM
    first_row = lax.broadcasted_iota(I32, (LANES, D_STATE), 0) < SSD_HEADDIM
    ys = []
    for pr in range(rpg // 2):
        r0, r1 = 2 * pr, 2 * pr + 1
        a0, a1 = acs[:, r0:r0 + 1], acs[:, r1:r1 + 1]
        xp = xs[:, pr * LANES:(pr + 1) * LANES]
        xd = xp * jnp.where(first, dt[:, r0:r0 + 1], dt[:, r1:r1 + 1])
        y = jnp.zeros((cs, LANES), F32)
        for r, keep in ((r0, first), (r1, jnp.logical_not(first))):
            lm = jnp.exp(jnp.where(causal, acs[:, r:r + 1] - acs_t[r:r + 1, :], -jnp.inf))
            y = y + jnp.dot((cb * lm).astype(BF16), jnp.where(keep, xd, 0.0).astype(BF16),
                            preferred_element_type=F32)
        hp = h_scr[pr]
        y = y + jnp.where(first, jnp.exp(a0), jnp.exp(a1)) * _nt(cm_b, hp.astype(BF16))
        al0, al1 = a_last[:, r0:r0 + 1], a_last[:, r1:r1 + 1]
        decay = jnp.where(first, jnp.exp(al0 - a0), jnp.exp(al1 - a1))
        h_scr[pr] = (hp * jnp.where(first_row, jnp.exp(al0), jnp.exp(al1))
                     + _tn((xd * decay).astype(BF16), bm_b))
        ys.append(y + xp * jnp.where(first[0:1, :], dsk[:, r0:r0 + 1], dsk[:, r1:r1 + 1]))
    y_ref[...] = jnp.concatenate(ys, axis=1)[:y_ref.shape[0]]

    @pl.when(c == nc - 1)
    def _():
        hout_ref[...] = h_scr[...]


def _ssd(xg, xoff_blocks, hist8, conv_w_g, conv_b_g, dt_g, dtb_g, alog_g, dsk_g, h0, *, nb, nc, cs, d, valid_len,
         rows=None):
    g_cnt = SSD_GROUPS
    rpg = d // SSD_HEADDIM // g_cnt
    gw = rpg * SSD_HEADDIM + 2 * D_STATE
    rows = cs if rows is None else rows
    assert rows == cs or nc == 1
    m = nb * nc * rows
    cpb = rows // SUBLANES
    nh = d // SSD_HEADDIM
    npair = rpg // 2
    assert rpg % 2 == 0 and 2 * SSD_HEADDIM == LANES
    kern = functools.partial(_ssd_kernel, cs=cs, rpg=rpg, valid_len=valid_len, nc=nc)
    y, hout = pl.pallas_call(
        kern,
        grid=(nb, g_cnt, nc),
        in_specs=[
            pl.BlockSpec((rows, gw), lambda b, g, c: (b * nc + c, xoff_blocks + g)),
            pl.BlockSpec((SUBLANES, gw), lambda b, g, c: (jnp.maximum((b * nc + c) * cpb - 1, 0), xoff_blocks + g)),
            pl.BlockSpec((None, SUBLANES, gw), lambda b, g, c: (b, 0, g)),
            pl.BlockSpec((CONV_W, gw), lambda b, g, c: (0, g)),
            pl.BlockSpec((1, gw), lambda b, g, c: (0, g)),
            pl.BlockSpec((None, rows, LANES), lambda b, g, c: (g, b * nc + c, 0)),
            pl.BlockSpec((None, 1, LANES), lambda b, g, c: (g, 0, 0)),
            pl.BlockSpec((None, 1, LANES), lambda b, g, c: (g, 0, 0)),
            pl.BlockSpec((None, 1, LANES), lambda b, g, c: (g, 0, 0)),
            pl.BlockSpec((None, npair, LANES, D_STATE), lambda b, g, c: (b, g, 0, 0)),
        ],
        out_specs=[
            pl.BlockSpec((rows, rpg * SSD_HEADDIM), lambda b, g, c: (b * nc + c, g)),
            pl.BlockSpec((None, npair, LANES, D_STATE), lambda b, g, c: (b, g, 0, 0)),
        ],
        out_shape=[jax.ShapeDtypeStruct((m, d), F32),
                   jax.ShapeDtypeStruct((nb, nh // 2, LANES, D_STATE), F32)],
        scratch_shapes=[pltpu.VMEM((npair, LANES, D_STATE), F32)],
        compiler_params=_cparams(("arbitrary", "arbitrary", "arbitrary")),
        name="ssd_scan",
    )(xg, xg, hist8, conv_w_g, conv_b_g, dt_g, dtb_g, alog_g, dsk_g, h0.reshape(nb, nh // 2, LANES, D_STATE))
    return y, hout.reshape(nb, nh, SSD_HEADDIM, D_STATE)


def _gate_norm_kernel(y_ref, z_ref, g_ref, o_ref):
    z = z_ref[...]
    o_ref[...] = _rms(y_ref[...] * (z * jax.nn.sigmoid(z)), g_ref[...]).astype(BF16)


def _gate_norm(y, proj, g, *, tm):
    m, d = y.shape
    return pl.pallas_call(
        _gate_norm_kernel,
        grid=(m // tm,),
        in_specs=[pl.BlockSpec((tm, d), lambda i: (i, 0)),
                  pl.BlockSpec((tm, d), lambda i: (i, 0)),
                  pl.BlockSpec((1, d), lambda i: (0, 0))],
        out_specs=pl.BlockSpec((tm, d), lambda i: (i, 0)),
        out_shape=jax.ShapeDtypeStruct((m, d), BF16),
        compiler_params=_cparams(("arbitrary",)),
        name="gate_norm",
    )(y, proj, g.reshape(1, d))


def _attn_prompt_kernel(qi_ref, tail_ref, ki_ref, q_ref, k_ref, v_ref, o_ref, key_scr, thr_scr,
                        *, topk, tk, n_ssd_heads, rq, gps):
    qb = pl.program_id(1)
    g = pl.program_id(2)
    nkt = (qb * Q_BLOCK + Q_BLOCK + tk - 1) // tk
    imin = jnp.int32(INT_MIN)

    @pl.when(g == 0)
    def _():
        qi = qi_ref[...]
        qis = jnp.concatenate([qi[:, h * IDX_DIM:(h + 1) * IDX_DIM] for h in range(IDX_HEADS)],
                              axis=0).astype(BF16)
        w_t = tail_ref[...].T[n_ssd_heads:n_ssd_heads + IDX_HEADS, :] * (IDX_HEADS ** -0.5 * IDX_DIM ** -0.5)
        tpos = qb * Q_BLOCK + lax.broadcasted_iota(I32, (tk, Q_BLOCK), 1)
        kiota = lax.broadcasted_iota(I32, (tk, Q_BLOCK), 0)

        def score_tile(kt, carry):
            rows = pl.ds(pl.multiple_of(kt * tk, tk), tk)
            x = _nt(ki_ref[rows, :].astype(BF16), qis)
            sc = jnp.zeros((tk, Q_BLOCK), F32)
            for h in range(IDX_HEADS):
                sc = sc + jnp.maximum(x[:, h * Q_BLOCK:(h + 1) * Q_BLOCK], 0.0) * w_t[h:h + 1, :]
            key = _float_key(sc + 0.0)
            key_scr[rows, :] = jnp.where(kt * tk + kiota <= tpos, key, imin)
            return carry

        lax.fori_loop(0, nkt, score_tile, 0)

        def count_ge(thr):
            def body(kt, cnt):
                rows = pl.ds(pl.multiple_of(kt * tk, tk), tk)
                ge = (key_scr[rows, :] >= thr).astype(I32)
                return cnt + jnp.sum(ge.reshape(tk // SUBLANES, SUBLANES, Q_BLOCK), axis=0)
            cnt = lax.fori_loop(0, nkt, body, jnp.zeros((SUBLANES, Q_BLOCK), I32))
            return jnp.sum(cnt, axis=0, keepdims=True)

        thr = jnp.maximum(_bisect_kth(count_ge, topk, (1, Q_BLOCK)), jnp.int32(INT_MIN + 1))
        thr_scr[...] = jnp.broadcast_to(thr, (SUBLANES, Q_BLOCK))

    q = q_ref[...]
    nq = rq * Q_BLOCK
    qs = []
    for u in range(gps):
        heads = [q[:, (u * rq + r) * HEAD_DIM:(u * rq + r + 1) * HEAD_DIM] for r in range(rq)]
        qs.append((jnp.concatenate(heads, axis=0) * (HEAD_DIM ** -0.5)).astype(BF16))
    thr_q = jnp.concatenate([thr_scr[0:1, :]] * rq, axis=1)

    def attend(kt, carry):
        rows = pl.ds(pl.multiple_of(kt * tk, tk), tk)
        sel = jnp.concatenate([key_scr[rows, :]] * rq, axis=1) >= thr_q
        k_all = k_ref[rows, :]
        v_all = v_ref[rows, :]
        s_ts = [_nt(k_all[:, u * HEAD_DIM:(u + 1) * HEAD_DIM].astype(BF16), qs[u]) for u in range(gps)]
        v_ts = [v_all[:, u * HEAD_DIM:(u + 1) * HEAD_DIM].T.astype(BF16) for u in range(gps)]
        out = []
        for u in range(gps):
            m, l, acc = carry[u]
            s_t = jnp.where(sel, s_ts[u], NEG_BIG)
            m_new = jnp.maximum(m, jnp.max(s_t, axis=0, keepdims=True))
            alpha = jnp.exp(m - m_new)
            p = jnp.exp(s_t - m_new)
            l = l * alpha + jnp.sum(p, axis=0, keepdims=True)
            acc = acc * alpha + jnp.dot(v_ts[u], p.astype(BF16), preferred_element_type=F32)
            out.append((m_new, l, acc))
        return tuple(out)

    init = tuple((jnp.full((1, nq), M_INIT, F32), jnp.zeros((1, nq), F32), jnp.zeros((HEAD_DIM, nq), F32))
                 for _ in range(gps))
    res = lax.fori_loop(0, nkt, attend, init)
    for u in range(gps):
        _, l, acc = res[u]
        o = (acc / l).T
        for r in range(rq):
            c0 = (u * rq + r) * HEAD_DIM
            o_ref[:, c0:c0 + HEAD_DIM] = o[r * Q_BLOCK:(r + 1) * Q_BLOCK, :]


def _attn_prompt(proj, lay, *, nb, seq, d):
    rq = d // HEAD_DIM // KV_HEADS
    nqb = seq // Q_BLOCK
    topk = min(IDX_TOPK, seq // 4)
    tk = min(ATTN_TK, seq)
    qiw = IDX_HEADS * IDX_DIM
    gps = min(ATTN_GROUPS_PER_STEP, KV_HEADS)
    qw = gps * rq * HEAD_DIM
    kw = gps * HEAD_DIM
    assert KV_HEADS % gps == 0 and lay["q"] % qw == 0 and lay["k"] % kw == 0 and lay["v"] % kw == 0
    kern = functools.partial(_attn_prompt_kernel, topk=topk, tk=tk, n_ssd_heads=d // SSD_HEADDIM, rq=rq, gps=gps)
    return pl.pallas_call(
        kern,
        grid=(nb, nqb, KV_HEADS // gps),
        in_specs=[
            pl.BlockSpec((Q_BLOCK, qiw), lambda b, i, g: (b * nqb + i, lay["qi"] // qiw)),
            pl.BlockSpec((Q_BLOCK, LANES), lambda b, i, g: (b * nqb + i, lay["tail"] // LANES + 1)),
            pl.BlockSpec((seq, IDX_DIM), lambda b, i, g: (b, lay["tail"] // IDX_DIM)),
            pl.BlockSpec((Q_BLOCK, qw), lambda b, i, g: (b * nqb + i, lay["q"] // qw + g)),
            pl.BlockSpec((seq, kw), lambda b, i, g: (b, lay["k"] // kw + g)),
            pl.BlockSpec((seq, kw), lambda b, i, g: (b, lay["v"] // kw + g)),
        ],
        out_specs=pl.BlockSpec((Q_BLOCK, qw), lambda b, i, g: (b * nqb + i, g)),
        out_shape=jax.ShapeDtypeStruct((nb * seq, d), F32),
        scratch_shapes=[pltpu.VMEM((seq, Q_BLOCK), I32), pltpu.VMEM((SUBLANES, Q_BLOCK), I32)],
        compiler_params=_cparams(("arbitrary", "arbitrary", "arbitrary")),
        name="attn_prompt",
    )(proj, proj, proj, proj, proj, proj)


def _head_rows(page_ref, head):
    return page_ref[pl.ds(head, PAGE_SIZE, stride=KV_HEADS), :]


def _fold_lane_groups(x):
    sh = LANES // 2
    while sh >= SAMPLE_PAD:
        x = x + pltpu.roll(x, sh, 1)
        sh //= 2
    return x


def _sample_scores_kernel(pt_ref, *refs, npages, topk, pps):
    kip_refs = refs[:pps]
    kp_refs = refs[pps:2 * pps]
    qi_ref, w_ref, qbd_ref, kinew_ref, knew_ref, p_ref, key_scr, kc_scr, s_scr, qbd_scr = refs[2 * pps:]
    j = pl.program_id(1)
    imin = jnp.int32(INT_MIN)
    nip = qi_ref.shape[0]
    nlp = qbd_ref.shape[0]
    groups_per_slab = LANES // SAMPLE_PAD
    lane = lax.broadcasted_iota(I32, (PAGE_SIZE, LANES), 1)
    lane_group = lane // SAMPLE_PAD

    @pl.when(j == 0)
    def _():
        qbd_scr[...] = (qbd_ref[...] * (HEAD_DIM ** -0.5)).astype(BF16)
        kc_scr[...] = jnp.full(kc_scr.shape, imin, I32)

    qi = qi_ref[...].astype(BF16)
    w_row = w_ref[...]

    def index_keys(kidx):
        x = _nt(kidx.astype(BF16), qi)
        r = jnp.maximum(x, 0.0) * w_row
        acc = r[:, :LANES]
        for c in range(1, nip // LANES):
            acc = acc + r[:, c * LANES:(c + 1) * LANES]
        return _float_key(_fold_lane_groups(acc) + 0.0)

    def put_page(page, keys, scores):
        rows = pl.ds(pl.multiple_of(page * PAGE_SIZE, PAGE_SIZE), PAGE_SIZE)
        key_scr[rows, :] = keys
        s_scr[rows, :] = scores
        slab = page // groups_per_slab
        kc_scr[slab] = jnp.where(lane_group == page % groups_per_slab, keys, kc_scr[slab])

    kidx_cat = jnp.concatenate([r[...] for r in kip_refs], axis=0)
    keys = index_keys(kidx_cat)
    k2d = jnp.concatenate(
        [jnp.concatenate([_head_rows(kp, h).astype(BF16) for h in range(KV_HEADS)], axis=1)
         for kp in kp_refs], axis=0)
    scores = _nt(k2d, qbd_scr[...])
    for i in range(pps):
        sl = slice(i * PAGE_SIZE, (i + 1) * PAGE_SIZE)
        put_page(j * pps + i, keys[sl], scores[sl])

    @pl.when(j == 0)
    def _():
        knew = index_keys(kinew_ref[...])
        srow = lax.broadcasted_iota(I32, (PAGE_SIZE, LANES), 0)
        knew = jnp.where(srow <= lane % SAMPLE_PAD, knew, imin)
        put_page(jnp.int32(npages), knew, _nt(knew_ref[...].astype(BF16), qbd_scr[...]))

    @pl.when(j == pl.num_programs(1) - 1)
    def _():
        kc = kc_scr[...]

        tok = lane[0:1, :] % SAMPLE_PAD

        def count_ge(thr):
            ge = (kc >= thr).astype(F32)
            cnt = jnp.sum(jnp.sum(ge.reshape(-1, SUBLANES, LANES), axis=0), axis=0, keepdims=True)
            tot = jnp.zeros((1, LANES), F32)
            for t in range(SAMPLE_PAD):
                tot = jnp.where(tok == t, jnp.sum(jnp.where(tok == t, cnt, 0.0), axis=1, keepdims=True), tot)
            return tot

        thr = _bisect_kth(count_ge, topk, (1, LANES))
        reps = nlp // LANES
        thr_q = jnp.concatenate([thr] * reps, axis=1)

        def masked(page):
            rows = pl.ds(pl.multiple_of(page * PAGE_SIZE, PAGE_SIZE), PAGE_SIZE)
            key = jnp.concatenate([key_scr[rows, :]] * reps, axis=1)
            return rows, (key >= thr_q) & (key != imin)

        def max_body(page, m):
            rows, sel = masked(page)
            return jnp.maximum(m, jnp.max(jnp.where(sel, s_scr[rows, :], NEG_BIG), axis=0, keepdims=True))

        unroll = next(u for u in range(MAX_PAGE_UNROLL, 0, -1) if (npages + 1) % u == 0)
        m = lax.fori_loop(0, npages + 1, max_body, jnp.full((1, nlp), NEG_BIG, F32), unroll=unroll)

        def exp_body(page, l):
            rows, sel = masked(page)
            e = jnp.where(sel, jnp.exp(s_scr[rows, :] - m), 0.0)
            s_scr[rows, :] = e
            return l + jnp.sum(e, axis=0, keepdims=True)

        l = lax.fori_loop(0, npages + 1, exp_body, jnp.zeros((1, nlp), F32), unroll=unroll)

        def out_body(page, carry):
            rows = pl.ds(pl.multiple_of(page * PAGE_SIZE, PAGE_SIZE), PAGE_SIZE)
            p_ref[page] = (s_scr[rows, :] / l).T.astype(BF16)
            return carry

        lax.fori_loop(0, npages + 1, out_body, 0, unroll=unroll)


def _sample_pv_kernel(pt_ref, *refs, pps, rq):
    vp_refs = refs[:pps]
    p_ref, pnew_ref, vnew_ref, o_ref, acc_scr = refs[pps:]
    j = pl.program_id(1)
    nrow = rq * SAMPLE_PAD

    @pl.when(j == 0)
    def _():
        pn = pnew_ref[0]
        vn = vnew_ref[...].astype(BF16)
        for g in range(KV_HEADS):
            acc_scr[g] = jnp.dot(pn[g * nrow:(g + 1) * nrow, :], vn[:, g * HEAD_DIM:(g + 1) * HEAD_DIM],
                                 preferred_element_type=F32)

    pcat = jnp.concatenate([p_ref[i] for i in range(pps)], axis=1)
    for g in range(KV_HEADS):
        vg = jnp.concatenate([_head_rows(vp, g).astype(BF16) for vp in vp_refs], axis=0)
        acc_scr[g] += jnp.dot(pcat[g * nrow:(g + 1) * nrow, :], vg, preferred_element_type=F32)

    @pl.when(j == pl.num_programs(1) - 1)
    def _():
        o_ref[...] = acc_scr[...]


def _attn_sample(q, qi, wi, ki_new, k_new, v_new, cache_k, cache_v, cache_kidx, page_table, *, t_valid):
    nb, npages = page_table.shape
    pps = min(PAGES_PER_STEP, npages)
    assert npages % pps == 0
    nsteps = npages // pps
    past = npages * PAGE_SIZE
    topk = min(IDX_TOPK, (past + t_valid) // 4)
    d = q.shape[-1]
    rq = d // HEAD_DIM // KV_HEADS
    kvw = KV_HEADS * HEAD_DIM
    nrow = rq * SAMPLE_PAD
    ck = cache_k.reshape(-1, PAGE_SIZE * KV_HEADS, HEAD_DIM)
    cv = cache_v.reshape(-1, PAGE_SIZE * KV_HEADS, HEAD_DIM)
    cki = cache_kidx.reshape(-1, PAGE_SIZE, IDX_DIM)

    def pad_to(a, axis, mult):
        n = a.shape[axis]
        widths = [(0, 0)] * a.ndim
        widths[axis] = (0, -(-n // mult) * mult - n)
        return jnp.pad(a, widths)

    qi_s = pad_to(qi.reshape(nb, SAMPLE_PAD, IDX_HEADS, IDX_DIM).transpose(0, 2, 1, 3)
                  .reshape(nb, IDX_HEADS * SAMPLE_PAD, IDX_DIM), 1, LANES)
    w_row = pad_to((wi * (IDX_HEADS ** -0.5 * IDX_DIM ** -0.5)).transpose(0, 2, 1)
                   .reshape(nb, 1, IDX_HEADS * SAMPLE_PAD), 2, LANES)
    nip = qi_s.shape[1]
    q_g = q.reshape(nb, SAMPLE_PAD, KV_HEADS, rq, HEAD_DIM).transpose(0, 2, 3, 1, 4).reshape(
        nb, KV_HEADS, nrow, HEAD_DIM)
    qbd = pad_to(jnp.einsum("bgrd,gh->bgrhd", q_g, jnp.eye(KV_HEADS, dtype=q.dtype))
                 .reshape(nb, KV_HEADS * nrow, kvw), 1, LANES)
    nlp = qbd.shape[1]
    pad_rows = ((0, 0), (0, PAGE_SIZE - SAMPLE_PAD), (0, 0))
    kinew_p = jnp.pad(ki_new, pad_rows)
    knew_p = jnp.pad(k_new, pad_rows)
    vnew_p = jnp.pad(v_new, pad_rows)
    npg = npages + 1
    nslab = -(-npg // (LANES // SAMPLE_PAD))

    def kidx_spec(i):
        return pl.BlockSpec((None, PAGE_SIZE, IDX_DIM), lambda b, j, pt, i=i: (pt[b, j * pps + i], 0, 0))

    def kv_specs():
        return [pl.BlockSpec((None, PAGE_SIZE * KV_HEADS, HEAD_DIM), lambda b, j, pt, i=i: (pt[b, j * pps + i], 0, 0))
                for i in range(pps)]

    probs = pl.pallas_call(
        functools.partial(_sample_scores_kernel, npages=npages, topk=topk, pps=pps),
        grid_spec=pltpu.PrefetchScalarGridSpec(
            num_scalar_prefetch=1,
            grid=(nb, nsteps),
            in_specs=[kidx_spec(i) for i in range(pps)] + kv_specs() + [
                pl.BlockSpec((None, nip, IDX_DIM), lambda b, j, pt: (b, 0, 0)),
                pl.BlockSpec((None, 1, nip), lambda b, j, pt: (b, 0, 0)),
                pl.BlockSpec((None, nlp, kvw), lambda b, j, pt: (b, 0, 0)),
                pl.BlockSpec((None, PAGE_SIZE, IDX_DIM), lambda b, j, pt: (b, 0, 0)),
                pl.BlockSpec((None, PAGE_SIZE, kvw), lambda b, j, pt: (b, 0, 0)),
            ],
            out_specs=pl.BlockSpec((None, npg, nlp, PAGE_SIZE), lambda b, j, pt: (b, 0, 0, 0)),
            scratch_shapes=[pltpu.VMEM((npg * PAGE_SIZE, LANES), I32),
                            pltpu.VMEM((nslab, PAGE_SIZE, LANES), I32),
                            pltpu.VMEM((npg * PAGE_SIZE, nlp), F32),
                            pltpu.VMEM((nlp, kvw), BF16)],
        ),
        out_shape=jax.ShapeDtypeStruct((nb, npg, nlp, PAGE_SIZE), BF16),
        compiler_params=_cparams(("arbitrary", "arbitrary")),
        name="sample_scores",
    )(page_table, *([cki] * pps), *([ck] * pps), qi_s, w_row, qbd, kinew_p, knew_p)

    o = pl.pallas_call(
        functools.partial(_sample_pv_kernel, pps=pps, rq=rq),
        grid_spec=pltpu.PrefetchScalarGridSpec(
            num_scalar_prefetch=1,
            grid=(nb, nsteps),
            in_specs=kv_specs() + [
                pl.BlockSpec((None, pps, nlp, PAGE_SIZE), lambda b, j, pt: (b, j, 0, 0)),
                pl.BlockSpec((None, 1, nlp, PAGE_SIZE), lambda b, j, pt: (b, npages, 0, 0)),
                pl.BlockSpec((None, PAGE_SIZE, kvw), lambda b, j, pt: (b, 0, 0)),
            ],
            out_specs=pl.BlockSpec((None, KV_HEADS, nrow, HEAD_DIM), lambda b, j, pt: (b, 0, 0, 0)),
            scratch_shapes=[pltpu.VMEM((KV_HEADS, nrow, HEAD_DIM), F32)],
        ),
        out_shape=jax.ShapeDtypeStruct((nb, KV_HEADS, nrow, HEAD_DIM), F32),
        compiler_params=_cparams(("arbitrary", "arbitrary")),
        name="sample_pv",
    )(page_table, *([cv] * pps), probs, probs, vnew_p)
    return o.reshape(nb, KV_HEADS, rq, SAMPLE_PAD, HEAD_DIM).transpose(0, 3, 1, 2, 4).reshape(nb * SAMPLE_PAD, d)


def _rms_cast_kernel(x_ref, g_ref, o_ref):
    o_ref[...] = _rms(x_ref[...], g_ref[...]).astype(BF16)


def _rms_cast(x, g, *, tm):
    m, d = x.shape
    return pl.pallas_call(
        _rms_cast_kernel,
        grid=(m // tm,),
        in_specs=[pl.BlockSpec((tm, d), lambda i: (i, 0)), pl.BlockSpec((1, d), lambda i: (0, 0))],
        out_specs=pl.BlockSpec((tm, d), lambda i: (i, 0)),
        out_shape=jax.ShapeDtypeStruct((m, d), BF16),
        compiler_params=_cparams(("arbitrary",)),
        name="rms_cast",
    )(x, g.reshape(1, d))


def _outproj_kernel(y_ref, a_ref, w1_ref, w2_ref, x_ref, gate_ref, o_ref):
    acc = jnp.dot(y_ref[...], w1_ref[...], preferred_element_type=F32)
    acc = acc + jnp.dot(a_ref[...], w2_ref[...], preferred_element_type=F32)
    o_ref[...] = x_ref[...] + gate_ref[...] * acc


def _outproj(y_n, attn_n, w, x, gate, *, tm, per_token, rows_per_batch):
    m, d = x.shape
    tn = PROJ_TN
    if per_token:
        gate_spec = pl.BlockSpec((tm, tn), lambda i, j: (i, j))
    else:
        tpb = rows_per_batch // tm
        gate_spec = pl.BlockSpec((None, 1, tn), lambda i, j: (i // tpb, 0, j))
    return pl.pallas_call(
        _outproj_kernel,
        grid=(m // tm, d // tn),
        in_specs=[pl.BlockSpec((tm, d), lambda i, j: (i, 0), pipeline_mode=pl.Buffered(1)),
                  pl.BlockSpec((tm, d), lambda i, j: (i, 0), pipeline_mode=pl.Buffered(1)),
                  pl.BlockSpec((d, tn), lambda i, j: (0, j)),
                  pl.BlockSpec((d, tn), lambda i, j: (1, j)),
                  pl.BlockSpec((tm, tn), lambda i, j: (i, j)),
                  gate_spec],
        out_specs=pl.BlockSpec((tm, tn), lambda i, j: (i, j)),
        out_shape=jax.ShapeDtypeStruct((m, d), F32),
        compiler_params=_cparams(("arbitrary", "arbitrary")),
        name="outproj",
    )(y_n, attn_n, w, w, x, gate)


def _peer_route_kernel(q_ref, k1_ref, k2_ref, s1_ref, s2_ref, tau_ref, cc_ref):
    q = q_ref[...]
    k1 = k1_ref[...].astype(BF16)
    k2 = k2_ref[...].astype(BF16)
    half = PEER_D_KEY // 2
    tm = q.shape[0]
    taus, ccs = [], []
    for h in range(PEER_HEADS):
        base = h * PEER_D_KEY
        s1 = _nt(k1, q[:, base:base + half].astype(BF16))
        s2 = _nt(k2, q[:, base + half:base + PEER_D_KEY].astype(BF16))
        s1_ref[h * PEER_KEYS:(h + 1) * PEER_KEYS, :] = s1
        s2_ref[h * PEER_KEYS:(h + 1) * PEER_KEYS, :] = s2

        def top_vals(x):
            vals = []
            for _ in range(PEER_TOPK):
                m = jnp.max(x, axis=0, keepdims=True)
                vals.append(m)
                x = jnp.where(x == m, -jnp.inf, x)
            return vals

        v1 = top_vals(s1)
        v2 = top_vals(s2)
        v1a = jnp.concatenate(v1, axis=0)
        v2a = jnp.concatenate(v2, axis=0)
        half_k = PEER_TOPK // 2
        cand = jnp.concatenate([v + v2a[:half_k] for v in v1[:half_k]]
                               + [v1[0] + v2a[half_k:], v1a[half_k:] + v2[0]], axis=0) + 0.0
        ckey = _float_key(cand)

        def count_ge(thr, ckey=ckey):
            return jnp.sum((ckey >= thr).astype(I32), axis=0, keepdims=True)

        tau = _key_float(_bisect_kth(count_ge, PEER_TOPK, (1, tm)))
        cmax = cand[0:1, :]
        zsum = jnp.sum(jnp.where(cand >= tau, jnp.exp(cand - cmax), 0.0), axis=0, keepdims=True)
        taus.append(tau)
        ccs.append(cmax + jnp.log(zsum))
    tau_ref[...] = jnp.concatenate(taus, axis=0)
    cc_ref[...] = jnp.concatenate(ccs, axis=0)


def _peer_route(q, k1, k2, *, tm):
    t = q.shape[0]
    rows = PEER_HEADS * PEER_KEYS
    half = PEER_D_KEY // 2
    return pl.pallas_call(
        _peer_route_kernel,
        grid=(t // tm,),
        in_specs=[pl.BlockSpec((tm, PEER_HEADS * PEER_D_KEY), lambda i: (i, 0)),
                  pl.BlockSpec((PEER_KEYS, half), lambda i: (0, 0)),
                  pl.BlockSpec((PEER_KEYS, half), lambda i: (0, 0))],
        out_specs=[pl.BlockSpec((rows, tm), lambda i: (0, i)),
                   pl.BlockSpec((rows, tm), lambda i: (0, i)),
                   pl.BlockSpec((PEER_HEADS, tm), lambda i: (0, i)),
                   pl.BlockSpec((PEER_HEADS, tm), lambda i: (0, i))],
        out_shape=[jax.ShapeDtypeStruct((rows, t), F32), jax.ShapeDtypeStruct((rows, t), F32),
                   jax.ShapeDtypeStruct((PEER_HEADS, t), F32), jax.ShapeDtypeStruct((PEER_HEADS, t), F32)],
        compiler_params=_cparams(("arbitrary",)),
        name="peer_route",
    )(q, k1, k2)


def _peer_dense_kernel(xb_ref, s1_ref, s2_ref, tau_ref, cc_ref, u_ref, v_ref, o_ref, *, te):
    e = pl.program_id(1)

    @pl.when(e == 0)
    def _():
        o_ref[...] = jnp.zeros(o_ref.shape, F32)

    tm = xb_ref.shape[0]
    nsub = MXU_TILE // PEER_KEYS
    coefs = []
    for c in range(te // MXU_TILE):
        ut = _nt(xb_ref[...], u_ref[c * MXU_TILE:(c + 1) * MXU_TILE, :]).T
        for i in range(nsub):
            k = c * nsub + i
            gate = jnp.zeros((PEER_KEYS, tm), F32)
            for h in range(PEER_HEADS):
                sm = s1_ref[k, h:h + 1, :] + s2_ref[h * PEER_KEYS:(h + 1) * PEER_KEYS, :]
                gate = gate + jnp.where(sm >= tau_ref[h:h + 1, :], jnp.exp(sm - cc_ref[h:h + 1, :]), 0.0)
            coefs.append(gate * jax.nn.gelu(ut[i * PEER_KEYS:(i + 1) * PEER_KEYS, :]))
    coef = jnp.concatenate(coefs, axis=0).T.astype(BF16)
    for n in range(o_ref.shape[1] // MXU_TILE):
        cols = slice(n * MXU_TILE, (n + 1) * MXU_TILE)
        o_ref[:, cols] += jnp.dot(coef, v_ref[:, cols], preferred_element_type=F32)


def _peer_dense(xb, s1, s2, tau, cc, u_b, v_b, *, tm):
    t, d = xb.shape
    ne = u_b.shape[0]
    te = min(PEER_TE, ne)
    rows = PEER_HEADS * PEER_KEYS
    once = pl.Buffered(1)
    s1 = s1.reshape(PEER_HEADS, PEER_KEYS, t).transpose(1, 0, 2)
    return pl.pallas_call(
        functools.partial(_peer_dense_kernel, te=te),
        grid=(t // tm, ne // te),
        in_specs=[pl.BlockSpec((tm, d), lambda i, e: (i, 0), pipeline_mode=once),
                  pl.BlockSpec((te // PEER_KEYS, PEER_HEADS, tm), lambda i, e: (e, 0, i)),
                  pl.BlockSpec((rows, tm), lambda i, e: (0, i), pipeline_mode=once),
                  pl.BlockSpec((PEER_HEADS, tm), lambda i, e: (0, i), pipeline_mode=once),
                  pl.BlockSpec((PEER_HEADS, tm), lambda i, e: (0, i), pipeline_mode=once),
                  pl.BlockSpec((te, d), lambda i, e: (e, 0)),
                  pl.BlockSpec((te, d), lambda i, e: (e, 0))],
        out_specs=pl.BlockSpec((tm, d), lambda i, e: (i, 0), pipeline_mode=once),
        out_shape=jax.ShapeDtypeStruct((t, d), F32),
        compiler_params=_cparams(("arbitrary", "arbitrary")),
        name="peer_dense",
    )(xb, s1, s2, tau, cc, u_b, v_b)


def _final_kernel(x_ref, f_ref, gate_ref, g_ref, o_ref):
    o_ref[...] = _rms(x_ref[...] + gate_ref[...] * f_ref[...], g_ref[...])


def _final(x, f, gate, g, *, tm, per_token, rows_per_batch):
    m, d = x.shape
    if per_token:
        gate_spec = pl.BlockSpec((tm, d), lambda i: (i, 0))
    else:
        tpb = rows_per_batch // tm
        gate_spec = pl.BlockSpec((None, 1, d), lambda i: (i // tpb, 0, 0))
    return pl.pallas_call(
        _final_kernel,
        grid=(m // tm,),
        in_specs=[pl.BlockSpec((tm, d), lambda i: (i, 0)),
                  pl.BlockSpec((tm, d), lambda i: (i, 0)),
                  gate_spec,
                  pl.BlockSpec((1, d), lambda i: (0, 0))],
        out_specs=pl.BlockSpec((tm, d), lambda i: (i, 0)),
        out_shape=jax.ShapeDtypeStruct((m, d), F32),
        compiler_params=_cparams(("arbitrary",)),
        name="final_norm",
    )(x, f, gate, g.reshape(1, d))


def _layout(d):
    nh = d // SSD_HEADDIM
    kvw = KV_HEADS * HEAD_DIM
    qiw = IDX_HEADS * IDX_DIM
    xbcw = d + 2 * SSD_GROUPS * D_STATE
    lay, off = {}, 0
    for name, w in (("z", d), ("q", d), ("qi", qiw), ("xbc", xbcw), ("k", kvw), ("v", kvw), ("tail", PROJ_TN)):
        assert w % PROJ_TN == 0
        lay[name] = off
        off += w
    lay["total"] = off
    assert IDX_DIM + nh + IDX_HEADS <= PROJ_TN
    return lay


def _group_cols(xbc, d):
    lead = xbc.shape[:-1]
    gn = SSD_GROUPS * D_STATE
    x = xbc[..., :d].reshape(lead + (SSD_GROUPS, d // SSD_GROUPS))
    b = xbc[..., d:d + gn].reshape(lead + (SSD_GROUPS, D_STATE))
    c = xbc[..., d + gn:].reshape(lead + (SSD_GROUPS, D_STATE))
    return jnp.concatenate([x, b, c], axis=-1).reshape(lead + (d + 2 * gn,))


def _ungroup_cols(xg, d):
    lead = xg.shape[:-1]
    gn = SSD_GROUPS * D_STATE
    xsw = d // SSD_GROUPS
    g3 = xg.reshape(lead + (SSD_GROUPS, xsw + 2 * D_STATE))
    return jnp.concatenate([g3[..., :xsw].reshape(lead + (d,)),
                            g3[..., xsw:xsw + D_STATE].reshape(lead + (gn,)),
                            g3[..., xsw + D_STATE:].reshape(lead + (gn,))], axis=-1)


def _w_in_moves(d):
    nh = d // SSD_HEADDIM
    kvw = KV_HEADS * HEAD_DIM
    qiw = IDX_HEADS * IDX_DIM
    xbcw = d + 2 * SSD_GROUPS * D_STATE
    names = ("z", "xbc", "dt", "q", "k", "v", "qi", "wi", "ki")
    sizes = (d, xbcw, nh, d, kvw, kvw, qiw, IDX_HEADS, IDX_DIM)
    src = dict(zip(names, np.concatenate([[0], np.cumsum(sizes)[:-1]]).tolist()))
    lay = _layout(d)
    moves = [(lay["z"], src["z"], d), (lay["q"], src["q"], d), (lay["qi"], src["qi"], qiw),
             (lay["k"], src["k"], kvw), (lay["v"], src["v"], kvw)]
    xsw = d // SSD_GROUPS
    gw = xsw + 2 * D_STATE
    gn = SSD_GROUPS * D_STATE
    for g in range(SSD_GROUPS):
        base = lay["xbc"] + g * gw
        moves += [(base, src["xbc"] + g * xsw, xsw),
                  (base + xsw, src["xbc"] + d + g * D_STATE, D_STATE),
                  (base + xsw + D_STATE, src["xbc"] + d + gn + g * D_STATE, D_STATE)]
    tail = [(src["ki"], IDX_DIM), (src["dt"], nh), (src["wi"], IDX_HEADS)]
    return moves, tail, int(sum(sizes))


def _w_in_prep_kernel(tbl_ref, x_ref, last_ref, o_ref):
    j = pl.program_id(0)
    is_last = j == pl.num_programs(0) - 1

    @pl.when(jnp.logical_not(is_last))
    def _():
        o_ref[...] = x_ref[...].astype(BF16)

    @pl.when(is_last)
    def _():
        o_ref[...] = last_ref[...].astype(BF16)


def _prep_w_in(w_in, d):
    moves, tail, ncols = _w_in_moves(d)
    total = _layout(d)["total"]
    w_t = jnp.swapaxes(w_in[0], 0, 1)
    nblk = total // LANES
    src_row = np.zeros((nblk,), np.int32)
    for dst, src, w in moves + [(total - PROJ_TN, tail[0][0], tail[0][1])]:
        assert dst % LANES == 0 and w % LANES == 0 and src % SUBLANES == 0
        for b in range(w // LANES):
            src_row[dst // LANES + b] = src + b * LANES
    assert PROJ_TN == 2 * LANES and tail[0][1] == LANES
    pieces = [lax.slice_in_dim(w_t, s, s + w, axis=0) for s, w in tail[1:]]
    pieces.append(jnp.zeros((LANES - sum(w for _, w in tail[1:]), d), w_t.dtype))
    last = jnp.concatenate(pieces, axis=0)
    return pl.pallas_call(
        _w_in_prep_kernel,
        grid_spec=pltpu.PrefetchScalarGridSpec(
            num_scalar_prefetch=1,
            grid=(nblk,),
            in_specs=[pl.BlockSpec((pl.Element(LANES), pl.Element(d)),
                                   lambda j, tbl: (tbl[j] * SUBLANES, 0)),
                      pl.BlockSpec((LANES, d), lambda j, tbl: (0, 0))],
            out_specs=pl.BlockSpec((LANES, d), lambda j, tbl: (j, 0)),
        ),
        out_shape=jax.ShapeDtypeStruct((total, d), BF16),
        compiler_params=_cparams(("arbitrary",)),
        name="w_in_prep",
    )(jnp.asarray(src_row // SUBLANES), w_t, last)


def _group_major(vec, rpg):
    return jnp.pad(vec.reshape(SSD_GROUPS, 1, rpg), ((0, 0), (0, 0), (0, LANES - rpg)))


def _rope_tables(pos):
    half = HEAD_DIM // 8
    inv = ROPE_THETA ** (-jnp.arange(half, dtype=F32) / half)
    ang = pos.astype(F32)[:, None] * inv[None, :]
    cos, sin = jnp.cos(ang), jnp.sin(ang)
    n = pos.shape[0]
    rest = HEAD_DIM - 2 * half
    return (jnp.concatenate([cos, cos, jnp.ones((n, rest), F32)], axis=1),
            jnp.concatenate([-sin, sin, jnp.zeros((n, rest), F32)], axis=1))


def _layer(x2d, mods, pos, prm, *, nb, seq, per_token, tm, ssd_cfg, attn_fn, peer_tm):
    m, d = x2d.shape
    lay = prm["lay"]
    nh = d // SSD_HEADDIM
    rpg = nh // SSD_GROUPS
    sh1, sc1, g1, sh2, sc2, g2 = mods
    cos, sin = _rope_tables(pos)
    xoff = lay["xbc"]
    rope = (cos, sin, prm["kidx_norm_g"],
            ((lay["q"] // PROJ_TN, lay["xbc"] // PROJ_TN), (lay["k"] // PROJ_TN, lay["v"] // PROJ_TN)),
            lay["tail"] // PROJ_TN)
    tm_e = min(tm, ELEMWISE_TM)
    tm_p = min(m if per_token else seq, PROJ_TM)
    h1 = _modulate(x2d, prm["norm1_g"], sh1, sc1, tm=tm_e, per_token=per_token, rows_per_batch=seq)
    proj = _proj(h1, prm["w_in"], tm=tm_p, rope=rope, w_transposed=True)

    gw = rpg * SSD_HEADDIM + 2 * D_STATE
    dt_raw = proj[:, lay["tail"] + IDX_DIM: lay["tail"] + IDX_DIM + nh]
    y_ssd, ssm_new = ssd_cfg(proj, dt_raw, xoff // gw)
    y_n = _gate_norm(y_ssd, proj, prm["ssd_norm_g"], tm=tm_e)

    o_attn = attn_fn(proj)

    a_n = _rms_cast(o_attn, prm["attn_norm_g"], tm=tm_e)
    x1 = _outproj(y_n, a_n, prm["w_out"], x2d, g1, tm=tm_p, per_token=per_token,
                  rows_per_batch=seq)

    xb = _modulate(x1, prm["norm2_g"], sh2, sc2, tm=tm_e, per_token=per_token, rows_per_batch=seq)
    qp = _proj(xb, prm["peer_wq"], tm=tm_p)
    s1, s2, tau, cc = _peer_route(qp, prm["peer_k1"], prm["peer_k2"], tm=min(peer_tm, PEER_ROUTE_TM))
    ffn = _peer_dense(xb, s1, s2, tau, cc, prm["peer_u"], prm["peer_v"], tm=peer_tm)
    y = _final(x1, ffn, g2, prm["final_norm_g"], tm=tm_e, per_token=per_token, rows_per_batch=seq)
    return y, proj, ssm_new


def kernel(x_prompt, x_sample, c_prompt, c_sample, cache_k, cache_v, cache_kidx, state_ssm, state_conv, page_table, w_mod, b_mod, norm1_g, w_in, conv_w, conv_b, dt_bias, a_log, d_skip, ssd_norm_g, kidx_norm_g, attn_norm_g, w_out, norm2_g, peer_wq, peer_k1, peer_k2, peer_u, peer_v, final_norm_g):
    nbp, seq, d = x_prompt.shape
    nbs, tdec, _ = x_sample.shape
    depth = w_mod.shape[0]
    assert depth == 1
    nh = d // SSD_HEADDIM
    rpg = nh // SSD_GROUPS
    gw = rpg * SSD_HEADDIM + 2 * D_STATE
    lay = _layout(d)
    assert lay["xbc"] % gw == 0
    kvw = KV_HEADS * HEAD_DIM
    xbcw = d + 2 * SSD_GROUPS * D_STATE

    prm = dict(
        lay=lay,
        norm1_g=norm1_g[0], kidx_norm_g=kidx_norm_g[0], ssd_norm_g=ssd_norm_g[0], attn_norm_g=attn_norm_g[0],
        norm2_g=norm2_g[0], final_norm_g=final_norm_g,
        w_in=_prep_w_in(w_in, d),
        w_out=_cast_bf16(w_out[0]),
        peer_wq=_cast_bf16(peer_wq[0]), peer_k1=peer_k1[0], peer_k2=peer_k2[0],
        peer_u=_cast_bf16(peer_u[0]), peer_v=_cast_bf16(peer_v[0]),
    )
    conv_w_g = _group_cols(conv_w[0], d)
    conv_b_g = _group_cols(conv_b[0], d).reshape(1, xbcw)
    dtb_g = _group_major(dt_bias[0], rpg)
    alog_g = _group_major(a_log[0], rpg)
    dsk_g = _group_major(d_skip[0], rpg)

    mod = _adaln(jnp.concatenate([c_prompt, c_sample], axis=0), w_mod[0], b_mod[0])
    mod_p = [a.reshape(nbp, 1, d) for a in jnp.split(mod[:nbp], 6, axis=-1)]
    mod_s = [jnp.repeat(a, SAMPLE_PAD, axis=0) for a in jnp.split(mod[nbp:], 6, axis=-1)]

    def dt_group_major(dt_raw):
        rows = dt_raw.shape[0]
        dtg = dt_raw.reshape(rows, SSD_GROUPS, rpg).transpose(1, 0, 2)
        return jnp.pad(dtg, ((0, 0), (0, 0), (0, LANES - rpg)))

    cs_p = min(SSD_CHUNK, seq)
    nc_p = seq // cs_p
    assert seq % cs_p == 0 and seq % Q_BLOCK == 0

    def ssd_prompt(proj, dt_raw, xoff_blocks):
        hist = jnp.zeros((nbp, SUBLANES, xbcw), F32)
        h0 = jnp.zeros((nbp, nh, SSD_HEADDIM, D_STATE), F32)
        return _ssd(proj, xoff_blocks, hist, conv_w_g, conv_b_g, dt_group_major(dt_raw), dtb_g, alog_g, dsk_g, h0,
                    nb=nbp, nc=nc_p, cs=cs_p, d=d, valid_len=cs_p)

    tm_p = min(512, seq)
    pos_p = jnp.tile(jnp.arange(seq), nbp)
    y_p, proj_p, ssm_p = _layer(
        x_prompt.reshape(nbp * seq, d), mod_p, pos_p, prm, nb=nbp, seq=seq, per_token=False, tm=tm_p,
        ssd_cfg=ssd_prompt, attn_fn=functools.partial(_attn_prompt, lay=lay, nb=nbp, seq=seq, d=d),
        peer_tm=min(PEER_TM, nbp * seq))

    npages = page_table.shape[1]
    past = npages * PAGE_SIZE
    ms = nbs * SAMPLE_PAD
    xs_pad = jnp.pad(x_sample, ((0, 0), (0, SAMPLE_PAD - tdec), (0, 0))).reshape(ms, d)
    pos_s = jnp.tile(past + jnp.arange(SAMPLE_PAD), nbs)

    def ssd_sample(proj, dt_raw, xoff_blocks):
        hist = jnp.pad(_group_cols(state_conv[0], d), ((0, 0), (SUBLANES - (CONV_W - 1), 0), (0, 0)))
        return _ssd(proj, xoff_blocks, hist, conv_w_g, conv_b_g, dt_group_major(dt_raw), dtb_g, alog_g, dsk_g,
                    state_ssm[0], nb=nbs, nc=1, cs=SAMPLE_CHUNK, d=d, valid_len=tdec, rows=SAMPLE_PAD)

    def attn_sample(proj):
        def seg(name, w):
            return proj[:, lay[name]:lay[name] + w].reshape(nbs, SAMPLE_PAD, w)
        tail = lay["tail"]
        ki_new = proj[:, tail:tail + IDX_DIM].reshape(nbs, SAMPLE_PAD, IDX_DIM)
        wi = proj[:, tail + IDX_DIM + nh: tail + IDX_DIM + nh + IDX_HEADS].reshape(nbs, SAMPLE_PAD, IDX_HEADS)
        return _attn_sample(seg("q", d), seg("qi", IDX_HEADS * IDX_DIM), wi, ki_new, seg("k", kvw), seg("v", kvw),
                            cache_k, cache_v, cache_kidx, page_table, t_valid=tdec)

    y_s, proj_s, ssm_s = _layer(
        xs_pad, mod_s, pos_s, prm, nb=nbs, seq=SAMPLE_PAD, per_token=True, tm=ms,
        ssd_cfg=ssd_sample, attn_fn=attn_sample, peer_tm=ms)

    def states(proj, nb, rows, valid):
        p3 = proj.reshape(nb, rows, lay["total"])[:, :valid]
        k = p3[:, :, lay["k"]:lay["k"] + kvw].reshape(1, nb, valid, KV_HEADS, HEAD_DIM)
        v = p3[:, :, lay["v"]:lay["v"] + kvw].reshape(1, nb, valid, KV_HEADS, HEAD_DIM)
        ki = p3[:, :, lay["tail"]:lay["tail"] + IDX_DIM].reshape(1, nb, valid, IDX_DIM)
        xbc = _ungroup_cols(p3[:, valid - (CONV_W - 1):valid, lay["xbc"]:lay["xbc"] + xbcw], d)
        return k, v, ki, xbc.reshape(1, nb, CONV_W - 1, xbcw)

    k_p, v_p, ki_p, conv_p = states(proj_p, nbp, seq, seq)
    k_s, v_s, ki_s, conv_s = states(proj_s, nbs, SAMPLE_PAD, tdec)
    y_prompt = y_p.reshape(nbp, seq, d)
    y_sample = y_s.reshape(nbs, SAMPLE_PAD, d)[:, :tdec]
    return (y_prompt, y_sample, k_p, v_p, ki_p, ssm_p[None], conv_p,
            k_s, v_s, ki_s, ssm_s[None], conv_s)
```
